```python
import math
import jax
import jax.numpy as jnp
from jax import lax
import numpy as np

D_MODEL = 1024
BATCH = 8
SEQ = 4096
DEPTH = 2

HEAD_DIM = D_MODEL // 16
NSA_HEADS = 6
NSA_KV_GROUPS = 2
NSA_HPG = NSA_HEADS // NSA_KV_GROUPS
CMP_BLOCK = 32
CMP_STRIDE = 16
CMP_HIDDEN = 2 * HEAD_DIM
SLC_BLOCK = 64
SLC_TOPN = 16
WINDOW = 512
NSA_QBLOCK = 64
FORCE_BONUS = 1000.0
NEG_BIG = -1e30
LB_FLOOR = 1e-30
SB_HEADS = 4
SB_QBLOCK = 128
HG_HEADS = 6
HG_KDIM = HEAD_DIM
HG_VDIM = HEAD_DIM
HG_CHUNK = 64
NUM_BUCKETS = 32
MAX_DISTANCE = 128
D_NSA = NSA_HEADS * HEAD_DIM
D_KV = NSA_KV_GROUPS * HEAD_DIM
D_SB = SB_HEADS * HEAD_DIM
D_HG = HG_HEADS * HG_VDIM
D_MIX = D_NSA + D_SB + D_HG
N_GATES = NSA_HEADS * 3
SPLIT_SIZES = (D_NSA, D_KV, D_KV, D_KV, D_KV, D_KV, D_KV, N_GATES, D_NSA,
               D_SB, D_SB, D_SB, D_SB,
               HG_HEADS * HG_KDIM, HG_HEADS * HG_KDIM, D_HG, D_HG)
D_IN = sum(SPLIT_SIZES)
ALPHA = (2 * DEPTH) ** 0.25
OUT_INIT_SCALE = (8 * DEPTH) ** -0.25
LN_EPS = 1e-5
RMS_EPS = 1e-6

kernel_name = 'hybrid_nsa_stickbreak_hgrn2_deepnorm'


def t5_bucket(rel):
    n = jnp.maximum(rel, 0)
    max_exact = NUM_BUCKETS // 2
    large = max_exact + (jnp.log(jnp.maximum(n, 1).astype(jnp.float32) / max_exact)
                         / math.log(MAX_DISTANCE / max_exact)
                         * (NUM_BUCKETS - max_exact)).astype(jnp.int32)
    large = jnp.clip(large, 0, NUM_BUCKETS - 1)
    return jnp.where(n < max_exact, n, large)


def masked_softmax(s, mask):
    s = jnp.where(mask, s.astype(jnp.float32), NEG_BIG)
    p = jax.nn.softmax(s, axis=-1)
    return jnp.where(mask, p, 0.0)


def layer_norm(v, g, b):
    v32 = v.astype(jnp.float32)
    mu = jnp.mean(v32, axis=-1, keepdims=True)
    var = jnp.mean(jnp.square(v32 - mu), axis=-1, keepdims=True)
    return ((v32 - mu) * lax.rsqrt(var + LN_EPS) * g + b).astype(v.dtype)


def nsa_mixer(q, kc_src, vc_src, ks_src, vs_src, kw_src, vw_src, gate_logits,
              cmp_pos, w_ck1, w_ck2, w_cv1, w_cv2, rel_bias):
    B, S = q.shape[:2]
    G, HPG, dk, QB = NSA_KV_GROUPS, NSA_HPG, HEAD_DIM, NSA_QBLOCK
    scale = 1.0 / math.sqrt(dk)
    q = q.reshape(B, S, G, HPG, dk).transpose(0, 2, 3, 1, 4)
    gates = jax.nn.sigmoid(gate_logits.reshape(B, S, G, HPG, 3).transpose(0, 2, 3, 1, 4))

    n_cmp = (S - CMP_BLOCK) // CMP_STRIDE + 1
    blk_idx = np.arange(n_cmp)[:, None] * CMP_STRIDE + np.arange(CMP_BLOCK)[None, :]

    def compress(src, w1, w2):
        blocks = src.reshape(B, S, G, dk)[:, blk_idx] + cmp_pos[None, None, :, None, :]
        blocks = blocks.transpose(0, 3, 1, 2, 4).reshape(B, G, n_cmp, CMP_BLOCK * dk)
        return jax.nn.gelu(blocks @ w1) @ w2

    kc = compress(kc_src, w_ck1, w_ck2)
    vc = compress(vc_src, w_cv1, w_cv2)
    cmp_start = np.arange(n_cmp) * CMP_STRIDE
    cmp_end = jnp.asarray(cmp_start + CMP_BLOCK - 1, dtype=jnp.int32)

    n_slc = S // SLC_BLOCK
    topn = min(SLC_TOPN, n_slc)
    slc_start = np.arange(n_slc) * SLC_BLOCK
    overlap = jnp.asarray(((cmp_start[:, None] < slc_start[None, :] + SLC_BLOCK)
                           & (cmp_start[:, None] + CMP_BLOCK > slc_start[None, :])).astype(np.float32))
    ks = ks_src.reshape(B, S, G, dk).transpose(0, 2, 1, 3).reshape(B, G, n_slc, SLC_BLOCK, dk)
    vs = vs_src.reshape(B, S, G, dk).transpose(0, 2, 1, 3).reshape(B, G, n_slc, SLC_BLOCK, dk)

    pad = ((0, 0), (0, 0), (WINDOW, 0), (0, 0))
    kw = jnp.pad(kw_src.reshape(B, S, G, dk).transpose(0, 2, 1, 3), pad)
    vw = jnp.pad(vw_src.reshape(B, S, G, dk).transpose(0, 2, 1, 3), pad)

    tbl = rel_bias.reshape(NUM_BUCKETS, G, HPG)
    tbl_g = tbl.transpose(1, 0, 2)
    b_ix = jnp.arange(B)[:, None, None, None]
    g_ix = jnp.arange(G)[None, :, None, None]
    blk = jnp.arange(n_slc)

    def block(i):
        t0 = i * QB
        t = t0 + jnp.arange(QB)
        qb = lax.dynamic_slice_in_dim(q, t0, QB, axis=3)
        gb = lax.dynamic_slice_in_dim(gates, t0, QB, axis=3)

        rel_c = t[:, None] - cmp_end[None, :]
        bias_c = tbl[t5_bucket(rel_c)].transpose(2, 3, 0, 1)
        s_c = jnp.einsum('bghtd,bgnd->bghtn', qb, kc) * scale + bias_c
        p_c = masked_softmax(s_c, rel_c >= 0)
        o_c = jnp.einsum('bghtn,bgnd->bghtd', p_c.astype(vc.dtype), vc)

        imp = jnp.einsum('bghtn,nj->bgtj', p_c, overlap)
        cur = t // SLC_BLOCK
        forced = (blk[None, :] == 0) | (blk[None, :] == cur[:, None]) | (blk[None, :] == cur[:, None] - 1)
        valid = blk[None, :] * SLC_BLOCK <= t[:, None]
        score = jnp.where(valid, imp + FORCE_BONUS * forced.astype(jnp.float32), NEG_BIG)
        _, idx = lax.top_k(score, topn)
        k_sel = ks[b_ix, g_ix, idx].reshape(B, G, QB, topn * SLC_BLOCK, dk)
        v_sel = vs[b_ix, g_ix, idx].reshape(B, G, QB, topn * SLC_BLOCK, dk)
        pos = (idx[..., None] * SLC_BLOCK + jnp.arange(SLC_BLOCK)).reshape(B, G, QB, topn * SLC_BLOCK)
        rel_s = t[:, None] - pos
        bias_s = jnp.moveaxis(tbl_g[g_ix, t5_bucket(rel_s)], -1, 2)
        s_s = jnp.einsum('bghtd,bgtkd->bghtk', qb, k_sel) * scale + bias_s
        p_s = masked_softmax(s_s, (rel_s >= 0)[:, :, None])
        o_s = jnp.einsum('bghtk,bgtkd->bghtd', p_s.astype(v_sel.dtype), v_sel)

        kwb = lax.dynamic_slice_in_dim(kw, t0, WINDOW + QB, axis=2)
        vwb = lax.dynamic_slice_in_dim(vw, t0, WINDOW + QB, axis=2)
        pos_w = t0 - WINDOW + jnp.arange(WINDOW + QB)
        rel_w = t[:, None] - pos_w[None, :]
        mask_w = (rel_w >= 0) & (rel_w < WINDOW) & (pos_w[None, :] >= 0)
        bias_w = tbl[t5_bucket(rel_w)].transpose(2, 3, 0, 1)
        s_w = jnp.einsum('bghtd,bgkd->bghtk', qb, kwb) * scale + bias_w
        p_w = masked_softmax(s_w, mask_w)
        o_w = jnp.einsum('bghtk,bgkd->bghtd', p_w.astype(vwb.dtype), vwb)

        return gb[..., 0:1] * o_c + gb[..., 1:2] * o_s + gb[..., 2:3] * o_w

    o = lax.map(block, jnp.arange(S // QB))
    return o.transpose(1, 0, 4, 2, 3, 5).reshape(B, S, D_NSA)


def stick_breaking_mixer(q, k, v):
    B, S = q.shape[:2]
    H, dk, QB = SB_HEADS, HEAD_DIM, SB_QBLOCK
    scale = 1.0 / math.sqrt(dk)
    q = q.reshape(B, S, H, dk).transpose(0, 2, 1, 3)
    k = k.reshape(B, S, H, dk).transpose(0, 2, 1, 3)
    v = v.reshape(B, S, H, dk).transpose(0, 2, 1, 3)
    s_idx = jnp.arange(S)

    def block(i):
        t0 = i * QB
        t = t0 + jnp.arange(QB)
        qb = lax.dynamic_slice_in_dim(q, t0, QB, axis=2)
        z = jnp.einsum('bhtd,bhsd->bhts', qb, k).astype(jnp.float32) * scale
        mask = s_idx[None, :] < t[:, None]
        log_rest = jnp.where(mask, jax.nn.log_sigmoid(-z), 0.0)
        rcs = lax.cumsum(log_rest, axis=3, reverse=True)
        between = jnp.pad(rcs[..., 1:], ((0, 0), (0, 0), (0, 0), (0, 1)))
        log_a = jnp.where(mask, jax.nn.log_sigmoid(z) + between, 0.0)
        a = jnp.where(mask, jnp.exp(log_a), 0.0)
        return jnp.einsum('bhts,bhsd->bhtd', a.astype(v.dtype), v)

    o = lax.map(block, jnp.arange(S // QB))
    return o.transpose(1, 0, 3, 2, 4).reshape(B, S, D_SB)


def hgrn2_mixer(q, f_logit, i_val, lb):
    B, S = q.shape[:2]
    H, dk, dv, C = HG_HEADS, HG_KDIM, HG_VDIM, HG_CHUNK
    n_c = S // C
    f32 = f_logit.astype(jnp.float32)
    log_f = jnp.logaddexp(jnp.log(jnp.maximum(lb, LB_FLOOR)), jnp.log1p(-lb) + jax.nn.log_sigmoid(f32))
    key = (1.0 - lb) * jax.nn.sigmoid(-f32)

    def heads(a, d):
        return a.astype(jnp.float32).reshape(B, n_c, C, H, d).transpose(1, 0, 3, 2, 4)

    qc, kc, lfc, vc = heads(q, dk), heads(key, dk), heads(log_f, dk), heads(i_val, dv)
    causal = jnp.asarray(np.tril(np.ones((C, C), dtype=bool)))[None, None, :, :, None]

    def step(state, inp):
        qx, kx, lf, vx = inp
        b = jnp.cumsum(lf, axis=2)
        diff = b[:, :, :, None, :] - b[:, :, None, :, :]
        decay = jnp.where(causal, jnp.exp(jnp.where(causal, diff, 0.0)), 0.0)
        att = jnp.einsum('bhtd,bhsd,bhtsd->bhts', qx, kx, decay)
        o = att @ vx + jnp.einsum('bhtd,bhde->bhte', qx * jnp.exp(b), state)
        b_last = b[:, :, -1:, :]
        state = (jnp.exp(b_last[:, :, 0, :])[..., None] * state
                 + jnp.einsum('bhsd,bhse->bhde', kx * jnp.exp(b_last - b), vx))
        return state, o

    s0 = jnp.zeros((B, H, dk, dv), jnp.float32)
    _, o = lax.scan(step, s0, (qc, kc, lfc, vc))
    return o.transpose(1, 0, 3, 2, 4).reshape(B, S, H, dv)


def setup_inputs(seed: int = 0) -> dict:
    key = jax.random.key(seed)
    ks = jax.random.split(key, 13)

    def nrm(k, shape, scale):
        return jax.random.normal(k, shape, jnp.float32) * scale

    x = nrm(ks[0], (BATCH, SEQ, D_MODEL), 1.0)
    w_in = nrm(ks[1], (DEPTH, D_MODEL, D_IN), D_MODEL ** -0.5)
    cmp_pos = nrm(ks[2], (DEPTH, CMP_BLOCK, HEAD_DIM), 0.1)
    w_ck1 = nrm(ks[3], (DEPTH, CMP_BLOCK * HEAD_DIM, CMP_HIDDEN), (CMP_BLOCK * HEAD_DIM) ** -0.5)
    w_ck2 = nrm(ks[4], (DEPTH, CMP_HIDDEN, HEAD_DIM), CMP_HIDDEN ** -0.5)
    w_cv1 = nrm(ks[5], (DEPTH, CMP_BLOCK * HEAD_DIM, CMP_HIDDEN), (CMP_BLOCK * HEAD_DIM) ** -0.5)
    w_cv2 = nrm(ks[6], (DEPTH, CMP_HIDDEN, HEAD_DIM), CMP_HIDDEN ** -0.5)
    hg_lb = nrm(ks[7], (DEPTH, HG_HEADS * HG_KDIM), 0.5)
    hg_norm_w = 1.0 + nrm(ks[8], (DEPTH, D_HG), 0.05)
    w_out = nrm(ks[9], (DEPTH, D_MIX, D_MODEL), D_MIX ** -0.5 * OUT_INIT_SCALE)
    ln_g = 1.0 + nrm(ks[10], (DEPTH, D_MODEL), 0.05)
    ln_b = nrm(ks[11], (DEPTH, D_MODEL), 0.02)
    rel_bias = nrm(ks[12], (NUM_BUCKETS, NSA_HEADS), 0.5)
    return {'x': x, 'w_in': w_in, 'cmp_pos': cmp_pos, 'w_ck1': w_ck1, 'w_ck2': w_ck2,
            'w_cv1': w_cv1, 'w_cv2': w_cv2, 'hg_lb': hg_lb, 'hg_norm_w': hg_norm_w,
            'w_out': w_out, 'ln_g': ln_g, 'ln_b': ln_b, 'rel_bias': rel_bias}


def reference(x, w_in, cmp_pos, w_ck1, w_ck2, w_cv1, w_cv2, hg_lb, hg_norm_w,
              w_out, ln_g, ln_b, rel_bias):
    lb_w = jax.nn.softmax(hg_lb.astype(jnp.float32), axis=0)
    lb_all = jnp.cumsum(lb_w, axis=0) - lb_w[0]
    offsets = np.cumsum(SPLIT_SIZES)[:-1].tolist()
    for l in range(DEPTH):
        h = x @ w_in[l]
        (nsa_q, nsa_kc, nsa_vc, nsa_ks, nsa_vs, nsa_kw, nsa_vw, nsa_g, nsa_z,
         sb_q, sb_k, sb_v, sb_z, hg_q, hg_f, hg_i, hg_z) = jnp.split(h, offsets, axis=-1)
        o_nsa = nsa_mixer(nsa_q, nsa_kc, nsa_vc, nsa_ks, nsa_vs, nsa_kw, nsa_vw, nsa_g,
                          cmp_pos[l], w_ck1[l], w_ck2[l], w_cv1[l], w_cv2[l], rel_bias)
        o_sb = stick_breaking_mixer(sb_q, sb_k, sb_v)
        o_hg = hgrn2_mixer(hg_q, hg_f, hg_i, lb_all[l])
        o_hg = o_hg * lax.rsqrt(jnp.mean(jnp.square(o_hg), axis=-1, keepdims=True) + RMS_EPS)
        o_hg = (o_hg * hg_norm_w[l].reshape(HG_HEADS, HG_VDIM)).reshape(o_hg.shape[0], o_hg.shape[1], D_HG)
        mixed = jnp.concatenate([o_nsa * jax.nn.silu(nsa_z),
                                 o_sb * jax.nn.silu(sb_z),
                                 o_hg.astype(x.dtype) * jax.nn.silu(hg_z)], axis=-1)
        x = layer_norm(ALPHA * x + (mixed @ w_out[l]).astype(x.dtype), ln_g[l], ln_b[l])
    return x
```

```python
import functools
import math

import numpy as np
import jax
import jax.numpy as jnp
from jax import lax
from jax.experimental import pallas as pl
from jax.experimental.pallas import tpu as pltpu

F32 = jnp.float32
BF16 = jnp.bfloat16

D_MODEL = 1024
DEPTH = 2
HEAD_DIM = 64
LANES = 128
NSA_HEADS = 6
NSA_KV_GROUPS = 2
NSA_HPG = NSA_HEADS // NSA_KV_GROUPS
CMP_BLOCK = 32
CMP_STRIDE = 16
CMP_HIDDEN = 2 * HEAD_DIM
SLC_BLOCK = 64
SLC_TOPN = 16
MAX_SLC = 64
WINDOW = 512
FORCE_BONUS = 1000.0
NEG_BIG = -1e30
LB_FLOOR = 1e-30
SB_HEADS = 4
HG_HEADS = 6
HG_SUB = 16
NUM_BUCKETS = 32
MAX_DISTANCE = 128
D_NSA = NSA_HEADS * HEAD_DIM
D_KV = NSA_KV_GROUPS * HEAD_DIM
D_SB = SB_HEADS * HEAD_DIM
D_HG = HG_HEADS * HEAD_DIM
D_MIX = D_NSA + D_SB + D_HG
N_GATES = NSA_HEADS * 3
SPLIT_SIZES = (D_NSA, D_KV, D_KV, D_KV, D_KV, D_KV, D_KV, N_GATES, D_NSA,
               D_SB, D_SB, D_SB, D_SB, D_HG, D_HG, D_HG, D_HG)
ALPHA = (2 * DEPTH) ** 0.25
LN_EPS = 1e-5
RMS_EPS = 1e-6
QK_SCALE = 1.0 / math.sqrt(HEAD_DIM)

T_NSA = 128
T_SB = 256
T_HG = 1024
T_PROJ = 256
VMEM_LIMIT = 56 * 1024 * 1024

_NT = (((1,), (1,)), ((), ()))
_TN = (((0,), (0,)), ((), ()))


def _dot(a, b):
    return jnp.dot(a, b, preferred_element_type=F32)


def _dot_nt(a, b):
    return lax.dot_general(a, b, _NT, preferred_element_type=F32)


def _dot_tn(a, b):
    return lax.dot_general(a, b, _TN, preferred_element_type=F32)


def _softplus(x):
    return jnp.maximum(x, 0.0) + jnp.log1p(jnp.exp(-jnp.abs(x)))


def _split_bf16(x):
    hi = x.astype(BF16)
    lo = (x - hi.astype(F32)).astype(BF16)
    return hi, lo


def _params(*sem):
    return pltpu.CompilerParams(dimension_semantics=sem, vmem_limit_bytes=VMEM_LIMIT)


def _inproj_kernel(x_ref, *refs, n_out):
    xb = x_ref[...].astype(BF16)
    for w_ref, o_ref in zip(refs[:n_out], refs[n_out:]):
        o_ref[...] = _dot(xb, w_ref[...]).astype(o_ref.dtype)


def _inproj(x2d, weights, out_dtypes):
    m = x2d.shape[0]
    n_out = len(weights)
    in_specs = [pl.BlockSpec((T_PROJ, D_MODEL), lambda i: (i, 0))]
    in_specs += [pl.BlockSpec(w.shape, lambda i: (0, 0)) for w in weights]
    out_specs = [pl.BlockSpec((T_PROJ, w.shape[1]), lambda i: (i, 0)) for w in weights]
    out_shape = [jax.ShapeDtypeStruct((m, w.shape[1]), dt) for w, dt in zip(weights, out_dtypes)]
    return pl.pallas_call(
        functools.partial(_inproj_kernel, n_out=n_out),
        grid=(m // T_PROJ,),
        in_specs=in_specs, out_specs=out_specs, out_shape=out_shape,
        compiler_params=_params("parallel"), name="inproj",
    )(x2d, *weights)


def _compress_kernel(ksrc_ref, vsrc_ref, pos_ref, w1k_ref, w2k_ref, w1v_ref, w2v_ref,
                     kc_ref, vc_ref, *, n_chunk):
    def run(src_ref, w1_ref, w2_ref, out_ref):
        top = jnp.zeros((n_chunk, 2 * CMP_HIDDEN), F32)
        bot = jnp.zeros((n_chunk, 2 * CMP_HIDDEN), F32)
        for p in range(CMP_STRIDE):
            xp = src_ref[0, pl.ds(p, n_chunk, stride=CMP_STRIDE), :]
            top += _dot((xp + pos_ref[p:p + 1, :]).astype(BF16), w1_ref[p])
            q = CMP_STRIDE + p
            bot += _dot((xp + pos_ref[q:q + 1, :]).astype(BF16), w1_ref[q])
        hid = top + pltpu.roll(bot, n_chunk - 1, 0)
        act = jax.nn.gelu(hid)
        out_ref[0] = _dot(act.astype(BF16), w2_ref[...]).astype(out_ref.dtype)

    run(ksrc_ref, w1k_ref, w2k_ref, kc_ref)
    run(vsrc_ref, w1v_ref, w2v_ref, vc_ref)


def _compress(kc_src, vc_src, pos2, w1k, w2k, w1v, w2v):
    b, s, _ = kc_src.shape
    n_chunk = s // CMP_STRIDE
    full = lambda a: pl.BlockSpec(a.shape, lambda i: (0,) * a.ndim)
    src = pl.BlockSpec((1, s, D_KV), lambda i: (i, 0, 0))
    out = pl.BlockSpec((1, n_chunk, D_KV), lambda i: (i, 0, 0))
    return pl.pallas_call(
        functools.partial(_compress_kernel, n_chunk=n_chunk),
        grid=(b,),
        in_specs=[src, src, full(pos2), full(w1k), full(w2k), full(w1v), full(w2v)],
        out_specs=[out, out],
        out_shape=[jax.ShapeDtypeStruct((b, n_chunk, D_KV), BF16)] * 2,
        compiler_params=_params("parallel"), name="nsa_compress",
    )(kc_src, vc_src, pos2, w1k, w2k, w1v, w2v)


def _online_init(m_ref, l_ref, acc_ref):
    m_ref[...] = jnp.full(m_ref.shape, NEG_BIG, F32)
    l_ref[...] = jnp.zeros(l_ref.shape, F32)
    acc_ref[...] = jnp.zeros(acc_ref.shape, F32)


def _online_update(m_ref, l_ref, acc_ref, s, v):
    m = m_ref[...]
    m_new = jnp.maximum(m, jnp.max(s, axis=-1, keepdims=True))
    alpha = jnp.exp(m - m_new)
    p = jnp.exp(s - m_new)
    m_ref[...] = m_new
    l_ref[...] = alpha * l_ref[...] + jnp.sum(p, axis=-1, keepdims=True)
    acc_ref[...] = alpha * acc_ref[...] + _dot(p.astype(BF16), v)


def _nsa_select_kernel(qa_ref, kc_ref, vc_ref, biasc_ref, ovl_ref, ident_ref, ocmp_ref, sel_ref, *, n_cmp_pad):
    T = T_NSA
    R = NSA_HEADS * T
    t0 = pl.program_id(1) * T
    q = jnp.concatenate([qa_ref[0, :, LANES * h:LANES * (h + 1)] for h in range(NSA_HEADS)], axis=0)

    row = lax.broadcasted_iota(jnp.int32, (R, n_cmp_pad), 0) & (T - 1)
    col = lax.broadcasted_iota(jnp.int32, (R, n_cmp_pad), 1)
    mask_c = (t0 + row) >= CMP_STRIDE * col + (CMP_BLOCK - 1)
    s_c = _dot_nt(q, kc_ref[0]) + biasc_ref[...].reshape(R, n_cmp_pad)
    s_c = jnp.where(mask_c, s_c, NEG_BIG)
    m_c = jnp.max(s_c, axis=-1, keepdims=True)
    p_c = jnp.where(mask_c, jnp.exp(s_c - m_c), 0.0)
    l_c = jnp.sum(p_c, axis=-1, keepdims=True)
    p_c = p_c / jnp.where(l_c > 0.0, l_c, 1.0)
    o_cmp = _dot(p_c.astype(BF16), vc_ref[0])
    for h in range(NSA_HEADS):
        ocmp_ref[0, :, LANES * h:LANES * (h + 1)] = o_cmp[h * T:(h + 1) * T]

    jblk = lax.broadcasted_iota(jnp.int32, (MAX_SLC, T), 0)
    tok = t0 + lax.broadcasted_iota(jnp.int32, (MAX_SLC, T), 1)
    cur = lax.shift_right_logical(tok, 6)
    forced = (jblk == 0) | (jblk == cur) | (jblk == cur - 1)
    valid = jblk * SLC_BLOCK <= tok
    jsub = lax.broadcasted_iota(jnp.int32, (8, T), 0)
    for g in range(NSA_KV_GROUPS):
        psum = p_c[(3 * g) * T:(3 * g + 1) * T]
        for h in range(1, NSA_HPG):
            psum = psum + p_c[(3 * g + h) * T:(3 * g + h + 1) * T]
        imp = _dot_nt(ovl_ref[...], psum.astype(BF16))
        score = jnp.where(valid, imp + jnp.where(forced, FORCE_BONUS, 0.0), NEG_BIG)
        blocks = [score[8 * rb:8 * rb + 8] for rb in range(MAX_SLC // 8)]
        ranks = [jnp.zeros((8, T), F32) for _ in blocks]
        for jp in range(MAX_SLC):
            other = score[jp:jp + 1, :]
            for rb, blk in enumerate(blocks):
                ge = jnp.where(other >= blk, 1.0, 0.0)
                gt = jnp.where(other > blk, 1.0, 0.0)
                if 8 * rb > jp:
                    inc = ge
                elif 8 * rb + 7 < jp:
                    inc = gt
                else:
                    inc = jnp.where(jsub > jp - 8 * rb, ge, gt)
                ranks[rb] = ranks[rb] + inc
        rank = jnp.concatenate(ranks, axis=0)
        selneg = jnp.where(rank < float(SLC_TOPN), 0.0, NEG_BIG).astype(BF16)
        selneg = jnp.concatenate([selneg, jnp.zeros((LANES - MAX_SLC, T), BF16)], axis=0)
        sel_ref[0, :, LANES * g:LANES * (g + 1)] = _dot_nt(ident_ref[...], selneg).astype(BF16)


def _nsa_select(qa, kc, vc, bias_c, ovl_t, ident):
    b, s, _ = qa.shape
    T = T_NSA
    n_cmp_pad = kc.shape[1]
    tile = lambda w: pl.BlockSpec((1, T, w), lambda i, j: (i, j, 0))
    per_b = lambda a: pl.BlockSpec((1,) + a.shape[1:], lambda i, j: (i,) + (0,) * (a.ndim - 1))
    full = lambda a: pl.BlockSpec(a.shape, lambda i, j: (0,) * a.ndim)
    return pl.pallas_call(
        functools.partial(_nsa_select_kernel, n_cmp_pad=n_cmp_pad),
        grid=(b, s // T),
        in_specs=[tile(NSA_HEADS * LANES), per_b(kc), per_b(vc),
                  pl.BlockSpec((NSA_HEADS, T, n_cmp_pad), lambda i, j: (0, j, 0)), full(ovl_t), full(ident)],
        out_specs=[tile(NSA_HEADS * LANES), tile(NSA_KV_GROUPS * LANES)],
        out_shape=[jax.ShapeDtypeStruct((b, s, NSA_HEADS * LANES), F32),
                   jax.ShapeDtypeStruct((b, s, NSA_KV_GROUPS * LANES), BF16)],
        compiler_params=_params("parallel", "parallel"), name="nsa_select",
    )(qa, kc, vc, bias_c, ovl_t, ident)


def _nsa_attend_kernel(qa_ref, sel_ref, ocmp_ref, ks_ref, vs_ref, kw_ref, vw_ref, g_ref, biasn_ref, onehot_ref,
                       o_ref, kaug_ref, qaug_ref, ms_ref, ls_ref, accs_ref, mw_ref, lw_ref, accw_ref):
    T = T_NSA
    R = NSA_HEADS * T
    qt = pl.program_id(1)

    @pl.when(qt == 0)
    def _():
        kaug_ref[:, 0:LANES] = ks_ref[0]
        kaug_ref[:, LANES:2 * LANES] = onehot_ref[...]

    for h in range(NSA_HEADS):
        g = h // NSA_HPG
        qaug_ref[h * T:(h + 1) * T, 0:LANES] = qa_ref[0, :, LANES * h:LANES * (h + 1)]
        qaug_ref[h * T:(h + 1) * T, LANES:2 * LANES] = sel_ref[0, :, LANES * g:LANES * (g + 1)]

    rr = lax.broadcasted_iota(jnp.int32, (R, T), 0) & (T - 1)
    cc = lax.broadcasted_iota(jnp.int32, (R, T), 1)
    causal = cc <= rr
    st_s = (ms_ref, ls_ref, accs_ref)
    st_w = (mw_ref, lw_ref, accw_ref)
    _online_init(*st_s)
    _online_init(*st_w)

    def far_body(kt, carry):
        r0 = pl.multiple_of(kt * T, T)
        s = _dot_nt(qaug_ref[...], kaug_ref[pl.ds(r0, T), :])
        _online_update(*st_s, s, vs_ref[0, pl.ds(r0, T), :])
        return carry

    lax.fori_loop(0, jnp.maximum(qt - 1, 0), far_body, 0)
    for d in (1, 0):
        kt = qt - d
        r0 = pl.multiple_of(jnp.maximum(kt, 0) * T, T)
        s = _dot_nt(qaug_ref[...], kaug_ref[pl.ds(r0, T), :]) + biasn_ref[d]
        ok = causal if d == 0 else cc < jnp.where(kt >= 0, T, -1)
        s = jnp.where(ok, s, NEG_BIG)
        _online_update(*st_s, s, vs_ref[0, pl.ds(r0, T), :])

    n_win = WINDOW // T
    for d in range(n_win + 1):
        kt = qt - d
        r0 = pl.multiple_of(jnp.maximum(kt, 0) * T, T)
        s = _dot_nt(qaug_ref[:, 0:LANES], kw_ref[0, pl.ds(r0, T), :])
        if d <= 1:
            s = s + biasn_ref[d]
        in_range = cc < jnp.where(kt >= 0, T, -1)
        if d == 0:
            ok = causal
        elif d == n_win:
            ok = (cc > rr) & in_range
        else:
            ok = in_range
        s = jnp.where(ok, s, NEG_BIG)
        _online_update(*st_w, s, vw_ref[0, pl.ds(r0, T), :])

    o_sel = accs_ref[...] / ls_ref[...]
    o_win = accw_ref[...] / lw_ref[...]
    gates = jax.nn.sigmoid(g_ref[0])
    lane = lax.broadcasted_iota(jnp.int32, (T, LANES), 1)
    heads = []
    for h in range(NSA_HEADS):
        sl = slice(h * T, (h + 1) * T)
        heads.append(gates[:, 3 * h:3 * h + 1] * ocmp_ref[0, :, LANES * h:LANES * (h + 1)]
                     + gates[:, 3 * h + 1:3 * h + 2] * o_sel[sl]
                     + gates[:, 3 * h + 2:3 * h + 3] * o_win[sl])
    lo = lane < HEAD_DIM
    o_ref[0, :, 0:LANES] = jnp.where(lo, heads[0], pltpu.roll(heads[1], HEAD_DIM, 1))
    o_ref[0, :, LANES:2 * LANES] = jnp.where(lo, heads[2], heads[3])
    o_ref[0, :, 2 * LANES:3 * LANES] = jnp.where(lo, pltpu.roll(heads[4], HEAD_DIM, 1), heads[5])


def _nsa_attend(qa, sel, o_cmp, ks, vs, kw, vw, gl, bias_n, onehot):
    b, s, _ = qa.shape
    T = T_NSA
    tile = lambda w: pl.BlockSpec((1, T, w), lambda i, j: (i, j, 0))
    per_b = lambda a: pl.BlockSpec((1,) + a.shape[1:], lambda i, j: (i,) + (0,) * (a.ndim - 1))
    full = lambda a: pl.BlockSpec(a.shape, lambda i, j: (0,) * a.ndim)
    return pl.pallas_call(
        _nsa_attend_kernel,
        grid=(b, s // T),
        in_specs=[tile(NSA_HEADS * LANES), tile(NSA_KV_GROUPS * LANES), tile(NSA_HEADS * LANES),
                  per_b(ks), per_b(vs), per_b(kw), per_b(vw), tile(LANES), full(bias_n), full(onehot)],
        out_specs=tile(D_NSA),
        out_shape=jax.ShapeDtypeStruct((b, s, D_NSA), F32),
        scratch_shapes=[pltpu.VMEM((s, 2 * LANES), BF16), pltpu.VMEM((NSA_HEADS * T, 2 * LANES), BF16)]
        + [pltpu.VMEM((NSA_HEADS * T, 1), F32), pltpu.VMEM((NSA_HEADS * T, 1), F32),
           pltpu.VMEM((NSA_HEADS * T, LANES), F32)] * 2,
        compiler_params=_params("parallel", "arbitrary"), name="nsa_attend",
    )(qa, sel, o_cmp, ks, vs, kw, vw, gl, bias_n, onehot)


def _sb_kernel(qa_ref, k_ref, v_ref, tri_ref, o_ref):
    T = T_SB
    qt = pl.program_id(1)
    n_pair = SB_HEADS // 2
    qs = [jnp.concatenate([qa_ref[0, :, LANES * (2 * p):LANES * (2 * p + 1)],
                           qa_ref[0, :, LANES * (2 * p + 1):LANES * (2 * p + 2)]], axis=0)
          for p in range(n_pair)]
    rr = lax.broadcasted_iota(jnp.int32, (2 * T, T), 0) & (T - 1)
    cc = lax.broadcasted_iota(jnp.int32, (2 * T, T), 1)
    strict = cc < rr

    def tile(kt, state, mask):
        r0 = pl.multiple_of(kt * T, T)
        out = []
        for p in range(n_pair):
            carry, acc = state[p]
            z = _dot_nt(qs[p], k_ref[0, pl.ds(r0, T), LANES * p:LANES * (p + 1)])
            log_rest = -_softplus(z)
            lr = log_rest if mask is None else jnp.where(mask, log_rest, 0.0)
            hi, lo = _split_bf16(lr)
            incl = _dot(hi, tri_ref[...]) + _dot(lo, tri_ref[...])
            log_a = (z + log_rest) + (incl - lr) + carry
            a = jnp.exp(log_a)
            if mask is not None:
                a = jnp.where(mask, a, 0.0)
            acc = acc + _dot(a.astype(BF16), v_ref[0, pl.ds(r0, T), LANES * p:LANES * (p + 1)])
            out.append((carry + incl[:, 0:1], acc))
        return tuple(out)

    init = tuple((jnp.zeros((2 * T, 1), F32), jnp.zeros((2 * T, LANES), F32)) for _ in range(n_pair))
    state = tile(qt, init, strict)
    state = lax.fori_loop(0, qt, lambda i, st: tile(qt - 1 - i, st, None), state)
    lane = lax.broadcasted_iota(jnp.int32, (T, LANES), 1)
    for p in range(n_pair):
        acc = state[p][1]
        o_ref[0, :, LANES * p:LANES * (p + 1)] = jnp.where(lane < HEAD_DIM, acc[0:T], acc[T:2 * T])


def _sb_attention(qa, k, v, tri):
    b, s, _ = qa.shape
    T = T_SB
    per_b = lambda a: pl.BlockSpec((1,) + a.shape[1:], lambda i, j: (i, 0, 0))
    return pl.pallas_call(
        _sb_kernel,
        grid=(b, s // T),
        in_specs=[pl.BlockSpec((1, T, SB_HEADS * LANES), lambda i, j: (i, j, 0)), per_b(k), per_b(v),
                  pl.BlockSpec(tri.shape, lambda i, j: (0, 0))],
        out_specs=pl.BlockSpec((1, T, D_SB), lambda i, j: (i, j, 0)),
        out_shape=jax.ShapeDtypeStruct((b, s, D_SB), F32),
        compiler_params=_params("parallel", "parallel"), name="sb_attention",
    )(qa, k, v, tri)


def _hgrn_kernel(q_ref, f_ref, i_ref, lb_ref, nw_ref, ones_ref, bd_ref, o_ref,
                 st_ref, qj_ref, kj_ref, bj_ref, vj_ref, qd_ref, kd_ref, oc_ref):
    C = HG_SUB
    n_blk = T_HG // C

    @pl.when(pl.program_id(2) == 0)
    def _():
        st_ref[...] = jnp.zeros(st_ref.shape, F32)

    lb = lb_ref[...]
    c_floor = jnp.log(jnp.maximum(lb, LB_FLOOR))
    c_rest = jnp.log1p(-lb)
    one_m_lb = 1.0 - lb
    b_run = jnp.zeros((n_blk, LANES), F32)
    for j in range(C):
        fj = f_ref[0, pl.ds(j, n_blk, stride=C), :]
        log_sig = -_softplus(-fj)
        a, bb = c_floor, c_rest + log_sig
        log_f = jnp.maximum(a, bb) + jnp.log1p(jnp.exp(-jnp.abs(a - bb)))
        b_run = b_run + log_f
        bj_ref[j] = b_run
        kj_ref[j] = one_m_lb * jax.nn.sigmoid(-fj)
        qj_ref[j] = q_ref[0, pl.ds(j, n_blk, stride=C), :]
        vj_ref[j] = i_ref[0, pl.ds(j, n_blk, stride=C), :]
    b_last = bj_ref[C - 1]
    for j in range(C):
        bj = bj_ref[j]
        qd_ref[pl.ds(j, n_blk, stride=C), :] = qj_ref[j] * jnp.exp(bj)
        kd_ref[pl.ds(j, n_blk, stride=C), :] = kj_ref[j] * jnp.exp(b_last - bj)

    def step(blk, _):
        r0 = pl.multiple_of(blk * C, C)
        qd = qd_ref[pl.ds(r0, C), :].astype(BF16)
        kd = kd_ref[pl.ds(r0, C), :].astype(BF16)
        vv = i_ref[0, pl.ds(r0, C), :].astype(BF16)
        dec = jnp.exp(bj_ref[C - 1, pl.ds(blk, 1), :])
        st = st_ref[...]
        oc_ref[pl.ds(r0, C), :] = _dot_nt(qd, st.astype(BF16))
        st_ref[...] = st * dec + bd_ref[...] * _dot_tn(vv, kd)
        return 0

    lax.fori_loop(0, n_blk, step, 0)

    nw = nw_ref[...]
    for j in range(C):
        oj = oc_ref[pl.ds(j, n_blk, stride=C), :]
        qj = qj_ref[j]
        bj = bj_ref[j]
        for jp in range(j + 1):
            x = qj * kj_ref[jp] * jnp.exp(bj - bj_ref[jp])
            att = _dot(x.astype(BF16), ones_ref[...])
            oj = oj + att * vj_ref[jp]
        hi, lo = _split_bf16(oj * oj)
        ms = (_dot(hi, ones_ref[...]) + _dot(lo, ones_ref[...])) * (1.0 / HEAD_DIM)
        o_ref[0, pl.ds(j, n_blk, stride=C), :] = oj * lax.rsqrt(ms + RMS_EPS) * nw


def _hgrn(q, f, i, lb, nw, ones_bd, bd_mask):
    b, s, _ = q.shape
    n_blk = T_HG // HG_SUB
    tile = pl.BlockSpec((1, T_HG, LANES), lambda bi, pi, ti: (bi, ti, pi))
    vec = pl.BlockSpec((1, LANES), lambda bi, pi, ti: (0, pi))
    full = lambda a: pl.BlockSpec(a.shape, lambda bi, pi, ti: (0,) * a.ndim)
    jm = pltpu.VMEM((HG_SUB, n_blk, LANES), F32)
    nat = pltpu.VMEM((T_HG, LANES), F32)
    return pl.pallas_call(
        _hgrn_kernel,
        grid=(b, D_HG // LANES, s // T_HG),
        in_specs=[tile, tile, tile, vec, vec, full(ones_bd), full(bd_mask)],
        out_specs=tile,
        out_shape=jax.ShapeDtypeStruct((b, s, D_HG), F32),
        scratch_shapes=[pltpu.VMEM((LANES, LANES), F32), jm, jm, jm, jm, nat, nat, nat],
        compiler_params=_params("parallel", "parallel", "arbitrary"), name="hgrn2",
    )(q, f, i, lb, nw, ones_bd, bd_mask)


def _out_kernel(onsa_ref, osb_ref, ohg_ref, z_ref, x_ref, wn_ref, ws_ref, wh_ref, g_ref, b_ref, o_ref):
    z = z_ref[...]
    sz = z * jax.nn.sigmoid(z)
    y = _dot((onsa_ref[...] * sz[:, 0:D_NSA]).astype(BF16), wn_ref[...])
    y += _dot((osb_ref[...] * sz[:, D_NSA:D_NSA + D_SB]).astype(BF16), ws_ref[...])
    y += _dot((ohg_ref[...] * sz[:, D_NSA + D_SB:D_MIX]).astype(BF16), wh_ref[...])
    v = ALPHA * x_ref[...] + y
    mu = jnp.mean(v, axis=-1, keepdims=True)
    vc = v - mu
    var = jnp.mean(vc * vc, axis=-1, keepdims=True)
    o_ref[...] = vc * lax.rsqrt(var + LN_EPS) * g_ref[...] + b_ref[...]


def _out_proj(o_nsa, o_sb, o_hg, z_all, x2d, wn, ws, wh, g, bvec):
    m = x2d.shape[0]
    rows = lambda w: pl.BlockSpec((T_PROJ, w), lambda i: (i, 0))
    full = lambda a: pl.BlockSpec(a.shape, lambda i: (0, 0))
    return pl.pallas_call(
        _out_kernel,
        grid=(m // T_PROJ,),
        in_specs=[rows(D_NSA), rows(D_SB), rows(D_HG), rows(D_MIX), rows(D_MODEL),
                  full(wn), full(ws), full(wh), full(g), full(bvec)],
        out_specs=rows(D_MODEL),
        out_shape=jax.ShapeDtypeStruct((m, D_MODEL), F32),
        compiler_params=_params("parallel"), name="out_proj_norm",
    )(o_nsa, o_sb, o_hg, z_all, x2d, wn, ws, wh, g, bvec)


def _t5_bucket_np(rel):
    n = np.maximum(rel, 0)
    max_exact = NUM_BUCKETS // 2
    large = max_exact + (np.log(np.maximum(n, 1).astype(np.float32) / max_exact)
                         / math.log(MAX_DISTANCE / max_exact) * (NUM_BUCKETS - max_exact)).astype(np.int32)
    large = np.clip(large, 0, NUM_BUCKETS - 1)
    return np.where(n < max_exact, n, large).astype(np.int32)


def _bias_tables(rel_bias, s):
    T = T_NSA
    n_chunk = s // CMP_STRIDE
    t = np.arange(s)[:, None]
    cmp_end = np.arange(n_chunk)[None, :] * CMP_STRIDE + CMP_BLOCK - 1
    bias_c = jnp.transpose(rel_bias[_t5_bucket_np(t - cmp_end)], (2, 0, 1))
    r = np.arange(T)[:, None]
    c = np.arange(T)[None, :]
    near = np.stack([_t5_bucket_np(r - c), _t5_bucket_np(T + r - c)])
    far = rel_bias[NUM_BUCKETS - 1]
    bias_n = jnp.transpose(rel_bias[near] - far, (0, 3, 1, 2)).reshape(2, NSA_HEADS * T, T)
    return bias_c.astype(F32), bias_n.astype(F32)


def _static_tables(s):
    n_chunk = s // CMP_STRIDE
    cmp_start = np.arange(n_chunk) * CMP_STRIDE
    slc_start = np.arange(MAX_SLC) * SLC_BLOCK
    ovl_t = ((cmp_start[None, :] < slc_start[:, None] + SLC_BLOCK)
             & (cmp_start[None, :] + CMP_BLOCK > slc_start[:, None])
             & (cmp_start[None, :] + CMP_BLOCK <= s)).astype(np.float32)
    onehot = (np.arange(s)[:, None] // SLC_BLOCK == np.arange(LANES)[None, :]).astype(np.float32)
    ident = np.eye(T_NSA, dtype=np.float32)
    tri = (np.arange(T_SB)[:, None] >= np.arange(T_SB)[None, :]).astype(np.float32)
    ones_bd = np.kron(np.eye(2), np.ones((HEAD_DIM, HEAD_DIM))).astype(np.float32)
    as_bf16 = lambda a: jnp.asarray(a, dtype=BF16)
    return as_bf16(ovl_t), as_bf16(onehot), as_bf16(ident), as_bf16(tri), as_bf16(ones_bd), jnp.asarray(ones_bd)


def _layer_weights(w_in_l, cmp_pos_l, w_ck1_l, w_ck2_l, w_cv1_l, w_cv2_l):
    offs = np.cumsum((0,) + SPLIT_SIZES)
    (w_q, w_kc, w_vc, w_ks, w_vs, w_kw, w_vw, w_g, w_nz,
     w_sq, w_sk, w_sv, w_sz, w_hq, w_hf, w_hi, w_hz) = [w_in_l[:, offs[i]:offs[i + 1]] for i in range(len(SPLIT_SIZES))]
    zeros = jnp.zeros((D_MODEL, HEAD_DIM), F32)
    cols = []
    for h in range(NSA_HEADS):
        wq = w_q[:, HEAD_DIM * h:HEAD_DIM * (h + 1)] * QK_SCALE
        cols += [wq, zeros] if h // NSA_HPG == 0 else [zeros, wq]
    w_qa = jnp.concatenate(cols, axis=1)
    cols = []
    for h in range(SB_HEADS):
        wq = w_sq[:, HEAD_DIM * h:HEAD_DIM * (h + 1)] * QK_SCALE
        cols += [wq, zeros] if h % 2 == 0 else [zeros, wq]
    w_sqa = jnp.concatenate(cols, axis=1)
    w_gp = jnp.concatenate([w_g, jnp.zeros((D_MODEL, LANES - N_GATES), F32)], axis=1)
    w_z = jnp.concatenate([w_nz, w_sz, w_hz], axis=1)
    weights = [w_qa, w_kc, w_vc, w_ks, w_vs, w_kw, w_vw, w_gp, w_z, w_sqa, w_sk, w_sv, w_hq, w_hf, w_hi]
    dtypes = [BF16, F32, F32, BF16, BF16, BF16, BF16, F32, F32, BF16, BF16, BF16, F32, F32, F32]
    weights = [w.astype(BF16) for w in weights]

    def block_diag(w):
        z = jnp.zeros_like(w)
        return jnp.concatenate([jnp.concatenate([w, z], axis=-1), jnp.concatenate([z, w], axis=-1)], axis=-2)

    pos2 = jnp.concatenate([cmp_pos_l, cmp_pos_l], axis=1)
    w1k = block_diag(w_ck1_l.reshape(CMP_BLOCK, HEAD_DIM, CMP_HIDDEN)).astype(BF16)
    w1v = block_diag(w_cv1_l.reshape(CMP_BLOCK, HEAD_DIM, CMP_HIDDEN)).astype(BF16)
    w2k = block_diag(w_ck2_l).astype(BF16)
    w2v = block_diag(w_cv2_l).astype(BF16)
    return weights, dtypes, (pos2, w1k, w2k, w1v, w2v)


def kernel(x, w_in, cmp_pos, w_ck1, w_ck2, w_cv1, w_cv2, hg_lb, hg_norm_w, w_out, ln_g, ln_b, rel_bias):
    b, s, d = x.shape
    assert d == D_MODEL and s % T_HG == 0 and s // SLC_BLOCK <= MAX_SLC and s >= WINDOW + T_NSA
    lb_w = jax.nn.softmax(hg_lb.astype(F32), axis=0)
    lb_all = jnp.cumsum(lb_w, axis=0) - lb_w[0]
    bias_c, bias_n = _bias_tables(rel_bias, s)
    ovl_t, onehot, ident, tri, ones_bd, bd_mask = _static_tables(s)

    x2d = x.reshape(b * s, d)
    for l in range(DEPTH):
        weights, dtypes, cmp_w = _layer_weights(w_in[l], cmp_pos[l], w_ck1[l], w_ck2[l], w_cv1[l], w_cv2[l])
        outs = _inproj(x2d, weights, dtypes)
        (qa, kc_src, vc_src, ks, vs, kw, vw, gl, z_all, sqa, sk, sv, hq, hf, hi) = [
            o.reshape(b, s, o.shape[-1]) for o in outs]
        kc, vc = _compress(kc_src, vc_src, *cmp_w)
        o_cmp, sel = _nsa_select(qa, kc, vc, bias_c, ovl_t, ident)
        o_nsa = _nsa_attend(qa, sel, o_cmp, ks, vs, kw, vw, gl, bias_n, onehot)
        o_sb = _sb_attention(sqa, sk, sv, tri)
        o_hg = _hgrn(hq, hf, hi, lb_all[l][None, :], hg_norm_w[l][None, :], ones_bd, bd_mask)
        wo = w_out[l].astype(BF16)
        x2d = _out_proj(o_nsa.reshape(b * s, D_NSA), o_sb.reshape(b * s, D_SB), o_hg.reshape(b * s, D_HG),
                        z_all.reshape(b * s, D_MIX), x2d,
                        wo[0:D_NSA], wo[D_NSA:D_NSA + D_SB], wo[D_NSA + D_SB:D_MIX],
                        ln_g[l][None, :], ln_b[l][None, :])
    return x2d.reshape(b, s, d)
```

```python
import functools
import math

import numpy as np
import jax
import jax.numpy as jnp
from jax import lax
from jax.experimental import pallas as pl
from jax.experimental.pallas import tpu as pltpu

F32 = jnp.float32
BF16 = jnp.bfloat16

D_MODEL = 1024
DEPTH = 2
HEAD_DIM = 64
LANES = 128
NSA_HEADS = 6
NSA_KV_GROUPS = 2
NSA_HPG = NSA_HEADS // NSA_KV_GROUPS
CMP_BLOCK = 32
CMP_STRIDE = 16
CMP_HIDDEN = 2 * HEAD_DIM
SLC_BLOCK = 64
SLC_TOPN = 16
MAX_SLC = 64
WINDOW = 512
FORCE_BONUS = 1000.0
NEG_BIG = -1e30
LB_FLOOR = 1e-30
SB_HEADS = 4
HG_HEADS = 6
HG_SUB = 16
NUM_BUCKETS = 32
MAX_DISTANCE = 128
D_NSA = NSA_HEADS * HEAD_DIM
D_KV = NSA_KV_GROUPS * HEAD_DIM
D_SB = SB_HEADS * HEAD_DIM
D_HG = HG_HEADS * HEAD_DIM
D_MIX = D_NSA + D_SB + D_HG
N_GATES = NSA_HEADS * 3
SPLIT_SIZES = (D_NSA, D_KV, D_KV, D_KV, D_KV, D_KV, D_KV, N_GATES, D_NSA,
               D_SB, D_SB, D_SB, D_SB, D_HG, D_HG, D_HG, D_HG)
ALPHA = (2 * DEPTH) ** 0.25
LN_EPS = 1e-5
RMS_EPS = 1e-6
QK_SCALE = 1.0 / math.sqrt(HEAD_DIM)

T_SEL = 128
T_NSA = 256
T_SB = 256
T_HG = 1024
T_PROJ = 256
VMEM_LIMIT = 56 * 1024 * 1024

_NT = (((1,), (1,)), ((), ()))
_TN = (((0,), (0,)), ((), ()))


def _dot(a, b):
    return jnp.dot(a, b, preferred_element_type=F32)


def _dot_nt(a, b):
    return lax.dot_general(a, b, _NT, preferred_element_type=F32)


def _dot_tn(a, b):
    return lax.dot_general(a, b, _TN, preferred_element_type=F32)


def _softplus(x):
    return jnp.maximum(x, 0.0) + jnp.log1p(jnp.exp(-jnp.abs(x)))


def _split_bf16(x):
    hi = x.astype(BF16)
    lo = (x - hi.astype(F32)).astype(BF16)
    return hi, lo


def _params(*sem):
    return pltpu.CompilerParams(dimension_semantics=sem, vmem_limit_bytes=VMEM_LIMIT)


def _inproj_kernel(x_ref, *refs, n_nat, n_tr):
    xb = x_ref[...].astype(BF16)
    n_in = n_nat + n_tr
    for w_ref, o_ref in zip(refs[:n_nat], refs[n_in:n_in + n_nat]):
        o_ref[...] = _dot(xb, w_ref[...]).astype(o_ref.dtype)
    for w_ref, o_ref in zip(refs[n_nat:n_in], refs[n_in + n_nat:]):
        o_ref[...] = _dot_nt(w_ref[...], xb).astype(o_ref.dtype)


def _inproj(x2d, w_nat, dt_nat, w_tr, dt_tr):
    m = x2d.shape[0]
    in_specs = [pl.BlockSpec((T_PROJ, D_MODEL), lambda i: (i, 0))]
    in_specs += [pl.BlockSpec(w.shape, lambda i: (0, 0)) for w in w_nat + w_tr]
    out_specs = [pl.BlockSpec((T_PROJ, w.shape[1]), lambda i: (i, 0)) for w in w_nat]
    out_specs += [pl.BlockSpec((w.shape[0], T_PROJ), lambda i: (0, i)) for w in w_tr]
    out_shape = [jax.ShapeDtypeStruct((m, w.shape[1]), dt) for w, dt in zip(w_nat, dt_nat)]
    out_shape += [jax.ShapeDtypeStruct((w.shape[0], m), dt) for w, dt in zip(w_tr, dt_tr)]
    outs = pl.pallas_call(
        functools.partial(_inproj_kernel, n_nat=len(w_nat), n_tr=len(w_tr)),
        grid=(m // T_PROJ,),
        in_specs=in_specs, out_specs=out_specs, out_shape=out_shape,
        compiler_params=_params("parallel"), name="inproj",
    )(x2d, *w_nat, *w_tr)
    return outs[:len(w_nat)], outs[len(w_nat):]


def _compress_kernel(ksrc_ref, vsrc_ref, pos_ref, w1k_ref, w2k_ref, w1v_ref, w2v_ref,
                     kc_ref, vc_ref, *, n_chunk):
    def hidden(src_ref, w1_ref):
        top = jnp.zeros((n_chunk, 2 * CMP_HIDDEN), F32)
        bot = jnp.zeros((n_chunk, 2 * CMP_HIDDEN), F32)
        for p in range(CMP_STRIDE):
            xp = src_ref[0, pl.ds(p, n_chunk, stride=CMP_STRIDE), :]
            top += _dot((xp + pos_ref[p:p + 1, :]).astype(BF16), w1_ref[p])
            q = CMP_STRIDE + p
            bot += _dot((xp + pos_ref[q:q + 1, :]).astype(BF16), w1_ref[q])
        hid = top + pltpu.roll(bot, n_chunk - 1, 0)
        return jax.nn.gelu(hid).astype(BF16)

    kc_ref[0] = _dot(hidden(ksrc_ref, w1k_ref), w2k_ref[...]).astype(kc_ref.dtype)
    vc_ref[0] = _dot_nt(w2v_ref[...], hidden(vsrc_ref, w1v_ref)).astype(vc_ref.dtype)


def _compress(kc_src, vc_src, pos2, w1k, w2k, w1v, w2v_t):
    b, s, _ = kc_src.shape
    n_chunk = s // CMP_STRIDE
    full = lambda a: pl.BlockSpec(a.shape, lambda i: (0,) * a.ndim)
    src = pl.BlockSpec((1, s, D_KV), lambda i: (i, 0, 0))
    return pl.pallas_call(
        functools.partial(_compress_kernel, n_chunk=n_chunk),
        grid=(b,),
        in_specs=[src, src, full(pos2), full(w1k), full(w2k), full(w1v), full(w2v_t)],
        out_specs=[pl.BlockSpec((1, n_chunk, D_KV), lambda i: (i, 0, 0)),
                   pl.BlockSpec((1, D_KV, n_chunk), lambda i: (i, 0, 0))],
        out_shape=[jax.ShapeDtypeStruct((b, n_chunk, D_KV), BF16),
                   jax.ShapeDtypeStruct((b, D_KV, n_chunk), BF16)],
        compiler_params=_params("parallel"), name="nsa_compress",
    )(kc_src, vc_src, pos2, w1k, w2k, w1v, w2v_t)


def _nsa_select_kernel(qt_ref, kc_ref, vct_ref, pbias_ref, ovl_ref, ocmp_ref, sel_ref, *, n_cmp_pad):
    T = T_SEL
    j = pl.program_id(1)
    t0 = j * T
    n_idx = lax.broadcasted_iota(jnp.int32, (n_cmp_pad, T), 0)
    tok_c = t0 + lax.broadcasted_iota(jnp.int32, (n_cmp_pad, T), 1)
    mask_c = tok_c >= CMP_STRIDE * n_idx + (CMP_BLOCK - 1)
    off = pl.multiple_of(n_cmp_pad - (T // CMP_STRIDE) * j, 8)
    psums = [None] * NSA_KV_GROUPS
    for h in range(NSA_HEADS):
        qh = qt_ref[LANES * h:LANES * (h + 1), :]
        s = _dot(kc_ref[0], qh) + pbias_ref[h, pl.ds(off, n_cmp_pad), :]
        s = jnp.where(mask_c, s, NEG_BIG)
        m = jnp.max(s, axis=0, keepdims=True)
        p = jnp.where(mask_c, jnp.exp(s - m), 0.0)
        l = jnp.sum(p, axis=0, keepdims=True)
        p = p / jnp.where(l > 0.0, l, 1.0)
        ocmp_ref[LANES * h:LANES * (h + 1), :] = _dot(vct_ref[0], p.astype(BF16))
        g = h // NSA_HPG
        psums[g] = p if psums[g] is None else psums[g] + p

    jblk = lax.broadcasted_iota(jnp.int32, (MAX_SLC, T), 0)
    tok = t0 + lax.broadcasted_iota(jnp.int32, (MAX_SLC, T), 1)
    cur = lax.shift_right_logical(tok, 6)
    forced = (jblk == 0) | (jblk == cur) | (jblk == cur - 1)
    valid = jblk * SLC_BLOCK <= tok
    jsub = lax.broadcasted_iota(jnp.int32, (8, T), 0)
    for g in range(NSA_KV_GROUPS):
        imp = _dot(ovl_ref[...], psums[g].astype(BF16))
        score = jnp.where(valid, imp + jnp.where(forced, FORCE_BONUS, 0.0), NEG_BIG)
        blocks = [score[8 * rb:8 * rb + 8] for rb in range(MAX_SLC // 8)]
        ranks = [jnp.zeros((8, T), F32) for _ in blocks]
        for jp in range(MAX_SLC):
            other = score[jp:jp + 1, :]
            for rb, blk in enumerate(blocks):
                ge = jnp.where(other >= blk, 1.0, 0.0)
                gt = jnp.where(other > blk, 1.0, 0.0)
                if 8 * rb > jp:
                    inc = ge
                elif 8 * rb + 7 < jp:
                    inc = gt
                else:
                    inc = jnp.where(jsub > jp - 8 * rb, ge, gt)
                ranks[rb] = ranks[rb] + inc
        rank = jnp.concatenate(ranks, axis=0)
        sel_ref[0, MAX_SLC * g:MAX_SLC * (g + 1), :] = jnp.where(rank < float(SLC_TOPN), 0.0, NEG_BIG).astype(BF16)


def _nsa_select(q_t, kc, vc_t, pbias, ovl_t):
    b, n_cmp_pad, _ = kc.shape
    m = q_t.shape[1]
    nq = m // b // T_SEL
    cols = lambda r: pl.BlockSpec((r, T_SEL), lambda i, j: (0, i * nq + j))
    per_b = lambda a: pl.BlockSpec((1,) + a.shape[1:], lambda i, j: (i, 0, 0))
    full = lambda a: pl.BlockSpec(a.shape, lambda i, j: (0,) * a.ndim)
    return pl.pallas_call(
        functools.partial(_nsa_select_kernel, n_cmp_pad=n_cmp_pad),
        grid=(b, nq),
        in_specs=[cols(NSA_HEADS * LANES), per_b(kc), per_b(vc_t), full(pbias), full(ovl_t)],
        out_specs=[cols(NSA_HEADS * LANES), pl.BlockSpec((1, NSA_KV_GROUPS * MAX_SLC, T_SEL), lambda i, j: (i, 0, j))],
        out_shape=[jax.ShapeDtypeStruct((NSA_HEADS * LANES, m), F32),
                   jax.ShapeDtypeStruct((b, NSA_KV_GROUPS * MAX_SLC, m // b), BF16)],
        compiler_params=_params("parallel", "parallel"), name="nsa_select",
    )(q_t, kc, vc_t, pbias, ovl_t)


def _online_init(m_ref, l_ref, acc_ref):
    m_ref[...] = jnp.full(m_ref.shape, NEG_BIG, F32)
    l_ref[...] = jnp.zeros(l_ref.shape, F32)
    acc_ref[...] = jnp.zeros(acc_ref.shape, F32)


def _online_update(m_ref, l_ref, acc_ref, s, v_t):
    m = m_ref[...]
    m_new = jnp.maximum(m, jnp.max(s, axis=0, keepdims=True))
    alpha = jnp.exp(m - m_new)
    p = jnp.exp(s - m_new)
    m_ref[...] = m_new
    l_ref[...] = alpha * l_ref[...] + jnp.sum(p, axis=0, keepdims=True)
    acc_ref[...] = alpha * acc_ref[...] + _dot(v_t, p.astype(BF16))


def _nsa_attend_kernel(qt_ref, sel_ref, ocmp_ref, gt_ref, ks_ref, kw_ref, vs_ref, vw_ref, bias_ref, onehot_ref,
                       o_ref, kaug_ref, qaug_ref, ms_ref, ls_ref, accs_ref, mw_ref, lw_ref, accw_ref):
    T = T_NSA
    j = pl.program_id(1)

    @pl.when(j == 0)
    def _():
        kaug_ref[:, 0:LANES] = ks_ref[0]
        kaug_ref[:, LANES:2 * LANES] = onehot_ref[...]
        qaug_ref[LANES + MAX_SLC:2 * LANES, :] = jnp.zeros((LANES - MAX_SLC, NSA_HEADS * T), BF16)

    for h in range(NSA_HEADS):
        g = h // NSA_HPG
        qaug_ref[0:LANES, h * T:(h + 1) * T] = qt_ref[LANES * h:LANES * (h + 1), :]
        qaug_ref[LANES:LANES + MAX_SLC, h * T:(h + 1) * T] = sel_ref[0, MAX_SLC * g:MAX_SLC * (g + 1), :]

    key = lax.broadcasted_iota(jnp.int32, (T, NSA_HEADS * T), 0)
    qry = lax.broadcasted_iota(jnp.int32, (T, NSA_HEADS * T), 1) & (T - 1)
    causal = key <= qry
    st_s = (ms_ref, ls_ref, accs_ref)
    st_w = (mw_ref, lw_ref, accw_ref)
    _online_init(*st_s)
    _online_init(*st_w)

    def far_body(c, carry):
        r0 = pl.multiple_of(c * T, T)
        s = _dot(kaug_ref[pl.ds(r0, T), :], qaug_ref[...])
        _online_update(*st_s, s, vs_ref[:, pl.ds(r0, T)])
        return carry

    lax.fori_loop(0, jnp.maximum(j - 1, 0), far_body, 0)
    for d in (1, 0):
        c = j - d
        r0 = pl.multiple_of(jnp.maximum(c, 0) * T, T)
        s = _dot(kaug_ref[pl.ds(r0, T), :], qaug_ref[...]) + bias_ref[d]
        ok = causal if d == 0 else key < jnp.where(c >= 0, T, -1)
        s = jnp.where(ok, s, NEG_BIG)
        _online_update(*st_s, s, vs_ref[:, pl.ds(r0, T)])

    n_win = WINDOW // T
    for d in range(n_win + 1):
        c = j - d
        r0 = pl.multiple_of(jnp.maximum(c, 0) * T, T)
        s = _dot(kw_ref[0, pl.ds(r0, T), :], qaug_ref[0:LANES, :])
        if d <= 1:
            s = s + bias_ref[d]
        in_range = key < jnp.where(c >= 0, T, -1)
        if d == 0:
            ok = causal
        elif d == n_win:
            ok = (key > qry) & in_range
        else:
            ok = in_range
        s = jnp.where(ok, s, NEG_BIG)
        _online_update(*st_w, s, vw_ref[:, pl.ds(r0, T)])

    o_sel = accs_ref[...] / ls_ref[...]
    o_win = accw_ref[...] / lw_ref[...]
    gates = jax.nn.sigmoid(gt_ref[...])
    lane = lax.broadcasted_iota(jnp.int32, (T, LANES), 1)
    heads = []
    for h in range(NSA_HEADS):
        cols = slice(h * T, (h + 1) * T)
        o_t = (gates[3 * h:3 * h + 1, :] * ocmp_ref[LANES * h:LANES * (h + 1), :]
               + gates[3 * h + 1:3 * h + 2, :] * o_sel[:, cols]
               + gates[3 * h + 2:3 * h + 3, :] * o_win[:, cols])
        heads.append(o_t.T)
    lo = lane < HEAD_DIM
    o_ref[0, :, 0:LANES] = jnp.where(lo, heads[0], pltpu.roll(heads[1], HEAD_DIM, 1))
    o_ref[0, :, LANES:2 * LANES] = jnp.where(lo, heads[2], heads[3])
    o_ref[0, :, 2 * LANES:3 * LANES] = jnp.where(lo, pltpu.roll(heads[4], HEAD_DIM, 1), heads[5])


def _nsa_attend(q_t, sel_t, ocmp_t, g_t, ks, kw, vs_t, vw_t, bias_t, onehot):
    b, s, _ = ks.shape
    T = T_NSA
    nq = s // T
    R = NSA_HEADS * T
    cols = lambda r: pl.BlockSpec((r, T), lambda i, j: (0, i * nq + j))
    row_b = lambda a: pl.BlockSpec((a.shape[0], s), lambda i, j: (0, i))
    per_b = lambda a: pl.BlockSpec((1,) + a.shape[1:], lambda i, j: (i, 0, 0))
    full = lambda a: pl.BlockSpec(a.shape, lambda i, j: (0,) * a.ndim)
    stat = [pltpu.VMEM((1, R), F32), pltpu.VMEM((1, R), F32), pltpu.VMEM((LANES, R), F32)]
    return pl.pallas_call(
        _nsa_attend_kernel,
        grid=(b, nq),
        in_specs=[cols(NSA_HEADS * LANES), pl.BlockSpec((1, NSA_KV_GROUPS * MAX_SLC, T), lambda i, j: (i, 0, j)),
                  cols(NSA_HEADS * LANES), cols(LANES), per_b(ks), per_b(kw), row_b(vs_t), row_b(vw_t),
                  full(bias_t), full(onehot)],
        out_specs=pl.BlockSpec((1, T, D_NSA), lambda i, j: (i, j, 0)),
        out_shape=jax.ShapeDtypeStruct((b, s, D_NSA), F32),
        scratch_shapes=[pltpu.VMEM((s, 2 * LANES), BF16), pltpu.VMEM((2 * LANES, R), BF16)] + stat + stat,
        compiler_params=_params("parallel", "arbitrary"), name="nsa_attend",
    )(q_t, sel_t, ocmp_t, g_t, ks, kw, vs_t, vw_t, bias_t, onehot)


def _sb_kernel(qa_ref, k_ref, v_ref, tri_ref, o_ref):
    T = T_SB
    qt = pl.program_id(1)
    n_pair = SB_HEADS // 2
    qs = [jnp.concatenate([qa_ref[0, :, LANES * (2 * p):LANES * (2 * p + 1)],
                           qa_ref[0, :, LANES * (2 * p + 1):LANES * (2 * p + 2)]], axis=0)
          for p in range(n_pair)]
    rr = lax.broadcasted_iota(jnp.int32, (2 * T, T), 0) & (T - 1)
    cc = lax.broadcasted_iota(jnp.int32, (2 * T, T), 1)
    strict = cc < rr

    def tile(kt, state, mask):
        r0 = pl.multiple_of(kt * T, T)
        out = []
        for p in range(n_pair):
            carry, acc = state[p]
            z = _dot_nt(qs[p], k_ref[0, pl.ds(r0, T), LANES * p:LANES * (p + 1)])
            log_rest = -_softplus(z)
            lr = log_rest if mask is None else jnp.where(mask, log_rest, 0.0)
            hi, lo = _split_bf16(lr)
            incl = _dot(hi, tri_ref[...]) + _dot(lo, tri_ref[...])
            log_a = (z + log_rest) + (incl - lr) + carry
            a = jnp.exp(log_a)
            if mask is not None:
                a = jnp.where(mask, a, 0.0)
            acc = acc + _dot(a.astype(BF16), v_ref[0, pl.ds(r0, T), LANES * p:LANES * (p + 1)])
            out.append((carry + incl[:, 0:1], acc))
        return tuple(out)

    init = tuple((jnp.zeros((2 * T, 1), F32), jnp.zeros((2 * T, LANES), F32)) for _ in range(n_pair))
    state = tile(qt, init, strict)
    state = lax.fori_loop(0, qt, lambda i, st: tile(qt - 1 - i, st, None), state)
    lane = lax.broadcasted_iota(jnp.int32, (T, LANES), 1)
    for p in range(n_pair):
        acc = state[p][1]
        o_ref[0, :, LANES * p:LANES * (p + 1)] = jnp.where(lane < HEAD_DIM, acc[0:T], acc[T:2 * T])


def _sb_attention(qa, k, v, tri):
    b, s, _ = qa.shape
    T = T_SB
    per_b = lambda a: pl.BlockSpec((1,) + a.shape[1:], lambda i, j: (i, 0, 0))
    return pl.pallas_call(
        _sb_kernel,
        grid=(b, s // T),
        in_specs=[pl.BlockSpec((1, T, SB_HEADS * LANES), lambda i, j: (i, j, 0)), per_b(k), per_b(v),
                  pl.BlockSpec(tri.shape, lambda i, j: (0, 0))],
        out_specs=pl.BlockSpec((1, T, D_SB), lambda i, j: (i, j, 0)),
        out_shape=jax.ShapeDtypeStruct((b, s, D_SB), F32),
        compiler_params=_params("parallel", "parallel"), name="sb_attention",
    )(qa, k, v, tri)


def _hgrn_kernel(q_ref, f_ref, i_ref, lb_ref, nw_ref, ones_ref, bd_ref, o_ref,
                 st_ref, qj_ref, kj_ref, bj_ref, vj_ref, qd_ref, kd_ref, oc_ref):
    C = HG_SUB
    n_blk = T_HG // C

    @pl.when(pl.program_id(2) == 0)
    def _():
        st_ref[...] = jnp.zeros(st_ref.shape, F32)

    lb = lb_ref[...]
    c_floor = jnp.log(jnp.maximum(lb, LB_FLOOR))
    c_rest = jnp.log1p(-lb)
    one_m_lb = 1.0 - lb
    b_run = jnp.zeros((n_blk, LANES), F32)
    for j in range(C):
        fj = f_ref[0, pl.ds(j, n_blk, stride=C), :]
        log_sig = -_softplus(-fj)
        a, bb = c_floor, c_rest + log_sig
        log_f = jnp.maximum(a, bb) + jnp.log1p(jnp.exp(-jnp.abs(a - bb)))
        b_run = b_run + log_f
        bj_ref[j] = b_run
        kj_ref[j] = one_m_lb * jax.nn.sigmoid(-fj)
        qj_ref[j] = q_ref[0, pl.ds(j, n_blk, stride=C), :]
        vj_ref[j] = i_ref[0, pl.ds(j, n_blk, stride=C), :]
    b_last = bj_ref[C - 1]
    for j in range(C):
        bj = bj_ref[j]
        qd_ref[pl.ds(j, n_blk, stride=C), :] = qj_ref[j] * jnp.exp(bj)
        kd_ref[pl.ds(j, n_blk, stride=C), :] = kj_ref[j] * jnp.exp(b_last - bj)

    def step(blk, _):
        r0 = pl.multiple_of(blk * C, C)
        qd = qd_ref[pl.ds(r0, C), :].astype(BF16)
        kd = kd_ref[pl.ds(r0, C), :].astype(BF16)
        vv = i_ref[0, pl.ds(r0, C), :].astype(BF16)
        dec = jnp.exp(bj_ref[C - 1, pl.ds(blk, 1), :])
        st = st_ref[...]
        oc_ref[pl.ds(r0, C), :] = _dot_nt(qd, st.astype(BF16))
        st_ref[...] = st * dec + bd_ref[...] * _dot_tn(vv, kd)
        return 0

    lax.fori_loop(0, n_blk, step, 0)

    nw = nw_ref[...]
    for j in range(C):
        oj = oc_ref[pl.ds(j, n_blk, stride=C), :]
        qj = qj_ref[j]
        bj = bj_ref[j]
        for jp in range(j + 1):
            x = qj * kj_ref[jp] * jnp.exp(bj - bj_ref[jp])
            att = _dot(x.astype(BF16), ones_ref[...])
            oj = oj + att * vj_ref[jp]
        hi, lo = _split_bf16(oj * oj)
        ms = (_dot(hi, ones_ref[...]) + _dot(lo, ones_ref[...])) * (1.0 / HEAD_DIM)
        o_ref[0, pl.ds(j, n_blk, stride=C), :] = oj * lax.rsqrt(ms + RMS_EPS) * nw


def _hgrn(q, f, i, lb, nw, ones_bd, bd_mask):
    b, s, _ = q.shape
    n_blk = T_HG // HG_SUB
    tile = pl.BlockSpec((1, T_HG, LANES), lambda bi, pi, ti: (bi, ti, pi))
    vec = pl.BlockSpec((1, LANES), lambda bi, pi, ti: (0, pi))
    full = lambda a: pl.BlockSpec(a.shape, lambda bi, pi, ti: (0,) * a.ndim)
    jm = pltpu.VMEM((HG_SUB, n_blk, LANES), F32)
    nat = pltpu.VMEM((T_HG, LANES), F32)
    return pl.pallas_call(
        _hgrn_kernel,
        grid=(b, D_HG // LANES, s // T_HG),
        in_specs=[tile, tile, tile, vec, vec, full(ones_bd), full(bd_mask)],
        out_specs=tile,
        out_shape=jax.ShapeDtypeStruct((b, s, D_HG), F32),
        scratch_shapes=[pltpu.VMEM((LANES, LANES), F32), jm, jm, jm, jm, nat, nat, nat],
        compiler_params=_params("parallel", "parallel", "arbitrary"), name="hgrn2",
    )(q, f, i, lb, nw, ones_bd, bd_mask)


def _out_kernel(onsa_ref, osb_ref, ohg_ref, z_ref, x_ref, wn_ref, ws_ref, wh_ref, g_ref, b_ref, o_ref):
    z = z_ref[...]
    sz = z * jax.nn.sigmoid(z)
    y = _dot((onsa_ref[...] * sz[:, 0:D_NSA]).astype(BF16), wn_ref[...])
    y += _dot((osb_ref[...] * sz[:, D_NSA:D_NSA + D_SB]).astype(BF16), ws_ref[...])
    y += _dot((ohg_ref[...] * sz[:, D_NSA + D_SB:D_MIX]).astype(BF16), wh_ref[...])
    v = ALPHA * x_ref[...] + y
    mu = jnp.mean(v, axis=-1, keepdims=True)
    vc = v - mu
    var = jnp.mean(vc * vc, axis=-1, keepdims=True)
    o_ref[...] = vc * lax.rsqrt(var + LN_EPS) * g_ref[...] + b_ref[...]


def _out_proj(o_nsa, o_sb, o_hg, z_all, x2d, wn, ws, wh, g, bvec):
    m = x2d.shape[0]
    rows = lambda w: pl.BlockSpec((T_PROJ, w), lambda i: (i, 0))
    full = lambda a: pl.BlockSpec(a.shape, lambda i: (0, 0))
    return pl.pallas_call(
        _out_kernel,
        grid=(m // T_PROJ,),
        in_specs=[rows(D_NSA), rows(D_SB), rows(D_HG), rows(D_MIX), rows(D_MODEL),
                  full(wn), full(ws), full(wh), full(g), full(bvec)],
        out_specs=rows(D_MODEL),
        out_shape=jax.ShapeDtypeStruct((m, D_MODEL), F32),
        compiler_params=_params("parallel"), name="out_proj_norm",
    )(o_nsa, o_sb, o_hg, z_all, x2d, wn, ws, wh, g, bvec)


def _t5_bucket_np(rel):
    n = np.maximum(rel, 0)
    max_exact = NUM_BUCKETS // 2
    large = max_exact + (np.log(np.maximum(n, 1).astype(np.float32) / max_exact)
                         / math.log(MAX_DISTANCE / max_exact) * (NUM_BUCKETS - max_exact)).astype(np.int32)
    large = np.clip(large, 0, NUM_BUCKETS - 1)
    return np.where(n < max_exact, n, large).astype(np.int32)


def _bias_tables(rel_bias, s):
    tbl = (rel_bias - rel_bias[NUM_BUCKETS - 1]).astype(F32)

    def expand(rel):
        onehot = (jnp.asarray(_t5_bucket_np(rel).reshape(-1, 1)) == jnp.arange(NUM_BUCKETS)[None, :]).astype(F32)
        return jnp.dot(onehot, tbl, precision=lax.Precision.HIGHEST).reshape(rel.shape + (NSA_HEADS,))

    n_cmp_pad = s // CMP_STRIDE
    n_rel = np.arange(2 * n_cmp_pad)[:, None] - n_cmp_pad
    r = np.arange(T_SEL)[None, :]
    pbias = jnp.transpose(expand(r - CMP_STRIDE * n_rel - (CMP_BLOCK - 1)), (2, 0, 1))
    T = T_NSA
    key = np.arange(T)[:, None]
    qry = np.arange(T)[None, :]
    near = np.stack([qry - key, T + qry - key])
    bias_t = jnp.transpose(expand(near), (0, 1, 3, 2)).reshape(2, T, NSA_HEADS * T)
    return pbias, bias_t


def _static_tables(s):
    n_chunk = s // CMP_STRIDE
    cmp_start = np.arange(n_chunk) * CMP_STRIDE
    slc_start = np.arange(MAX_SLC) * SLC_BLOCK
    ovl_t = ((cmp_start[None, :] < slc_start[:, None] + SLC_BLOCK)
             & (cmp_start[None, :] + CMP_BLOCK > slc_start[:, None])
             & (cmp_start[None, :] + CMP_BLOCK <= s)).astype(np.float32)
    onehot = (np.arange(s)[:, None] // SLC_BLOCK == np.arange(LANES)[None, :]).astype(np.float32)
    tri = (np.arange(T_SB)[:, None] >= np.arange(T_SB)[None, :]).astype(np.float32)
    ones_bd = np.kron(np.eye(2), np.ones((HEAD_DIM, HEAD_DIM))).astype(np.float32)
    as_bf16 = lambda a: jnp.asarray(a, dtype=BF16)
    return as_bf16(ovl_t), as_bf16(onehot), as_bf16(tri), as_bf16(ones_bd), jnp.asarray(ones_bd)


def _layer_weights(w_in_l, cmp_pos_l, w_ck1_l, w_ck2_l, w_cv1_l, w_cv2_l):
    offs = np.cumsum((0,) + SPLIT_SIZES)
    (w_q, w_kc, w_vc, w_ks, w_vs, w_kw, w_vw, w_g, w_nz,
     w_sq, w_sk, w_sv, w_sz, w_hq, w_hf, w_hi, w_hz) = [w_in_l[:, offs[i]:offs[i + 1]] for i in range(len(SPLIT_SIZES))]
    zeros = jnp.zeros((D_MODEL, HEAD_DIM), F32)
    cols = []
    for h in range(NSA_HEADS):
        wq = w_q[:, HEAD_DIM * h:HEAD_DIM * (h + 1)] * QK_SCALE
        cols += [wq, zeros] if h // NSA_HPG == 0 else [zeros, wq]
    w_qa = jnp.concatenate(cols, axis=1)
    cols = []
    for h in range(SB_HEADS):
        wq = w_sq[:, HEAD_DIM * h:HEAD_DIM * (h + 1)] * QK_SCALE
        cols += [wq, zeros] if h % 2 == 0 else [zeros, wq]
    w_sqa = jnp.concatenate(cols, axis=1)
    w_gp = jnp.concatenate([w_g, jnp.zeros((D_MODEL, LANES - N_GATES), F32)], axis=1)
    w_z = jnp.concatenate([w_nz, w_sz, w_hz], axis=1)
    w_nat = [w_kc, w_vc, w_ks, w_kw, w_z, w_sqa, w_sk, w_sv, w_hq, w_hf, w_hi]
    dt_nat = [F32, F32, BF16, BF16, F32, BF16, BF16, BF16, F32, F32, F32]
    w_tr = [w_qa.T, w_vs.T, w_vw.T, w_gp.T]
    dt_tr = [BF16, BF16, BF16, F32]
    w_nat = [w.astype(BF16) for w in w_nat]
    w_tr = [w.astype(BF16) for w in w_tr]

    def block_diag(w):
        z = jnp.zeros_like(w)
        return jnp.concatenate([jnp.concatenate([w, z], axis=-1), jnp.concatenate([z, w], axis=-1)], axis=-2)

    pos2 = jnp.concatenate([cmp_pos_l, cmp_pos_l], axis=1)
    w1k = block_diag(w_ck1_l.reshape(CMP_BLOCK, HEAD_DIM, CMP_HIDDEN)).astype(BF16)
    w1v = block_diag(w_cv1_l.reshape(CMP_BLOCK, HEAD_DIM, CMP_HIDDEN)).astype(BF16)
    w2k = block_diag(w_ck2_l).astype(BF16)
    w2v_t = block_diag(w_cv2_l).T.astype(BF16)
    return (w_nat, dt_nat, w_tr, dt_tr), (pos2, w1k, w2k, w1v, w2v_t)


def kernel(x, w_in, cmp_pos, w_ck1, w_ck2, w_cv1, w_cv2, hg_lb, hg_norm_w, w_out, ln_g, ln_b, rel_bias):
    b, s, d = x.shape
    assert d == D_MODEL and s % T_HG == 0 and s // SLC_BLOCK <= MAX_SLC and s >= WINDOW + T_NSA
    lb_w = jax.nn.softmax(hg_lb.astype(F32), axis=0)
    lb_all = jnp.cumsum(lb_w, axis=0) - lb_w[0]
    pbias, bias_t = _bias_tables(rel_bias, s)
    ovl_t, onehot, tri, ones_bd, bd_mask = _static_tables(s)

    x2d = x.reshape(b * s, d)
    for l in range(DEPTH):
        proj_w, cmp_w = _layer_weights(w_in[l], cmp_pos[l], w_ck1[l], w_ck2[l], w_cv1[l], w_cv2[l])
        nat, (q_t, vs_t, vw_t, g_t) = _inproj(x2d, *proj_w)
        kc_src, vc_src, ks, kw, z_all, sqa, sk, sv, hq, hf, hi = [o.reshape(b, s, o.shape[-1]) for o in nat]
        kc, vc_t = _compress(kc_src, vc_src, *cmp_w)
        ocmp_t, sel_t = _nsa_select(q_t, kc, vc_t, pbias, ovl_t)
        o_nsa = _nsa_attend(q_t, sel_t, ocmp_t, g_t, ks, kw, vs_t, vw_t, bias_t, onehot)
        o_sb = _sb_attention(sqa, sk, sv, tri)
        o_hg = _hgrn(hq, hf, hi, lb_all[l][None, :], hg_norm_w[l][None, :], ones_bd, bd_mask)
        wo = w_out[l].astype(BF16)
        x2d = _out_proj(o_nsa.reshape(b * s, D_NSA), o_sb.reshape(b * s, D_SB), o_hg.reshape(b * s, D_HG),
                        z_all.reshape(b * s, D_MIX), x2d,
                        wo[0:D_NSA], wo[D_NSA:D_NSA + D_SB], wo[D_NSA + D_SB:D_MIX],
                        ln_g[l][None, :], ln_b[l][None, :])
    return x2d.reshape(b, s, d)
```

```python
import functools
import math

import numpy as np
import jax
import jax.numpy as jnp
from jax import lax
from jax.experimental import pallas as pl
from jax.experimental.pallas import tpu as pltpu

F32 = jnp.float32
BF16 = jnp.bfloat16

D_MODEL = 1024
DEPTH = 2
HEAD_DIM = 64
LANES = 128
NSA_HEADS = 6
NSA_KV_GROUPS = 2
NSA_HPG = NSA_HEADS // NSA_KV_GROUPS
CMP_BLOCK = 32
CMP_STRIDE = 16
CMP_HIDDEN = 2 * HEAD_DIM
SLC_BLOCK = 64
SLC_TOPN = 16
MAX_SLC = 64
WINDOW = 512
FORCE_BONUS = 1000.0
NEG_BIG = -1e30
LB_FLOOR = 1e-30
SB_HEADS = 4
HG_HEADS = 6
HG_SUB = 16
HG_UNROLL = 16
NUM_BUCKETS = 32
MAX_DISTANCE = 128
D_NSA = NSA_HEADS * HEAD_DIM
D_KV = NSA_KV_GROUPS * HEAD_DIM
D_SB = SB_HEADS * HEAD_DIM
D_HG = HG_HEADS * HEAD_DIM
D_MIX = D_NSA + D_SB + D_HG
N_GATES = NSA_HEADS * 3
SPLIT_SIZES = (D_NSA, D_KV, D_KV, D_KV, D_KV, D_KV, D_KV, N_GATES, D_NSA,
               D_SB, D_SB, D_SB, D_SB, D_HG, D_HG, D_HG, D_HG)
ALPHA = (2 * DEPTH) ** 0.25
LN_EPS = 1e-5
RMS_EPS = 1e-6
QK_SCALE = 1.0 / math.sqrt(HEAD_DIM)

T_SEL = 128
T_NSA = 256
T_SB = 256
T_HG = 1024
T_PROJ = 256
VMEM_LIMIT = 56 * 1024 * 1024

_NT = (((1,), (1,)), ((), ()))
_TN = (((0,), (0,)), ((), ()))


def _dot(a, b):
    return jnp.dot(a, b, preferred_element_type=F32)


def _dot_nt(a, b):
    return lax.dot_general(a, b, _NT, preferred_element_type=F32)


def _dot_tn(a, b):
    return lax.dot_general(a, b, _TN, preferred_element_type=F32)


def _softplus(x):
    return jnp.maximum(x, 0.0) + jnp.log1p(jnp.exp(-jnp.abs(x)))


def _split_bf16(x):
    hi = x.astype(BF16)
    lo = (x - hi.astype(F32)).astype(BF16)
    return hi, lo


def _params(*sem):
    return pltpu.CompilerParams(dimension_semantics=sem, vmem_limit_bytes=VMEM_LIMIT)


def _inproj_kernel(x_ref, *refs, n_nat, n_tr):
    xb = x_ref[...].astype(BF16)
    n_in = n_nat + n_tr
    for w_ref, o_ref in zip(refs[:n_nat], refs[n_in:n_in + n_nat]):
        o_ref[...] = _dot(xb, w_ref[...]).astype(o_ref.dtype)
    for w_ref, o_ref in zip(refs[n_nat:n_in], refs[n_in + n_nat:]):
        o_ref[...] = _dot_nt(w_ref[...], xb).astype(o_ref.dtype)


def _inproj(x2d, w_nat, dt_nat, w_tr, dt_tr):
    m = x2d.shape[0]
    in_specs = [pl.BlockSpec((T_PROJ, D_MODEL), lambda i: (i, 0))]
    in_specs += [pl.BlockSpec(w.shape, lambda i: (0, 0)) for w in w_nat + w_tr]
    out_specs = [pl.BlockSpec((T_PROJ, w.shape[1]), lambda i: (i, 0)) for w in w_nat]
    out_specs += [pl.BlockSpec((w.shape[0], T_PROJ), lambda i: (0, i)) for w in w_tr]
    out_shape = [jax.ShapeDtypeStruct((m, w.shape[1]), dt) for w, dt in zip(w_nat, dt_nat)]
    out_shape += [jax.ShapeDtypeStruct((w.shape[0], m), dt) for w, dt in zip(w_tr, dt_tr)]
    outs = pl.pallas_call(
        functools.partial(_inproj_kernel, n_nat=len(w_nat), n_tr=len(w_tr)),
        grid=(m // T_PROJ,),
        in_specs=in_specs, out_specs=out_specs, out_shape=out_shape,
        compiler_params=_params("parallel"), name="inproj",
    )(x2d, *w_nat, *w_tr)
    return outs[:len(w_nat)], outs[len(w_nat):]


def _compress_kernel(ksrc_ref, vsrc_ref, pos_ref, w1k_ref, w2k_ref, w1v_ref, w2v_ref,
                     kc_ref, vc_ref, *, n_chunk):
    def hidden(src_ref, w1_ref):
        top = jnp.zeros((n_chunk, 2 * CMP_HIDDEN), F32)
        bot = jnp.zeros((n_chunk, 2 * CMP_HIDDEN), F32)
        for p in range(CMP_STRIDE):
            xp = src_ref[0, pl.ds(p, n_chunk, stride=CMP_STRIDE), :]
            top += _dot((xp + pos_ref[p:p + 1, :]).astype(BF16), w1_ref[p])
            q = CMP_STRIDE + p
            bot += _dot((xp + pos_ref[q:q + 1, :]).astype(BF16), w1_ref[q])
        hid = top + pltpu.roll(bot, n_chunk - 1, 0)
        return jax.nn.gelu(hid).astype(BF16)

    kc_ref[0] = _dot(hidden(ksrc_ref, w1k_ref), w2k_ref[...]).astype(kc_ref.dtype)
    vc_ref[0] = _dot_nt(w2v_ref[...], hidden(vsrc_ref, w1v_ref)).astype(vc_ref.dtype)


def _compress(kc_src, vc_src, pos2, w1k, w2k, w1v, w2v_t):
    b, s, _ = kc_src.shape
    n_chunk = s // CMP_STRIDE
    full = lambda a: pl.BlockSpec(a.shape, lambda i: (0,) * a.ndim)
    src = pl.BlockSpec((1, s, D_KV), lambda i: (i, 0, 0))
    return pl.pallas_call(
        functools.partial(_compress_kernel, n_chunk=n_chunk),
        grid=(b,),
        in_specs=[src, src, full(pos2), full(w1k), full(w2k), full(w1v), full(w2v_t)],
        out_specs=[pl.BlockSpec((1, n_chunk, D_KV), lambda i: (i, 0, 0)),
                   pl.BlockSpec((1, D_KV, n_chunk), lambda i: (i, 0, 0))],
        out_shape=[jax.ShapeDtypeStruct((b, n_chunk, D_KV), BF16),
                   jax.ShapeDtypeStruct((b, D_KV, n_chunk), BF16)],
        compiler_params=_params("parallel"), name="nsa_compress",
    )(kc_src, vc_src, pos2, w1k, w2k, w1v, w2v_t)


def _nsa_select_kernel(qt_ref, kc_ref, vct_ref, pbias_ref, ovl_ref, ocmp_ref, sel_ref, *, n_cmp_pad):
    T = T_SEL
    j = pl.program_id(1)
    t0 = j * T
    n_idx = lax.broadcasted_iota(jnp.int32, (n_cmp_pad, T), 0)
    tok_c = t0 + lax.broadcasted_iota(jnp.int32, (n_cmp_pad, T), 1)
    mask_c = tok_c >= CMP_STRIDE * n_idx + (CMP_BLOCK - 1)
    off = pl.multiple_of(n_cmp_pad - (T // CMP_STRIDE) * j, 8)
    psums = [None] * NSA_KV_GROUPS
    for h in range(NSA_HEADS):
        qh = qt_ref[LANES * h:LANES * (h + 1), :]
        s = _dot(kc_ref[0], qh) + pbias_ref[h, pl.ds(off, n_cmp_pad), :]
        s = jnp.where(mask_c, s, NEG_BIG)
        m = jnp.max(s, axis=0, keepdims=True)
        p = jnp.where(mask_c, jnp.exp(s - m), 0.0)
        l = jnp.sum(p, axis=0, keepdims=True)
        p = p / jnp.where(l > 0.0, l, 1.0)
        ocmp_ref[LANES * h:LANES * (h + 1), :] = _dot(vct_ref[0], p.astype(BF16))
        g = h // NSA_HPG
        psums[g] = p if psums[g] is None else psums[g] + p

    jblk = lax.broadcasted_iota(jnp.int32, (MAX_SLC, T), 0)
    tok = t0 + lax.broadcasted_iota(jnp.int32, (MAX_SLC, T), 1)
    cur = lax.shift_right_logical(tok, 6)
    forced = (jblk == 0) | (jblk == cur) | (jblk == cur - 1)
    valid = jblk * SLC_BLOCK <= tok
    jsub = lax.broadcasted_iota(jnp.int32, (8, T), 0)
    for g in range(NSA_KV_GROUPS):
        imp = _dot(ovl_ref[...], psums[g].astype(BF16))
        score = jnp.where(valid, imp + jnp.where(forced, FORCE_BONUS, 0.0), NEG_BIG)
        blocks = [score[8 * rb:8 * rb + 8] for rb in range(MAX_SLC // 8)]
        ranks = [jnp.zeros((8, T), F32) for _ in blocks]
        for jp in range(MAX_SLC):
            other = score[jp:jp + 1, :]
            for rb, blk in enumerate(blocks):
                ge = jnp.where(other >= blk, 1.0, 0.0)
                gt = jnp.where(other > blk, 1.0, 0.0)
                if 8 * rb > jp:
                    inc = ge
                elif 8 * rb + 7 < jp:
                    inc = gt
                else:
                    inc = jnp.where(jsub > jp - 8 * rb, ge, gt)
                ranks[rb] = ranks[rb] + inc
        rank = jnp.concatenate(ranks, axis=0)
        sel_ref[0, MAX_SLC * g:MAX_SLC * (g + 1), :] = jnp.where(rank < float(SLC_TOPN), 0.0, NEG_BIG).astype(BF16)


def _nsa_select(q_t, kc, vc_t, pbias, ovl_t):
    b, n_cmp_pad, _ = kc.shape
    m = q_t.shape[1]
    nq = m // b // T_SEL
    cols = lambda r: pl.BlockSpec((r, T_SEL), lambda i, j: (0, i * nq + j))
    per_b = lambda a: pl.BlockSpec((1,) + a.shape[1:], lambda i, j: (i, 0, 0))
    full = lambda a: pl.BlockSpec(a.shape, lambda i, j: (0,) * a.ndim)
    return pl.pallas_call(
        functools.partial(_nsa_select_kernel, n_cmp_pad=n_cmp_pad),
        grid=(b, nq),
        in_specs=[cols(NSA_HEADS * LANES), per_b(kc), per_b(vc_t), full(pbias), full(ovl_t)],
        out_specs=[cols(NSA_HEADS * LANES), pl.BlockSpec((1, NSA_KV_GROUPS * MAX_SLC, T_SEL), lambda i, j: (i, 0, j))],
        out_shape=[jax.ShapeDtypeStruct((NSA_HEADS * LANES, m), F32),
                   jax.ShapeDtypeStruct((b, NSA_KV_GROUPS * MAX_SLC, m // b), BF16)],
        compiler_params=_params("parallel", "parallel"), name="nsa_select",
    )(q_t, kc, vc_t, pbias, ovl_t)


def _online_init(m_ref, l_ref, acc_ref):
    m_ref[...] = jnp.full(m_ref.shape, NEG_BIG, F32)
    l_ref[...] = jnp.zeros(l_ref.shape, F32)
    acc_ref[...] = jnp.zeros(acc_ref.shape, F32)


def _online_update(m_ref, l_ref, acc_ref, s, v_t):
    m = m_ref[...]
    m_new = jnp.maximum(m, jnp.max(s, axis=0, keepdims=True))
    alpha = jnp.exp(m - m_new)
    p = jnp.exp(s - m_new)
    m_ref[...] = m_new
    l_ref[...] = alpha * l_ref[...] + jnp.sum(p, axis=0, keepdims=True)
    acc_ref[...] = alpha * acc_ref[...] + _dot(v_t, p.astype(BF16))


def _nsa_attend_kernel(qt_ref, sel_ref, ocmp_ref, gt_ref, ks_ref, kw_ref, vs_ref, vw_ref, bias_ref, onehot_ref,
                       o_ref, kaug_ref, qaug_ref, ms_ref, ls_ref, accs_ref, mw_ref, lw_ref, accw_ref):
    T = T_NSA
    j = pl.program_id(1)

    @pl.when(j == 0)
    def _():
        kaug_ref[:, 0:LANES] = ks_ref[0]
        kaug_ref[:, LANES:2 * LANES] = onehot_ref[...]
        qaug_ref[LANES + MAX_SLC:2 * LANES, :] = jnp.zeros((LANES - MAX_SLC, NSA_HEADS * T), BF16)

    for h in range(NSA_HEADS):
        g = h // NSA_HPG
        qaug_ref[0:LANES, h * T:(h + 1) * T] = qt_ref[LANES * h:LANES * (h + 1), :]
        qaug_ref[LANES:LANES + MAX_SLC, h * T:(h + 1) * T] = sel_ref[0, MAX_SLC * g:MAX_SLC * (g + 1), :]

    key = lax.broadcasted_iota(jnp.int32, (T, NSA_HEADS * T), 0)
    qry = lax.broadcasted_iota(jnp.int32, (T, NSA_HEADS * T), 1) & (T - 1)
    causal = key <= qry
    st_s = (ms_ref, ls_ref, accs_ref)
    st_w = (mw_ref, lw_ref, accw_ref)
    _online_init(*st_s)
    _online_init(*st_w)

    def far_body(c, carry):
        r0 = pl.multiple_of(c * T, T)
        s = _dot(kaug_ref[pl.ds(r0, T), :], qaug_ref[...])
        _online_update(*st_s, s, vs_ref[:, pl.ds(r0, T)])
        return carry

    lax.fori_loop(0, jnp.maximum(j - 1, 0), far_body, 0)
    for d in (1, 0):
        c = j - d
        r0 = pl.multiple_of(jnp.maximum(c, 0) * T, T)
        s = _dot(kaug_ref[pl.ds(r0, T), :], qaug_ref[...]) + bias_ref[d]
        ok = causal if d == 0 else key < jnp.where(c >= 0, T, -1)
        s = jnp.where(ok, s, NEG_BIG)
        _online_update(*st_s, s, vs_ref[:, pl.ds(r0, T)])

    n_win = WINDOW // T
    for d in range(n_win + 1):
        c = j - d
        r0 = pl.multiple_of(jnp.maximum(c, 0) * T, T)
        s = _dot(kw_ref[0, pl.ds(r0, T), :], qaug_ref[0:LANES, :])
        if d <= 1:
            s = s + bias_ref[d]
        in_range = key < jnp.where(c >= 0, T, -1)
        if d == 0:
            ok = causal
        elif d == n_win:
            ok = (key > qry) & in_range
        else:
            ok = in_range
        s = jnp.where(ok, s, NEG_BIG)
        _online_update(*st_w, s, vw_ref[:, pl.ds(r0, T)])

    o_sel = accs_ref[...] / ls_ref[...]
    o_win = accw_ref[...] / lw_ref[...]
    gates = jax.nn.sigmoid(gt_ref[...])
    lane = lax.broadcasted_iota(jnp.int32, (T, LANES), 1)
    heads = []
    for h in range(NSA_HEADS):
        cols = slice(h * T, (h + 1) * T)
        o_t = (gates[3 * h:3 * h + 1, :] * ocmp_ref[LANES * h:LANES * (h + 1), :]
               + gates[3 * h + 1:3 * h + 2, :] * o_sel[:, cols]
               + gates[3 * h + 2:3 * h + 3, :] * o_win[:, cols])
        heads.append(o_t.T)
    lo = lane < HEAD_DIM
    o_ref[0, :, 0:LANES] = jnp.where(lo, heads[0], pltpu.roll(heads[1], HEAD_DIM, 1))
    o_ref[0, :, LANES:2 * LANES] = jnp.where(lo, heads[2], heads[3])
    o_ref[0, :, 2 * LANES:3 * LANES] = jnp.where(lo, pltpu.roll(heads[4], HEAD_DIM, 1), heads[5])


def _nsa_attend(q_t, sel_t, ocmp_t, g_t, ks, kw, vs_t, vw_t, bias_t, onehot):
    b, s, _ = ks.shape
    T = T_NSA
    nq = s // T
    R = NSA_HEADS * T
    cols = lambda r: pl.BlockSpec((r, T), lambda i, j: (0, i * nq + j))
    row_b = lambda a: pl.BlockSpec((a.shape[0], s), lambda i, j: (0, i))
    per_b = lambda a: pl.BlockSpec((1,) + a.shape[1:], lambda i, j: (i, 0, 0))
    full = lambda a: pl.BlockSpec(a.shape, lambda i, j: (0,) * a.ndim)
    stat = [pltpu.VMEM((1, R), F32), pltpu.VMEM((1, R), F32), pltpu.VMEM((LANES, R), F32)]
    return pl.pallas_call(
        _nsa_attend_kernel,
        grid=(b, nq),
        in_specs=[cols(NSA_HEADS * LANES), pl.BlockSpec((1, NSA_KV_GROUPS * MAX_SLC, T), lambda i, j: (i, 0, j)),
                  cols(NSA_HEADS * LANES), cols(LANES), per_b(ks), per_b(kw), row_b(vs_t), row_b(vw_t),
                  full(bias_t), full(onehot)],
        out_specs=pl.BlockSpec((1, T, D_NSA), lambda i, j: (i, j, 0)),
        out_shape=jax.ShapeDtypeStruct((b, s, D_NSA), F32),
        scratch_shapes=[pltpu.VMEM((s, 2 * LANES), BF16), pltpu.VMEM((2 * LANES, R), BF16)] + stat + stat,
        compiler_params=_params("parallel", "arbitrary"), name="nsa_attend",
    )(q_t, sel_t, ocmp_t, g_t, ks, kw, vs_t, vw_t, bias_t, onehot)


def _sb_kernel(qt_ref, k_ref, vt_ref, tri_ref, o_ref, carry_ref, acc_ref):
    T = T_SB
    j = pl.program_id(1)
    n_pair = SB_HEADS // 2
    key = lax.broadcasted_iota(jnp.int32, (T, 2 * T), 0)
    qry = lax.broadcasted_iota(jnp.int32, (T, 2 * T), 1) & (T - 1)
    strict = key < qry
    carry_ref[...] = jnp.zeros(carry_ref.shape, F32)
    acc_ref[...] = jnp.zeros(acc_ref.shape, F32)

    def tile(c, mask):
        r0 = pl.multiple_of(c * T, T)
        for p in range(n_pair):
            q_pair = jnp.concatenate([qt_ref[LANES * (2 * p):LANES * (2 * p + 1), :],
                                      qt_ref[LANES * (2 * p + 1):LANES * (2 * p + 2), :]], axis=1)
            z = _dot(k_ref[0, pl.ds(r0, T), LANES * p:LANES * (p + 1)], q_pair)
            log_rest = -_softplus(z)
            lr = log_rest if mask is None else jnp.where(mask, log_rest, 0.0)
            hi, lo = _split_bf16(lr)
            incl = _dot(tri_ref[...], hi) + _dot(tri_ref[...], lo)
            log_a = (z + log_rest) + (incl - lr) + carry_ref[p]
            a = jnp.exp(log_a)
            if mask is not None:
                a = jnp.where(mask, a, 0.0)
            acc_ref[p] = acc_ref[p] + _dot(vt_ref[LANES * p:LANES * (p + 1), pl.ds(r0, T)], a.astype(BF16))
            carry_ref[p] = carry_ref[p] + incl[0:1, :]

    tile(j, strict)

    def back(i, carry):
        tile(j - 1 - i, None)
        return carry

    lax.fori_loop(0, j, back, 0)
    row = lax.broadcasted_iota(jnp.int32, (LANES, T), 0)
    for p in range(n_pair):
        acc = acc_ref[p]
        o_ref[0, :, LANES * p:LANES * (p + 1)] = jnp.where(row < HEAD_DIM, acc[:, 0:T], acc[:, T:2 * T]).T


def _sb_attention(q_t, k, v_t, tri_t):
    b, s, _ = k.shape
    T = T_SB
    nq = s // T
    n_pair = SB_HEADS // 2
    return pl.pallas_call(
        _sb_kernel,
        grid=(b, nq),
        in_specs=[pl.BlockSpec((SB_HEADS * LANES, T), lambda i, j: (0, i * nq + j)),
                  pl.BlockSpec((1, s, D_SB), lambda i, j: (i, 0, 0)),
                  pl.BlockSpec((D_SB, s), lambda i, j: (0, i)),
                  pl.BlockSpec(tri_t.shape, lambda i, j: (0, 0))],
        out_specs=pl.BlockSpec((1, T, D_SB), lambda i, j: (i, j, 0)),
        out_shape=jax.ShapeDtypeStruct((b, s, D_SB), F32),
        scratch_shapes=[pltpu.VMEM((n_pair, 1, 2 * T), F32), pltpu.VMEM((n_pair, LANES, 2 * T), F32)],
        compiler_params=_params("parallel", "parallel"), name="sb_attention",
    )(q_t, k, v_t, tri_t)


def _hgrn_kernel(q_ref, f_ref, i_ref, lb_ref, nw_ref, ones_ref, bd_ref, o_ref,
                 st_ref, qj_ref, kj_ref, bj_ref, vj_ref, qd_ref, kd_ref, oc_ref, x_ref, u_ref, sb_ref, dec_ref):
    C = HG_SUB
    n_blk = T_HG // C

    @pl.when(pl.program_id(2) == 0)
    def _():
        st_ref[...] = jnp.zeros(st_ref.shape, F32)

    lb = lb_ref[...]
    c_floor = jnp.log(jnp.maximum(lb, LB_FLOOR))
    c_rest = jnp.log1p(-lb)
    one_m_lb = 1.0 - lb
    b_run = jnp.zeros((n_blk, LANES), F32)
    for j in range(C):
        fj = f_ref[0, pl.ds(j, n_blk, stride=C), :]
        log_sig = -_softplus(-fj)
        a, bb = c_floor, c_rest + log_sig
        log_f = jnp.maximum(a, bb) + jnp.log1p(jnp.exp(-jnp.abs(a - bb)))
        b_run = b_run + log_f
        bj_ref[j] = b_run
        kj_ref[j] = one_m_lb * jax.nn.sigmoid(-fj)
        qj_ref[j] = q_ref[0, pl.ds(j, n_blk, stride=C), :]
        vj_ref[j] = i_ref[0, pl.ds(j, n_blk, stride=C), :]
    b_last = bj_ref[C - 1]
    for j in range(C):
        bj = bj_ref[j]
        qd_ref[pl.ds(j, n_blk, stride=C), :] = qj_ref[j] * jnp.exp(bj)
        kd_ref[pl.ds(j, n_blk, stride=C), :] = kj_ref[j] * jnp.exp(b_last - bj)

    def kv_products(blk, carry):
        r0 = pl.multiple_of(blk * C, C)
        kd = kd_ref[pl.ds(r0, C), :].astype(BF16)
        vv = i_ref[0, pl.ds(r0, C), :].astype(BF16)
        u_ref[blk] = bd_ref[...] * _dot_tn(vv, kd)
        return carry

    lax.fori_loop(0, n_blk, kv_products, 0, unroll=HG_UNROLL)
    dec_ref[...] = jnp.exp(b_last)

    def scan(blk, st):
        sb_ref[blk] = st.astype(BF16)
        return st * dec_ref[pl.ds(blk, 1), :] + u_ref[blk]

    st_ref[...] = lax.fori_loop(0, n_blk, scan, st_ref[...], unroll=HG_UNROLL)

    def outputs(blk, carry):
        r0 = pl.multiple_of(blk * C, C)
        oc_ref[pl.ds(r0, C), :] = _dot_nt(qd_ref[pl.ds(r0, C), :].astype(BF16), sb_ref[blk])
        return carry

    lax.fori_loop(0, n_blk, outputs, 0, unroll=HG_UNROLL)

    nw = nw_ref[...]
    for j in range(C):
        oj = oc_ref[pl.ds(j, n_blk, stride=C), :]
        qj = qj_ref[j]
        bj = bj_ref[j]
        for jp in range(j + 1):
            x = qj * kj_ref[jp] * jnp.exp(bj - bj_ref[jp])
            x_ref[jp * n_blk:(jp + 1) * n_blk, :] = x.astype(BF16)
        att = _dot(x_ref[0:(j + 1) * n_blk, :], ones_ref[...])
        for jp in range(j + 1):
            oj = oj + att[jp * n_blk:(jp + 1) * n_blk] * vj_ref[jp]
        hi, lo = _split_bf16(oj * oj)
        ms = (_dot(hi, ones_ref[...]) + _dot(lo, ones_ref[...])) * (1.0 / HEAD_DIM)
        o_ref[0, pl.ds(j, n_blk, stride=C), :] = oj * lax.rsqrt(ms + RMS_EPS) * nw


def _hgrn(q, f, i, lb, nw, ones_bd, bd_mask):
    b, s, _ = q.shape
    n_blk = T_HG // HG_SUB
    tile = pl.BlockSpec((1, T_HG, LANES), lambda bi, pi, ti: (bi, ti, pi))
    vec = pl.BlockSpec((1, LANES), lambda bi, pi, ti: (0, pi))
    full = lambda a: pl.BlockSpec(a.shape, lambda bi, pi, ti: (0,) * a.ndim)
    jm = pltpu.VMEM((HG_SUB, n_blk, LANES), F32)
    nat = pltpu.VMEM((T_HG, LANES), F32)
    return pl.pallas_call(
        _hgrn_kernel,
        grid=(b, D_HG // LANES, s // T_HG),
        in_specs=[tile, tile, tile, vec, vec, full(ones_bd), full(bd_mask)],
        out_specs=tile,
        out_shape=jax.ShapeDtypeStruct((b, s, D_HG), F32),
        scratch_shapes=[pltpu.VMEM((LANES, LANES), F32), jm, jm, jm, jm, nat, nat, nat,
                        pltpu.VMEM((T_HG, LANES), BF16), pltpu.VMEM((n_blk, LANES, LANES), F32),
                        pltpu.VMEM((n_blk, LANES, LANES), BF16), pltpu.VMEM((n_blk, LANES), F32)],
        compiler_params=_params("parallel", "parallel", "arbitrary"), name="hgrn2",
    )(q, f, i, lb, nw, ones_bd, bd_mask)


def _out_kernel(onsa_ref, osb_ref, ohg_ref, z_ref, x_ref, wn_ref, ws_ref, wh_ref, g_ref, b_ref, o_ref):
    z = z_ref[...]
    sz = z * jax.nn.sigmoid(z)
    y = _dot((onsa_ref[...] * sz[:, 0:D_NSA]).astype(BF16), wn_ref[...])
    y += _dot((osb_ref[...] * sz[:, D_NSA:D_NSA + D_SB]).astype(BF16), ws_ref[...])
    y += _dot((ohg_ref[...] * sz[:, D_NSA + D_SB:D_MIX]).astype(BF16), wh_ref[...])
    v = ALPHA * x_ref[...] + y
    mu = jnp.mean(v, axis=-1, keepdims=True)
    vc = v - mu
    var = jnp.mean(vc * vc, axis=-1, keepdims=True)
    o_ref[...] = vc * lax.rsqrt(var + LN_EPS) * g_ref[...] + b_ref[...]


def _out_proj(o_nsa, o_sb, o_hg, z_all, x2d, wn, ws, wh, g, bvec):
    m = x2d.shape[0]
    rows = lambda w: pl.BlockSpec((T_PROJ, w), lambda i: (i, 0))
    full = lambda a: pl.BlockSpec(a.shape, lambda i: (0, 0))
    return pl.pallas_call(
        _out_kernel,
        grid=(m // T_PROJ,),
        in_specs=[rows(D_NSA), rows(D_SB), rows(D_HG), rows(D_MIX), rows(D_MODEL),
                  full(wn), full(ws), full(wh), full(g), full(bvec)],
        out_specs=rows(D_MODEL),
        out_shape=jax.ShapeDtypeStruct((m, D_MODEL), F32),
        compiler_params=_params("parallel"), name="out_proj_norm",
    )(o_nsa, o_sb, o_hg, z_all, x2d, wn, ws, wh, g, bvec)


def _t5_bucket_np(rel):
    n = np.maximum(rel, 0)
    max_exact = NUM_BUCKETS // 2
    large = max_exact + (np.log(np.maximum(n, 1).astype(np.float32) / max_exact)
                         / math.log(MAX_DISTANCE / max_exact) * (NUM_BUCKETS - max_exact)).astype(np.int32)
    large = np.clip(large, 0, NUM_BUCKETS - 1)
    return np.where(n < max_exact, n, large).astype(np.int32)


def _bias_tables(rel_bias, s):
    tbl = (rel_bias - rel_bias[NUM_BUCKETS - 1]).astype(F32)

    def expand(rel):
        onehot = (jnp.asarray(_t5_bucket_np(rel).reshape(-1, 1)) == jnp.arange(NUM_BUCKETS)[None, :]).astype(F32)
        return jnp.dot(onehot, tbl, precision=lax.Precision.HIGHEST).reshape(rel.shape + (NSA_HEADS,))

    n_cmp_pad = s // CMP_STRIDE
    n_rel = np.arange(2 * n_cmp_pad)[:, None] - n_cmp_pad
    r = np.arange(T_SEL)[None, :]
    pbias = jnp.transpose(expand(r - CMP_STRIDE * n_rel - (CMP_BLOCK - 1)), (2, 0, 1))
    T = T_NSA
    key = np.arange(T)[:, None]
    qry = np.arange(T)[None, :]
    near = np.stack([qry - key, T + qry - key])
    bias_t = jnp.transpose(expand(near), (0, 1, 3, 2)).reshape(2, T, NSA_HEADS * T)
    return pbias, bias_t


def _static_tables(s):
    n_chunk = s // CMP_STRIDE
    cmp_start = np.arange(n_chunk) * CMP_STRIDE
    slc_start = np.arange(MAX_SLC) * SLC_BLOCK
    ovl_t = ((cmp_start[None, :] < slc_start[:, None] + SLC_BLOCK)
             & (cmp_start[None, :] + CMP_BLOCK > slc_start[:, None])
             & (cmp_start[None, :] + CMP_BLOCK <= s)).astype(np.float32)
    onehot = (np.arange(s)[:, None] // SLC_BLOCK == np.arange(LANES)[None, :]).astype(np.float32)
    tri = (np.arange(T_SB)[None, :] >= np.arange(T_SB)[:, None]).astype(np.float32)
    ones_bd = np.kron(np.eye(2), np.ones((HEAD_DIM, HEAD_DIM))).astype(np.float32)
    as_bf16 = lambda a: jnp.asarray(a, dtype=BF16)
    return as_bf16(ovl_t), as_bf16(onehot), as_bf16(tri), as_bf16(ones_bd), jnp.asarray(ones_bd)


def _layer_weights(w_in_l, cmp_pos_l, w_ck1_l, w_ck2_l, w_cv1_l, w_cv2_l):
    offs = np.cumsum((0,) + SPLIT_SIZES)
    (w_q, w_kc, w_vc, w_ks, w_vs, w_kw, w_vw, w_g, w_nz,
     w_sq, w_sk, w_sv, w_sz, w_hq, w_hf, w_hi, w_hz) = [w_in_l[:, offs[i]:offs[i + 1]] for i in range(len(SPLIT_SIZES))]
    zeros = jnp.zeros((D_MODEL, HEAD_DIM), F32)
    cols = []
    for h in range(NSA_HEADS):
        wq = w_q[:, HEAD_DIM * h:HEAD_DIM * (h + 1)] * QK_SCALE
        cols += [wq, zeros] if h // NSA_HPG == 0 else [zeros, wq]
    w_qa = jnp.concatenate(cols, axis=1)
    cols = []
    for h in range(SB_HEADS):
        wq = w_sq[:, HEAD_DIM * h:HEAD_DIM * (h + 1)] * QK_SCALE
        cols += [wq, zeros] if h % 2 == 0 else [zeros, wq]
    w_sqa = jnp.concatenate(cols, axis=1)
    w_gp = jnp.concatenate([w_g, jnp.zeros((D_MODEL, LANES - N_GATES), F32)], axis=1)
    w_z = jnp.concatenate([w_nz, w_sz, w_hz], axis=1)
    w_nat = [w_kc, w_vc, w_ks, w_kw, w_z, w_sk, w_hq, w_hf, w_hi]
    dt_nat = [F32, F32, BF16, BF16, F32, BF16, F32, F32, F32]
    w_tr = [w_qa.T, w_vs.T, w_vw.T, w_gp.T, w_sqa.T, w_sv.T]
    dt_tr = [BF16, BF16, BF16, F32, BF16, BF16]
    w_nat = [w.astype(BF16) for w in w_nat]
    w_tr = [w.astype(BF16) for w in w_tr]

    def block_diag(w):
        z = jnp.zeros_like(w)
        return jnp.concatenate([jnp.concatenate([w, z], axis=-1), jnp.concatenate([z, w], axis=-1)], axis=-2)

    pos2 = jnp.concatenate([cmp_pos_l, cmp_pos_l], axis=1)
    w1k = block_diag(w_ck1_l.reshape(CMP_BLOCK, HEAD_DIM, CMP_HIDDEN)).astype(BF16)
    w1v = block_diag(w_cv1_l.reshape(CMP_BLOCK, HEAD_DIM, CMP_HIDDEN)).astype(BF16)
    w2k = block_diag(w_ck2_l).astype(BF16)
    w2v_t = block_diag(w_cv2_l).T.astype(BF16)
    return (w_nat, dt_nat, w_tr, dt_tr), (pos2, w1k, w2k, w1v, w2v_t)


def kernel(x, w_in, cmp_pos, w_ck1, w_ck2, w_cv1, w_cv2, hg_lb, hg_norm_w, w_out, ln_g, ln_b, rel_bias):
    b, s, d = x.shape
    assert d == D_MODEL and s % T_HG == 0 and s // SLC_BLOCK <= MAX_SLC and s >= WINDOW + T_NSA
    lb_w = jax.nn.softmax(hg_lb.astype(F32), axis=0)
    lb_all = jnp.cumsum(lb_w, axis=0) - lb_w[0]
    pbias, bias_t = _bias_tables(rel_bias, s)
    ovl_t, onehot, tri, ones_bd, bd_mask = _static_tables(s)

    x2d = x.reshape(b * s, d)
    for l in range(DEPTH):
        proj_w, cmp_w = _layer_weights(w_in[l], cmp_pos[l], w_ck1[l], w_ck2[l], w_cv1[l], w_cv2[l])
        nat, (q_t, vs_t, vw_t, g_t, sq_t, sv_t) = _inproj(x2d, *proj_w)
        kc_src, vc_src, ks, kw, z_all, sk, hq, hf, hi = [o.reshape(b, s, o.shape[-1]) for o in nat]
        kc, vc_t = _compress(kc_src, vc_src, *cmp_w)
        ocmp_t, sel_t = _nsa_select(q_t, kc, vc_t, pbias, ovl_t)
        o_nsa = _nsa_attend(q_t, sel_t, ocmp_t, g_t, ks, kw, vs_t, vw_t, bias_t, onehot)
        o_sb = _sb_attention(sq_t, sk, sv_t, tri)
        o_hg = _hgrn(hq, hf, hi, lb_all[l][None, :], hg_norm_w[l][None, :], ones_bd, bd_mask)
        wo = w_out[l].astype(BF16)
        x2d = _out_proj(o_nsa.reshape(b * s, D_NSA), o_sb.reshape(b * s, D_SB), o_hg.reshape(b * s, D_HG),
                        z_all.reshape(b * s, D_MIX), x2d,
                        wo[0:D_NSA], wo[D_NSA:D_NSA + D_SB], wo[D_NSA + D_SB:D_MIX],
                        ln_g[l][None, :], ln_b[l][None, :])
    return x2d.reshape(b, s, d)
```

```python
import functools
import math

import numpy as np
import jax
import jax.numpy as jnp
from jax import lax
from jax.experimental import pallas as pl
from jax.experimental.pallas import tpu as pltpu

F32 = jnp.float32
BF16 = jnp.bfloat16

D_MODEL = 1024
DEPTH = 2
HEAD_DIM = 64
LANES = 128
NSA_HEADS = 6
NSA_KV_GROUPS = 2
NSA_HPG = NSA_HEADS // NSA_KV_GROUPS
CMP_BLOCK = 32
CMP_STRIDE = 16
CMP_HIDDEN = 2 * HEAD_DIM
SLC_BLOCK = 64
SLC_TOPN = 16
MAX_SLC = 64
WINDOW = 512
FORCE_BONUS = 1000.0
NEG_BIG = -1e30
LB_FLOOR = 1e-30
SB_HEADS = 4
HG_HEADS = 6
HG_SUB = 16
HG_UNROLL = 16
NUM_BUCKETS = 32
MAX_DISTANCE = 128
D_NSA = NSA_HEADS * HEAD_DIM
D_KV = NSA_KV_GROUPS * HEAD_DIM
D_SB = SB_HEADS * HEAD_DIM
D_HG = HG_HEADS * HEAD_DIM
D_MIX = D_NSA + D_SB + D_HG
N_GATES = NSA_HEADS * 3
SPLIT_SIZES = (D_NSA, D_KV, D_KV, D_KV, D_KV, D_KV, D_KV, N_GATES, D_NSA,
               D_SB, D_SB, D_SB, D_SB, D_HG, D_HG, D_HG, D_HG)
ALPHA = (2 * DEPTH) ** 0.25
LN_EPS = 1e-5
RMS_EPS = 1e-6
LOG2E = math.log2(math.e)
QK_SCALE2 = LOG2E / math.sqrt(HEAD_DIM)

T_SEL = 128
T_NSA = 256
T_SB = 256
T_HG = 1024
T_PROJ = 256
VMEM_LIMIT = 56 * 1024 * 1024

_NT = (((1,), (1,)), ((), ()))
_TN = (((0,), (0,)), ((), ()))


def _dot(a, b):
    return jnp.dot(a, b, preferred_element_type=F32)


def _dot_nt(a, b):
    return lax.dot_general(a, b, _NT, preferred_element_type=F32)


def _dot_tn(a, b):
    return lax.dot_general(a, b, _TN, preferred_element_type=F32)


def _softplus(x):
    return jnp.maximum(x, 0.0) + jnp.log1p(jnp.exp(-jnp.abs(x)))


def _split_bf16(x):
    hi = x.astype(BF16)
    lo = (x - hi.astype(F32)).astype(BF16)
    return hi, lo


def _params(*sem):
    return pltpu.CompilerParams(dimension_semantics=sem, vmem_limit_bytes=VMEM_LIMIT)


def _inproj_kernel(x_ref, *refs, n_nat, n_tr):
    xb = x_ref[...].astype(BF16)
    n_in = n_nat + n_tr
    for w_ref, o_ref in zip(refs[:n_nat], refs[n_in:n_in + n_nat]):
        o_ref[...] = _dot(xb, w_ref[...]).astype(o_ref.dtype)
    for w_ref, o_ref in zip(refs[n_nat:n_in], refs[n_in + n_nat:]):
        o_ref[...] = _dot_nt(w_ref[...], xb).astype(o_ref.dtype)


def _inproj(x2d, w_nat, dt_nat, w_tr, dt_tr):
    m = x2d.shape[0]
    in_specs = [pl.BlockSpec((T_PROJ, D_MODEL), lambda i: (i, 0))]
    in_specs += [pl.BlockSpec(w.shape, lambda i: (0, 0)) for w in w_nat + w_tr]
    out_specs = [pl.BlockSpec((T_PROJ, w.shape[1]), lambda i: (i, 0)) for w in w_nat]
    out_specs += [pl.BlockSpec((w.shape[0], T_PROJ), lambda i: (0, i)) for w in w_tr]
    out_shape = [jax.ShapeDtypeStruct((m, w.shape[1]), dt) for w, dt in zip(w_nat, dt_nat)]
    out_shape += [jax.ShapeDtypeStruct((w.shape[0], m), dt) for w, dt in zip(w_tr, dt_tr)]
    outs = pl.pallas_call(
        functools.partial(_inproj_kernel, n_nat=len(w_nat), n_tr=len(w_tr)),
        grid=(m // T_PROJ,),
        in_specs=in_specs, out_specs=out_specs, out_shape=out_shape,
        compiler_params=_params("parallel"), name="inproj",
    )(x2d, *w_nat, *w_tr)
    return outs[:len(w_nat)], outs[len(w_nat):]


def _compress_kernel(ksrc_ref, vsrc_ref, pos_ref, w1k_ref, w2k_ref, w1v_ref, w2v_ref,
                     kc_ref, vc_ref, *, n_chunk):
    def hidden(src_ref, w1_ref):
        top = jnp.zeros((n_chunk, 2 * CMP_HIDDEN), F32)
        bot = jnp.zeros((n_chunk, 2 * CMP_HIDDEN), F32)
        for p in range(CMP_STRIDE):
            xp = src_ref[0, pl.ds(p, n_chunk, stride=CMP_STRIDE), :]
            top += _dot((xp + pos_ref[p:p + 1, :]).astype(BF16), w1_ref[p])
            q = CMP_STRIDE + p
            bot += _dot((xp + pos_ref[q:q + 1, :]).astype(BF16), w1_ref[q])
        hid = top + pltpu.roll(bot, n_chunk - 1, 0)
        return jax.nn.gelu(hid).astype(BF16)

    kc_ref[0] = _dot(hidden(ksrc_ref, w1k_ref), w2k_ref[...]).astype(kc_ref.dtype)
    vc_ref[0] = _dot_nt(w2v_ref[...], hidden(vsrc_ref, w1v_ref)).astype(vc_ref.dtype)


def _compress(kc_src, vc_src, pos2, w1k, w2k, w1v, w2v_t):
    b, s, _ = kc_src.shape
    n_chunk = s // CMP_STRIDE
    full = lambda a: pl.BlockSpec(a.shape, lambda i: (0,) * a.ndim)
    src = pl.BlockSpec((1, s, D_KV), lambda i: (i, 0, 0))
    return pl.pallas_call(
        functools.partial(_compress_kernel, n_chunk=n_chunk),
        grid=(b,),
        in_specs=[src, src, full(pos2), full(w1k), full(w2k), full(w1v), full(w2v_t)],
        out_specs=[pl.BlockSpec((1, n_chunk, D_KV), lambda i: (i, 0, 0)),
                   pl.BlockSpec((1, D_KV, n_chunk), lambda i: (i, 0, 0))],
        out_shape=[jax.ShapeDtypeStruct((b, n_chunk, D_KV), BF16),
                   jax.ShapeDtypeStruct((b, D_KV, n_chunk), BF16)],
        compiler_params=_params("parallel"), name="nsa_compress",
    )(kc_src, vc_src, pos2, w1k, w2k, w1v, w2v_t)


def _nsa_select_kernel(qt_ref, kc_ref, vct_ref, pbias_ref, ovl_ref, ocmp_ref, sel_ref, *, n_cmp_pad):
    T = T_SEL
    j = pl.program_id(1)
    t0 = j * T
    n_idx = lax.broadcasted_iota(jnp.int32, (n_cmp_pad, T), 0)
    tok_c = t0 + lax.broadcasted_iota(jnp.int32, (n_cmp_pad, T), 1)
    mask_c = tok_c >= CMP_STRIDE * n_idx + (CMP_BLOCK - 1)
    off = pl.multiple_of(n_cmp_pad - (T // CMP_STRIDE) * j, 8)
    psums = [None] * NSA_KV_GROUPS
    for h in range(NSA_HEADS):
        qh = qt_ref[LANES * h:LANES * (h + 1), :]
        s = _dot(kc_ref[0], qh) + pbias_ref[h, pl.ds(off, n_cmp_pad), :]
        s = jnp.where(mask_c, s, NEG_BIG)
        m = jnp.max(s, axis=0, keepdims=True)
        p = jnp.where(mask_c, jnp.exp2(s - m), 0.0)
        l = jnp.sum(p, axis=0, keepdims=True)
        p = p / jnp.where(l > 0.0, l, 1.0)
        ocmp_ref[LANES * h:LANES * (h + 1), :] = _dot(vct_ref[0], p.astype(BF16))
        g = h // NSA_HPG
        psums[g] = p if psums[g] is None else psums[g] + p

    jblk = lax.broadcasted_iota(jnp.int32, (MAX_SLC, T), 0)
    tok = t0 + lax.broadcasted_iota(jnp.int32, (MAX_SLC, T), 1)
    cur = lax.shift_right_logical(tok, 6)
    forced = (jblk == 0) | (jblk == cur) | (jblk == cur - 1)
    valid = jblk * SLC_BLOCK <= tok
    jsub = lax.broadcasted_iota(jnp.int32, (8, T), 0)
    for g in range(NSA_KV_GROUPS):
        imp = _dot(ovl_ref[...], psums[g].astype(BF16))
        score = jnp.where(valid, imp + jnp.where(forced, FORCE_BONUS, 0.0), NEG_BIG)
        blocks = [score[8 * rb:8 * rb + 8] for rb in range(MAX_SLC // 8)]
        ranks = [jnp.zeros((8, T), F32) for _ in blocks]
        for jp in range(MAX_SLC):
            other = score[jp:jp + 1, :]
            for rb, blk in enumerate(blocks):
                ge = jnp.where(other >= blk, 1.0, 0.0)
                gt = jnp.where(other > blk, 1.0, 0.0)
                if 8 * rb > jp:
                    inc = ge
                elif 8 * rb + 7 < jp:
                    inc = gt
                else:
                    inc = jnp.where(jsub > jp - 8 * rb, ge, gt)
                ranks[rb] = ranks[rb] + inc
        rank = jnp.concatenate(ranks, axis=0)
        sel_ref[0, MAX_SLC * g:MAX_SLC * (g + 1), :] = jnp.where(rank < float(SLC_TOPN), 0.0, NEG_BIG).astype(BF16)


def _nsa_select(q_t, kc, vc_t, pbias, ovl_t):
    b, n_cmp_pad, _ = kc.shape
    m = q_t.shape[1]
    nq = m // b // T_SEL
    cols = lambda r: pl.BlockSpec((r, T_SEL), lambda i, j: (0, i * nq + j))
    per_b = lambda a: pl.BlockSpec((1,) + a.shape[1:], lambda i, j: (i, 0, 0))
    full = lambda a: pl.BlockSpec(a.shape, lambda i, j: (0,) * a.ndim)
    return pl.pallas_call(
        functools.partial(_nsa_select_kernel, n_cmp_pad=n_cmp_pad),
        grid=(b, nq),
        in_specs=[cols(NSA_HEADS * LANES), per_b(kc), per_b(vc_t), full(pbias), full(ovl_t)],
        out_specs=[cols(NSA_HEADS * LANES), pl.BlockSpec((1, NSA_KV_GROUPS * MAX_SLC, T_SEL), lambda i, j: (i, 0, j))],
        out_shape=[jax.ShapeDtypeStruct((NSA_HEADS * LANES, m), F32),
                   jax.ShapeDtypeStruct((b, NSA_KV_GROUPS * MAX_SLC, m // b), BF16)],
        compiler_params=_params("parallel", "parallel"), name="nsa_select",
    )(q_t, kc, vc_t, pbias, ovl_t)


def _online_init(m_ref, l_ref, acc_ref):
    m_ref[...] = jnp.full(m_ref.shape, NEG_BIG, F32)
    l_ref[...] = jnp.zeros(l_ref.shape, F32)
    acc_ref[...] = jnp.zeros(acc_ref.shape, F32)


def _online_chunk(m_ref, l_ref, acc_ref, k_tile, q_ref, q_rows, v_t, bias_ref=None, ok=None):
    s = _dot(k_tile, q_ref[0:q_rows, :])
    if bias_ref is not None:
        s = s + bias_ref[...]
    if ok is not None:
        s = jnp.where(ok, s, NEG_BIG)
    m = m_ref[...]
    m_new = jnp.maximum(m, jnp.max(s, axis=0, keepdims=True))
    alpha = jnp.exp2(m - m_new)
    p = jnp.exp2(s - m_new)
    m_ref[...] = m_new
    l_ref[...] = alpha * l_ref[...] + jnp.sum(p, axis=0, keepdims=True)
    acc_ref[...] = alpha * acc_ref[...] + _dot(v_t, p.astype(BF16))


def _nsa_attend_kernel(qt_ref, sel_ref, ocmp_ref, gt_ref, ks_ref, kw_ref, vs_ref, vw_ref, bias_ref, onehot_ref,
                       o_ref, kaug_ref, qaug_ref, ms_ref, ls_ref, accs_ref, mw_ref, lw_ref, accw_ref):
    T = T_NSA
    j = pl.program_id(1)

    @pl.when(j == 0)
    def _():
        kaug_ref[:, 0:LANES] = ks_ref[0]
        kaug_ref[:, LANES:2 * LANES] = onehot_ref[...]
        qaug_ref[LANES + MAX_SLC:2 * LANES, :] = jnp.zeros((LANES - MAX_SLC, NSA_HEADS * T), BF16)

    for h in range(NSA_HEADS):
        g = h // NSA_HPG
        qaug_ref[0:LANES, h * T:(h + 1) * T] = qt_ref[LANES * h:LANES * (h + 1), :]
        qaug_ref[LANES:LANES + MAX_SLC, h * T:(h + 1) * T] = sel_ref[0, MAX_SLC * g:MAX_SLC * (g + 1), :]

    key = lax.broadcasted_iota(jnp.int32, (T, NSA_HEADS * T), 0)
    qry = lax.broadcasted_iota(jnp.int32, (T, NSA_HEADS * T), 1) & (T - 1)
    causal = key <= qry
    st_s = (ms_ref, ls_ref, accs_ref)
    st_w = (mw_ref, lw_ref, accw_ref)
    _online_init(*st_s)
    _online_init(*st_w)

    n_far = jnp.maximum(j - 1, 0)

    def far_body(c, carry):
        r0 = pl.multiple_of(c * (2 * T), 2 * T)
        _online_chunk(*st_s, kaug_ref[pl.ds(r0, 2 * T), :], qaug_ref, 2 * LANES, vs_ref[:, pl.ds(r0, 2 * T)])
        return carry

    lax.fori_loop(0, lax.shift_right_logical(n_far, 1), far_body, 0)

    @pl.when((n_far & 1) == 1)
    def _():
        r0 = pl.multiple_of((n_far - 1) * T, T)
        _online_chunk(*st_s, kaug_ref[pl.ds(r0, T), :], qaug_ref, 2 * LANES, vs_ref[:, pl.ds(r0, T)])
    for d in (1, 0):
        c = j - d
        r0 = pl.multiple_of(jnp.maximum(c, 0) * T, T)
        ok = causal if d == 0 else key < jnp.where(c >= 0, T, -1)
        _online_chunk(*st_s, kaug_ref[pl.ds(r0, T), :], qaug_ref, 2 * LANES, vs_ref[:, pl.ds(r0, T)],
                      bias_ref.at[d], ok)

    n_win = WINDOW // T
    for d in range(n_win + 1):
        c = j - d
        r0 = pl.multiple_of(jnp.maximum(c, 0) * T, T)
        in_range = key < jnp.where(c >= 0, T, -1)
        if d == 0:
            ok = causal
        elif d == n_win:
            ok = (key > qry) & in_range
        else:
            ok = in_range
        _online_chunk(*st_w, kw_ref[0, pl.ds(r0, T), :], qaug_ref, LANES, vw_ref[:, pl.ds(r0, T)],
                      bias_ref.at[d] if d <= 1 else None, ok)

    o_sel = accs_ref[...] / ls_ref[...]
    o_win = accw_ref[...] / lw_ref[...]
    gates = jax.nn.sigmoid(gt_ref[...])
    lane = lax.broadcasted_iota(jnp.int32, (T, LANES), 1)
    heads = []
    for h in range(NSA_HEADS):
        cols = slice(h * T, (h + 1) * T)
        o_t = (gates[3 * h:3 * h + 1, :] * ocmp_ref[LANES * h:LANES * (h + 1), :]
               + gates[3 * h + 1:3 * h + 2, :] * o_sel[:, cols]
               + gates[3 * h + 2:3 * h + 3, :] * o_win[:, cols])
        heads.append(o_t.T)
    lo = lane < HEAD_DIM
    o_ref[0, :, 0:LANES] = jnp.where(lo, heads[0], pltpu.roll(heads[1], HEAD_DIM, 1))
    o_ref[0, :, LANES:2 * LANES] = jnp.where(lo, heads[2], heads[3])
    o_ref[0, :, 2 * LANES:3 * LANES] = jnp.where(lo, pltpu.roll(heads[4], HEAD_DIM, 1), heads[5])


def _nsa_attend(q_t, sel_t, ocmp_t, g_t, ks, kw, vs_t, vw_t, bias_t, onehot):
    b, s, _ = ks.shape
    T = T_NSA
    nq = s // T
    R = NSA_HEADS * T
    cols = lambda r: pl.BlockSpec((r, T), lambda i, j: (0, i * nq + j))
    row_b = lambda a: pl.BlockSpec((a.shape[0], s), lambda i, j: (0, i))
    per_b = lambda a: pl.BlockSpec((1,) + a.shape[1:], lambda i, j: (i, 0, 0))
    full = lambda a: pl.BlockSpec(a.shape, lambda i, j: (0,) * a.ndim)
    stat = [pltpu.VMEM((1, R), F32), pltpu.VMEM((1, R), F32), pltpu.VMEM((LANES, R), F32)]
    return pl.pallas_call(
        _nsa_attend_kernel,
        grid=(b, nq),
        in_specs=[cols(NSA_HEADS * LANES), pl.BlockSpec((1, NSA_KV_GROUPS * MAX_SLC, T), lambda i, j: (i, 0, j)),
                  cols(NSA_HEADS * LANES), cols(LANES), per_b(ks), per_b(kw), row_b(vs_t), row_b(vw_t),
                  full(bias_t), full(onehot)],
        out_specs=pl.BlockSpec((1, T, D_NSA), lambda i, j: (i, j, 0)),
        out_shape=jax.ShapeDtypeStruct((b, s, D_NSA), F32),
        scratch_shapes=[pltpu.VMEM((s, 2 * LANES), BF16), pltpu.VMEM((2 * LANES, R), BF16)] + stat + stat,
        compiler_params=_params("parallel", "arbitrary"), name="nsa_attend",
    )(q_t, sel_t, ocmp_t, g_t, ks, kw, vs_t, vw_t, bias_t, onehot)


def _sb_kernel(qt_ref, k_ref, vt_ref, tri_ref, o_ref, carry_ref, acc_ref):
    T = T_SB
    j = pl.program_id(1)
    n_pair = SB_HEADS // 2
    key = lax.broadcasted_iota(jnp.int32, (T, 2 * T), 0)
    qry = lax.broadcasted_iota(jnp.int32, (T, 2 * T), 1) & (T - 1)
    strict = key < qry
    carry_ref[...] = jnp.zeros(carry_ref.shape, F32)
    acc_ref[...] = jnp.zeros(acc_ref.shape, F32)

    def tiles(cs, mask):
        starts = [pl.multiple_of(c * T, T) for c in cs]
        units = [(ci, p) for ci in range(len(cs)) for p in range(n_pair)]
        z2s, incls = [], []
        for c, p in units:
            q_pair = jnp.concatenate([qt_ref[LANES * (2 * p):LANES * (2 * p + 1), :],
                                      qt_ref[LANES * (2 * p + 1):LANES * (2 * p + 2), :]], axis=1)
            z2s.append(_dot(k_ref[0, pl.ds(starts[c], T), LANES * p:LANES * (p + 1)], q_pair))
        for z2 in z2s:
            rest = jnp.maximum(z2, 0.0) + jnp.log2(1.0 + jnp.exp2(-jnp.abs(z2)))
            if mask is not None:
                rest = jnp.where(mask, rest, 0.0)
            hi, lo = _split_bf16(rest)
            incls.append(_dot(tri_ref[...], jnp.concatenate([hi, lo], axis=0)))
        carry = [carry_ref[p] for p in range(n_pair)]
        acc = [acc_ref[p] for p in range(n_pair)]
        for (c, p), z2, incl in zip(units, z2s, incls):
            a = jnp.exp2(z2 - incl - carry[p])
            if mask is not None:
                a = jnp.where(mask, a, 0.0)
            acc[p] = acc[p] + _dot(vt_ref[LANES * p:LANES * (p + 1), pl.ds(starts[c], T)], a.astype(BF16))
            carry[p] = carry[p] + incl[0:1, :]
        for p in range(n_pair):
            carry_ref[p] = carry[p]
            acc_ref[p] = acc[p]

    tiles([j], strict)

    def back_two(i, carry):
        c = j - 1 - 2 * i
        tiles([c, c - 1], None)
        return carry

    lax.fori_loop(0, lax.shift_right_logical(j, 1), back_two, 0)

    @pl.when((j & 1) == 1)
    def _():
        tiles([0], None)
    row = lax.broadcasted_iota(jnp.int32, (LANES, T), 0)
    for p in range(n_pair):
        acc = acc_ref[p]
        o_ref[0, :, LANES * p:LANES * (p + 1)] = jnp.where(row < HEAD_DIM, acc[:, 0:T], acc[:, T:2 * T]).T


def _sb_attention(q_t, k, v_t, tri_t):
    b, s, _ = k.shape
    T = T_SB
    nq = s // T
    n_pair = SB_HEADS // 2
    return pl.pallas_call(
        _sb_kernel,
        grid=(b, nq),
        in_specs=[pl.BlockSpec((SB_HEADS * LANES, T), lambda i, j: (0, i * nq + j)),
                  pl.BlockSpec((1, s, D_SB), lambda i, j: (i, 0, 0)),
                  pl.BlockSpec((D_SB, s), lambda i, j: (0, i)),
                  pl.BlockSpec(tri_t.shape, lambda i, j: (0, 0))],
        out_specs=pl.BlockSpec((1, T, D_SB), lambda i, j: (i, j, 0)),
        out_shape=jax.ShapeDtypeStruct((b, s, D_SB), F32),
        scratch_shapes=[pltpu.VMEM((n_pair, 1, 2 * T), F32), pltpu.VMEM((n_pair, LANES, 2 * T), F32)],
        compiler_params=_params("parallel", "parallel"), name="sb_attention",
    )(q_t, k, v_t, tri_t)


def _hgrn_kernel(q_ref, f_ref, i_ref, lb_ref, nw_ref, ones_ref, bd_ref, o_ref,
                 st_ref, qj_ref, kj_ref, bj_ref, vj_ref, qd_ref, kd_ref, oc_ref, x_ref, u_ref, sb_ref, dec_ref):
    C = HG_SUB
    n_blk = T_HG // C

    @pl.when(pl.program_id(2) == 0)
    def _():
        st_ref[...] = jnp.zeros(st_ref.shape, F32)

    lb = lb_ref[...]
    c_floor = jnp.log(jnp.maximum(lb, LB_FLOOR))
    c_rest = jnp.log1p(-lb)
    one_m_lb = 1.0 - lb
    b_run = jnp.zeros((n_blk, LANES), F32)
    for j in range(C):
        fj = f_ref[0, pl.ds(j, n_blk, stride=C), :]
        log_sig = -_softplus(-fj)
        a, bb = c_floor, c_rest + log_sig
        log_f = jnp.maximum(a, bb) + jnp.log1p(jnp.exp(-jnp.abs(a - bb)))
        b_run = b_run + log_f
        bj_ref[j] = b_run
        kj_ref[j] = one_m_lb * jax.nn.sigmoid(-fj)
        qj_ref[j] = q_ref[0, pl.ds(j, n_blk, stride=C), :]
        vj_ref[j] = i_ref[0, pl.ds(j, n_blk, stride=C), :]
    b_last = bj_ref[C - 1]
    for j in range(C):
        bj = bj_ref[j]
        qd_ref[pl.ds(j, n_blk, stride=C), :] = qj_ref[j] * jnp.exp(bj)
        kd_ref[pl.ds(j, n_blk, stride=C), :] = kj_ref[j] * jnp.exp(b_last - bj)

    def kv_products(blk, carry):
        r0 = pl.multiple_of(blk * C, C)
        kd = kd_ref[pl.ds(r0, C), :].astype(BF16)
        vv = i_ref[0, pl.ds(r0, C), :].astype(BF16)
        u_ref[blk] = bd_ref[...] * _dot_tn(vv, kd)
        return carry

    lax.fori_loop(0, n_blk, kv_products, 0, unroll=HG_UNROLL)
    dec_ref[...] = jnp.exp(b_last)

    def scan(blk, st):
        sb_ref[blk] = st.astype(BF16)
        return st * dec_ref[pl.ds(blk, 1), :] + u_ref[blk]

    st_ref[...] = lax.fori_loop(0, n_blk, scan, st_ref[...], unroll=HG_UNROLL)

    def outputs(blk, carry):
        r0 = pl.multiple_of(blk * C, C)
        oc_ref[pl.ds(r0, C), :] = _dot_nt(qd_ref[pl.ds(r0, C), :].astype(BF16), sb_ref[blk])
        return carry

    lax.fori_loop(0, n_blk, outputs, 0, unroll=HG_UNROLL)

    nw = nw_ref[...]
    for j in range(C):
        oj = oc_ref[pl.ds(j, n_blk, stride=C), :]
        qj = qj_ref[j]
        bj = bj_ref[j]
        for jp in range(j + 1):
            x = qj * kj_ref[jp] * jnp.exp(bj - bj_ref[jp])
            x_ref[jp * n_blk:(jp + 1) * n_blk, :] = x.astype(BF16)
        att = _dot(x_ref[0:(j + 1) * n_blk, :], ones_ref[...])
        for jp in range(j + 1):
            oj = oj + att[jp * n_blk:(jp + 1) * n_blk] * vj_ref[jp]
        hi, lo = _split_bf16(oj * oj)
        ms = (_dot(hi, ones_ref[...]) + _dot(lo, ones_ref[...])) * (1.0 / HEAD_DIM)
        o_ref[0, pl.ds(j, n_blk, stride=C), :] = oj * lax.rsqrt(ms + RMS_EPS) * nw


def _hgrn(q, f, i, lb, nw, ones_bd, bd_mask):
    b, s, _ = q.shape
    n_blk = T_HG // HG_SUB
    tile = pl.BlockSpec((1, T_HG, LANES), lambda bi, pi, ti: (bi, ti, pi))
    vec = pl.BlockSpec((1, LANES), lambda bi, pi, ti: (0, pi))
    full = lambda a: pl.BlockSpec(a.shape, lambda bi, pi, ti: (0,) * a.ndim)
    jm = pltpu.VMEM((HG_SUB, n_blk, LANES), F32)
    nat = pltpu.VMEM((T_HG, LANES), F32)
    return pl.pallas_call(
        _hgrn_kernel,
        grid=(b, D_HG // LANES, s // T_HG),
        in_specs=[tile, tile, tile, vec, vec, full(ones_bd), full(bd_mask)],
        out_specs=tile,
        out_shape=jax.ShapeDtypeStruct((b, s, D_HG), F32),
        scratch_shapes=[pltpu.VMEM((LANES, LANES), F32), jm, jm, jm, jm, nat, nat, nat,
                        pltpu.VMEM((T_HG, LANES), BF16), pltpu.VMEM((n_blk, LANES, LANES), F32),
                        pltpu.VMEM((n_blk, LANES, LANES), BF16), pltpu.VMEM((n_blk, LANES), F32)],
        compiler_params=_params("parallel", "parallel", "arbitrary"), name="hgrn2",
    )(q, f, i, lb, nw, ones_bd, bd_mask)


def _out_kernel(onsa_ref, osb_ref, ohg_ref, z_ref, x_ref, wn_ref, ws_ref, wh_ref, g_ref, b_ref, o_ref):
    z = z_ref[...]
    sz = z * jax.nn.sigmoid(z)
    y = _dot((onsa_ref[...] * sz[:, 0:D_NSA]).astype(BF16), wn_ref[...])
    y += _dot((osb_ref[...] * sz[:, D_NSA:D_NSA + D_SB]).astype(BF16), ws_ref[...])
    y += _dot((ohg_ref[...] * sz[:, D_NSA + D_SB:D_MIX]).astype(BF16), wh_ref[...])
    v = ALPHA * x_ref[...] + y
    mu = jnp.mean(v, axis=-1, keepdims=True)
    vc = v - mu
    var = jnp.mean(vc * vc, axis=-1, keepdims=True)
    o_ref[...] = vc * lax.rsqrt(var + LN_EPS) * g_ref[...] + b_ref[...]


def _out_proj(o_nsa, o_sb, o_hg, z_all, x2d, wn, ws, wh, g, bvec):
    m = x2d.shape[0]
    rows = lambda w: pl.BlockSpec((T_PROJ, w), lambda i: (i, 0))
    full = lambda a: pl.BlockSpec(a.shape, lambda i: (0, 0))
    return pl.pallas_call(
        _out_kernel,
        grid=(m // T_PROJ,),
        in_specs=[rows(D_NSA), rows(D_SB), rows(D_HG), rows(D_MIX), rows(D_MODEL),
                  full(wn), full(ws), full(wh), full(g), full(bvec)],
        out_specs=rows(D_MODEL),
        out_shape=jax.ShapeDtypeStruct((m, D_MODEL), F32),
        compiler_params=_params("parallel"), name="out_proj_norm",
    )(o_nsa, o_sb, o_hg, z_all, x2d, wn, ws, wh, g, bvec)


def _t5_bucket_np(rel):
    n = np.maximum(rel, 0)
    max_exact = NUM_BUCKETS // 2
    large = max_exact + (np.log(np.maximum(n, 1).astype(np.float32) / max_exact)
                         / math.log(MAX_DISTANCE / max_exact) * (NUM_BUCKETS - max_exact)).astype(np.int32)
    large = np.clip(large, 0, NUM_BUCKETS - 1)
    return np.where(n < max_exact, n, large).astype(np.int32)


def _bias_tables(rel_bias, s):
    tbl = ((rel_bias - rel_bias[NUM_BUCKETS - 1]) * LOG2E).astype(F32)

    def expand(rel):
        onehot = (jnp.asarray(_t5_bucket_np(rel).reshape(-1, 1)) == jnp.arange(NUM_BUCKETS)[None, :]).astype(F32)
        return jnp.dot(onehot, tbl, precision=lax.Precision.HIGHEST).reshape(rel.shape + (NSA_HEADS,))

    n_cmp_pad = s // CMP_STRIDE
    n_rel = np.arange(2 * n_cmp_pad)[:, None] - n_cmp_pad
    r = np.arange(T_SEL)[None, :]
    pbias = jnp.transpose(expand(r - CMP_STRIDE * n_rel - (CMP_BLOCK - 1)), (2, 0, 1))
    T = T_NSA
    key = np.arange(T)[:, None]
    qry = np.arange(T)[None, :]
    near = np.stack([qry - key, T + qry - key])
    bias_t = jnp.transpose(expand(near), (0, 1, 3, 2)).reshape(2, T, NSA_HEADS * T)
    return pbias, bias_t


def _static_tables(s):
    n_chunk = s // CMP_STRIDE
    cmp_start = np.arange(n_chunk) * CMP_STRIDE
    slc_start = np.arange(MAX_SLC) * SLC_BLOCK
    ovl_t = ((cmp_start[None, :] < slc_start[:, None] + SLC_BLOCK)
             & (cmp_start[None, :] + CMP_BLOCK > slc_start[:, None])
             & (cmp_start[None, :] + CMP_BLOCK <= s)).astype(np.float32)
    onehot = (np.arange(s)[:, None] // SLC_BLOCK == np.arange(LANES)[None, :]).astype(np.float32)
    tri = (np.arange(T_SB)[None, :] >= np.arange(T_SB)[:, None]).astype(np.float32)
    tri = np.concatenate([tri, tri], axis=1)
    ones_bd = np.kron(np.eye(2), np.ones((HEAD_DIM, HEAD_DIM))).astype(np.float32)
    as_bf16 = lambda a: jnp.asarray(a, dtype=BF16)
    return as_bf16(ovl_t), as_bf16(onehot), as_bf16(tri), as_bf16(ones_bd), jnp.asarray(ones_bd)


def _layer_weights(w_in_l, cmp_pos_l, w_ck1_l, w_ck2_l, w_cv1_l, w_cv2_l):
    offs = np.cumsum((0,) + SPLIT_SIZES)
    (w_q, w_kc, w_vc, w_ks, w_vs, w_kw, w_vw, w_g, w_nz,
     w_sq, w_sk, w_sv, w_sz, w_hq, w_hf, w_hi, w_hz) = [w_in_l[:, offs[i]:offs[i + 1]] for i in range(len(SPLIT_SIZES))]
    zeros = jnp.zeros((D_MODEL, HEAD_DIM), F32)
    cols = []
    for h in range(NSA_HEADS):
        wq = w_q[:, HEAD_DIM * h:HEAD_DIM * (h + 1)] * QK_SCALE2
        cols += [wq, zeros] if h // NSA_HPG == 0 else [zeros, wq]
    w_qa = jnp.concatenate(cols, axis=1)
    cols = []
    for h in range(SB_HEADS):
        wq = w_sq[:, HEAD_DIM * h:HEAD_DIM * (h + 1)] * QK_SCALE2
        cols += [wq, zeros] if h % 2 == 0 else [zeros, wq]
    w_sqa = jnp.concatenate(cols, axis=1)
    w_gp = jnp.concatenate([w_g, jnp.zeros((D_MODEL, LANES - N_GATES), F32)], axis=1)
    w_z = jnp.concatenate([w_nz, w_sz, w_hz], axis=1)
    w_nat = [w_kc, w_vc, w_ks, w_kw, w_z, w_sk, w_hq, w_hf, w_hi]
    dt_nat = [F32, F32, BF16, BF16, F32, BF16, F32, F32, F32]
    w_tr = [w_qa.T, w_vs.T, w_vw.T, w_gp.T, w_sqa.T, w_sv.T]
    dt_tr = [BF16, BF16, BF16, F32, BF16, BF16]
    w_nat = [w.astype(BF16) for w in w_nat]
    w_tr = [w.astype(BF16) for w in w_tr]

    def block_diag(w):
        z = jnp.zeros_like(w)
        return jnp.concatenate([jnp.concatenate([w, z], axis=-1), jnp.concatenate([z, w], axis=-1)], axis=-2)

    pos2 = jnp.concatenate([cmp_pos_l, cmp_pos_l], axis=1)
    w1k = block_diag(w_ck1_l.reshape(CMP_BLOCK, HEAD_DIM, CMP_HIDDEN)).astype(BF16)
    w1v = block_diag(w_cv1_l.reshape(CMP_BLOCK, HEAD_DIM, CMP_HIDDEN)).astype(BF16)
    w2k = block_diag(w_ck2_l).astype(BF16)
    w2v_t = block_diag(w_cv2_l).T.astype(BF16)
    return (w_nat, dt_nat, w_tr, dt_tr), (pos2, w1k, w2k, w1v, w2v_t)


def kernel(x, w_in, cmp_pos, w_ck1, w_ck2, w_cv1, w_cv2, hg_lb, hg_norm_w, w_out, ln_g, ln_b, rel_bias):
    b, s, d = x.shape
    assert d == D_MODEL and s % T_HG == 0 and s // SLC_BLOCK <= MAX_SLC and s >= WINDOW + T_NSA
    lb_w = jax.nn.softmax(hg_lb.astype(F32), axis=0)
    lb_all = jnp.cumsum(lb_w, axis=0) - lb_w[0]
    pbias, bias_t = _bias_tables(rel_bias, s)
    ovl_t, onehot, tri, ones_bd, bd_mask = _static_tables(s)

    x2d = x.reshape(b * s, d)
    for l in range(DEPTH):
        proj_w, cmp_w = _layer_weights(w_in[l], cmp_pos[l], w_ck1[l], w_ck2[l], w_cv1[l], w_cv2[l])
        nat, (q_t, vs_t, vw_t, g_t, sq_t, sv_t) = _inproj(x2d, *proj_w)
        kc_src, vc_src, ks, kw, z_all, sk, hq, hf, hi = [o.reshape(b, s, o.shape[-1]) for o in nat]
        kc, vc_t = _compress(kc_src, vc_src, *cmp_w)
        ocmp_t, sel_t = _nsa_select(q_t, kc, vc_t, pbias, ovl_t)
        o_nsa = _nsa_attend(q_t, sel_t, ocmp_t, g_t, ks, kw, vs_t, vw_t, bias_t, onehot)
        o_sb = _sb_attention(sq_t, sk, sv_t, tri)
        o_hg = _hgrn(hq, hf, hi, lb_all[l][None, :], hg_norm_w[l][None, :], ones_bd, bd_mask)
        wo = w_out[l].astype(BF16)
        x2d = _out_proj(o_nsa.reshape(b * s, D_NSA), o_sb.reshape(b * s, D_SB), o_hg.reshape(b * s, D_HG),
                        z_all.reshape(b * s, D_MIX), x2d,
                        wo[0:D_NSA], wo[D_NSA:D_NSA + D_SB], wo[D_NSA + D_SB:D_MIX],
                        ln_g[l][None, :], ln_b[l][None, :])
    return x2d.reshape(b, s, d)
```

```python
import functools
import math

import numpy as np
import jax
import jax.numpy as jnp
from jax import lax
from jax.experimental import pallas as pl
from jax.experimental.pallas import tpu as pltpu

F32 = jnp.float32
BF16 = jnp.bfloat16

D_MODEL = 1024
DEPTH = 2
HEAD_DIM = 64
LANES = 128
NSA_HEADS = 6
NSA_KV_GROUPS = 2
NSA_HPG = NSA_HEADS // NSA_KV_GROUPS
CMP_BLOCK = 32
CMP_STRIDE = 16
CMP_HIDDEN = 2 * HEAD_DIM
SLC_BLOCK = 64
SLC_TOPN = 16
MAX_SLC = 64
WINDOW = 512
FORCE_BONUS = 1000.0
NEG_BIG = -1e30
LB_FLOOR = 1e-30
SB_HEADS = 4
HG_HEADS = 6
HG_SUB = 16
HG_UNROLL = 16
NUM_BUCKETS = 32
MAX_DISTANCE = 128
D_NSA = NSA_HEADS * HEAD_DIM
D_KV = NSA_KV_GROUPS * HEAD_DIM
D_SB = SB_HEADS * HEAD_DIM
D_HG = HG_HEADS * HEAD_DIM
D_MIX = D_NSA + D_SB + D_HG
N_GATES = NSA_HEADS * 3
SPLIT_SIZES = (D_NSA, D_KV, D_KV, D_KV, D_KV, D_KV, D_KV, N_GATES, D_NSA,
               D_SB, D_SB, D_SB, D_SB, D_HG, D_HG, D_HG, D_HG)
ALPHA = (2 * DEPTH) ** 0.25
LN_EPS = 1e-5
RMS_EPS = 1e-6
LOG2E = math.log2(math.e)
QK_SCALE2 = LOG2E / math.sqrt(HEAD_DIM)

T_SEL = 128
T_NSA = 256
T_SB = 256
T_HG = 1024
T_PROJ = 512
VMEM_LIMIT = 56 * 1024 * 1024

_NT = (((1,), (1,)), ((), ()))
_TN = (((0,), (0,)), ((), ()))


def _dot(a, b):
    return jnp.dot(a, b, preferred_element_type=F32)


def _dot_nt(a, b):
    return lax.dot_general(a, b, _NT, preferred_element_type=F32)


def _dot_tn(a, b):
    return lax.dot_general(a, b, _TN, preferred_element_type=F32)


def _softplus(x):
    return jnp.maximum(x, 0.0) + jnp.log1p(jnp.exp(-jnp.abs(x)))


def _split_bf16(x):
    hi = x.astype(BF16)
    lo = (x - hi.astype(F32)).astype(BF16)
    return hi, lo


def _params(*sem):
    return pltpu.CompilerParams(dimension_semantics=sem, vmem_limit_bytes=VMEM_LIMIT)


def _inproj_kernel(x_ref, *refs, n_nat, n_tr):
    xb = x_ref[...].astype(BF16)
    n_in = n_nat + n_tr
    for w_ref, o_ref in zip(refs[:n_nat], refs[n_in:n_in + n_nat]):
        o_ref[...] = _dot(xb, w_ref[...]).astype(o_ref.dtype)
    for w_ref, o_ref in zip(refs[n_nat:n_in], refs[n_in + n_nat:]):
        o_ref[...] = _dot_nt(w_ref[...], xb).astype(o_ref.dtype)


def _inproj(x2d, w_nat, dt_nat, w_tr, dt_tr):
    m = x2d.shape[0]
    in_specs = [pl.BlockSpec((T_PROJ, D_MODEL), lambda i: (i, 0))]
    in_specs += [pl.BlockSpec(w.shape, lambda i: (0, 0)) for w in w_nat + w_tr]
    out_specs = [pl.BlockSpec((T_PROJ, w.shape[1]), lambda i: (i, 0)) for w in w_nat]
    out_specs += [pl.BlockSpec((w.shape[0], T_PROJ), lambda i: (0, i)) for w in w_tr]
    out_shape = [jax.ShapeDtypeStruct((m, w.shape[1]), dt) for w, dt in zip(w_nat, dt_nat)]
    out_shape += [jax.ShapeDtypeStruct((w.shape[0], m), dt) for w, dt in zip(w_tr, dt_tr)]
    outs = pl.pallas_call(
        functools.partial(_inproj_kernel, n_nat=len(w_nat), n_tr=len(w_tr)),
        grid=(m // T_PROJ,),
        in_specs=in_specs, out_specs=out_specs, out_shape=out_shape,
        compiler_params=_params("parallel"), name="inproj",
    )(x2d, *w_nat, *w_tr)
    return outs[:len(w_nat)], outs[len(w_nat):]


def _compress_kernel(ksrc_ref, vsrc_ref, pos_ref, w1k_ref, w2k_ref, w1v_ref, w2v_ref,
                     kc_ref, vc_ref, *, n_chunk):
    def hidden(src_ref, w1_ref):
        top = jnp.zeros((n_chunk, 2 * CMP_HIDDEN), F32)
        bot = jnp.zeros((n_chunk, 2 * CMP_HIDDEN), F32)
        for p in range(CMP_STRIDE):
            xp = src_ref[0, pl.ds(p, n_chunk, stride=CMP_STRIDE), :]
            top += _dot((xp + pos_ref[p:p + 1, :]).astype(BF16), w1_ref[p])
            q = CMP_STRIDE + p
            bot += _dot((xp + pos_ref[q:q + 1, :]).astype(BF16), w1_ref[q])
        hid = top + pltpu.roll(bot, n_chunk - 1, 0)
        return jax.nn.gelu(hid).astype(BF16)

    kc_ref[0] = _dot(hidden(ksrc_ref, w1k_ref), w2k_ref[...]).astype(kc_ref.dtype)
    vc_ref[0] = _dot_nt(w2v_ref[...], hidden(vsrc_ref, w1v_ref)).astype(vc_ref.dtype)


def _compress(kc_src, vc_src, pos2, w1k, w2k, w1v, w2v_t):
    b, s, _ = kc_src.shape
    n_chunk = s // CMP_STRIDE
    full = lambda a: pl.BlockSpec(a.shape, lambda i: (0,) * a.ndim)
    src = pl.BlockSpec((1, s, D_KV), lambda i: (i, 0, 0))
    return pl.pallas_call(
        functools.partial(_compress_kernel, n_chunk=n_chunk),
        grid=(b,),
        in_specs=[src, src, full(pos2), full(w1k), full(w2k), full(w1v), full(w2v_t)],
        out_specs=[pl.BlockSpec((1, n_chunk, D_KV), lambda i: (i, 0, 0)),
                   pl.BlockSpec((1, D_KV, n_chunk), lambda i: (i, 0, 0))],
        out_shape=[jax.ShapeDtypeStruct((b, n_chunk, D_KV), BF16),
                   jax.ShapeDtypeStruct((b, D_KV, n_chunk), BF16)],
        compiler_params=_params("parallel"), name="nsa_compress",
    )(kc_src, vc_src, pos2, w1k, w2k, w1v, w2v_t)


def _nsa_select_kernel(qt_ref, kc_ref, vct_ref, pbias_ref, ovl_ref, ocmp_ref, sel_ref, *, n_cmp_pad):
    T = T_SEL
    j = pl.program_id(1)
    t0 = j * T
    n_idx = lax.broadcasted_iota(jnp.int32, (n_cmp_pad, T), 0)
    tok_c = t0 + lax.broadcasted_iota(jnp.int32, (n_cmp_pad, T), 1)
    mask_c = tok_c >= CMP_STRIDE * n_idx + (CMP_BLOCK - 1)
    off = pl.multiple_of(n_cmp_pad - (T // CMP_STRIDE) * j, 8)
    psums = [None] * NSA_KV_GROUPS
    for h in range(NSA_HEADS):
        g = h // NSA_HPG
        qh = qt_ref[HEAD_DIM * h:HEAD_DIM * (h + 1), :]
        zq = jnp.zeros_like(qh)
        qh = jnp.concatenate([qh, zq] if g == 0 else [zq, qh], axis=0)
        s = _dot(kc_ref[0], qh) + pbias_ref[h, pl.ds(off, n_cmp_pad), :]
        s = jnp.where(mask_c, s, NEG_BIG)
        m = jnp.max(s, axis=0, keepdims=True)
        p = jnp.where(mask_c, jnp.exp2(s - m), 0.0)
        l = jnp.sum(p, axis=0, keepdims=True)
        p = p / jnp.where(l > 0.0, l, 1.0)
        o_both = _dot(vct_ref[0], p.astype(BF16))
        ocmp_ref[HEAD_DIM * h:HEAD_DIM * (h + 1), :] = o_both[HEAD_DIM * g:HEAD_DIM * (g + 1)]
        psums[g] = p if psums[g] is None else psums[g] + p

    jblk = lax.broadcasted_iota(jnp.int32, (MAX_SLC, T), 0)
    tok = t0 + lax.broadcasted_iota(jnp.int32, (MAX_SLC, T), 1)
    cur = lax.shift_right_logical(tok, 6)
    forced = (jblk == 0) | (jblk == cur) | (jblk == cur - 1)
    valid = jblk * SLC_BLOCK <= tok
    jsub = lax.broadcasted_iota(jnp.int32, (8, T), 0)
    for g in range(NSA_KV_GROUPS):
        imp = _dot(ovl_ref[...], psums[g].astype(BF16))
        score = jnp.where(valid, imp + jnp.where(forced, FORCE_BONUS, 0.0), NEG_BIG)
        blocks = [score[8 * rb:8 * rb + 8] for rb in range(MAX_SLC // 8)]
        ranks = [jnp.zeros((8, T), F32) for _ in blocks]
        for jp in range(MAX_SLC):
            other = score[jp:jp + 1, :]
            for rb, blk in enumerate(blocks):
                ge = jnp.where(other >= blk, 1.0, 0.0)
                gt = jnp.where(other > blk, 1.0, 0.0)
                if 8 * rb > jp:
                    inc = ge
                elif 8 * rb + 7 < jp:
                    inc = gt
                else:
                    inc = jnp.where(jsub > jp - 8 * rb, ge, gt)
                ranks[rb] = ranks[rb] + inc
        rank = jnp.concatenate(ranks, axis=0)
        sel_ref[0, MAX_SLC * g:MAX_SLC * (g + 1), :] = jnp.where(rank < float(SLC_TOPN), 0.0, NEG_BIG).astype(BF16)


def _nsa_select(q_t, kc, vc_t, pbias, ovl_t):
    b, n_cmp_pad, _ = kc.shape
    m = q_t.shape[1]
    nq = m // b // T_SEL
    cols = lambda r: pl.BlockSpec((r, T_SEL), lambda i, j: (0, i * nq + j))
    per_b = lambda a: pl.BlockSpec((1,) + a.shape[1:], lambda i, j: (i, 0, 0))
    full = lambda a: pl.BlockSpec(a.shape, lambda i, j: (0,) * a.ndim)
    return pl.pallas_call(
        functools.partial(_nsa_select_kernel, n_cmp_pad=n_cmp_pad),
        grid=(b, nq),
        in_specs=[cols(D_NSA), per_b(kc), per_b(vc_t), full(pbias), full(ovl_t)],
        out_specs=[cols(D_NSA), pl.BlockSpec((1, NSA_KV_GROUPS * MAX_SLC, T_SEL), lambda i, j: (i, 0, j))],
        out_shape=[jax.ShapeDtypeStruct((D_NSA, m), F32),
                   jax.ShapeDtypeStruct((b, NSA_KV_GROUPS * MAX_SLC, m // b), BF16)],
        compiler_params=_params("parallel", "parallel"), name="nsa_select",
    )(q_t, kc, vc_t, pbias, ovl_t)


def _online_init(m_ref, l_ref, acc_ref):
    m_ref[...] = jnp.full(m_ref.shape, NEG_BIG, F32)
    l_ref[...] = jnp.zeros(l_ref.shape, F32)
    acc_ref[...] = jnp.zeros(acc_ref.shape, F32)


def _online_chunk(m_ref, l_ref, acc_ref, k_tile, q_ref, q_rows, v_t, bias_ref=None, ok=None):
    s = _dot(k_tile, q_ref[0:q_rows, :])
    if bias_ref is not None:
        s = s + bias_ref[...]
    if ok is not None:
        s = jnp.where(ok, s, NEG_BIG)
    m = m_ref[...]
    m_new = jnp.maximum(m, jnp.max(s, axis=0, keepdims=True))
    alpha = jnp.exp2(m - m_new)
    p = jnp.exp2(s - m_new)
    m_ref[...] = m_new
    l_ref[...] = alpha * l_ref[...] + jnp.sum(p, axis=0, keepdims=True)
    acc_ref[...] = alpha * acc_ref[...] + _dot(v_t, p.astype(BF16))


def _nsa_attend_kernel(qt_ref, sel_ref, ocmp_ref, gt_ref, ks_ref, kw_ref, vs_ref, vw_ref, bias_ref, onehot_ref,
                       o_ref, kaug_ref, qaug_ref, ms_ref, ls_ref, accs_ref, mw_ref, lw_ref, accw_ref):
    T = T_NSA
    j = pl.program_id(1)

    @pl.when(j == 0)
    def _():
        kaug_ref[:, 0:LANES] = ks_ref[0]
        kaug_ref[:, LANES:2 * LANES] = onehot_ref[...]
        qaug_ref[...] = jnp.zeros(qaug_ref.shape, BF16)

    for h in range(NSA_HEADS):
        g = h // NSA_HPG
        qaug_ref[HEAD_DIM * g:HEAD_DIM * (g + 1), h * T:(h + 1) * T] = qt_ref[HEAD_DIM * h:HEAD_DIM * (h + 1), :]
        qaug_ref[LANES:LANES + MAX_SLC, h * T:(h + 1) * T] = sel_ref[0, MAX_SLC * g:MAX_SLC * (g + 1), :]

    key = lax.broadcasted_iota(jnp.int32, (T, NSA_HEADS * T), 0)
    qry = lax.broadcasted_iota(jnp.int32, (T, NSA_HEADS * T), 1) & (T - 1)
    causal = key <= qry
    st_s = (ms_ref, ls_ref, accs_ref)
    st_w = (mw_ref, lw_ref, accw_ref)
    _online_init(*st_s)
    _online_init(*st_w)

    n_far = jnp.maximum(j - 1, 0)

    def far_body(c, carry):
        r0 = pl.multiple_of(c * (2 * T), 2 * T)
        _online_chunk(*st_s, kaug_ref[pl.ds(r0, 2 * T), :], qaug_ref, 2 * LANES, vs_ref[:, pl.ds(r0, 2 * T)])
        return carry

    lax.fori_loop(0, lax.shift_right_logical(n_far, 1), far_body, 0)

    @pl.when((n_far & 1) == 1)
    def _():
        r0 = pl.multiple_of((n_far - 1) * T, T)
        _online_chunk(*st_s, kaug_ref[pl.ds(r0, T), :], qaug_ref, 2 * LANES, vs_ref[:, pl.ds(r0, T)])
    for d in (1, 0):
        c = j - d
        r0 = pl.multiple_of(jnp.maximum(c, 0) * T, T)
        ok = causal if d == 0 else key < jnp.where(c >= 0, T, -1)
        _online_chunk(*st_s, kaug_ref[pl.ds(r0, T), :], qaug_ref, 2 * LANES, vs_ref[:, pl.ds(r0, T)],
                      bias_ref.at[d], ok)

    n_win = WINDOW // T
    for d in range(n_win + 1):
        c = j - d
        r0 = pl.multiple_of(jnp.maximum(c, 0) * T, T)
        in_range = key < jnp.where(c >= 0, T, -1)
        if d == 0:
            ok = causal
        elif d == n_win:
            ok = (key > qry) & in_range
        else:
            ok = in_range
        _online_chunk(*st_w, kw_ref[0, pl.ds(r0, T), :], qaug_ref, LANES, vw_ref[:, pl.ds(r0, T)],
                      bias_ref.at[d] if d <= 1 else None, ok)

    o_sel = accs_ref[...] / ls_ref[...]
    o_win = accw_ref[...] / lw_ref[...]
    gates = jax.nn.sigmoid(gt_ref[...])
    heads = []
    for h in range(NSA_HEADS):
        cols = slice(h * T, (h + 1) * T)
        rows = slice(HEAD_DIM * (h // NSA_HPG), HEAD_DIM * (h // NSA_HPG + 1))
        heads.append(gates[3 * h:3 * h + 1, :] * ocmp_ref[HEAD_DIM * h:HEAD_DIM * (h + 1), :]
                     + gates[3 * h + 1:3 * h + 2, :] * o_sel[rows, cols]
                     + gates[3 * h + 2:3 * h + 3, :] * o_win[rows, cols])
    o_ref[0] = jnp.concatenate(heads, axis=0).T.astype(o_ref.dtype)


def _nsa_attend(q_t, sel_t, ocmp_t, g_t, ks, kw, vs_t, vw_t, bias_t, onehot):
    b, s, _ = ks.shape
    T = T_NSA
    nq = s // T
    R = NSA_HEADS * T
    cols = lambda r: pl.BlockSpec((r, T), lambda i, j: (0, i * nq + j))
    row_b = lambda a: pl.BlockSpec((a.shape[0], s), lambda i, j: (0, i))
    per_b = lambda a: pl.BlockSpec((1,) + a.shape[1:], lambda i, j: (i, 0, 0))
    full = lambda a: pl.BlockSpec(a.shape, lambda i, j: (0,) * a.ndim)
    stat = [pltpu.VMEM((1, R), F32), pltpu.VMEM((1, R), F32), pltpu.VMEM((LANES, R), F32)]
    return pl.pallas_call(
        _nsa_attend_kernel,
        grid=(b, nq),
        in_specs=[cols(D_NSA), pl.BlockSpec((1, NSA_KV_GROUPS * MAX_SLC, T), lambda i, j: (i, 0, j)),
                  cols(D_NSA), cols(LANES), per_b(ks), per_b(kw), row_b(vs_t), row_b(vw_t),
                  full(bias_t), full(onehot)],
        out_specs=pl.BlockSpec((1, T, D_NSA), lambda i, j: (i, j, 0)),
        out_shape=jax.ShapeDtypeStruct((b, s, D_NSA), BF16),
        scratch_shapes=[pltpu.VMEM((s, 2 * LANES), BF16), pltpu.VMEM((2 * LANES, R), BF16)] + stat + stat,
        compiler_params=_params("parallel", "arbitrary"), name="nsa_attend",
    )(q_t, sel_t, ocmp_t, g_t, ks, kw, vs_t, vw_t, bias_t, onehot)


def _sb_kernel(qt_ref, k_ref, vt_ref, tri_ref, o_ref, carry_ref, acc_ref):
    T = T_SB
    j = pl.program_id(1)
    n_pair = SB_HEADS // 2
    key = lax.broadcasted_iota(jnp.int32, (T, 2 * T), 0)
    qry = lax.broadcasted_iota(jnp.int32, (T, 2 * T), 1) & (T - 1)
    strict = key < qry
    carry_ref[...] = jnp.zeros(carry_ref.shape, F32)
    acc_ref[...] = jnp.zeros(acc_ref.shape, F32)

    def tiles(cs, mask):
        starts = [pl.multiple_of(c * T, T) for c in cs]
        units = [(ci, p) for ci in range(len(cs)) for p in range(n_pair)]
        z2s, incls = [], []
        for c, p in units:
            q_even = qt_ref[HEAD_DIM * (2 * p):HEAD_DIM * (2 * p + 1), :]
            q_odd = qt_ref[HEAD_DIM * (2 * p + 1):HEAD_DIM * (2 * p + 2), :]
            zq = jnp.zeros_like(q_even)
            q_pair = jnp.concatenate([jnp.concatenate([q_even, zq], axis=0),
                                      jnp.concatenate([zq, q_odd], axis=0)], axis=1)
            z2s.append(_dot(k_ref[0, pl.ds(starts[c], T), LANES * p:LANES * (p + 1)], q_pair))
        for z2 in z2s:
            rest = jnp.maximum(z2, 0.0) + jnp.log2(1.0 + jnp.exp2(-jnp.abs(z2)))
            if mask is not None:
                rest = jnp.where(mask, rest, 0.0)
            hi, lo = _split_bf16(rest)
            incls.append(_dot(tri_ref[...], jnp.concatenate([hi, lo], axis=0)))
        carry = [carry_ref[p] for p in range(n_pair)]
        acc = [acc_ref[p] for p in range(n_pair)]
        for (c, p), z2, incl in zip(units, z2s, incls):
            a = jnp.exp2(z2 - incl - carry[p])
            if mask is not None:
                a = jnp.where(mask, a, 0.0)
            acc[p] = acc[p] + _dot(vt_ref[LANES * p:LANES * (p + 1), pl.ds(starts[c], T)], a.astype(BF16))
            carry[p] = carry[p] + incl[0:1, :]
        for p in range(n_pair):
            carry_ref[p] = carry[p]
            acc_ref[p] = acc[p]

    tiles([j], strict)

    def back_two(i, carry):
        c = j - 1 - 2 * i
        tiles([c, c - 1], None)
        return carry

    lax.fori_loop(0, lax.shift_right_logical(j, 1), back_two, 0)

    @pl.when((j & 1) == 1)
    def _():
        tiles([0], None)
    row = lax.broadcasted_iota(jnp.int32, (LANES, T), 0)
    for p in range(n_pair):
        acc = acc_ref[p]
        o_ref[0, :, LANES * p:LANES * (p + 1)] = (
            jnp.where(row < HEAD_DIM, acc[:, 0:T], acc[:, T:2 * T]).T.astype(o_ref.dtype))


def _sb_attention(q_t, k, v_t, tri_t):
    b, s, _ = k.shape
    T = T_SB
    nq = s // T
    n_pair = SB_HEADS // 2
    return pl.pallas_call(
        _sb_kernel,
        grid=(b, nq),
        in_specs=[pl.BlockSpec((D_SB, T), lambda i, j: (0, i * nq + j)),
                  pl.BlockSpec((1, s, D_SB), lambda i, j: (i, 0, 0)),
                  pl.BlockSpec((D_SB, s), lambda i, j: (0, i)),
                  pl.BlockSpec(tri_t.shape, lambda i, j: (0, 0))],
        out_specs=pl.BlockSpec((1, T, D_SB), lambda i, j: (i, j, 0)),
        out_shape=jax.ShapeDtypeStruct((b, s, D_SB), BF16),
        scratch_shapes=[pltpu.VMEM((n_pair, 1, 2 * T), F32), pltpu.VMEM((n_pair, LANES, 2 * T), F32)],
        compiler_params=_params("parallel", "parallel"), name="sb_attention",
    )(q_t, k, v_t, tri_t)


def _hgrn_kernel(q_ref, f_ref, i_ref, lb_ref, nw_ref, ones_ref, bd_ref, o_ref,
                 st_ref, qj_ref, kj_ref, bj_ref, vj_ref, qd_ref, kd_ref, oc_ref, x_ref, u_ref, sb_ref, dec_ref):
    C = HG_SUB
    n_blk = T_HG // C

    @pl.when(pl.program_id(2) == 0)
    def _():
        st_ref[...] = jnp.zeros(st_ref.shape, F32)

    lb = lb_ref[...]
    c_floor = jnp.log(jnp.maximum(lb, LB_FLOOR))
    c_rest = jnp.log1p(-lb)
    one_m_lb = 1.0 - lb
    b_run = jnp.zeros((n_blk, LANES), F32)
    for j in range(C):
        fj = f_ref[0, pl.ds(j, n_blk, stride=C), :]
        log_sig = -_softplus(-fj)
        a, bb = c_floor, c_rest + log_sig
        log_f = jnp.maximum(a, bb) + jnp.log1p(jnp.exp(-jnp.abs(a - bb)))
        b_run = b_run + log_f
        bj_ref[j] = b_run
        kj_ref[j] = one_m_lb * jax.nn.sigmoid(-fj)
        qj_ref[j] = q_ref[0, pl.ds(j, n_blk, stride=C), :]
        vj_ref[j] = i_ref[0, pl.ds(j, n_blk, stride=C), :]
    b_last = bj_ref[C - 1]
    for j in range(C):
        bj = bj_ref[j]
        qd_ref[pl.ds(j, n_blk, stride=C), :] = qj_ref[j] * jnp.exp(bj)
        kd_ref[pl.ds(j, n_blk, stride=C), :] = kj_ref[j] * jnp.exp(b_last - bj)

    def kv_products(blk, carry):
        r0 = pl.multiple_of(blk * C, C)
        kd = kd_ref[pl.ds(r0, C), :].astype(BF16)
        vv = i_ref[0, pl.ds(r0, C), :].astype(BF16)
        u_ref[blk] = bd_ref[...] * _dot_tn(vv, kd)
        return carry

    lax.fori_loop(0, n_blk, kv_products, 0, unroll=HG_UNROLL)
    dec_ref[...] = jnp.exp(b_last)

    def scan(blk, st):
        sb_ref[blk] = st.astype(BF16)
        return st * dec_ref[pl.ds(blk, 1), :] + u_ref[blk]

    st_ref[...] = lax.fori_loop(0, n_blk, scan, st_ref[...], unroll=HG_UNROLL)

    def outputs(blk, carry):
        r0 = pl.multiple_of(blk * C, C)
        oc_ref[pl.ds(r0, C), :] = _dot_nt(qd_ref[pl.ds(r0, C), :].astype(BF16), sb_ref[blk])
        return carry

    lax.fori_loop(0, n_blk, outputs, 0, unroll=HG_UNROLL)

    nw = nw_ref[...]
    for j in range(C):
        oj = oc_ref[pl.ds(j, n_blk, stride=C), :]
        qj = qj_ref[j]
        bj = bj_ref[j]
        for jp in range(j + 1):
            x = qj * kj_ref[jp] * jnp.exp(bj - bj_ref[jp])
            x_ref[jp * n_blk:(jp + 1) * n_blk, :] = x.astype(BF16)
        att = _dot(x_ref[0:(j + 1) * n_blk, :], ones_ref[...])
        for jp in range(j + 1):
            oj = oj + att[jp * n_blk:(jp + 1) * n_blk] * vj_ref[jp]
        hi, lo = _split_bf16(oj * oj)
        ms = (_dot(hi, ones_ref[...]) + _dot(lo, ones_ref[...])) * (1.0 / HEAD_DIM)
        o_ref[0, pl.ds(j, n_blk, stride=C), :] = oj * lax.rsqrt(ms + RMS_EPS) * nw


def _hgrn(q, f, i, lb, nw, ones_bd, bd_mask):
    b, s, _ = q.shape
    n_blk = T_HG // HG_SUB
    tile = pl.BlockSpec((1, T_HG, LANES), lambda bi, pi, ti: (bi, ti, pi))
    vec = pl.BlockSpec((1, LANES), lambda bi, pi, ti: (0, pi))
    full = lambda a: pl.BlockSpec(a.shape, lambda bi, pi, ti: (0,) * a.ndim)
    jm = pltpu.VMEM((HG_SUB, n_blk, LANES), F32)
    nat = pltpu.VMEM((T_HG, LANES), F32)
    return pl.pallas_call(
        _hgrn_kernel,
        grid=(b, D_HG // LANES, s // T_HG),
        in_specs=[tile, tile, tile, vec, vec, full(ones_bd), full(bd_mask)],
        out_specs=tile,
        out_shape=jax.ShapeDtypeStruct((b, s, D_HG), F32),
        scratch_shapes=[pltpu.VMEM((LANES, LANES), F32), jm, jm, jm, jm, nat, nat, nat,
                        pltpu.VMEM((T_HG, LANES), BF16), pltpu.VMEM((n_blk, LANES, LANES), F32),
                        pltpu.VMEM((n_blk, LANES, LANES), BF16), pltpu.VMEM((n_blk, LANES), F32)],
        compiler_params=_params("parallel", "parallel", "arbitrary"), name="hgrn2",
    )(q, f, i, lb, nw, ones_bd, bd_mask)


def _out_kernel(onsa_ref, osb_ref, ohg_ref, z_ref, x_ref, wn_ref, ws_ref, wh_ref, g_ref, b_ref, o_ref):
    z = z_ref[...].astype(F32)
    sz = z * jax.nn.sigmoid(z)
    y = _dot((onsa_ref[...] * sz[:, 0:D_NSA]).astype(BF16), wn_ref[...])
    y += _dot((osb_ref[...] * sz[:, D_NSA:D_NSA + D_SB]).astype(BF16), ws_ref[...])
    y += _dot((ohg_ref[...] * sz[:, D_NSA + D_SB:D_MIX]).astype(BF16), wh_ref[...])
    v = ALPHA * x_ref[...] + y
    mu = jnp.mean(v, axis=-1, keepdims=True)
    vc = v - mu
    var = jnp.mean(vc * vc, axis=-1, keepdims=True)
    o_ref[...] = vc * lax.rsqrt(var + LN_EPS) * g_ref[...] + b_ref[...]


def _out_proj(o_nsa, o_sb, o_hg, z_all, x2d, wn, ws, wh, g, bvec):
    m = x2d.shape[0]
    rows = lambda w: pl.BlockSpec((T_PROJ, w), lambda i: (i, 0))
    full = lambda a: pl.BlockSpec(a.shape, lambda i: (0, 0))
    return pl.pallas_call(
        _out_kernel,
        grid=(m // T_PROJ,),
        in_specs=[rows(D_NSA), rows(D_SB), rows(D_HG), rows(D_MIX), rows(D_MODEL),
                  full(wn), full(ws), full(wh), full(g), full(bvec)],
        out_specs=rows(D_MODEL),
        out_shape=jax.ShapeDtypeStruct((m, D_MODEL), F32),
        compiler_params=_params("parallel"), name="out_proj_norm",
    )(o_nsa, o_sb, o_hg, z_all, x2d, wn, ws, wh, g, bvec)


def _t5_bucket_np(rel):
    n = np.maximum(rel, 0)
    max_exact = NUM_BUCKETS // 2
    large = max_exact + (np.log(np.maximum(n, 1).astype(np.float32) / max_exact)
                         / math.log(MAX_DISTANCE / max_exact) * (NUM_BUCKETS - max_exact)).astype(np.int32)
    large = np.clip(large, 0, NUM_BUCKETS - 1)
    return np.where(n < max_exact, n, large).astype(np.int32)


def _bias_tables(rel_bias, s):
    tbl = ((rel_bias - rel_bias[NUM_BUCKETS - 1]) * LOG2E).astype(F32)

    def expand(rel):
        onehot = (jnp.asarray(_t5_bucket_np(rel).reshape(-1, 1)) == jnp.arange(NUM_BUCKETS)[None, :]).astype(F32)
        return jnp.dot(onehot, tbl, precision=lax.Precision.HIGHEST).reshape(rel.shape + (NSA_HEADS,))

    n_cmp_pad = s // CMP_STRIDE
    n_rel = np.arange(2 * n_cmp_pad)[:, None] - n_cmp_pad
    r = np.arange(T_SEL)[None, :]
    pbias = jnp.transpose(expand(r - CMP_STRIDE * n_rel - (CMP_BLOCK - 1)), (2, 0, 1))
    T = T_NSA
    key = np.arange(T)[:, None]
    qry = np.arange(T)[None, :]
    near = np.stack([qry - key, T + qry - key])
    bias_t = jnp.transpose(expand(near), (0, 1, 3, 2)).reshape(2, T, NSA_HEADS * T)
    return pbias, bias_t


def _static_tables(s):
    n_chunk = s // CMP_STRIDE
    cmp_start = np.arange(n_chunk) * CMP_STRIDE
    slc_start = np.arange(MAX_SLC) * SLC_BLOCK
    ovl_t = ((cmp_start[None, :] < slc_start[:, None] + SLC_BLOCK)
             & (cmp_start[None, :] + CMP_BLOCK > slc_start[:, None])
             & (cmp_start[None, :] + CMP_BLOCK <= s)).astype(np.float32)
    onehot = (np.arange(s)[:, None] // SLC_BLOCK == np.arange(LANES)[None, :]).astype(np.float32)
    tri = (np.arange(T_SB)[None, :] >= np.arange(T_SB)[:, None]).astype(np.float32)
    tri = np.concatenate([tri, tri], axis=1)
    ones_bd = np.kron(np.eye(2), np.ones((HEAD_DIM, HEAD_DIM))).astype(np.float32)
    as_bf16 = lambda a: jnp.asarray(a, dtype=BF16)
    return as_bf16(ovl_t), as_bf16(onehot), as_bf16(tri), as_bf16(ones_bd), jnp.asarray(ones_bd)


def _layer_weights(w_in_l, cmp_pos_l, w_ck1_l, w_ck2_l, w_cv1_l, w_cv2_l):
    offs = np.cumsum((0,) + SPLIT_SIZES)
    (w_q, w_kc, w_vc, w_ks, w_vs, w_kw, w_vw, w_g, w_nz,
     w_sq, w_sk, w_sv, w_sz, w_hq, w_hf, w_hi, w_hz) = [w_in_l[:, offs[i]:offs[i + 1]] for i in range(len(SPLIT_SIZES))]
    w_gp = jnp.concatenate([w_g, jnp.zeros((D_MODEL, LANES - N_GATES), F32)], axis=1)
    w_z = jnp.concatenate([w_nz, w_sz, w_hz], axis=1)
    w_nat = [w_kc, w_vc, w_ks, w_kw, w_z, w_sk, w_hq, w_hf, w_hi]
    dt_nat = [F32, F32, BF16, BF16, BF16, BF16, F32, F32, F32]
    w_tr = [(w_q * QK_SCALE2).T, w_vs.T, w_vw.T, w_gp.T, (w_sq * QK_SCALE2).T, w_sv.T]
    dt_tr = [BF16, BF16, BF16, F32, BF16, BF16]
    w_nat = [w.astype(BF16) for w in w_nat]
    w_tr = [w.astype(BF16) for w in w_tr]

    def block_diag(w):
        z = jnp.zeros_like(w)
        return jnp.concatenate([jnp.concatenate([w, z], axis=-1), jnp.concatenate([z, w], axis=-1)], axis=-2)

    pos2 = jnp.concatenate([cmp_pos_l, cmp_pos_l], axis=1)
    w1k = block_diag(w_ck1_l.reshape(CMP_BLOCK, HEAD_DIM, CMP_HIDDEN)).astype(BF16)
    w1v = block_diag(w_cv1_l.reshape(CMP_BLOCK, HEAD_DIM, CMP_HIDDEN)).astype(BF16)
    w2k = block_diag(w_ck2_l).astype(BF16)
    w2v_t = block_diag(w_cv2_l).T.astype(BF16)
    return (w_nat, dt_nat, w_tr, dt_tr), (pos2, w1k, w2k, w1v, w2v_t)


def kernel(x, w_in, cmp_pos, w_ck1, w_ck2, w_cv1, w_cv2, hg_lb, hg_norm_w, w_out, ln_g, ln_b, rel_bias):
    b, s, d = x.shape
    assert d == D_MODEL and s % T_HG == 0 and s // SLC_BLOCK <= MAX_SLC and s >= WINDOW + T_NSA
    lb_w = jax.nn.softmax(hg_lb.astype(F32), axis=0)
    lb_all = jnp.cumsum(lb_w, axis=0) - lb_w[0]
    pbias, bias_t = _bias_tables(rel_bias, s)
    ovl_t, onehot, tri, ones_bd, bd_mask = _static_tables(s)

    x2d = x.reshape(b * s, d)
    for l in range(DEPTH):
        proj_w, cmp_w = _layer_weights(w_in[l], cmp_pos[l], w_ck1[l], w_ck2[l], w_cv1[l], w_cv2[l])
        nat, (q_t, vs_t, vw_t, g_t, sq_t, sv_t) = _inproj(x2d, *proj_w)
        kc_src, vc_src, ks, kw, z_all, sk, hq, hf, hi = [o.reshape(b, s, o.shape[-1]) for o in nat]
        kc, vc_t = _compress(kc_src, vc_src, *cmp_w)
        ocmp_t, sel_t = _nsa_select(q_t, kc, vc_t, pbias, ovl_t)
        o_nsa = _nsa_attend(q_t, sel_t, ocmp_t, g_t, ks, kw, vs_t, vw_t, bias_t, onehot)
        o_sb = _sb_attention(sq_t, sk, sv_t, tri)
        o_hg = _hgrn(hq, hf, hi, lb_all[l][None, :], hg_norm_w[l][None, :], ones_bd, bd_mask)
        wo = w_out[l].astype(BF16)
        x2d = _out_proj(o_nsa.reshape(b * s, D_NSA), o_sb.reshape(b * s, D_SB), o_hg.reshape(b * s, D_HG),
                        z_all.reshape(b * s, D_MIX), x2d,
                        wo[0:D_NSA], wo[D_NSA:D_NSA + D_SB], wo[D_NSA + D_SB:D_MIX],
                        ln_g[l][None, :], ln_b[l][None, :])
    return x2d.reshape(b, s, d)
```

```python
import functools
import math

import numpy as np
import jax
import jax.numpy as jnp
from jax import lax
from jax.experimental import pallas as pl
from jax.experimental.pallas import tpu as pltpu

F32 = jnp.float32
BF16 = jnp.bfloat16

D_MODEL = 1024
DEPTH = 2
HEAD_DIM = 64
LANES = 128
NSA_HEADS = 6
NSA_KV_GROUPS = 2
NSA_HPG = NSA_HEADS // NSA_KV_GROUPS
CMP_BLOCK = 32
CMP_STRIDE = 16
CMP_HIDDEN = 2 * HEAD_DIM
SLC_BLOCK = 64
SLC_TOPN = 16
MAX_SLC = 64
WINDOW = 512
FORCE_BONUS = 1000.0
NEG_BIG = -1e30
LB_FLOOR = 1e-30
SB_HEADS = 4
HG_HEADS = 6
HG_SUB = 16
HG_PITCH = 24
HG_UNROLL = 16
NUM_BUCKETS = 32
MAX_DISTANCE = 128
D_NSA = NSA_HEADS * HEAD_DIM
D_KV = NSA_KV_GROUPS * HEAD_DIM
D_SB = SB_HEADS * HEAD_DIM
D_HG = HG_HEADS * HEAD_DIM
D_MIX = D_NSA + D_SB + D_HG
N_GATES = NSA_HEADS * 3
SPLIT_SIZES = (D_NSA, D_KV, D_KV, D_KV, D_KV, D_KV, D_KV, N_GATES, D_NSA,
               D_SB, D_SB, D_SB, D_SB, D_HG, D_HG, D_HG, D_HG)
ALPHA = (2 * DEPTH) ** 0.25
LN_EPS = 1e-5
RMS_EPS = 1e-6
LOG2E = math.log2(math.e)
QK_SCALE2 = LOG2E / math.sqrt(HEAD_DIM)

T_SEL = 128
T_NSA = 256
T_SB = 256
SB_GROUP_LOG2 = 1
T_HG = 1024
T_PROJ = 512
VMEM_LIMIT = 56 * 1024 * 1024

_NT = (((1,), (1,)), ((), ()))
_TN = (((0,), (0,)), ((), ()))


def _dot(a, b):
    return jnp.dot(a, b, preferred_element_type=F32)


def _dot_nt(a, b):
    return lax.dot_general(a, b, _NT, preferred_element_type=F32)


def _dot_tn(a, b):
    return lax.dot_general(a, b, _TN, preferred_element_type=F32)


def _softplus(x):
    return jnp.maximum(x, 0.0) + jnp.log1p(jnp.exp(-jnp.abs(x)))


def _split_bf16(x):
    hi = x.astype(BF16)
    lo = (x - hi.astype(F32)).astype(BF16)
    return hi, lo


def _params(*sem, flags=None):
    return pltpu.CompilerParams(dimension_semantics=sem, vmem_limit_bytes=VMEM_LIMIT, flags=flags)


def _inproj_kernel(x_ref, *refs, n_nat, n_tr):
    xb = x_ref[...].astype(BF16)
    n_in = n_nat + n_tr
    for w_ref, o_ref in zip(refs[:n_nat], refs[n_in:n_in + n_nat]):
        o_ref[...] = _dot(xb, w_ref[...]).astype(o_ref.dtype)
    for w_ref, o_ref in zip(refs[n_nat:n_in], refs[n_in + n_nat:]):
        o_ref[...] = _dot_nt(w_ref[...], xb).astype(o_ref.dtype)


def _inproj(x2d, w_nat, dt_nat, w_tr, dt_tr):
    m = x2d.shape[0]
    in_specs = [pl.BlockSpec((T_PROJ, D_MODEL), lambda i: (i, 0))]
    in_specs += [pl.BlockSpec(w.shape, lambda i: (0, 0)) for w in w_nat + w_tr]
    out_specs = [pl.BlockSpec((T_PROJ, w.shape[1]), lambda i: (i, 0)) for w in w_nat]
    out_specs += [pl.BlockSpec((w.shape[0], T_PROJ), lambda i: (0, i)) for w in w_tr]
    out_shape = [jax.ShapeDtypeStruct((m, w.shape[1]), dt) for w, dt in zip(w_nat, dt_nat)]
    out_shape += [jax.ShapeDtypeStruct((w.shape[0], m), dt) for w, dt in zip(w_tr, dt_tr)]
    outs = pl.pallas_call(
        functools.partial(_inproj_kernel, n_nat=len(w_nat), n_tr=len(w_tr)),
        grid=(m // T_PROJ,),
        in_specs=in_specs, out_specs=out_specs, out_shape=out_shape,
        compiler_params=_params("parallel"), name="inproj",
    )(x2d, *w_nat, *w_tr)
    return outs[:len(w_nat)], outs[len(w_nat):]


def _compress_kernel(ksrc_ref, vsrc_ref, pos_ref, w1k_ref, w2k_ref, w1v_ref, w2v_ref,
                     kc_ref, vc_ref, *, n_chunk):
    def hidden(src_ref, w1_ref):
        top = jnp.zeros((n_chunk, 2 * CMP_HIDDEN), F32)
        bot = jnp.zeros((n_chunk, 2 * CMP_HIDDEN), F32)
        for p in range(CMP_STRIDE):
            xp = src_ref[0, pl.ds(p, n_chunk, stride=CMP_STRIDE), :]
            top += _dot((xp + pos_ref[p:p + 1, :]).astype(BF16), w1_ref[p])
            q = CMP_STRIDE + p
            bot += _dot((xp + pos_ref[q:q + 1, :]).astype(BF16), w1_ref[q])
        hid = top + pltpu.roll(bot, n_chunk - 1, 0)
        return jax.nn.gelu(hid).astype(BF16)

    kc_ref[0] = _dot(hidden(ksrc_ref, w1k_ref), w2k_ref[...]).astype(kc_ref.dtype)
    vc_ref[0] = _dot_nt(w2v_ref[...], hidden(vsrc_ref, w1v_ref)).astype(vc_ref.dtype)


def _compress(kc_src, vc_src, pos2, w1k, w2k, w1v, w2v_t):
    b, s, _ = kc_src.shape
    n_chunk = s // CMP_STRIDE
    full = lambda a: pl.BlockSpec(a.shape, lambda i: (0,) * a.ndim)
    src = pl.BlockSpec((1, s, D_KV), lambda i: (i, 0, 0))
    return pl.pallas_call(
        functools.partial(_compress_kernel, n_chunk=n_chunk),
        grid=(b,),
        in_specs=[src, src, full(pos2), full(w1k), full(w2k), full(w1v), full(w2v_t)],
        out_specs=[pl.BlockSpec((1, n_chunk, D_KV), lambda i: (i, 0, 0)),
                   pl.BlockSpec((1, D_KV, n_chunk), lambda i: (i, 0, 0))],
        out_shape=[jax.ShapeDtypeStruct((b, n_chunk, D_KV), BF16),
                   jax.ShapeDtypeStruct((b, D_KV, n_chunk), BF16)],
        compiler_params=_params("parallel"), name="nsa_compress",
    )(kc_src, vc_src, pos2, w1k, w2k, w1v, w2v_t)


def _nsa_select_kernel(qt_ref, kc_ref, vct_ref, pbias_ref, ovl_ref, ocmp_ref, sel_ref, *, n_cmp_pad):
    T = T_SEL
    j = pl.program_id(1)
    t0 = j * T
    n_idx = lax.broadcasted_iota(jnp.int32, (n_cmp_pad, T), 0)
    tok_c = t0 + lax.broadcasted_iota(jnp.int32, (n_cmp_pad, T), 1)
    mask_c = tok_c >= CMP_STRIDE * n_idx + (CMP_BLOCK - 1)
    off = pl.multiple_of(n_cmp_pad - (T // CMP_STRIDE) * j, 8)
    psums = [None] * NSA_KV_GROUPS
    for h in range(NSA_HEADS):
        g = h // NSA_HPG
        qh = qt_ref[HEAD_DIM * h:HEAD_DIM * (h + 1), :]
        zq = jnp.zeros_like(qh)
        qh = jnp.concatenate([qh, zq] if g == 0 else [zq, qh], axis=0)
        s = _dot(kc_ref[0], qh) + pbias_ref[h, pl.ds(off, n_cmp_pad), :]
        s = jnp.where(mask_c, s, NEG_BIG)
        m = jnp.max(s, axis=0, keepdims=True)
        p = jnp.where(mask_c, jnp.exp2(s - m), 0.0)
        l = jnp.sum(p, axis=0, keepdims=True)
        p = p / jnp.where(l > 0.0, l, 1.0)
        o_both = _dot(vct_ref[0], p.astype(BF16))
        ocmp_ref[HEAD_DIM * h:HEAD_DIM * (h + 1), :] = o_both[HEAD_DIM * g:HEAD_DIM * (g + 1)]
        psums[g] = p if psums[g] is None else psums[g] + p

    jblk = lax.broadcasted_iota(jnp.int32, (MAX_SLC, T), 0)
    tok = t0 + lax.broadcasted_iota(jnp.int32, (MAX_SLC, T), 1)
    cur = lax.shift_right_logical(tok, 6)
    forced = (jblk == 0) | (jblk == cur) | (jblk == cur - 1)
    valid = jblk * SLC_BLOCK <= tok
    jsub = lax.broadcasted_iota(jnp.int32, (8, T), 0)
    for g in range(NSA_KV_GROUPS):
        imp = _dot(ovl_ref[...], psums[g].astype(BF16))
        score = jnp.where(valid, imp + jnp.where(forced, FORCE_BONUS, 0.0), NEG_BIG)
        blocks = [score[8 * rb:8 * rb + 8] for rb in range(MAX_SLC // 8)]
        ranks = [jnp.zeros((8, T), F32) for _ in blocks]
        for jp in range(MAX_SLC):
            other = score[jp:jp + 1, :]
            for rb, blk in enumerate(blocks):
                ge = jnp.where(other >= blk, 1.0, 0.0)
                gt = jnp.where(other > blk, 1.0, 0.0)
                if 8 * rb > jp:
                    inc = ge
                elif 8 * rb + 7 < jp:
                    inc = gt
                else:
                    inc = jnp.where(jsub > jp - 8 * rb, ge, gt)
                ranks[rb] = ranks[rb] + inc
        rank = jnp.concatenate(ranks, axis=0)
        sel_ref[0, MAX_SLC * g:MAX_SLC * (g + 1), :] = jnp.where(rank < float(SLC_TOPN), 0.0, NEG_BIG).astype(BF16)


def _nsa_select(q_t, kc, vc_t, pbias, ovl_t):
    b, n_cmp_pad, _ = kc.shape
    m = q_t.shape[1]
    nq = m // b // T_SEL
    cols = lambda r: pl.BlockSpec((r, T_SEL), lambda i, j: (0, i * nq + j))
    per_b = lambda a: pl.BlockSpec((1,) + a.shape[1:], lambda i, j: (i, 0, 0))
    full = lambda a: pl.BlockSpec(a.shape, lambda i, j: (0,) * a.ndim)
    return pl.pallas_call(
        functools.partial(_nsa_select_kernel, n_cmp_pad=n_cmp_pad),
        grid=(b, nq),
        in_specs=[cols(D_NSA), per_b(kc), per_b(vc_t), full(pbias), full(ovl_t)],
        out_specs=[cols(D_NSA), pl.BlockSpec((1, NSA_KV_GROUPS * MAX_SLC, T_SEL), lambda i, j: (i, 0, j))],
        out_shape=[jax.ShapeDtypeStruct((D_NSA, m), F32),
                   jax.ShapeDtypeStruct((b, NSA_KV_GROUPS * MAX_SLC, m // b), BF16)],
        compiler_params=_params("parallel", "parallel"), name="nsa_select",
    )(q_t, kc, vc_t, pbias, ovl_t)


def _online_init(m_ref, l_ref, acc_ref):
    m_ref[...] = jnp.full(m_ref.shape, NEG_BIG, F32)
    l_ref[...] = jnp.zeros(l_ref.shape, F32)
    acc_ref[...] = jnp.zeros(acc_ref.shape, F32)


def _online_chunk(m_ref, l_ref, acc_ref, k_tile, q_ref, q_rows, v_t, bias_ref=None, ok=None):
    s = _dot(k_tile, q_ref[0:q_rows, :])
    if bias_ref is not None:
        s = s + bias_ref[...]
    if ok is not None:
        s = jnp.where(ok, s, NEG_BIG)
    m = m_ref[...]
    m_new = jnp.maximum(m, jnp.max(s, axis=0, keepdims=True))
    alpha = jnp.exp2(m - m_new)
    p = jnp.exp2(s - m_new)
    m_ref[...] = m_new
    l_ref[...] = alpha * l_ref[...] + jnp.sum(p, axis=0, keepdims=True)
    acc_ref[...] = alpha * acc_ref[...] + _dot(v_t, p.astype(BF16))


def _nsa_attend_kernel(qt_ref, sel_ref, ocmp_ref, gt_ref, ks_ref, kw_ref, vs_ref, vw_ref, bias_ref, onehot_ref,
                       o_ref, kaug_ref, qaug_ref, ms_ref, ls_ref, accs_ref, mw_ref, lw_ref, accw_ref):
    T = T_NSA
    j = pl.program_id(1)

    @pl.when(j == 0)
    def _():
        kaug_ref[:, 0:LANES] = ks_ref[0]
        kaug_ref[:, LANES:2 * LANES] = onehot_ref[...]
        qaug_ref[...] = jnp.zeros(qaug_ref.shape, BF16)

    for h in range(NSA_HEADS):
        g = h // NSA_HPG
        qaug_ref[HEAD_DIM * g:HEAD_DIM * (g + 1), h * T:(h + 1) * T] = qt_ref[HEAD_DIM * h:HEAD_DIM * (h + 1), :]
        qaug_ref[LANES:LANES + MAX_SLC, h * T:(h + 1) * T] = sel_ref[0, MAX_SLC * g:MAX_SLC * (g + 1), :]

    key = lax.broadcasted_iota(jnp.int32, (T, NSA_HEADS * T), 0)
    qry = lax.broadcasted_iota(jnp.int32, (T, NSA_HEADS * T), 1) & (T - 1)
    causal = key <= qry
    st_s = (ms_ref, ls_ref, accs_ref)
    st_w = (mw_ref, lw_ref, accw_ref)
    _online_init(*st_s)
    _online_init(*st_w)

    n_far = jnp.maximum(j - 1, 0)

    def far_body(c, carry):
        r0 = pl.multiple_of(c * (2 * T), 2 * T)
        _online_chunk(*st_s, kaug_ref[pl.ds(r0, 2 * T), :], qaug_ref, 2 * LANES, vs_ref[:, pl.ds(r0, 2 * T)])
        return carry

    lax.fori_loop(0, lax.shift_right_logical(n_far, 1), far_body, 0)

    @pl.when((n_far & 1) == 1)
    def _():
        r0 = pl.multiple_of((n_far - 1) * T, T)
        _online_chunk(*st_s, kaug_ref[pl.ds(r0, T), :], qaug_ref, 2 * LANES, vs_ref[:, pl.ds(r0, T)])
    for d in (1, 0):
        c = j - d
        r0 = pl.multiple_of(jnp.maximum(c, 0) * T, T)
        ok = causal if d == 0 else key < jnp.where(c >= 0, T, -1)
        _online_chunk(*st_s, kaug_ref[pl.ds(r0, T), :], qaug_ref, 2 * LANES, vs_ref[:, pl.ds(r0, T)],
                      bias_ref.at[d], ok)

    n_win = WINDOW // T
    for d in range(n_win + 1):
        c = j - d
        r0 = pl.multiple_of(jnp.maximum(c, 0) * T, T)
        in_range = key < jnp.where(c >= 0, T, -1)
        if d == 0:
            ok = causal
        elif d == n_win:
            ok = (key > qry) & in_range
        else:
            ok = in_range
        _online_chunk(*st_w, kw_ref[0, pl.ds(r0, T), :], qaug_ref, LANES, vw_ref[:, pl.ds(r0, T)],
                      bias_ref.at[d] if d <= 1 else None, ok)

    o_sel = accs_ref[...] / ls_ref[...]
    o_win = accw_ref[...] / lw_ref[...]
    gates = jax.nn.sigmoid(gt_ref[...])
    heads = []
    for h in range(NSA_HEADS):
        cols = slice(h * T, (h + 1) * T)
        rows = slice(HEAD_DIM * (h // NSA_HPG), HEAD_DIM * (h // NSA_HPG + 1))
        heads.append(gates[3 * h:3 * h + 1, :] * ocmp_ref[HEAD_DIM * h:HEAD_DIM * (h + 1), :]
                     + gates[3 * h + 1:3 * h + 2, :] * o_sel[rows, cols]
                     + gates[3 * h + 2:3 * h + 3, :] * o_win[rows, cols])
    o_ref[0] = jnp.concatenate(heads, axis=0).T.astype(o_ref.dtype)


def _nsa_attend(q_t, sel_t, ocmp_t, g_t, ks, kw, vs_t, vw_t, bias_t, onehot):
    b, s, _ = ks.shape
    T = T_NSA
    nq = s // T
    R = NSA_HEADS * T
    cols = lambda r: pl.BlockSpec((r, T), lambda i, j: (0, i * nq + j))
    row_b = lambda a: pl.BlockSpec((a.shape[0], s), lambda i, j: (0, i))
    per_b = lambda a: pl.BlockSpec((1,) + a.shape[1:], lambda i, j: (i, 0, 0))
    full = lambda a: pl.BlockSpec(a.shape, lambda i, j: (0,) * a.ndim)
    stat = [pltpu.VMEM((1, R), F32), pltpu.VMEM((1, R), F32), pltpu.VMEM((LANES, R), F32)]
    return pl.pallas_call(
        _nsa_attend_kernel,
        grid=(b, nq),
        in_specs=[cols(D_NSA), pl.BlockSpec((1, NSA_KV_GROUPS * MAX_SLC, T), lambda i, j: (i, 0, j)),
                  cols(D_NSA), cols(LANES), per_b(ks), per_b(kw), row_b(vs_t), row_b(vw_t),
                  full(bias_t), full(onehot)],
        out_specs=pl.BlockSpec((1, T, D_NSA), lambda i, j: (i, j, 0)),
        out_shape=jax.ShapeDtypeStruct((b, s, D_NSA), BF16),
        scratch_shapes=[pltpu.VMEM((s, 2 * LANES), BF16), pltpu.VMEM((2 * LANES, R), BF16)] + stat + stat,
        compiler_params=_params("parallel", "arbitrary"), name="nsa_attend",
    )(q_t, sel_t, ocmp_t, g_t, ks, kw, vs_t, vw_t, bias_t, onehot)


def _sb_kernel(qt_ref, k_ref, vt_ref, tri_ref, o_ref, carry_ref, acc_ref):
    T = T_SB
    j = pl.program_id(1)
    n_pair = SB_HEADS // 2
    key = lax.broadcasted_iota(jnp.int32, (T, 2 * T), 0)
    qry = lax.broadcasted_iota(jnp.int32, (T, 2 * T), 1) & (T - 1)
    strict = key < qry
    carry_ref[...] = jnp.zeros(carry_ref.shape, F32)
    acc_ref[...] = jnp.zeros(acc_ref.shape, F32)

    def tiles(cs, mask):
        starts = [pl.multiple_of(c * T, T) for c in cs]
        units = [(ci, p) for ci in range(len(cs)) for p in range(n_pair)]
        z2s, incls = [], []
        for c, p in units:
            q_even = qt_ref[HEAD_DIM * (2 * p):HEAD_DIM * (2 * p + 1), :]
            q_odd = qt_ref[HEAD_DIM * (2 * p + 1):HEAD_DIM * (2 * p + 2), :]
            zq = jnp.zeros_like(q_even)
            q_pair = jnp.concatenate([jnp.concatenate([q_even, zq], axis=0),
                                      jnp.concatenate([zq, q_odd], axis=0)], axis=1)
            z2s.append(_dot(k_ref[0, pl.ds(starts[c], T), LANES * p:LANES * (p + 1)], q_pair))
        for z2 in z2s:
            rest = jnp.maximum(z2, 0.0) + jnp.log2(1.0 + jnp.exp2(-jnp.abs(z2)))
            if mask is not None:
                rest = jnp.where(mask, rest, 0.0)
            hi, lo = _split_bf16(rest)
            incls.append(_dot(tri_ref[...], jnp.concatenate([hi, lo], axis=0)))
        carry = [carry_ref[p] for p in range(n_pair)]
        acc = [acc_ref[p] for p in range(n_pair)]
        for (c, p), z2, incl in zip(units, z2s, incls):
            a = jnp.exp2(z2 - incl - carry[p])
            if mask is not None:
                a = jnp.where(mask, a, 0.0)
            acc[p] = acc[p] + _dot(vt_ref[LANES * p:LANES * (p + 1), pl.ds(starts[c], T)], a.astype(BF16))
            carry[p] = carry[p] + incl[0:1, :]
        for p in range(n_pair):
            carry_ref[p] = carry[p]
            acc_ref[p] = acc[p]

    tiles([j], strict)

    n_group = lax.shift_right_logical(j, SB_GROUP_LOG2)

    def back_group(i, carry):
        c = j - 1 - (i << SB_GROUP_LOG2)
        tiles([c - k for k in range(1 << SB_GROUP_LOG2)], None)
        return carry

    lax.fori_loop(0, n_group, back_group, 0)

    def back_one(i, carry):
        tiles([j - 1 - (n_group << SB_GROUP_LOG2) - i], None)
        return carry

    lax.fori_loop(0, j & ((1 << SB_GROUP_LOG2) - 1), back_one, 0)
    row = lax.broadcasted_iota(jnp.int32, (LANES, T), 0)
    for p in range(n_pair):
        acc = acc_ref[p]
        o_ref[0, :, LANES * p:LANES * (p + 1)] = (
            jnp.where(row < HEAD_DIM, acc[:, 0:T], acc[:, T:2 * T]).T.astype(o_ref.dtype))


def _sb_attention(q_t, k, v_t, tri_t):
    b, s, _ = k.shape
    T = T_SB
    nq = s // T
    n_pair = SB_HEADS // 2
    return pl.pallas_call(
        _sb_kernel,
        grid=(b, nq),
        in_specs=[pl.BlockSpec((D_SB, T), lambda i, j: (0, i * nq + j)),
                  pl.BlockSpec((1, s, D_SB), lambda i, j: (i, 0, 0)),
                  pl.BlockSpec((D_SB, s), lambda i, j: (0, i)),
                  pl.BlockSpec(tri_t.shape, lambda i, j: (0, 0))],
        out_specs=pl.BlockSpec((1, T, D_SB), lambda i, j: (i, j, 0)),
        out_shape=jax.ShapeDtypeStruct((b, s, D_SB), BF16),
        scratch_shapes=[pltpu.VMEM((n_pair, 1, 2 * T), F32), pltpu.VMEM((n_pair, LANES, 2 * T), F32)],
        compiler_params=_params("parallel", "parallel"), name="sb_attention",
    )(q_t, k, v_t, tri_t)


def _hgrn_kernel(q_ref, f_ref, i_ref, lb_ref, nw_ref, ones_ref, bd_ref, o_ref, st_all_ref, *scratch):
    @pl.when(pl.program_id(1) == 0)
    def _():
        st_all_ref[...] = jnp.zeros(st_all_ref.shape, F32)

    def pair_body(p, carry):
        lanes = lambda j: pl.ds(pl.multiple_of((j * (D_HG // LANES) + p) * LANES, LANES), LANES)
        vec = pl.ds(pl.multiple_of(p * LANES, LANES), LANES)
        _hgrn_pair([q_ref.at[0, :, lanes(j)] for j in range(HG_SUB)],
                   [f_ref.at[0, :, lanes(j)] for j in range(HG_SUB)],
                   [i_ref.at[0, :, lanes(j)] for j in range(HG_SUB)],
                   lb_ref.at[:, vec], nw_ref.at[:, vec], ones_ref, bd_ref,
                   [o_ref.at[0, :, lanes(j)] for j in range(HG_SUB)], st_all_ref.at[p], *scratch)
        return carry

    lax.fori_loop(0, D_HG // LANES, pair_body, 0)


def _hgrn_pair(q_refs, f_refs, i_refs, lb_ref, nw_ref, ones_ref, bd_ref, o_refs, st_ref,
               kj_ref, bj_ref, qd_ref, kd_ref, vn_ref, oc_ref, x_ref, u_ref, sb_ref, dec_ref):
    C = HG_SUB
    n_blk = T_HG // C

    lb = lb_ref[...]
    c_floor = jnp.log(jnp.maximum(lb, LB_FLOOR))
    c_rest = jnp.log1p(-lb)
    one_m_lb = 1.0 - lb
    b_run = jnp.zeros((n_blk, LANES), F32)
    for j in range(C):
        fj = f_refs[j][...]
        log_sig = -(jnp.maximum(-fj, 0.0) + jnp.log(1.0 + jnp.exp(-jnp.abs(fj))))
        a, bb = c_floor, c_rest + log_sig
        log_f = jnp.maximum(a, bb) + jnp.log(1.0 + jnp.exp(-jnp.abs(a - bb)))
        b_run = b_run + log_f
        bj_ref[j] = b_run * LOG2E
        kj_ref[j] = one_m_lb * jax.nn.sigmoid(-fj)
    b_last = bj_ref[C - 1]
    for j in range(C):
        bj = bj_ref[j]
        qd_ref[pl.ds(j, n_blk, stride=HG_PITCH), :] = q_refs[j][...] * jnp.exp2(bj)
        kd_ref[pl.ds(j, n_blk, stride=HG_PITCH), :] = kj_ref[j] * jnp.exp2(b_last - bj)
        vn_ref[pl.ds(j, n_blk, stride=HG_PITCH), :] = i_refs[j][...]

    def kv_products(blk, carry):
        r0 = pl.multiple_of(blk * HG_PITCH, 8)
        kd = kd_ref[pl.ds(r0, C), :].astype(BF16)
        vv = vn_ref[pl.ds(r0, C), :].astype(BF16)
        u_ref[blk] = bd_ref[...] * _dot_tn(vv, kd)
        return carry

    lax.fori_loop(0, n_blk, kv_products, 0, unroll=HG_UNROLL)
    dec_ref[...] = jnp.exp2(b_last)

    def scan(blk, st):
        sb_ref[blk] = st.astype(BF16)
        return st * dec_ref[pl.ds(blk, 1), :] + u_ref[blk]

    st_ref[...] = lax.fori_loop(0, n_blk, scan, st_ref[...], unroll=HG_UNROLL)

    def outputs(blk, carry):
        r0 = pl.multiple_of(blk * HG_PITCH, 8)
        oc_ref[pl.ds(r0, C), :] = _dot_nt(qd_ref[pl.ds(r0, C), :].astype(BF16), sb_ref[blk])
        return carry

    lax.fori_loop(0, n_blk, outputs, 0, unroll=HG_UNROLL)

    nw = nw_ref[...]
    for j in range(C):
        oj = oc_ref[pl.ds(j, n_blk, stride=HG_PITCH), :]
        qj = q_refs[j][...]
        bj = bj_ref[j]
        for jp in range(j + 1):
            x = qj * kj_ref[jp] * jnp.exp2(bj - bj_ref[jp])
            x_ref[jp * n_blk:(jp + 1) * n_blk, :] = x.astype(BF16)
        att = _dot(x_ref[0:(j + 1) * n_blk, :], ones_ref[...])
        for jp in range(j + 1):
            oj = oj + att[jp * n_blk:(jp + 1) * n_blk] * i_refs[jp][...]
        hi, lo = _split_bf16(oj * oj)
        ms = (_dot(hi, ones_ref[...]) + _dot(lo, ones_ref[...])) * (1.0 / HEAD_DIM)
        o_refs[j][...] = oj * lax.rsqrt(ms + RMS_EPS) * nw


def _hgrn(q, f, i, lb, nw, ones_bd, bd_mask):
    b, s, _ = q.shape
    C = HG_SUB
    n_blk = T_HG // C
    split = lambda a: a.reshape(b, s // C, C * D_HG)
    tile = pl.BlockSpec((1, n_blk, C * D_HG), lambda bi, ti: (bi, ti, 0))
    full = lambda a: pl.BlockSpec(a.shape, lambda bi, ti: (0,) * a.ndim)
    jm = pltpu.VMEM((C, n_blk, LANES), F32)
    nat = pltpu.VMEM((n_blk * HG_PITCH, LANES), F32)
    out = pl.pallas_call(
        _hgrn_kernel,
        grid=(b, s // T_HG),
        in_specs=[tile, tile, tile, full(lb), full(nw), full(ones_bd), full(bd_mask)],
        out_specs=tile,
        out_shape=jax.ShapeDtypeStruct((b, s // C, C * D_HG), F32),
        scratch_shapes=[pltpu.VMEM((D_HG // LANES, LANES, LANES), F32), jm, jm, nat, nat, nat, nat,
                        pltpu.VMEM((T_HG, LANES), BF16), pltpu.VMEM((n_blk, LANES, LANES), F32),
                        pltpu.VMEM((n_blk, LANES, LANES), BF16), pltpu.VMEM((n_blk, LANES), F32)],
        compiler_params=_params("parallel", "arbitrary"), name="hgrn2",
    )(split(q), split(f), split(i), lb, nw, ones_bd, bd_mask)
    return out.reshape(b, s, D_HG)


def _out_kernel(onsa_ref, osb_ref, ohg_ref, z_ref, x_ref, wn_ref, ws_ref, wh_ref, g_ref, b_ref, o_ref):
    z = z_ref[...].astype(F32)
    sz = z * jax.nn.sigmoid(z)
    y = _dot((onsa_ref[...] * sz[:, 0:D_NSA]).astype(BF16), wn_ref[...])
    y += _dot((osb_ref[...] * sz[:, D_NSA:D_NSA + D_SB]).astype(BF16), ws_ref[...])
    y += _dot((ohg_ref[...] * sz[:, D_NSA + D_SB:D_MIX]).astype(BF16), wh_ref[...])
    v = ALPHA * x_ref[...] + y
    mu = jnp.mean(v, axis=-1, keepdims=True)
    vc = v - mu
    var = jnp.mean(vc * vc, axis=-1, keepdims=True)
    o_ref[...] = vc * lax.rsqrt(var + LN_EPS) * g_ref[...] + b_ref[...]


def _out_proj(o_nsa, o_sb, o_hg, z_all, x2d, wn, ws, wh, g, bvec):
    m = x2d.shape[0]
    rows = lambda w: pl.BlockSpec((T_PROJ, w), lambda i: (i, 0))
    full = lambda a: pl.BlockSpec(a.shape, lambda i: (0, 0))
    return pl.pallas_call(
        _out_kernel,
        grid=(m // T_PROJ,),
        in_specs=[rows(D_NSA), rows(D_SB), rows(D_HG), rows(D_MIX), rows(D_MODEL),
                  full(wn), full(ws), full(wh), full(g), full(bvec)],
        out_specs=rows(D_MODEL),
        out_shape=jax.ShapeDtypeStruct((m, D_MODEL), F32),
        compiler_params=_params("parallel"), name="out_proj_norm",
    )(o_nsa, o_sb, o_hg, z_all, x2d, wn, ws, wh, g, bvec)


def _t5_bucket_np(rel):
    n = np.maximum(rel, 0)
    max_exact = NUM_BUCKETS // 2
    large = max_exact + (np.log(np.maximum(n, 1).astype(np.float32) / max_exact)
                         / math.log(MAX_DISTANCE / max_exact) * (NUM_BUCKETS - max_exact)).astype(np.int32)
    large = np.clip(large, 0, NUM_BUCKETS - 1)
    return np.where(n < max_exact, n, large).astype(np.int32)


def _bias_tables(rel_bias, s):
    tbl = ((rel_bias - rel_bias[NUM_BUCKETS - 1]) * LOG2E).astype(F32)

    def expand(rel):
        onehot = (jnp.asarray(_t5_bucket_np(rel).reshape(-1, 1)) == jnp.arange(NUM_BUCKETS)[None, :]).astype(F32)
        return jnp.dot(onehot, tbl, precision=lax.Precision.HIGHEST).reshape(rel.shape + (NSA_HEADS,))

    n_cmp_pad = s // CMP_STRIDE
    n_rel = np.arange(2 * n_cmp_pad)[:, None] - n_cmp_pad
    r = np.arange(T_SEL)[None, :]
    pbias = jnp.transpose(expand(r - CMP_STRIDE * n_rel - (CMP_BLOCK - 1)), (2, 0, 1))
    T = T_NSA
    key = np.arange(T)[:, None]
    qry = np.arange(T)[None, :]
    near = np.stack([qry - key, T + qry - key])
    bias_t = jnp.transpose(expand(near), (0, 1, 3, 2)).reshape(2, T, NSA_HEADS * T)
    return pbias, bias_t


def _static_tables(s):
    n_chunk = s // CMP_STRIDE
    cmp_start = np.arange(n_chunk) * CMP_STRIDE
    slc_start = np.arange(MAX_SLC) * SLC_BLOCK
    ovl_t = ((cmp_start[None, :] < slc_start[:, None] + SLC_BLOCK)
             & (cmp_start[None, :] + CMP_BLOCK > slc_start[:, None])
             & (cmp_start[None, :] + CMP_BLOCK <= s)).astype(np.float32)
    onehot = (np.arange(s)[:, None] // SLC_BLOCK == np.arange(LANES)[None, :]).astype(np.float32)
    tri = (np.arange(T_SB)[None, :] >= np.arange(T_SB)[:, None]).astype(np.float32)
    tri = np.concatenate([tri, tri], axis=1)
    ones_bd = np.kron(np.eye(2), np.ones((HEAD_DIM, HEAD_DIM))).astype(np.float32)
    as_bf16 = lambda a: jnp.asarray(a, dtype=BF16)
    return as_bf16(ovl_t), as_bf16(onehot), as_bf16(tri), as_bf16(ones_bd), jnp.asarray(ones_bd)


def _layer_weights(w_in_l, cmp_pos_l, w_ck1_l, w_ck2_l, w_cv1_l, w_cv2_l):
    offs = np.cumsum((0,) + SPLIT_SIZES)
    (w_q, w_kc, w_vc, w_ks, w_vs, w_kw, w_vw, w_g, w_nz,
     w_sq, w_sk, w_sv, w_sz, w_hq, w_hf, w_hi, w_hz) = [w_in_l[:, offs[i]:offs[i + 1]] for i in range(len(SPLIT_SIZES))]
    w_gp = jnp.concatenate([w_g, jnp.zeros((D_MODEL, LANES - N_GATES), F32)], axis=1)
    w_z = jnp.concatenate([w_nz, w_sz, w_hz], axis=1)
    w_nat = [w_kc, w_vc, w_ks, w_kw, w_z, w_sk, w_hq, w_hf, w_hi]
    dt_nat = [F32, F32, BF16, BF16, BF16, BF16, F32, F32, F32]
    w_tr = [(w_q * QK_SCALE2).T, w_vs.T, w_vw.T, w_gp.T, (w_sq * QK_SCALE2).T, w_sv.T]
    dt_tr = [BF16, BF16, BF16, F32, BF16, BF16]
    w_nat = [w.astype(BF16) for w in w_nat]
    w_tr = [w.astype(BF16) for w in w_tr]

    def block_diag(w):
        z = jnp.zeros_like(w)
        return jnp.concatenate([jnp.concatenate([w, z], axis=-1), jnp.concatenate([z, w], axis=-1)], axis=-2)

    pos2 = jnp.concatenate([cmp_pos_l, cmp_pos_l], axis=1)
    w1k = block_diag(w_ck1_l.reshape(CMP_BLOCK, HEAD_DIM, CMP_HIDDEN)).astype(BF16)
    w1v = block_diag(w_cv1_l.reshape(CMP_BLOCK, HEAD_DIM, CMP_HIDDEN)).astype(BF16)
    w2k = block_diag(w_ck2_l).astype(BF16)
    w2v_t = block_diag(w_cv2_l).T.astype(BF16)
    return (w_nat, dt_nat, w_tr, dt_tr), (pos2, w1k, w2k, w1v, w2v_t)


def kernel(x, w_in, cmp_pos, w_ck1, w_ck2, w_cv1, w_cv2, hg_lb, hg_norm_w, w_out, ln_g, ln_b, rel_bias):
    b, s, d = x.shape
    assert d == D_MODEL and s % T_HG == 0 and s // SLC_BLOCK <= MAX_SLC and s >= WINDOW + T_NSA
    lb_w = jax.nn.softmax(hg_lb.astype(F32), axis=0)
    lb_all = jnp.cumsum(lb_w, axis=0) - lb_w[0]
    pbias, bias_t = _bias_tables(rel_bias, s)
    ovl_t, onehot, tri, ones_bd, bd_mask = _static_tables(s)

    x2d = x.reshape(b * s, d)
    for l in range(DEPTH):
        proj_w, cmp_w = _layer_weights(w_in[l], cmp_pos[l], w_ck1[l], w_ck2[l], w_cv1[l], w_cv2[l])
        nat, (q_t, vs_t, vw_t, g_t, sq_t, sv_t) = _inproj(x2d, *proj_w)
        kc_src, vc_src, ks, kw, z_all, sk, hq, hf, hi = [o.reshape(b, s, o.shape[-1]) for o in nat]
        kc, vc_t = _compress(kc_src, vc_src, *cmp_w)
        ocmp_t, sel_t = _nsa_select(q_t, kc, vc_t, pbias, ovl_t)
        o_nsa = _nsa_attend(q_t, sel_t, ocmp_t, g_t, ks, kw, vs_t, vw_t, bias_t, onehot)
        o_sb = _sb_attention(sq_t, sk, sv_t, tri)
        o_hg = _hgrn(hq, hf, hi, lb_all[l][None, :], hg_norm_w[l][None, :], ones_bd, bd_mask)
        wo = w_out[l].astype(BF16)
        x2d = _out_proj(o_nsa.reshape(b * s, D_NSA), o_sb.reshape(b * s, D_SB), o_hg.reshape(b * s, D_HG),
                        z_all.reshape(b * s, D_MIX), x2d,
                        wo[0:D_NSA], wo[D_NSA:D_NSA + D_SB], wo[D_NSA + D_SB:D_MIX],
                        ln_g[l][None, :], ln_b[l][None, :])
    return x2d.reshape(b, s, d)
```

```python
import functools
import math

import numpy as np
import jax
import jax.numpy as jnp
from jax import lax
from jax.experimental import pallas as pl
from jax.experimental.pallas import tpu as pltpu

F32 = jnp.float32
BF16 = jnp.bfloat16

D_MODEL = 1024
DEPTH = 2
HEAD_DIM = 64
LANES = 128
NSA_HEADS = 6
NSA_KV_GROUPS = 2
NSA_HPG = NSA_HEADS // NSA_KV_GROUPS
CMP_BLOCK = 32
CMP_STRIDE = 16
CMP_HIDDEN = 2 * HEAD_DIM
SLC_BLOCK = 64
SLC_TOPN = 16
MAX_SLC = 64
WINDOW = 512
FORCE_BONUS = 1000.0
NEG_BIG = -1e30
LB_FLOOR = 1e-30
SB_HEADS = 4
HG_HEADS = 6
HG_SUB = 16
HG_PITCH = 24
HG_UNROLL = 16
NUM_BUCKETS = 32
MAX_DISTANCE = 128
D_NSA = NSA_HEADS * HEAD_DIM
D_KV = NSA_KV_GROUPS * HEAD_DIM
D_SB = SB_HEADS * HEAD_DIM
D_HG = HG_HEADS * HEAD_DIM
D_MIX = D_NSA + D_SB + D_HG
N_GATES = NSA_HEADS * 3
SPLIT_SIZES = (D_NSA, D_KV, D_KV, D_KV, D_KV, D_KV, D_KV, N_GATES, D_NSA,
               D_SB, D_SB, D_SB, D_SB, D_HG, D_HG, D_HG, D_HG)
ALPHA = (2 * DEPTH) ** 0.25
LN_EPS = 1e-5
RMS_EPS = 1e-6
LOG2E = math.log2(math.e)
QK_SCALE2 = LOG2E / math.sqrt(HEAD_DIM)

T_SEL = 128
T_NSA = 256
T_SB = 256
SB_GROUP_LOG2 = 1
T_HG = 1024
T_PROJ = 512
VMEM_LIMIT = 56 * 1024 * 1024

_NT = (((1,), (1,)), ((), ()))
_TN = (((0,), (0,)), ((), ()))


def _dot(a, b):
    return jnp.dot(a, b, preferred_element_type=F32)


def _dot_nt(a, b):
    return lax.dot_general(a, b, _NT, preferred_element_type=F32)


def _dot_tn(a, b):
    return lax.dot_general(a, b, _TN, preferred_element_type=F32)


def _softplus(x):
    return jnp.maximum(x, 0.0) + jnp.log1p(jnp.exp(-jnp.abs(x)))


def _split_bf16(x):
    hi = x.astype(BF16)
    lo = (x - hi.astype(F32)).astype(BF16)
    return hi, lo


def _params(*sem, flags=None):
    return pltpu.CompilerParams(dimension_semantics=sem, vmem_limit_bytes=VMEM_LIMIT, flags=flags)


def _inproj_kernel(x_ref, *refs, n_nat, n_tr):
    xb = x_ref[...].astype(BF16)
    n_in = n_nat + n_tr
    for w_ref, o_ref in zip(refs[:n_nat], refs[n_in:n_in + n_nat]):
        o_ref[...] = _dot(xb, w_ref[...]).astype(o_ref.dtype)
    for w_ref, o_ref in zip(refs[n_nat:n_in], refs[n_in + n_nat:]):
        o_ref[...] = _dot_nt(w_ref[...], xb).astype(o_ref.dtype)


def _inproj(x2d, w_nat, dt_nat, w_tr, dt_tr):
    m = x2d.shape[0]
    in_specs = [pl.BlockSpec((T_PROJ, D_MODEL), lambda i: (i, 0))]
    in_specs += [pl.BlockSpec(w.shape, lambda i: (0, 0)) for w in w_nat + w_tr]
    out_specs = [pl.BlockSpec((T_PROJ, w.shape[1]), lambda i: (i, 0)) for w in w_nat]
    out_specs += [pl.BlockSpec((w.shape[0], T_PROJ), lambda i: (0, i)) for w in w_tr]
    out_shape = [jax.ShapeDtypeStruct((m, w.shape[1]), dt) for w, dt in zip(w_nat, dt_nat)]
    out_shape += [jax.ShapeDtypeStruct((w.shape[0], m), dt) for w, dt in zip(w_tr, dt_tr)]
    outs = pl.pallas_call(
        functools.partial(_inproj_kernel, n_nat=len(w_nat), n_tr=len(w_tr)),
        grid=(m // T_PROJ,),
        in_specs=in_specs, out_specs=out_specs, out_shape=out_shape,
        compiler_params=_params("parallel"), name="inproj",
    )(x2d, *w_nat, *w_tr)
    return outs[:len(w_nat)], outs[len(w_nat):]


def _compress_kernel(ksrc_ref, vsrc_ref, pos_ref, w1k_ref, w2k_ref, w1v_ref, w2v_ref,
                     kc_ref, vc_ref, *, n_chunk):
    def hidden(src_ref, w1_ref):
        top = jnp.zeros((n_chunk, 2 * CMP_HIDDEN), F32)
        bot = jnp.zeros((n_chunk, 2 * CMP_HIDDEN), F32)
        for p in range(CMP_STRIDE):
            xp = src_ref[0, pl.ds(p, n_chunk, stride=CMP_STRIDE), :]
            top += _dot((xp + pos_ref[p:p + 1, :]).astype(BF16), w1_ref[p])
            q = CMP_STRIDE + p
            bot += _dot((xp + pos_ref[q:q + 1, :]).astype(BF16), w1_ref[q])
        hid = top + pltpu.roll(bot, n_chunk - 1, 0)
        return jax.nn.gelu(hid).astype(BF16)

    kc_ref[0] = _dot(hidden(ksrc_ref, w1k_ref), w2k_ref[...]).astype(kc_ref.dtype)
    vc_ref[0] = _dot_nt(w2v_ref[...], hidden(vsrc_ref, w1v_ref)).astype(vc_ref.dtype)


def _compress(kc_src, vc_src, pos2, w1k, w2k, w1v, w2v_t):
    b, s, _ = kc_src.shape
    n_chunk = s // CMP_STRIDE
    full = lambda a: pl.BlockSpec(a.shape, lambda i: (0,) * a.ndim)
    src = pl.BlockSpec((1, s, D_KV), lambda i: (i, 0, 0))
    return pl.pallas_call(
        functools.partial(_compress_kernel, n_chunk=n_chunk),
        grid=(b,),
        in_specs=[src, src, full(pos2), full(w1k), full(w2k), full(w1v), full(w2v_t)],
        out_specs=[pl.BlockSpec((1, n_chunk, D_KV), lambda i: (i, 0, 0)),
                   pl.BlockSpec((1, D_KV, n_chunk), lambda i: (i, 0, 0))],
        out_shape=[jax.ShapeDtypeStruct((b, n_chunk, D_KV), BF16),
                   jax.ShapeDtypeStruct((b, D_KV, n_chunk), BF16)],
        compiler_params=_params("parallel"), name="nsa_compress",
    )(kc_src, vc_src, pos2, w1k, w2k, w1v, w2v_t)


def _nsa_select_kernel(qt_ref, kc_ref, vct_ref, pbias_ref, ovl_ref, ocmp_ref, sel_ref, *, n_cmp_pad):
    T = T_SEL
    j = pl.program_id(1)
    t0 = j * T
    n_idx = lax.broadcasted_iota(jnp.int32, (n_cmp_pad, T), 0)
    tok_c = t0 + lax.broadcasted_iota(jnp.int32, (n_cmp_pad, T), 1)
    mask_c = tok_c >= CMP_STRIDE * n_idx + (CMP_BLOCK - 1)
    off = pl.multiple_of(n_cmp_pad - (T // CMP_STRIDE) * j, 8)
    psums = [None] * NSA_KV_GROUPS
    for h in range(NSA_HEADS):
        g = h // NSA_HPG
        qh = qt_ref[HEAD_DIM * h:HEAD_DIM * (h + 1), :]
        zq = jnp.zeros_like(qh)
        qh = jnp.concatenate([qh, zq] if g == 0 else [zq, qh], axis=0)
        s = _dot(kc_ref[0], qh) + pbias_ref[h, pl.ds(off, n_cmp_pad), :]
        s = jnp.where(mask_c, s, NEG_BIG)
        m = jnp.max(s, axis=0, keepdims=True)
        p = jnp.where(mask_c, jnp.exp2(s - m), 0.0)
        l = jnp.sum(p, axis=0, keepdims=True)
        p = p / jnp.where(l > 0.0, l, 1.0)
        o_both = _dot(vct_ref[0], p.astype(BF16))
        ocmp_ref[HEAD_DIM * h:HEAD_DIM * (h + 1), :] = o_both[HEAD_DIM * g:HEAD_DIM * (g + 1)]
        psums[g] = p if psums[g] is None else psums[g] + p

    jblk = lax.broadcasted_iota(jnp.int32, (MAX_SLC, T), 0)
    tok = t0 + lax.broadcasted_iota(jnp.int32, (MAX_SLC, T), 1)
    cur = lax.shift_right_logical(tok, 6)
    forced = (jblk == 0) | (jblk == cur) | (jblk == cur - 1)
    valid = jblk * SLC_BLOCK <= tok
    jsub = lax.broadcasted_iota(jnp.int32, (8, T), 0)
    few_blocks = t0 + T <= SLC_TOPN * SLC_BLOCK

    @pl.when(few_blocks)
    def _():
        for g in range(NSA_KV_GROUPS):
            sel_ref[0, MAX_SLC * g:MAX_SLC * (g + 1), :] = jnp.where(valid, 0.0, NEG_BIG).astype(BF16)

    @pl.when(jnp.logical_not(few_blocks))
    def _():
        for g in range(NSA_KV_GROUPS):
            imp = _dot(ovl_ref[...], psums[g].astype(BF16))
            score = jnp.where(valid, imp + jnp.where(forced, FORCE_BONUS, 0.0), NEG_BIG)
            blocks = [score[8 * rb:8 * rb + 8] for rb in range(MAX_SLC // 8)]
            ranks = [jnp.zeros((8, T), F32) for _ in blocks]
            for jp in range(MAX_SLC):
                other = score[jp:jp + 1, :]
                for rb, blk in enumerate(blocks):
                    ge = jnp.where(other >= blk, 1.0, 0.0)
                    gt = jnp.where(other > blk, 1.0, 0.0)
                    if 8 * rb > jp:
                        inc = ge
                    elif 8 * rb + 7 < jp:
                        inc = gt
                    else:
                        inc = jnp.where(jsub > jp - 8 * rb, ge, gt)
                    ranks[rb] = ranks[rb] + inc
            rank = jnp.concatenate(ranks, axis=0)
            sel_ref[0, MAX_SLC * g:MAX_SLC * (g + 1), :] = (
                jnp.where(rank < float(SLC_TOPN), 0.0, NEG_BIG).astype(BF16))


def _nsa_select(q_t, kc, vc_t, pbias, ovl_t):
    b, n_cmp_pad, _ = kc.shape
    m = q_t.shape[1]
    nq = m // b // T_SEL
    cols = lambda r: pl.BlockSpec((r, T_SEL), lambda i, j: (0, i * nq + j))
    per_b = lambda a: pl.BlockSpec((1,) + a.shape[1:], lambda i, j: (i, 0, 0))
    full = lambda a: pl.BlockSpec(a.shape, lambda i, j: (0,) * a.ndim)
    return pl.pallas_call(
        functools.partial(_nsa_select_kernel, n_cmp_pad=n_cmp_pad),
        grid=(b, nq),
        in_specs=[cols(D_NSA), per_b(kc), per_b(vc_t), full(pbias), full(ovl_t)],
        out_specs=[cols(D_NSA), pl.BlockSpec((1, NSA_KV_GROUPS * MAX_SLC, T_SEL), lambda i, j: (i, 0, j))],
        out_shape=[jax.ShapeDtypeStruct((D_NSA, m), F32),
                   jax.ShapeDtypeStruct((b, NSA_KV_GROUPS * MAX_SLC, m // b), BF16)],
        compiler_params=_params("parallel", "parallel"), name="nsa_select",
    )(q_t, kc, vc_t, pbias, ovl_t)


def _online_init(m_ref, l_ref, acc_ref):
    m_ref[...] = jnp.full(m_ref.shape, NEG_BIG, F32)
    l_ref[...] = jnp.zeros(l_ref.shape, F32)
    acc_ref[...] = jnp.zeros(acc_ref.shape, F32)


def _online_chunk(m_ref, l_ref, acc_ref, k_tile, q_ref, q_rows, v_t, bias_ref=None, ok=None):
    s = _dot(k_tile, q_ref[0:q_rows, :])
    if bias_ref is not None:
        s = s + bias_ref[...]
    if ok is not None:
        s = jnp.where(ok, s, NEG_BIG)
    m = m_ref[...]
    m_new = jnp.maximum(m, jnp.max(s, axis=0, keepdims=True))
    alpha = jnp.exp2(m - m_new)
    p = jnp.exp2(s - m_new)
    m_ref[...] = m_new
    l_ref[...] = alpha * l_ref[...] + jnp.sum(p, axis=0, keepdims=True)
    acc_ref[...] = alpha * acc_ref[...] + _dot(v_t, p.astype(BF16))


def _nsa_attend_kernel(qt_ref, sel_ref, ocmp_ref, gt_ref, ks_ref, kw_ref, vs_ref, vw_ref, bias_ref, onehot_ref,
                       o_ref, kaug_ref, qaug_ref, ms_ref, ls_ref, accs_ref, mw_ref, lw_ref, accw_ref):
    T = T_NSA
    j = pl.program_id(1)

    @pl.when(j == 0)
    def _():
        kaug_ref[:, 0:LANES] = ks_ref[0]
        kaug_ref[:, LANES:2 * LANES] = onehot_ref[...]
        qaug_ref[...] = jnp.zeros(qaug_ref.shape, BF16)

    for h in range(NSA_HEADS):
        g = h // NSA_HPG
        qaug_ref[HEAD_DIM * g:HEAD_DIM * (g + 1), h * T:(h + 1) * T] = qt_ref[HEAD_DIM * h:HEAD_DIM * (h + 1), :]
        qaug_ref[LANES:LANES + MAX_SLC, h * T:(h + 1) * T] = sel_ref[0, MAX_SLC * g:MAX_SLC * (g + 1), :]

    key = lax.broadcasted_iota(jnp.int32, (T, NSA_HEADS * T), 0)
    qry = lax.broadcasted_iota(jnp.int32, (T, NSA_HEADS * T), 1) & (T - 1)
    causal = key <= qry
    st_s = (ms_ref, ls_ref, accs_ref)
    st_w = (mw_ref, lw_ref, accw_ref)
    _online_init(*st_s)
    _online_init(*st_w)

    n_far = jnp.maximum(j - 1, 0)

    def far_body(c, carry):
        r0 = pl.multiple_of(c * (2 * T), 2 * T)
        _online_chunk(*st_s, kaug_ref[pl.ds(r0, 2 * T), :], qaug_ref, 2 * LANES, vs_ref[:, pl.ds(r0, 2 * T)])
        return carry

    lax.fori_loop(0, lax.shift_right_logical(n_far, 1), far_body, 0)

    @pl.when((n_far & 1) == 1)
    def _():
        r0 = pl.multiple_of((n_far - 1) * T, T)
        _online_chunk(*st_s, kaug_ref[pl.ds(r0, T), :], qaug_ref, 2 * LANES, vs_ref[:, pl.ds(r0, T)])
    for d in (1, 0):
        c = j - d
        r0 = pl.multiple_of(jnp.maximum(c, 0) * T, T)
        ok = causal if d == 0 else key < jnp.where(c >= 0, T, -1)
        _online_chunk(*st_s, kaug_ref[pl.ds(r0, T), :], qaug_ref, 2 * LANES, vs_ref[:, pl.ds(r0, T)],
                      bias_ref.at[d], ok)

    n_win = WINDOW // T
    for d in range(n_win + 1):
        c = j - d
        r0 = pl.multiple_of(jnp.maximum(c, 0) * T, T)
        in_range = key < jnp.where(c >= 0, T, -1)
        if d == 0:
            ok = causal
        elif d == n_win:
            ok = (key > qry) & in_range
        else:
            ok = in_range
        _online_chunk(*st_w, kw_ref[0, pl.ds(r0, T), :], qaug_ref, LANES, vw_ref[:, pl.ds(r0, T)],
                      bias_ref.at[d] if d <= 1 else None, ok)

    o_sel = accs_ref[...] / ls_ref[...]
    o_win = accw_ref[...] / lw_ref[...]
    gates = jax.nn.sigmoid(gt_ref[...])
    heads = []
    for h in range(NSA_HEADS):
        cols = slice(h * T, (h + 1) * T)
        rows = slice(HEAD_DIM * (h // NSA_HPG), HEAD_DIM * (h // NSA_HPG + 1))
        heads.append(gates[3 * h:3 * h + 1, :] * ocmp_ref[HEAD_DIM * h:HEAD_DIM * (h + 1), :]
                     + gates[3 * h + 1:3 * h + 2, :] * o_sel[rows, cols]
                     + gates[3 * h + 2:3 * h + 3, :] * o_win[rows, cols])
    o_ref[0] = jnp.concatenate(heads, axis=0).T.astype(o_ref.dtype)


def _nsa_attend(q_t, sel_t, ocmp_t, g_t, ks, kw, vs_t, vw_t, bias_t, onehot):
    b, s, _ = ks.shape
    T = T_NSA
    nq = s // T
    R = NSA_HEADS * T
    cols = lambda r: pl.BlockSpec((r, T), lambda i, j: (0, i * nq + j))
    row_b = lambda a: pl.BlockSpec((a.shape[0], s), lambda i, j: (0, i))
    per_b = lambda a: pl.BlockSpec((1,) + a.shape[1:], lambda i, j: (i, 0, 0))
    full = lambda a: pl.BlockSpec(a.shape, lambda i, j: (0,) * a.ndim)
    stat = [pltpu.VMEM((1, R), F32), pltpu.VMEM((1, R), F32), pltpu.VMEM((LANES, R), F32)]
    return pl.pallas_call(
        _nsa_attend_kernel,
        grid=(b, nq),
        in_specs=[cols(D_NSA), pl.BlockSpec((1, NSA_KV_GROUPS * MAX_SLC, T), lambda i, j: (i, 0, j)),
                  cols(D_NSA), cols(LANES), per_b(ks), per_b(kw), row_b(vs_t), row_b(vw_t),
                  full(bias_t), full(onehot)],
        out_specs=pl.BlockSpec((1, T, D_NSA), lambda i, j: (i, j, 0)),
        out_shape=jax.ShapeDtypeStruct((b, s, D_NSA), BF16),
        scratch_shapes=[pltpu.VMEM((s, 2 * LANES), BF16), pltpu.VMEM((2 * LANES, R), BF16)] + stat + stat,
        compiler_params=_params("parallel", "arbitrary"), name="nsa_attend",
    )(q_t, sel_t, ocmp_t, g_t, ks, kw, vs_t, vw_t, bias_t, onehot)


def _sb_kernel(qt_ref, k_ref, vt_ref, tri_ref, o_ref, carry_ref, acc_ref):
    T = T_SB
    j = pl.program_id(1)
    n_pair = SB_HEADS // 2
    key = lax.broadcasted_iota(jnp.int32, (T, 2 * T), 0)
    qry = lax.broadcasted_iota(jnp.int32, (T, 2 * T), 1) & (T - 1)
    strict = key < qry
    carry_ref[...] = jnp.zeros(carry_ref.shape, F32)
    acc_ref[...] = jnp.zeros(acc_ref.shape, F32)

    def tiles(cs, mask):
        starts = [pl.multiple_of(c * T, T) for c in cs]
        units = [(ci, p) for ci in range(len(cs)) for p in range(n_pair)]
        z2s, incls = [], []
        for c, p in units:
            q_even = qt_ref[HEAD_DIM * (2 * p):HEAD_DIM * (2 * p + 1), :]
            q_odd = qt_ref[HEAD_DIM * (2 * p + 1):HEAD_DIM * (2 * p + 2), :]
            zq = jnp.zeros_like(q_even)
            q_pair = jnp.concatenate([jnp.concatenate([q_even, zq], axis=0),
                                      jnp.concatenate([zq, q_odd], axis=0)], axis=1)
            z2s.append(_dot(k_ref[0, pl.ds(starts[c], T), LANES * p:LANES * (p + 1)], q_pair))
        for z2 in z2s:
            rest = jnp.maximum(z2, 0.0) + jnp.log2(1.0 + jnp.exp2(-jnp.abs(z2)))
            if mask is not None:
                rest = jnp.where(mask, rest, 0.0)
            hi, lo = _split_bf16(rest)
            incls.append(_dot(tri_ref[...], jnp.concatenate([hi, lo], axis=0)))
        carry = [carry_ref[p] for p in range(n_pair)]
        acc = [acc_ref[p] for p in range(n_pair)]
        for (c, p), z2, incl in zip(units, z2s, incls):
            a = jnp.exp2(z2 - incl - carry[p])
            if mask is not None:
                a = jnp.where(mask, a, 0.0)
            acc[p] = acc[p] + _dot(vt_ref[LANES * p:LANES * (p + 1), pl.ds(starts[c], T)], a.astype(BF16))
            carry[p] = carry[p] + incl[0:1, :]
        for p in range(n_pair):
            carry_ref[p] = carry[p]
            acc_ref[p] = acc[p]

    tiles([j], strict)

    n_group = lax.shift_right_logical(j, SB_GROUP_LOG2)

    def back_group(i, carry):
        c = j - 1 - (i << SB_GROUP_LOG2)
        tiles([c - k for k in range(1 << SB_GROUP_LOG2)], None)
        return carry

    lax.fori_loop(0, n_group, back_group, 0)

    def back_one(i, carry):
        tiles([j - 1 - (n_group << SB_GROUP_LOG2) - i], None)
        return carry

    lax.fori_loop(0, j & ((1 << SB_GROUP_LOG2) - 1), back_one, 0)
    row = lax.broadcasted_iota(jnp.int32, (LANES, T), 0)
    for p in range(n_pair):
        acc = acc_ref[p]
        o_ref[0, :, LANES * p:LANES * (p + 1)] = (
            jnp.where(row < HEAD_DIM, acc[:, 0:T], acc[:, T:2 * T]).T.astype(o_ref.dtype))


def _sb_attention(q_t, k, v_t, tri_t):
    b, s, _ = k.shape
    T = T_SB
    nq = s // T
    n_pair = SB_HEADS // 2
    return pl.pallas_call(
        _sb_kernel,
        grid=(b, nq),
        in_specs=[pl.BlockSpec((D_SB, T), lambda i, j: (0, i * nq + j)),
                  pl.BlockSpec((1, s, D_SB), lambda i, j: (i, 0, 0)),
                  pl.BlockSpec((D_SB, s), lambda i, j: (0, i)),
                  pl.BlockSpec(tri_t.shape, lambda i, j: (0, 0))],
        out_specs=pl.BlockSpec((1, T, D_SB), lambda i, j: (i, j, 0)),
        out_shape=jax.ShapeDtypeStruct((b, s, D_SB), BF16),
        scratch_shapes=[pltpu.VMEM((n_pair, 1, 2 * T), F32), pltpu.VMEM((n_pair, LANES, 2 * T), F32)],
        compiler_params=_params("parallel", "parallel"), name="sb_attention",
    )(q_t, k, v_t, tri_t)


def _hgrn_kernel(q_ref, f_ref, i_ref, lb_ref, nw_ref, ones_ref, bd_ref, o_ref,
                 st_ref, qj_ref, kj_ref, bj_ref, vj_ref, qd_ref, kd_ref, oc_ref, x_ref, u_ref, sb_ref, dec_ref):
    C = HG_SUB
    n_blk = T_HG // C

    @pl.when(pl.program_id(2) == 0)
    def _():
        st_ref[...] = jnp.zeros(st_ref.shape, F32)

    lb = lb_ref[...]
    c_floor = jnp.log(jnp.maximum(lb, LB_FLOOR))
    c_rest = jnp.log1p(-lb)
    one_m_lb = 1.0 - lb
    b_run = jnp.zeros((n_blk, LANES), F32)
    for j in range(C):
        fj = f_ref[0, pl.ds(j, n_blk, stride=C), :]
        log_sig = -(jnp.maximum(-fj, 0.0) + jnp.log(1.0 + jnp.exp(-jnp.abs(fj))))
        a, bb = c_floor, c_rest + log_sig
        log_f = jnp.maximum(a, bb) + jnp.log(1.0 + jnp.exp(-jnp.abs(a - bb)))
        b_run = b_run + log_f
        bj_ref[j] = b_run * LOG2E
        kj_ref[j] = one_m_lb * jax.nn.sigmoid(-fj)
        qj_ref[j] = q_ref[0, pl.ds(j, n_blk, stride=C), :]
        vj_ref[j] = i_ref[0, pl.ds(j, n_blk, stride=C), :]
    b_last = bj_ref[C - 1]
    for j in range(C):
        bj = bj_ref[j]
        qd_ref[pl.ds(j, n_blk, stride=HG_PITCH), :] = qj_ref[j] * jnp.exp2(bj)
        kd_ref[pl.ds(j, n_blk, stride=HG_PITCH), :] = kj_ref[j] * jnp.exp2(b_last - bj)

    def kv_products(blk, carry):
        kd = kd_ref[pl.ds(pl.multiple_of(blk * HG_PITCH, 8), C), :].astype(BF16)
        vv = i_ref[0, pl.ds(pl.multiple_of(blk * C, C), C), :].astype(BF16)
        u_ref[blk] = bd_ref[...] * _dot_tn(vv, kd)
        return carry

    lax.fori_loop(0, n_blk, kv_products, 0, unroll=HG_UNROLL)
    dec_ref[...] = jnp.exp2(b_last)

    def scan(blk, st):
        sb_ref[blk] = st.astype(BF16)
        return st * dec_ref[pl.ds(blk, 1), :] + u_ref[blk]

    st_ref[...] = lax.fori_loop(0, n_blk, scan, st_ref[...], unroll=HG_UNROLL)

    def outputs(blk, carry):
        r0 = pl.multiple_of(blk * HG_PITCH, 8)
        oc_ref[pl.ds(r0, C), :] = _dot_nt(qd_ref[pl.ds(r0, C), :].astype(BF16), sb_ref[blk])
        return carry

    lax.fori_loop(0, n_blk, outputs, 0, unroll=HG_UNROLL)

    nw = nw_ref[...]
    for j in range(C):
        oj = oc_ref[pl.ds(j, n_blk, stride=HG_PITCH), :]
        qj = qj_ref[j]
        bj = bj_ref[j]
        for jp in range(j + 1):
            x = qj * kj_ref[jp] * jnp.exp2(bj - bj_ref[jp])
            x_ref[jp * n_blk:(jp + 1) * n_blk, :] = x.astype(BF16)
        att = _dot(x_ref[0:(j + 1) * n_blk, :], ones_ref[...])
        for jp in range(j + 1):
            oj = oj + att[jp * n_blk:(jp + 1) * n_blk] * vj_ref[jp]
        hi, lo = _split_bf16(oj * oj)
        ms = (_dot(hi, ones_ref[...]) + _dot(lo, ones_ref[...])) * (1.0 / HEAD_DIM)
        o_ref[0, pl.ds(j, n_blk, stride=C), :] = oj * lax.rsqrt(ms + RMS_EPS) * nw


def _hgrn(q, f, i, lb, nw, ones_bd, bd_mask):
    b, s, _ = q.shape
    n_blk = T_HG // HG_SUB
    tile = pl.BlockSpec((1, T_HG, LANES), lambda bi, pi, ti: (bi, ti, pi))
    vec = pl.BlockSpec((1, LANES), lambda bi, pi, ti: (0, pi))
    full = lambda a: pl.BlockSpec(a.shape, lambda bi, pi, ti: (0,) * a.ndim)
    jm = pltpu.VMEM((HG_SUB, n_blk, LANES), F32)
    nat = pltpu.VMEM((n_blk * HG_PITCH, LANES), F32)
    return pl.pallas_call(
        _hgrn_kernel,
        grid=(b, D_HG // LANES, s // T_HG),
        in_specs=[tile, tile, tile, vec, vec, full(ones_bd), full(bd_mask)],
        out_specs=tile,
        out_shape=jax.ShapeDtypeStruct((b, s, D_HG), F32),
        scratch_shapes=[pltpu.VMEM((LANES, LANES), F32), jm, jm, jm, jm, nat, nat, nat,
                        pltpu.VMEM((T_HG, LANES), BF16), pltpu.VMEM((n_blk, LANES, LANES), F32),
                        pltpu.VMEM((n_blk, LANES, LANES), BF16), pltpu.VMEM((n_blk, LANES), F32)],
        compiler_params=_params("parallel", "parallel", "arbitrary"), name="hgrn2",
    )(q, f, i, lb, nw, ones_bd, bd_mask)


def _out_kernel(onsa_ref, osb_ref, ohg_ref, z_ref, x_ref, wn_ref, ws_ref, wh_ref, g_ref, b_ref, o_ref):
    z = z_ref[...].astype(F32)
    sz = z * jax.nn.sigmoid(z)
    y = _dot((onsa_ref[...] * sz[:, 0:D_NSA]).astype(BF16), wn_ref[...])
    y += _dot((osb_ref[...] * sz[:, D_NSA:D_NSA + D_SB]).astype(BF16), ws_ref[...])
    y += _dot((ohg_ref[...] * sz[:, D_NSA + D_SB:D_MIX]).astype(BF16), wh_ref[...])
    v = ALPHA * x_ref[...] + y
    mu = jnp.mean(v, axis=-1, keepdims=True)
    vc = v - mu
    var = jnp.mean(vc * vc, axis=-1, keepdims=True)
    o_ref[...] = vc * lax.rsqrt(var + LN_EPS) * g_ref[...] + b_ref[...]


def _out_proj(o_nsa, o_sb, o_hg, z_all, x2d, wn, ws, wh, g, bvec):
    m = x2d.shape[0]
    rows = lambda w: pl.BlockSpec((T_PROJ, w), lambda i: (i, 0))
    full = lambda a: pl.BlockSpec(a.shape, lambda i: (0, 0))
    return pl.pallas_call(
        _out_kernel,
        grid=(m // T_PROJ,),
        in_specs=[rows(D_NSA), rows(D_SB), rows(D_HG), rows(D_MIX), rows(D_MODEL),
                  full(wn), full(ws), full(wh), full(g), full(bvec)],
        out_specs=rows(D_MODEL),
        out_shape=jax.ShapeDtypeStruct((m, D_MODEL), F32),
        compiler_params=_params("parallel"), name="out_proj_norm",
    )(o_nsa, o_sb, o_hg, z_all, x2d, wn, ws, wh, g, bvec)


def _t5_bucket_np(rel):
    n = np.maximum(rel, 0)
    max_exact = NUM_BUCKETS // 2
    large = max_exact + (np.log(np.maximum(n, 1).astype(np.float32) / max_exact)
                         / math.log(MAX_DISTANCE / max_exact) * (NUM_BUCKETS - max_exact)).astype(np.int32)
    large = np.clip(large, 0, NUM_BUCKETS - 1)
    return np.where(n < max_exact, n, large).astype(np.int32)


def _bias_tables(rel_bias, s):
    tbl = ((rel_bias - rel_bias[NUM_BUCKETS - 1]) * LOG2E).astype(F32)

    def expand(rel):
        onehot = (jnp.asarray(_t5_bucket_np(rel).reshape(-1, 1)) == jnp.arange(NUM_BUCKETS)[None, :]).astype(F32)
        return jnp.dot(onehot, tbl, precision=lax.Precision.HIGHEST).reshape(rel.shape + (NSA_HEADS,))

    n_cmp_pad = s // CMP_STRIDE
    n_rel = np.arange(2 * n_cmp_pad)[:, None] - n_cmp_pad
    r = np.arange(T_SEL)[None, :]
    pbias = jnp.transpose(expand(r - CMP_STRIDE * n_rel - (CMP_BLOCK - 1)), (2, 0, 1))
    T = T_NSA
    key = np.arange(T)[:, None]
    qry = np.arange(T)[None, :]
    near = np.stack([qry - key, T + qry - key])
    bias_t = jnp.transpose(expand(near), (0, 1, 3, 2)).reshape(2, T, NSA_HEADS * T)
    return pbias, bias_t


def _static_tables(s):
    n_chunk = s // CMP_STRIDE
    cmp_start = np.arange(n_chunk) * CMP_STRIDE
    slc_start = np.arange(MAX_SLC) * SLC_BLOCK
    ovl_t = ((cmp_start[None, :] < slc_start[:, None] + SLC_BLOCK)
             & (cmp_start[None, :] + CMP_BLOCK > slc_start[:, None])
             & (cmp_start[None, :] + CMP_BLOCK <= s)).astype(np.float32)
    onehot = (np.arange(s)[:, None] // SLC_BLOCK == np.arange(LANES)[None, :]).astype(np.float32)
    tri = (np.arange(T_SB)[None, :] >= np.arange(T_SB)[:, None]).astype(np.float32)
    tri = np.concatenate([tri, tri], axis=1)
    ones_bd = np.kron(np.eye(2), np.ones((HEAD_DIM, HEAD_DIM))).astype(np.float32)
    as_bf16 = lambda a: jnp.asarray(a, dtype=BF16)
    return as_bf16(ovl_t), as_bf16(onehot), as_bf16(tri), as_bf16(ones_bd), jnp.asarray(ones_bd)


def _layer_weights(w_in_l, cmp_pos_l, w_ck1_l, w_ck2_l, w_cv1_l, w_cv2_l):
    offs = np.cumsum((0,) + SPLIT_SIZES)
    (w_q, w_kc, w_vc, w_ks, w_vs, w_kw, w_vw, w_g, w_nz,
     w_sq, w_sk, w_sv, w_sz, w_hq, w_hf, w_hi, w_hz) = [w_in_l[:, offs[i]:offs[i + 1]] for i in range(len(SPLIT_SIZES))]
    w_gp = jnp.concatenate([w_g, jnp.zeros((D_MODEL, LANES - N_GATES), F32)], axis=1)
    w_z = jnp.concatenate([w_nz, w_sz, w_hz], axis=1)
    w_nat = [w_kc, w_vc, w_ks, w_kw, w_z, w_sk, w_hq, w_hf, w_hi]
    dt_nat = [F32, F32, BF16, BF16, BF16, BF16, F32, F32, F32]
    w_tr = [(w_q * QK_SCALE2).T, w_vs.T, w_vw.T, w_gp.T, (w_sq * QK_SCALE2).T, w_sv.T]
    dt_tr = [BF16, BF16, BF16, F32, BF16, BF16]
    w_nat = [w.astype(BF16) for w in w_nat]
    w_tr = [w.astype(BF16) for w in w_tr]

    def block_diag(w):
        z = jnp.zeros_like(w)
        return jnp.concatenate([jnp.concatenate([w, z], axis=-1), jnp.concatenate([z, w], axis=-1)], axis=-2)

    pos2 = jnp.concatenate([cmp_pos_l, cmp_pos_l], axis=1)
    w1k = block_diag(w_ck1_l.reshape(CMP_BLOCK, HEAD_DIM, CMP_HIDDEN)).astype(BF16)
    w1v = block_diag(w_cv1_l.reshape(CMP_BLOCK, HEAD_DIM, CMP_HIDDEN)).astype(BF16)
    w2k = block_diag(w_ck2_l).astype(BF16)
    w2v_t = block_diag(w_cv2_l).T.astype(BF16)
    return (w_nat, dt_nat, w_tr, dt_tr), (pos2, w1k, w2k, w1v, w2v_t)


def kernel(x, w_in, cmp_pos, w_ck1, w_ck2, w_cv1, w_cv2, hg_lb, hg_norm_w, w_out, ln_g, ln_b, rel_bias):
    b, s, d = x.shape
    assert d == D_MODEL and s % T_HG == 0 and s // SLC_BLOCK <= MAX_SLC and s >= WINDOW + T_NSA
    lb_w = jax.nn.softmax(hg_lb.astype(F32), axis=0)
    lb_all = jnp.cumsum(lb_w, axis=0) - lb_w[0]
    pbias, bias_t = _bias_tables(rel_bias, s)
    ovl_t, onehot, tri, ones_bd, bd_mask = _static_tables(s)

    x2d = x.reshape(b * s, d)
    for l in range(DEPTH):
        proj_w, cmp_w = _layer_weights(w_in[l], cmp_pos[l], w_ck1[l], w_ck2[l], w_cv1[l], w_cv2[l])
        nat, (q_t, vs_t, vw_t, g_t, sq_t, sv_t) = _inproj(x2d, *proj_w)
        kc_src, vc_src, ks, kw, z_all, sk, hq, hf, hi = [o.reshape(b, s, o.shape[-1]) for o in nat]
        kc, vc_t = _compress(kc_src, vc_src, *cmp_w)
        ocmp_t, sel_t = _nsa_select(q_t, kc, vc_t, pbias, ovl_t)
        o_nsa = _nsa_attend(q_t, sel_t, ocmp_t, g_t, ks, kw, vs_t, vw_t, bias_t, onehot)
        o_sb = _sb_attention(sq_t, sk, sv_t, tri)
        o_hg = _hgrn(hq, hf, hi, lb_all[l][None, :], hg_norm_w[l][None, :], ones_bd, bd_mask)
        wo = w_out[l].astype(BF16)
        x2d = _out_proj(o_nsa.reshape(b * s, D_NSA), o_sb.reshape(b * s, D_SB), o_hg.reshape(b * s, D_HG),
                        z_all.reshape(b * s, D_MIX), x2d,
                        wo[0:D_NSA], wo[D_NSA:D_NSA + D_SB], wo[D_NSA + D_SB:D_MIX],
                        ln_g[l][None, :], ln_b[l][None, :])
    return x2d.reshape(b, s, d)
```

```python
import functools
import math

import numpy as np
import jax
import jax.numpy as jnp
from jax import lax
from jax.experimental import pallas as pl
from jax.experimental.pallas import tpu as pltpu

F32 = jnp.float32
BF16 = jnp.bfloat16

D_MODEL = 1024
DEPTH = 2
HEAD_DIM = 64
LANES = 128
NSA_HEADS = 6
NSA_KV_GROUPS = 2
NSA_HPG = NSA_HEADS // NSA_KV_GROUPS
CMP_BLOCK = 32
CMP_STRIDE = 16
CMP_HIDDEN = 2 * HEAD_DIM
SLC_BLOCK = 64
SLC_TOPN = 16
MAX_SLC = 64
WINDOW = 512
FORCE_BONUS = 1000.0
NEG_BIG = -1e30
LB_FLOOR = 1e-30
SB_HEADS = 4
HG_HEADS = 6
HG_SUB = 16
HG_PITCH = 24
HG_UNROLL = 16
NUM_BUCKETS = 32
MAX_DISTANCE = 128
D_NSA = NSA_HEADS * HEAD_DIM
D_KV = NSA_KV_GROUPS * HEAD_DIM
D_SB = SB_HEADS * HEAD_DIM
D_HG = HG_HEADS * HEAD_DIM
D_MIX = D_NSA + D_SB + D_HG
N_GATES = NSA_HEADS * 3
SPLIT_SIZES = (D_NSA, D_KV, D_KV, D_KV, D_KV, D_KV, D_KV, N_GATES, D_NSA,
               D_SB, D_SB, D_SB, D_SB, D_HG, D_HG, D_HG, D_HG)
ALPHA = (2 * DEPTH) ** 0.25
LN_EPS = 1e-5
RMS_EPS = 1e-6
LOG2E = math.log2(math.e)
QK_SCALE2 = LOG2E / math.sqrt(HEAD_DIM)

T_SEL = 128
T_NSA = 256
T_SB = 256
SB_GROUP_LOG2 = 1
T_HG = 1024
T_PROJ = 512
VMEM_LIMIT = 56 * 1024 * 1024

_NT = (((1,), (1,)), ((), ()))
_TN = (((0,), (0,)), ((), ()))


def _dot(a, b):
    return jnp.dot(a, b, preferred_element_type=F32)


def _dot_nt(a, b):
    return lax.dot_general(a, b, _NT, preferred_element_type=F32)


def _dot_tn(a, b):
    return lax.dot_general(a, b, _TN, preferred_element_type=F32)


def _softplus(x):
    return jnp.maximum(x, 0.0) + jnp.log1p(jnp.exp(-jnp.abs(x)))


def _split_bf16(x):
    hi = x.astype(BF16)
    lo = (x - hi.astype(F32)).astype(BF16)
    return hi, lo


def _params(*sem, flags=None):
    return pltpu.CompilerParams(dimension_semantics=sem, vmem_limit_bytes=VMEM_LIMIT, flags=flags)


def _inproj_kernel(x_ref, wn_ref, wt_ref, *o_refs, n_nat):
    xb = x_ref[...].astype(BF16)
    nat = _dot(xb, wn_ref[...])
    off = 0
    for o_ref in o_refs[:n_nat]:
        o_ref[...] = nat[:, off:off + o_ref.shape[1]].astype(o_ref.dtype)
        off += o_ref.shape[1]
    tr = _dot_nt(wt_ref[...], xb)
    off = 0
    for o_ref in o_refs[n_nat:]:
        o_ref[...] = tr[off:off + o_ref.shape[0], :].astype(o_ref.dtype)
        off += o_ref.shape[0]


def _inproj(x2d, w_nat, dt_nat, w_tr, dt_tr):
    m = x2d.shape[0]
    wn = jnp.concatenate(w_nat, axis=1)
    wt = jnp.concatenate(w_tr, axis=0)
    in_specs = [pl.BlockSpec((T_PROJ, D_MODEL), lambda i: (i, 0)),
                pl.BlockSpec(wn.shape, lambda i: (0, 0)), pl.BlockSpec(wt.shape, lambda i: (0, 0))]
    out_specs = [pl.BlockSpec((T_PROJ, w.shape[1]), lambda i: (i, 0)) for w in w_nat]
    out_specs += [pl.BlockSpec((w.shape[0], T_PROJ), lambda i: (0, i)) for w in w_tr]
    out_shape = [jax.ShapeDtypeStruct((m, w.shape[1]), dt) for w, dt in zip(w_nat, dt_nat)]
    out_shape += [jax.ShapeDtypeStruct((w.shape[0], m), dt) for w, dt in zip(w_tr, dt_tr)]
    outs = pl.pallas_call(
        functools.partial(_inproj_kernel, n_nat=len(w_nat)),
        grid=(m // T_PROJ,),
        in_specs=in_specs, out_specs=out_specs, out_shape=out_shape,
        compiler_params=_params("parallel"), name="inproj",
    )(x2d, wn, wt)
    return outs[:len(w_nat)], outs[len(w_nat):]


def _compress_kernel(ksrc_ref, vsrc_ref, pos_ref, w1k_ref, w2k_ref, w1v_ref, w2v_ref,
                     kc_ref, vc_ref, *, n_chunk):
    def hidden(src_ref, w1_ref):
        top = jnp.zeros((n_chunk, 2 * CMP_HIDDEN), F32)
        bot = jnp.zeros((n_chunk, 2 * CMP_HIDDEN), F32)
        for p in range(CMP_STRIDE):
            xp = src_ref[0, pl.ds(p, n_chunk, stride=CMP_STRIDE), :]
            top += _dot((xp + pos_ref[p:p + 1, :]).astype(BF16), w1_ref[p])
            q = CMP_STRIDE + p
            bot += _dot((xp + pos_ref[q:q + 1, :]).astype(BF16), w1_ref[q])
        hid = top + pltpu.roll(bot, n_chunk - 1, 0)
        return jax.nn.gelu(hid).astype(BF16)

    kc_ref[0] = _dot(hidden(ksrc_ref, w1k_ref), w2k_ref[...]).astype(kc_ref.dtype)
    vc_ref[0] = _dot_nt(w2v_ref[...], hidden(vsrc_ref, w1v_ref)).astype(vc_ref.dtype)


def _compress(kc_src, vc_src, pos2, w1k, w2k, w1v, w2v_t):
    b, s, _ = kc_src.shape
    n_chunk = s // CMP_STRIDE
    full = lambda a: pl.BlockSpec(a.shape, lambda i: (0,) * a.ndim)
    src = pl.BlockSpec((1, s, D_KV), lambda i: (i, 0, 0))
    return pl.pallas_call(
        functools.partial(_compress_kernel, n_chunk=n_chunk),
        grid=(b,),
        in_specs=[src, src, full(pos2), full(w1k), full(w2k), full(w1v), full(w2v_t)],
        out_specs=[pl.BlockSpec((1, n_chunk, D_KV), lambda i: (i, 0, 0)),
                   pl.BlockSpec((1, D_KV, n_chunk), lambda i: (i, 0, 0))],
        out_shape=[jax.ShapeDtypeStruct((b, n_chunk, D_KV), BF16),
                   jax.ShapeDtypeStruct((b, D_KV, n_chunk), BF16)],
        compiler_params=_params("parallel"), name="nsa_compress",
    )(kc_src, vc_src, pos2, w1k, w2k, w1v, w2v_t)


def _nsa_select_kernel(qt_ref, kc_ref, vct_ref, pbias_ref, ovl_ref, ocmp_ref, sel_ref, *, n_cmp_pad):
    T = T_SEL
    j = pl.program_id(1)
    t0 = j * T
    n_idx = lax.broadcasted_iota(jnp.int32, (n_cmp_pad, T), 0)
    tok_c = t0 + lax.broadcasted_iota(jnp.int32, (n_cmp_pad, T), 1)
    mask_c = tok_c >= CMP_STRIDE * n_idx + (CMP_BLOCK - 1)
    off = pl.multiple_of(n_cmp_pad - (T // CMP_STRIDE) * j, 8)
    psums = [None] * NSA_KV_GROUPS
    for h in range(NSA_HEADS):
        g = h // NSA_HPG
        qh = qt_ref[HEAD_DIM * h:HEAD_DIM * (h + 1), :]
        zq = jnp.zeros_like(qh)
        qh = jnp.concatenate([qh, zq] if g == 0 else [zq, qh], axis=0)
        s = _dot(kc_ref[0], qh) + pbias_ref[h, pl.ds(off, n_cmp_pad), :]
        s = jnp.where(mask_c, s, NEG_BIG)
        m = jnp.max(s, axis=0, keepdims=True)
        p = jnp.where(mask_c, jnp.exp2(s - m), 0.0)
        l = jnp.sum(p, axis=0, keepdims=True)
        p = p / jnp.where(l > 0.0, l, 1.0)
        o_both = _dot(vct_ref[0], p.astype(BF16))
        ocmp_ref[HEAD_DIM * h:HEAD_DIM * (h + 1), :] = o_both[HEAD_DIM * g:HEAD_DIM * (g + 1)]
        psums[g] = p if psums[g] is None else psums[g] + p

    jblk = lax.broadcasted_iota(jnp.int32, (MAX_SLC, T), 0)
    tok = t0 + lax.broadcasted_iota(jnp.int32, (MAX_SLC, T), 1)
    cur = lax.shift_right_logical(tok, 6)
    forced = (jblk == 0) | (jblk == cur) | (jblk == cur - 1)
    valid = jblk * SLC_BLOCK <= tok
    jsub = lax.broadcasted_iota(jnp.int32, (8, T), 0)
    for g in range(NSA_KV_GROUPS):
        imp = _dot(ovl_ref[...], psums[g].astype(BF16))
        score = jnp.where(valid, imp + jnp.where(forced, FORCE_BONUS, 0.0), NEG_BIG)
        blocks = [score[8 * rb:8 * rb + 8] for rb in range(MAX_SLC // 8)]
        ranks = [jnp.zeros((8, T), F32) for _ in blocks]
        for jp in range(MAX_SLC):
            other = score[jp:jp + 1, :]
            for rb, blk in enumerate(blocks):
                ge = jnp.where(other >= blk, 1.0, 0.0)
                gt = jnp.where(other > blk, 1.0, 0.0)
                if 8 * rb > jp:
                    inc = ge
                elif 8 * rb + 7 < jp:
                    inc = gt
                else:
                    inc = jnp.where(jsub > jp - 8 * rb, ge, gt)
                ranks[rb] = ranks[rb] + inc
        rank = jnp.concatenate(ranks, axis=0)
        sel_ref[0, MAX_SLC * g:MAX_SLC * (g + 1), :] = jnp.where(rank < float(SLC_TOPN), 0.0, NEG_BIG).astype(BF16)


def _nsa_select(q_t, kc, vc_t, pbias, ovl_t):
    b, n_cmp_pad, _ = kc.shape
    m = q_t.shape[1]
    nq = m // b // T_SEL
    cols = lambda r: pl.BlockSpec((r, T_SEL), lambda i, j: (0, i * nq + j))
    per_b = lambda a: pl.BlockSpec((1,) + a.shape[1:], lambda i, j: (i, 0, 0))
    full = lambda a: pl.BlockSpec(a.shape, lambda i, j: (0,) * a.ndim)
    return pl.pallas_call(
        functools.partial(_nsa_select_kernel, n_cmp_pad=n_cmp_pad),
        grid=(b, nq),
        in_specs=[cols(D_NSA), per_b(kc), per_b(vc_t), full(pbias), full(ovl_t)],
        out_specs=[cols(D_NSA), pl.BlockSpec((1, NSA_KV_GROUPS * MAX_SLC, T_SEL), lambda i, j: (i, 0, j))],
        out_shape=[jax.ShapeDtypeStruct((D_NSA, m), F32),
                   jax.ShapeDtypeStruct((b, NSA_KV_GROUPS * MAX_SLC, m // b), BF16)],
        compiler_params=_params("parallel", "parallel"), name="nsa_select",
    )(q_t, kc, vc_t, pbias, ovl_t)


def _online_init(m_ref, l_ref, acc_ref):
    m_ref[...] = jnp.full(m_ref.shape, NEG_BIG, F32)
    l_ref[...] = jnp.zeros(l_ref.shape, F32)
    acc_ref[...] = jnp.zeros(acc_ref.shape, F32)


def _online_chunk(m_ref, l_ref, acc_ref, k_tile, q_ref, q_rows, v_t, bias_ref=None, ok=None):
    s = _dot(k_tile, q_ref[0:q_rows, :])
    if bias_ref is not None:
        s = s + bias_ref[...]
    if ok is not None:
        s = jnp.where(ok, s, NEG_BIG)
    m = m_ref[...]
    m_new = jnp.maximum(m, jnp.max(s, axis=0, keepdims=True))
    alpha = jnp.exp2(m - m_new)
    p = jnp.exp2(s - m_new)
    m_ref[...] = m_new
    l_ref[...] = alpha * l_ref[...] + jnp.sum(p, axis=0, keepdims=True)
    acc_ref[...] = alpha * acc_ref[...] + _dot(v_t, p.astype(BF16))


def _nsa_attend_kernel(qt_ref, sel_ref, ocmp_ref, gt_ref, ks_ref, kw_ref, vs_ref, vw_ref, bias_ref, onehot_ref,
                       o_ref, kaug_ref, qaug_ref, ms_ref, ls_ref, accs_ref, mw_ref, lw_ref, accw_ref):
    T = T_NSA
    j = pl.program_id(1)

    @pl.when(j == 0)
    def _():
        kaug_ref[:, 0:LANES] = ks_ref[0]
        kaug_ref[:, LANES:2 * LANES] = onehot_ref[...]
        qaug_ref[...] = jnp.zeros(qaug_ref.shape, BF16)

    for h in range(NSA_HEADS):
        g = h // NSA_HPG
        qaug_ref[HEAD_DIM * g:HEAD_DIM * (g + 1), h * T:(h + 1) * T] = qt_ref[HEAD_DIM * h:HEAD_DIM * (h + 1), :]
        qaug_ref[LANES:LANES + MAX_SLC, h * T:(h + 1) * T] = sel_ref[0, MAX_SLC * g:MAX_SLC * (g + 1), :]

    key = lax.broadcasted_iota(jnp.int32, (T, NSA_HEADS * T), 0)
    qry = lax.broadcasted_iota(jnp.int32, (T, NSA_HEADS * T), 1) & (T - 1)
    causal = key <= qry
    st_s = (ms_ref, ls_ref, accs_ref)
    st_w = (mw_ref, lw_ref, accw_ref)
    _online_init(*st_s)
    _online_init(*st_w)

    n_far = jnp.maximum(j - 1, 0)

    def far_body(c, carry):
        r0 = pl.multiple_of(c * (2 * T), 2 * T)
        _online_chunk(*st_s, kaug_ref[pl.ds(r0, 2 * T), :], qaug_ref, 2 * LANES, vs_ref[:, pl.ds(r0, 2 * T)])
        return carry

    lax.fori_loop(0, lax.shift_right_logical(n_far, 1), far_body, 0)

    @pl.when((n_far & 1) == 1)
    def _():
        r0 = pl.multiple_of((n_far - 1) * T, T)
        _online_chunk(*st_s, kaug_ref[pl.ds(r0, T), :], qaug_ref, 2 * LANES, vs_ref[:, pl.ds(r0, T)])
    for d in (1, 0):
        c = j - d
        r0 = pl.multiple_of(jnp.maximum(c, 0) * T, T)
        ok = causal if d == 0 else key < jnp.where(c >= 0, T, -1)
        _online_chunk(*st_s, kaug_ref[pl.ds(r0, T), :], qaug_ref, 2 * LANES, vs_ref[:, pl.ds(r0, T)],
                      bias_ref.at[d], ok)

    n_win = WINDOW // T
    for d in range(n_win + 1):
        c = j - d
        r0 = pl.multiple_of(jnp.maximum(c, 0) * T, T)
        in_range = key < jnp.where(c >= 0, T, -1)
        if d == 0:
            ok = causal
        elif d == n_win:
            ok = (key > qry) & in_range
        else:
            ok = in_range
        _online_chunk(*st_w, kw_ref[0, pl.ds(r0, T), :], qaug_ref, LANES, vw_ref[:, pl.ds(r0, T)],
                      bias_ref.at[d] if d <= 1 else None, ok)

    o_sel = accs_ref[...] / ls_ref[...]
    o_win = accw_ref[...] / lw_ref[...]
    gates = jax.nn.sigmoid(gt_ref[...])
    heads = []
    for h in range(NSA_HEADS):
        cols = slice(h * T, (h + 1) * T)
        rows = slice(HEAD_DIM * (h // NSA_HPG), HEAD_DIM * (h // NSA_HPG + 1))
        heads.append(gates[3 * h:3 * h + 1, :] * ocmp_ref[HEAD_DIM * h:HEAD_DIM * (h + 1), :]
                     + gates[3 * h + 1:3 * h + 2, :] * o_sel[rows, cols]
                     + gates[3 * h + 2:3 * h + 3, :] * o_win[rows, cols])
    o_ref[0] = jnp.concatenate(heads, axis=0).T.astype(o_ref.dtype)


def _nsa_attend(q_t, sel_t, ocmp_t, g_t, ks, kw, vs_t, vw_t, bias_t, onehot):
    b, s, _ = ks.shape
    T = T_NSA
    nq = s // T
    R = NSA_HEADS * T
    cols = lambda r: pl.BlockSpec((r, T), lambda i, j: (0, i * nq + j))
    row_b = lambda a: pl.BlockSpec((a.shape[0], s), lambda i, j: (0, i))
    per_b = lambda a: pl.BlockSpec((1,) + a.shape[1:], lambda i, j: (i, 0, 0))
    full = lambda a: pl.BlockSpec(a.shape, lambda i, j: (0,) * a.ndim)
    stat = [pltpu.VMEM((1, R), F32), pltpu.VMEM((1, R), F32), pltpu.VMEM((LANES, R), F32)]
    return pl.pallas_call(
        _nsa_attend_kernel,
        grid=(b, nq),
        in_specs=[cols(D_NSA), pl.BlockSpec((1, NSA_KV_GROUPS * MAX_SLC, T), lambda i, j: (i, 0, j)),
                  cols(D_NSA), cols(LANES), per_b(ks), per_b(kw), row_b(vs_t), row_b(vw_t),
                  full(bias_t), full(onehot)],
        out_specs=pl.BlockSpec((1, T, D_NSA), lambda i, j: (i, j, 0)),
        out_shape=jax.ShapeDtypeStruct((b, s, D_NSA), BF16),
        scratch_shapes=[pltpu.VMEM((s, 2 * LANES), BF16), pltpu.VMEM((2 * LANES, R), BF16)] + stat + stat,
        compiler_params=_params("parallel", "arbitrary"), name="nsa_attend",
    )(q_t, sel_t, ocmp_t, g_t, ks, kw, vs_t, vw_t, bias_t, onehot)


def _sb_kernel(qt_ref, k_ref, vt_ref, tri_ref, o_ref, carry_ref, acc_ref):
    T = T_SB
    j = pl.program_id(1)
    n_pair = SB_HEADS // 2
    key = lax.broadcasted_iota(jnp.int32, (T, 2 * T), 0)
    qry = lax.broadcasted_iota(jnp.int32, (T, 2 * T), 1) & (T - 1)
    strict = key < qry
    carry_ref[...] = jnp.zeros(carry_ref.shape, F32)
    acc_ref[...] = jnp.zeros(acc_ref.shape, F32)

    def tiles(cs, mask):
        starts = [pl.multiple_of(c * T, T) for c in cs]
        units = [(ci, p) for ci in range(len(cs)) for p in range(n_pair)]
        z2s, incls = [], []
        for c, p in units:
            q_even = qt_ref[HEAD_DIM * (2 * p):HEAD_DIM * (2 * p + 1), :]
            q_odd = qt_ref[HEAD_DIM * (2 * p + 1):HEAD_DIM * (2 * p + 2), :]
            zq = jnp.zeros_like(q_even)
            q_pair = jnp.concatenate([jnp.concatenate([q_even, zq], axis=0),
                                      jnp.concatenate([zq, q_odd], axis=0)], axis=1)
            z2s.append(_dot(k_ref[0, pl.ds(starts[c], T), LANES * p:LANES * (p + 1)], q_pair))
        for z2 in z2s:
            rest = jnp.maximum(z2, 0.0) + jnp.log2(1.0 + jnp.exp2(-jnp.abs(z2)))
            if mask is not None:
                rest = jnp.where(mask, rest, 0.0)
            hi, lo = _split_bf16(rest)
            incls.append(_dot(tri_ref[...], jnp.concatenate([hi, lo], axis=0)))
        carry = [carry_ref[p] for p in range(n_pair)]
        acc = [acc_ref[p] for p in range(n_pair)]
        for (c, p), z2, incl in zip(units, z2s, incls):
            a = jnp.exp2(z2 - incl - carry[p])
            if mask is not None:
                a = jnp.where(mask, a, 0.0)
            acc[p] = acc[p] + _dot(vt_ref[LANES * p:LANES * (p + 1), pl.ds(starts[c], T)], a.astype(BF16))
            carry[p] = carry[p] + incl[0:1, :]
        for p in range(n_pair):
            carry_ref[p] = carry[p]
            acc_ref[p] = acc[p]

    tiles([j], strict)

    n_group = lax.shift_right_logical(j, SB_GROUP_LOG2)

    def back_group(i, carry):
        c = j - 1 - (i << SB_GROUP_LOG2)
        tiles([c - k for k in range(1 << SB_GROUP_LOG2)], None)
        return carry

    lax.fori_loop(0, n_group, back_group, 0)

    def back_one(i, carry):
        tiles([j - 1 - (n_group << SB_GROUP_LOG2) - i], None)
        return carry

    lax.fori_loop(0, j & ((1 << SB_GROUP_LOG2) - 1), back_one, 0)
    row = lax.broadcasted_iota(jnp.int32, (LANES, T), 0)
    for p in range(n_pair):
        acc = acc_ref[p]
        o_ref[0, :, LANES * p:LANES * (p + 1)] = (
            jnp.where(row < HEAD_DIM, acc[:, 0:T], acc[:, T:2 * T]).T.astype(o_ref.dtype))


def _sb_attention(q_t, k, v_t, tri_t):
    b, s, _ = k.shape
    T = T_SB
    nq = s // T
    n_pair = SB_HEADS // 2
    return pl.pallas_call(
        _sb_kernel,
        grid=(b, nq),
        in_specs=[pl.BlockSpec((D_SB, T), lambda i, j: (0, i * nq + j)),
                  pl.BlockSpec((1, s, D_SB), lambda i, j: (i, 0, 0)),
                  pl.BlockSpec((D_SB, s), lambda i, j: (0, i)),
                  pl.BlockSpec(tri_t.shape, lambda i, j: (0, 0))],
        out_specs=pl.BlockSpec((1, T, D_SB), lambda i, j: (i, j, 0)),
        out_shape=jax.ShapeDtypeStruct((b, s, D_SB), BF16),
        scratch_shapes=[pltpu.VMEM((n_pair, 1, 2 * T), F32), pltpu.VMEM((n_pair, LANES, 2 * T), F32)],
        compiler_params=_params("parallel", "parallel"), name="sb_attention",
    )(q_t, k, v_t, tri_t)


def _hgrn_kernel(q_ref, f_ref, i_ref, lb_ref, nw_ref, ones_ref, bd_ref, o_ref,
                 st_ref, qj_ref, kj_ref, bj_ref, vj_ref, qd_ref, kd_ref, oc_ref, x_ref, u_ref, sb_ref, dec_ref):
    C = HG_SUB
    n_blk = T_HG // C

    @pl.when(pl.program_id(2) == 0)
    def _():
        st_ref[...] = jnp.zeros(st_ref.shape, F32)

    lb = lb_ref[...]
    c_floor = jnp.log(jnp.maximum(lb, LB_FLOOR))
    c_rest = jnp.log1p(-lb)
    one_m_lb = 1.0 - lb
    b_run = jnp.zeros((n_blk, LANES), F32)
    for j in range(C):
        fj = f_ref[0, pl.ds(j, n_blk, stride=C), :]
        log_sig = -(jnp.maximum(-fj, 0.0) + jnp.log(1.0 + jnp.exp(-jnp.abs(fj))))
        a, bb = c_floor, c_rest + log_sig
        log_f = jnp.maximum(a, bb) + jnp.log(1.0 + jnp.exp(-jnp.abs(a - bb)))
        b_run = b_run + log_f
        bj_ref[j] = b_run * LOG2E
        kj_ref[j] = one_m_lb * jax.nn.sigmoid(-fj)
        qj_ref[j] = q_ref[0, pl.ds(j, n_blk, stride=C), :]
        vj_ref[j] = i_ref[0, pl.ds(j, n_blk, stride=C), :]
    b_last = bj_ref[C - 1]
    for j in range(C):
        bj = bj_ref[j]
        qd_ref[pl.ds(j, n_blk, stride=HG_PITCH), :] = qj_ref[j] * jnp.exp2(bj)
        kd_ref[pl.ds(j, n_blk, stride=HG_PITCH), :] = kj_ref[j] * jnp.exp2(b_last - bj)

    def kv_products(blk, carry):
        kd = kd_ref[pl.ds(pl.multiple_of(blk * HG_PITCH, 8), C), :].astype(BF16)
        vv = i_ref[0, pl.ds(pl.multiple_of(blk * C, C), C), :].astype(BF16)
        u_ref[blk] = bd_ref[...] * _dot_tn(vv, kd)
        return carry

    lax.fori_loop(0, n_blk, kv_products, 0, unroll=HG_UNROLL)
    dec_ref[...] = jnp.exp2(b_last)

    def scan(blk, st):
        sb_ref[blk] = st.astype(BF16)
        return st * dec_ref[pl.ds(blk, 1), :] + u_ref[blk]

    st_ref[...] = lax.fori_loop(0, n_blk, scan, st_ref[...], unroll=HG_UNROLL)

    def outputs(blk, carry):
        r0 = pl.multiple_of(blk * HG_PITCH, 8)
        oc_ref[pl.ds(r0, C), :] = _dot_nt(qd_ref[pl.ds(r0, C), :].astype(BF16), sb_ref[blk])
        return carry

    lax.fori_loop(0, n_blk, outputs, 0, unroll=HG_UNROLL)

    nw = nw_ref[...]
    for j in range(C):
        oj = oc_ref[pl.ds(j, n_blk, stride=HG_PITCH), :]
        qj = qj_ref[j]
        bj = bj_ref[j]
        for jp in range(j + 1):
            x = qj * kj_ref[jp] * jnp.exp2(bj - bj_ref[jp])
            x_ref[jp * n_blk:(jp + 1) * n_blk, :] = x.astype(BF16)
        att = _dot(x_ref[0:(j + 1) * n_blk, :], ones_ref[...])
        for jp in range(j + 1):
            oj = oj + att[jp * n_blk:(jp + 1) * n_blk] * vj_ref[jp]
        hi, lo = _split_bf16(oj * oj)
        ms = (_dot(hi, ones_ref[...]) + _dot(lo, ones_ref[...])) * (1.0 / HEAD_DIM)
        o_ref[0, pl.ds(j, n_blk, stride=C), :] = oj * lax.rsqrt(ms + RMS_EPS) * nw


def _hgrn(q, f, i, lb, nw, ones_bd, bd_mask):
    b, s, _ = q.shape
    n_blk = T_HG // HG_SUB
    tile = pl.BlockSpec((1, T_HG, LANES), lambda bi, pi, ti: (bi, ti, pi))
    vec = pl.BlockSpec((1, LANES), lambda bi, pi, ti: (0, pi))
    full = lambda a: pl.BlockSpec(a.shape, lambda bi, pi, ti: (0,) * a.ndim)
    jm = pltpu.VMEM((HG_SUB, n_blk, LANES), F32)
    nat = pltpu.VMEM((n_blk * HG_PITCH, LANES), F32)
    return pl.pallas_call(
        _hgrn_kernel,
        grid=(b, D_HG // LANES, s // T_HG),
        in_specs=[tile, tile, tile, vec, vec, full(ones_bd), full(bd_mask)],
        out_specs=tile,
        out_shape=jax.ShapeDtypeStruct((b, s, D_HG), F32),
        scratch_shapes=[pltpu.VMEM((LANES, LANES), F32), jm, jm, jm, jm, nat, nat, nat,
                        pltpu.VMEM((T_HG, LANES), BF16), pltpu.VMEM((n_blk, LANES, LANES), F32),
                        pltpu.VMEM((n_blk, LANES, LANES), BF16), pltpu.VMEM((n_blk, LANES), F32)],
        compiler_params=_params("parallel", "parallel", "arbitrary"), name="hgrn2",
    )(q, f, i, lb, nw, ones_bd, bd_mask)


def _out_kernel(onsa_ref, osb_ref, ohg_ref, z_ref, x_ref, w_ref, g_ref, b_ref, o_ref):
    z = z_ref[...].astype(F32)
    sz = z * jax.nn.sigmoid(z)
    mixed = jnp.concatenate([(onsa_ref[...] * sz[:, 0:D_NSA]).astype(BF16),
                             (osb_ref[...] * sz[:, D_NSA:D_NSA + D_SB]).astype(BF16),
                             (ohg_ref[...] * sz[:, D_NSA + D_SB:D_MIX]).astype(BF16)], axis=1)
    v = ALPHA * x_ref[...] + _dot(mixed, w_ref[...])
    mu = jnp.mean(v, axis=-1, keepdims=True)
    vc = v - mu
    var = jnp.mean(vc * vc, axis=-1, keepdims=True)
    o_ref[...] = vc * lax.rsqrt(var + LN_EPS) * g_ref[...] + b_ref[...]


def _out_proj(o_nsa, o_sb, o_hg, z_all, x2d, w, g, bvec):
    m = x2d.shape[0]
    rows = lambda n: pl.BlockSpec((T_PROJ, n), lambda i: (i, 0))
    full = lambda a: pl.BlockSpec(a.shape, lambda i: (0, 0))
    return pl.pallas_call(
        _out_kernel,
        grid=(m // T_PROJ,),
        in_specs=[rows(D_NSA), rows(D_SB), rows(D_HG), rows(D_MIX), rows(D_MODEL), full(w), full(g), full(bvec)],
        out_specs=rows(D_MODEL),
        out_shape=jax.ShapeDtypeStruct((m, D_MODEL), F32),
        compiler_params=_params("parallel"), name="out_proj_norm",
    )(o_nsa, o_sb, o_hg, z_all, x2d, w, g, bvec)


def _t5_bucket_np(rel):
    n = np.maximum(rel, 0)
    max_exact = NUM_BUCKETS // 2
    large = max_exact + (np.log(np.maximum(n, 1).astype(np.float32) / max_exact)
                         / math.log(MAX_DISTANCE / max_exact) * (NUM_BUCKETS - max_exact)).astype(np.int32)
    large = np.clip(large, 0, NUM_BUCKETS - 1)
    return np.where(n < max_exact, n, large).astype(np.int32)


def _bias_tables(rel_bias, s):
    tbl = ((rel_bias - rel_bias[NUM_BUCKETS - 1]) * LOG2E).astype(F32)

    def expand(rel):
        onehot = (jnp.asarray(_t5_bucket_np(rel).reshape(-1, 1)) == jnp.arange(NUM_BUCKETS)[None, :]).astype(F32)
        return jnp.dot(onehot, tbl, precision=lax.Precision.HIGHEST).reshape(rel.shape + (NSA_HEADS,))

    n_cmp_pad = s // CMP_STRIDE
    n_rel = np.arange(2 * n_cmp_pad)[:, None] - n_cmp_pad
    r = np.arange(T_SEL)[None, :]
    pbias = jnp.transpose(expand(r - CMP_STRIDE * n_rel - (CMP_BLOCK - 1)), (2, 0, 1))
    T = T_NSA
    key = np.arange(T)[:, None]
    qry = np.arange(T)[None, :]
    near = np.stack([qry - key, T + qry - key])
    bias_t = jnp.transpose(expand(near), (0, 1, 3, 2)).reshape(2, T, NSA_HEADS * T)
    return pbias, bias_t


def _static_tables(s):
    n_chunk = s // CMP_STRIDE
    cmp_start = np.arange(n_chunk) * CMP_STRIDE
    slc_start = np.arange(MAX_SLC) * SLC_BLOCK
    ovl_t = ((cmp_start[None, :] < slc_start[:, None] + SLC_BLOCK)
             & (cmp_start[None, :] + CMP_BLOCK > slc_start[:, None])
             & (cmp_start[None, :] + CMP_BLOCK <= s)).astype(np.float32)
    onehot = (np.arange(s)[:, None] // SLC_BLOCK == np.arange(LANES)[None, :]).astype(np.float32)
    tri = (np.arange(T_SB)[None, :] >= np.arange(T_SB)[:, None]).astype(np.float32)
    tri = np.concatenate([tri, tri], axis=1)
    ones_bd = np.kron(np.eye(2), np.ones((HEAD_DIM, HEAD_DIM))).astype(np.float32)
    as_bf16 = lambda a: jnp.asarray(a, dtype=BF16)
    return as_bf16(ovl_t), as_bf16(onehot), as_bf16(tri), as_bf16(ones_bd), jnp.asarray(ones_bd)


def _layer_weights(w_in_l, cmp_pos_l, w_ck1_l, w_ck2_l, w_cv1_l, w_cv2_l):
    offs = np.cumsum((0,) + SPLIT_SIZES)
    (w_q, w_kc, w_vc, w_ks, w_vs, w_kw, w_vw, w_g, w_nz,
     w_sq, w_sk, w_sv, w_sz, w_hq, w_hf, w_hi, w_hz) = [w_in_l[:, offs[i]:offs[i + 1]] for i in range(len(SPLIT_SIZES))]
    w_gp = jnp.concatenate([w_g, jnp.zeros((D_MODEL, LANES - N_GATES), F32)], axis=1)
    w_z = jnp.concatenate([w_nz, w_sz, w_hz], axis=1)
    w_nat = [w_kc, w_vc, w_ks, w_kw, w_z, w_sk, w_hq, w_hf, w_hi]
    dt_nat = [F32, F32, BF16, BF16, BF16, BF16, F32, F32, F32]
    w_tr = [(w_q * QK_SCALE2).T, w_vs.T, w_vw.T, w_gp.T, (w_sq * QK_SCALE2).T, w_sv.T]
    dt_tr = [BF16, BF16, BF16, F32, BF16, BF16]
    w_nat = [w.astype(BF16) for w in w_nat]
    w_tr = [w.astype(BF16) for w in w_tr]

    def block_diag(w):
        z = jnp.zeros_like(w)
        return jnp.concatenate([jnp.concatenate([w, z], axis=-1), jnp.concatenate([z, w], axis=-1)], axis=-2)

    pos2 = jnp.concatenate([cmp_pos_l, cmp_pos_l], axis=1)
    w1k = block_diag(w_ck1_l.reshape(CMP_BLOCK, HEAD_DIM, CMP_HIDDEN)).astype(BF16)
    w1v = block_diag(w_cv1_l.reshape(CMP_BLOCK, HEAD_DIM, CMP_HIDDEN)).astype(BF16)
    w2k = block_diag(w_ck2_l).astype(BF16)
    w2v_t = block_diag(w_cv2_l).T.astype(BF16)
    return (w_nat, dt_nat, w_tr, dt_tr), (pos2, w1k, w2k, w1v, w2v_t)


def kernel(x, w_in, cmp_pos, w_ck1, w_ck2, w_cv1, w_cv2, hg_lb, hg_norm_w, w_out, ln_g, ln_b, rel_bias):
    b, s, d = x.shape
    assert d == D_MODEL and s % T_HG == 0 and s // SLC_BLOCK <= MAX_SLC and s >= WINDOW + T_NSA
    lb_w = jax.nn.softmax(hg_lb.astype(F32), axis=0)
    lb_all = jnp.cumsum(lb_w, axis=0) - lb_w[0]
    pbias, bias_t = _bias_tables(rel_bias, s)
    ovl_t, onehot, tri, ones_bd, bd_mask = _static_tables(s)

    x2d = x.reshape(b * s, d)
    for l in range(DEPTH):
        proj_w, cmp_w = _layer_weights(w_in[l], cmp_pos[l], w_ck1[l], w_ck2[l], w_cv1[l], w_cv2[l])
        nat, (q_t, vs_t, vw_t, g_t, sq_t, sv_t) = _inproj(x2d, *proj_w)
        kc_src, vc_src, ks, kw, z_all, sk, hq, hf, hi = [o.reshape(b, s, o.shape[-1]) for o in nat]
        kc, vc_t = _compress(kc_src, vc_src, *cmp_w)
        ocmp_t, sel_t = _nsa_select(q_t, kc, vc_t, pbias, ovl_t)
        o_nsa = _nsa_attend(q_t, sel_t, ocmp_t, g_t, ks, kw, vs_t, vw_t, bias_t, onehot)
        o_sb = _sb_attention(sq_t, sk, sv_t, tri)
        o_hg = _hgrn(hq, hf, hi, lb_all[l][None, :], hg_norm_w[l][None, :], ones_bd, bd_mask)
        x2d = _out_proj(o_nsa.reshape(b * s, D_NSA), o_sb.reshape(b * s, D_SB), o_hg.reshape(b * s, D_HG),
                        z_all.reshape(b * s, D_MIX), x2d, w_out[l].astype(BF16),
                        ln_g[l][None, :], ln_b[l][None, :])
    return x2d.reshape(b, s, d)
```

```python
import functools
import math

import numpy as np
import jax
import jax.numpy as jnp
from jax import lax
from jax.experimental import pallas as pl
from jax.experimental.pallas import tpu as pltpu

F32 = jnp.float32
BF16 = jnp.bfloat16

D_MODEL = 1024
DEPTH = 2
HEAD_DIM = 64
LANES = 128
NSA_HEADS = 6
NSA_KV_GROUPS = 2
NSA_HPG = NSA_HEADS // NSA_KV_GROUPS
CMP_BLOCK = 32
CMP_STRIDE = 16
CMP_HIDDEN = 2 * HEAD_DIM
SLC_BLOCK = 64
SLC_TOPN = 16
MAX_SLC = 64
WINDOW = 512
FORCE_BONUS = 1000.0
NEG_BIG = -1e30
LB_FLOOR = 1e-30
SB_HEADS = 4
HG_HEADS = 6
HG_SUB = 16
HG_PITCH = 24
HG_UNROLL = 16
NUM_BUCKETS = 32
MAX_DISTANCE = 128
D_NSA = NSA_HEADS * HEAD_DIM
D_KV = NSA_KV_GROUPS * HEAD_DIM
D_SB = SB_HEADS * HEAD_DIM
D_HG = HG_HEADS * HEAD_DIM
D_MIX = D_NSA + D_SB + D_HG
N_GATES = NSA_HEADS * 3
SPLIT_SIZES = (D_NSA, D_KV, D_KV, D_KV, D_KV, D_KV, D_KV, N_GATES, D_NSA,
               D_SB, D_SB, D_SB, D_SB, D_HG, D_HG, D_HG, D_HG)
ALPHA = (2 * DEPTH) ** 0.25
LN_EPS = 1e-5
RMS_EPS = 1e-6
LOG2E = math.log2(math.e)
QK_SCALE2 = LOG2E / math.sqrt(HEAD_DIM)

T_SEL = 128
T_NSA = 256
T_SB = 256
SB_DEAD_BITS = 151.0
T_HG = 1024
T_PROJ = 512
VMEM_LIMIT = 56 * 1024 * 1024

_NT = (((1,), (1,)), ((), ()))
_TN = (((0,), (0,)), ((), ()))


def _dot(a, b):
    return jnp.dot(a, b, preferred_element_type=F32)


def _dot_nt(a, b):
    return lax.dot_general(a, b, _NT, preferred_element_type=F32)


def _dot_tn(a, b):
    return lax.dot_general(a, b, _TN, preferred_element_type=F32)


def _softplus(x):
    return jnp.maximum(x, 0.0) + jnp.log1p(jnp.exp(-jnp.abs(x)))


def _split_bf16(x):
    hi = x.astype(BF16)
    lo = (x - hi.astype(F32)).astype(BF16)
    return hi, lo


def _params(*sem, flags=None):
    return pltpu.CompilerParams(dimension_semantics=sem, vmem_limit_bytes=VMEM_LIMIT, flags=flags)


def _inproj_kernel(x_ref, wn_ref, wt_ref, *o_refs, n_nat):
    xb = x_ref[...].astype(BF16)
    nat = _dot(xb, wn_ref[...])
    off = 0
    for o_ref in o_refs[:n_nat]:
        o_ref[...] = nat[:, off:off + o_ref.shape[1]].astype(o_ref.dtype)
        off += o_ref.shape[1]
    tr = _dot_nt(wt_ref[...], xb)
    off = 0
    for o_ref in o_refs[n_nat:]:
        o_ref[...] = tr[off:off + o_ref.shape[0], :].astype(o_ref.dtype)
        off += o_ref.shape[0]


def _inproj(x2d, w_nat, dt_nat, w_tr, dt_tr):
    m = x2d.shape[0]
    wn = jnp.concatenate(w_nat, axis=1)
    wt = jnp.concatenate(w_tr, axis=0)
    in_specs = [pl.BlockSpec((T_PROJ, D_MODEL), lambda i: (i, 0)),
                pl.BlockSpec(wn.shape, lambda i: (0, 0)), pl.BlockSpec(wt.shape, lambda i: (0, 0))]
    out_specs = [pl.BlockSpec((T_PROJ, w.shape[1]), lambda i: (i, 0)) for w in w_nat]
    out_specs += [pl.BlockSpec((w.shape[0], T_PROJ), lambda i: (0, i)) for w in w_tr]
    out_shape = [jax.ShapeDtypeStruct((m, w.shape[1]), dt) for w, dt in zip(w_nat, dt_nat)]
    out_shape += [jax.ShapeDtypeStruct((w.shape[0], m), dt) for w, dt in zip(w_tr, dt_tr)]
    outs = pl.pallas_call(
        functools.partial(_inproj_kernel, n_nat=len(w_nat)),
        grid=(m // T_PROJ,),
        in_specs=in_specs, out_specs=out_specs, out_shape=out_shape,
        compiler_params=_params("parallel"), name="inproj",
    )(x2d, wn, wt)
    return outs[:len(w_nat)], outs[len(w_nat):]


def _compress_kernel(ksrc_ref, vsrc_ref, pos_ref, w1k_ref, w2k_ref, w1v_ref, w2v_ref,
                     kc_ref, vc_ref, *, n_chunk):
    def hidden(src_ref, w1_ref):
        top = jnp.zeros((n_chunk, 2 * CMP_HIDDEN), F32)
        bot = jnp.zeros((n_chunk, 2 * CMP_HIDDEN), F32)
        for p in range(CMP_STRIDE):
            xp = src_ref[0, pl.ds(p, n_chunk, stride=CMP_STRIDE), :]
            top += _dot((xp + pos_ref[p:p + 1, :]).astype(BF16), w1_ref[p])
            q = CMP_STRIDE + p
            bot += _dot((xp + pos_ref[q:q + 1, :]).astype(BF16), w1_ref[q])
        hid = top + pltpu.roll(bot, n_chunk - 1, 0)
        return jax.nn.gelu(hid).astype(BF16)

    kc_ref[0] = _dot(hidden(ksrc_ref, w1k_ref), w2k_ref[...]).astype(kc_ref.dtype)
    vc_ref[0] = _dot_nt(w2v_ref[...], hidden(vsrc_ref, w1v_ref)).astype(vc_ref.dtype)


def _compress(kc_src, vc_src, pos2, w1k, w2k, w1v, w2v_t):
    b, s, _ = kc_src.shape
    n_chunk = s // CMP_STRIDE
    full = lambda a: pl.BlockSpec(a.shape, lambda i: (0,) * a.ndim)
    src = pl.BlockSpec((1, s, D_KV), lambda i: (i, 0, 0))
    return pl.pallas_call(
        functools.partial(_compress_kernel, n_chunk=n_chunk),
        grid=(b,),
        in_specs=[src, src, full(pos2), full(w1k), full(w2k), full(w1v), full(w2v_t)],
        out_specs=[pl.BlockSpec((1, n_chunk, D_KV), lambda i: (i, 0, 0)),
                   pl.BlockSpec((1, D_KV, n_chunk), lambda i: (i, 0, 0))],
        out_shape=[jax.ShapeDtypeStruct((b, n_chunk, D_KV), BF16),
                   jax.ShapeDtypeStruct((b, D_KV, n_chunk), BF16)],
        compiler_params=_params("parallel"), name="nsa_compress",
    )(kc_src, vc_src, pos2, w1k, w2k, w1v, w2v_t)


def _nsa_select_kernel(qt_ref, kc_ref, vct_ref, pbias_ref, ovl_ref, ocmp_ref, sel_ref, *, n_cmp_pad):
    T = T_SEL
    j = pl.program_id(1)
    t0 = j * T
    n_idx = lax.broadcasted_iota(jnp.int32, (n_cmp_pad, T), 0)
    tok_c = t0 + lax.broadcasted_iota(jnp.int32, (n_cmp_pad, T), 1)
    mask_c = tok_c >= CMP_STRIDE * n_idx + (CMP_BLOCK - 1)
    off = pl.multiple_of(n_cmp_pad - (T // CMP_STRIDE) * j, 8)
    psums = [None] * NSA_KV_GROUPS
    for h in range(NSA_HEADS):
        g = h // NSA_HPG
        qh = qt_ref[HEAD_DIM * h:HEAD_DIM * (h + 1), :]
        zq = jnp.zeros_like(qh)
        qh = jnp.concatenate([qh, zq] if g == 0 else [zq, qh], axis=0)
        s = _dot(kc_ref[0], qh) + pbias_ref[h, pl.ds(off, n_cmp_pad), :]
        s = jnp.where(mask_c, s, NEG_BIG)
        m = jnp.max(s, axis=0, keepdims=True)
        p = jnp.where(mask_c, jnp.exp2(s - m), 0.0)
        l = jnp.sum(p, axis=0, keepdims=True)
        p = p / jnp.where(l > 0.0, l, 1.0)
        o_both = _dot(vct_ref[0], p.astype(BF16))
        ocmp_ref[HEAD_DIM * h:HEAD_DIM * (h + 1), :] = o_both[HEAD_DIM * g:HEAD_DIM * (g + 1)]
        psums[g] = p if psums[g] is None else psums[g] + p

    jblk = lax.broadcasted_iota(jnp.int32, (MAX_SLC, T), 0)
    tok = t0 + lax.broadcasted_iota(jnp.int32, (MAX_SLC, T), 1)
    cur = lax.shift_right_logical(tok, 6)
    forced = (jblk == 0) | (jblk == cur) | (jblk == cur - 1)
    valid = jblk * SLC_BLOCK <= tok
    jsub = lax.broadcasted_iota(jnp.int32, (8, T), 0)
    for g in range(NSA_KV_GROUPS):
        imp = _dot(ovl_ref[...], psums[g].astype(BF16))
        score = jnp.where(valid, imp + jnp.where(forced, FORCE_BONUS, 0.0), NEG_BIG)
        blocks = [score[8 * rb:8 * rb + 8] for rb in range(MAX_SLC // 8)]
        ranks = [jnp.zeros((8, T), F32) for _ in blocks]
        for jp in range(MAX_SLC):
            other = score[jp:jp + 1, :]
            for rb, blk in enumerate(blocks):
                ge = jnp.where(other >= blk, 1.0, 0.0)
                gt = jnp.where(other > blk, 1.0, 0.0)
                if 8 * rb > jp:
                    inc = ge
                elif 8 * rb + 7 < jp:
                    inc = gt
                else:
                    inc = jnp.where(jsub > jp - 8 * rb, ge, gt)
                ranks[rb] = ranks[rb] + inc
        rank = jnp.concatenate(ranks, axis=0)
        sel_ref[0, MAX_SLC * g:MAX_SLC * (g + 1), :] = jnp.where(rank < float(SLC_TOPN), 0.0, NEG_BIG).astype(BF16)


def _nsa_select(q_t, kc, vc_t, pbias, ovl_t):
    b, n_cmp_pad, _ = kc.shape
    m = q_t.shape[1]
    nq = m // b // T_SEL
    cols = lambda r: pl.BlockSpec((r, T_SEL), lambda i, j: (0, i * nq + j))
    per_b = lambda a: pl.BlockSpec((1,) + a.shape[1:], lambda i, j: (i, 0, 0))
    full = lambda a: pl.BlockSpec(a.shape, lambda i, j: (0,) * a.ndim)
    return pl.pallas_call(
        functools.partial(_nsa_select_kernel, n_cmp_pad=n_cmp_pad),
        grid=(b, nq),
        in_specs=[cols(D_NSA), per_b(kc), per_b(vc_t), full(pbias), full(ovl_t)],
        out_specs=[cols(D_NSA), pl.BlockSpec((1, NSA_KV_GROUPS * MAX_SLC, T_SEL), lambda i, j: (i, 0, j))],
        out_shape=[jax.ShapeDtypeStruct((D_NSA, m), F32),
                   jax.ShapeDtypeStruct((b, NSA_KV_GROUPS * MAX_SLC, m // b), BF16)],
        compiler_params=_params("parallel", "parallel"), name="nsa_select",
    )(q_t, kc, vc_t, pbias, ovl_t)


def _online_init(m_ref, l_ref, acc_ref):
    m_ref[...] = jnp.full(m_ref.shape, NEG_BIG, F32)
    l_ref[...] = jnp.zeros(l_ref.shape, F32)
    acc_ref[...] = jnp.zeros(acc_ref.shape, F32)


def _online_chunk(m_ref, l_ref, acc_ref, k_tile, q_ref, q_rows, v_t, bias_ref=None, ok=None):
    s = _dot(k_tile, q_ref[0:q_rows, :])
    if bias_ref is not None:
        s = s + bias_ref[...]
    if ok is not None:
        s = jnp.where(ok, s, NEG_BIG)
    m = m_ref[...]
    m_new = jnp.maximum(m, jnp.max(s, axis=0, keepdims=True))
    alpha = jnp.exp2(m - m_new)
    p = jnp.exp2(s - m_new)
    m_ref[...] = m_new
    l_ref[...] = alpha * l_ref[...] + jnp.sum(p, axis=0, keepdims=True)
    acc_ref[...] = alpha * acc_ref[...] + _dot(v_t, p.astype(BF16))


def _nsa_attend_kernel(qt_ref, sel_ref, ocmp_ref, gt_ref, ks_ref, kw_ref, vs_ref, vw_ref, bias_ref, onehot_ref,
                       o_ref, kaug_ref, qaug_ref, ms_ref, ls_ref, accs_ref, mw_ref, lw_ref, accw_ref):
    T = T_NSA
    j = pl.program_id(1)

    @pl.when(j == 0)
    def _():
        kaug_ref[:, 0:LANES] = ks_ref[0]
        kaug_ref[:, LANES:2 * LANES] = onehot_ref[...]
        qaug_ref[...] = jnp.zeros(qaug_ref.shape, BF16)

    for h in range(NSA_HEADS):
        g = h // NSA_HPG
        qaug_ref[HEAD_DIM * g:HEAD_DIM * (g + 1), h * T:(h + 1) * T] = qt_ref[HEAD_DIM * h:HEAD_DIM * (h + 1), :]
        qaug_ref[LANES:LANES + MAX_SLC, h * T:(h + 1) * T] = sel_ref[0, MAX_SLC * g:MAX_SLC * (g + 1), :]

    key = lax.broadcasted_iota(jnp.int32, (T, NSA_HEADS * T), 0)
    qry = lax.broadcasted_iota(jnp.int32, (T, NSA_HEADS * T), 1) & (T - 1)
    causal = key <= qry
    st_s = (ms_ref, ls_ref, accs_ref)
    st_w = (mw_ref, lw_ref, accw_ref)
    _online_init(*st_s)
    _online_init(*st_w)

    n_far = jnp.maximum(j - 1, 0)

    def far_body(c, carry):
        r0 = pl.multiple_of(c * (2 * T), 2 * T)
        _online_chunk(*st_s, kaug_ref[pl.ds(r0, 2 * T), :], qaug_ref, 2 * LANES, vs_ref[:, pl.ds(r0, 2 * T)])
        return carry

    lax.fori_loop(0, lax.shift_right_logical(n_far, 1), far_body, 0)

    @pl.when((n_far & 1) == 1)
    def _():
        r0 = pl.multiple_of((n_far - 1) * T, T)
        _online_chunk(*st_s, kaug_ref[pl.ds(r0, T), :], qaug_ref, 2 * LANES, vs_ref[:, pl.ds(r0, T)])
    for d in (1, 0):
        c = j - d
        r0 = pl.multiple_of(jnp.maximum(c, 0) * T, T)
        ok = causal if d == 0 else key < jnp.where(c >= 0, T, -1)
        _online_chunk(*st_s, kaug_ref[pl.ds(r0, T), :], qaug_ref, 2 * LANES, vs_ref[:, pl.ds(r0, T)],
                      bias_ref.at[d], ok)

    n_win = WINDOW // T
    for d in range(n_win + 1):
        c = j - d
        r0 = pl.multiple_of(jnp.maximum(c, 0) * T, T)
        in_range = key < jnp.where(c >= 0, T, -1)
        if d == 0:
            ok = causal
        elif d == n_win:
            ok = (key > qry) & in_range
        else:
            ok = in_range
        _online_chunk(*st_w, kw_ref[0, pl.ds(r0, T), :], qaug_ref, LANES, vw_ref[:, pl.ds(r0, T)],
                      bias_ref.at[d] if d <= 1 else None, ok)

    o_sel = accs_ref[...] / ls_ref[...]
    o_win = accw_ref[...] / lw_ref[...]
    gates = jax.nn.sigmoid(gt_ref[...])
    heads = []
    for h in range(NSA_HEADS):
        cols = slice(h * T, (h + 1) * T)
        rows = slice(HEAD_DIM * (h // NSA_HPG), HEAD_DIM * (h // NSA_HPG + 1))
        heads.append(gates[3 * h:3 * h + 1, :] * ocmp_ref[HEAD_DIM * h:HEAD_DIM * (h + 1), :]
                     + gates[3 * h + 1:3 * h + 2, :] * o_sel[rows, cols]
                     + gates[3 * h + 2:3 * h + 3, :] * o_win[rows, cols])
    o_ref[0] = jnp.concatenate(heads, axis=0).T.astype(o_ref.dtype)


def _nsa_attend(q_t, sel_t, ocmp_t, g_t, ks, kw, vs_t, vw_t, bias_t, onehot):
    b, s, _ = ks.shape
    T = T_NSA
    nq = s // T
    R = NSA_HEADS * T
    cols = lambda r: pl.BlockSpec((r, T), lambda i, j: (0, i * nq + j))
    row_b = lambda a: pl.BlockSpec((a.shape[0], s), lambda i, j: (0, i))
    per_b = lambda a: pl.BlockSpec((1,) + a.shape[1:], lambda i, j: (i, 0, 0))
    full = lambda a: pl.BlockSpec(a.shape, lambda i, j: (0,) * a.ndim)
    stat = [pltpu.VMEM((1, R), F32), pltpu.VMEM((1, R), F32), pltpu.VMEM((LANES, R), F32)]
    return pl.pallas_call(
        _nsa_attend_kernel,
        grid=(b, nq),
        in_specs=[cols(D_NSA), pl.BlockSpec((1, NSA_KV_GROUPS * MAX_SLC, T), lambda i, j: (i, 0, j)),
                  cols(D_NSA), cols(LANES), per_b(ks), per_b(kw), row_b(vs_t), row_b(vw_t),
                  full(bias_t), full(onehot)],
        out_specs=pl.BlockSpec((1, T, D_NSA), lambda i, j: (i, j, 0)),
        out_shape=jax.ShapeDtypeStruct((b, s, D_NSA), BF16),
        scratch_shapes=[pltpu.VMEM((s, 2 * LANES), BF16), pltpu.VMEM((2 * LANES, R), BF16)] + stat + stat,
        compiler_params=_params("parallel", "arbitrary"), name="nsa_attend",
    )(q_t, sel_t, ocmp_t, g_t, ks, kw, vs_t, vw_t, bias_t, onehot)


def _sb_kernel(qt_ref, k_ref, vt_ref, tri_ref, o_ref, carry_ref, acc_ref):
    T = T_SB
    j = pl.program_id(1)
    n_pair = SB_HEADS // 2
    key = lax.broadcasted_iota(jnp.int32, (T, 2 * T), 0)
    qry = lax.broadcasted_iota(jnp.int32, (T, 2 * T), 1) & (T - 1)
    strict = key < qry
    carry_ref[...] = jnp.zeros(carry_ref.shape, F32)
    acc_ref[...] = jnp.zeros(acc_ref.shape, F32)

    def tiles(cs, mask):
        starts = [pl.multiple_of(c * T, T) for c in cs]
        units = [(ci, p) for ci in range(len(cs)) for p in range(n_pair)]
        z2s, incls = [], []
        for c, p in units:
            q_even = qt_ref[HEAD_DIM * (2 * p):HEAD_DIM * (2 * p + 1), :]
            q_odd = qt_ref[HEAD_DIM * (2 * p + 1):HEAD_DIM * (2 * p + 2), :]
            zq = jnp.zeros_like(q_even)
            q_pair = jnp.concatenate([jnp.concatenate([q_even, zq], axis=0),
                                      jnp.concatenate([zq, q_odd], axis=0)], axis=1)
            z2s.append(_dot(k_ref[0, pl.ds(starts[c], T), LANES * p:LANES * (p + 1)], q_pair))
        for z2 in z2s:
            rest = jnp.maximum(z2, 0.0) + jnp.log2(1.0 + jnp.exp2(-jnp.abs(z2)))
            if mask is not None:
                rest = jnp.where(mask, rest, 0.0)
            hi, lo = _split_bf16(rest)
            incls.append(_dot(tri_ref[...], jnp.concatenate([hi, lo], axis=0)))
        carry = [carry_ref[p] for p in range(n_pair)]
        acc = [acc_ref[p] for p in range(n_pair)]
        for (c, p), z2, incl in zip(units, z2s, incls):
            a = jnp.exp2(z2 - incl - carry[p])
            if mask is not None:
                a = jnp.where(mask, a, 0.0)
            acc[p] = acc[p] + _dot(vt_ref[LANES * p:LANES * (p + 1), pl.ds(starts[c], T)], a.astype(BF16))
            carry[p] = carry[p] + incl[0:1, :]
        for p in range(n_pair):
            carry_ref[p] = carry[p]
            acc_ref[p] = acc[p]

    tiles([j], strict)

    def live(c):
        return (c >= 0) & (jnp.min(carry_ref[...]) < SB_DEAD_BITS)

    def back_one(c):
        tiles([c], None)
        return c - 1

    lax.while_loop(live, back_one, j - 1)
    row = lax.broadcasted_iota(jnp.int32, (LANES, T), 0)
    for p in range(n_pair):
        acc = acc_ref[p]
        o_ref[0, :, LANES * p:LANES * (p + 1)] = (
            jnp.where(row < HEAD_DIM, acc[:, 0:T], acc[:, T:2 * T]).T.astype(o_ref.dtype))


def _sb_attention(q_t, k, v_t, tri_t):
    b, s, _ = k.shape
    T = T_SB
    nq = s // T
    n_pair = SB_HEADS // 2
    return pl.pallas_call(
        _sb_kernel,
        grid=(b, nq),
        in_specs=[pl.BlockSpec((D_SB, T), lambda i, j: (0, i * nq + j)),
                  pl.BlockSpec((1, s, D_SB), lambda i, j: (i, 0, 0)),
                  pl.BlockSpec((D_SB, s), lambda i, j: (0, i)),
                  pl.BlockSpec(tri_t.shape, lambda i, j: (0, 0))],
        out_specs=pl.BlockSpec((1, T, D_SB), lambda i, j: (i, j, 0)),
        out_shape=jax.ShapeDtypeStruct((b, s, D_SB), BF16),
        scratch_shapes=[pltpu.VMEM((n_pair, 1, 2 * T), F32), pltpu.VMEM((n_pair, LANES, 2 * T), F32)],
        compiler_params=_params("parallel", "parallel"), name="sb_attention",
    )(q_t, k, v_t, tri_t)


def _hgrn_kernel(q_ref, f_ref, i_ref, lb_ref, nw_ref, ones_ref, bd_ref, o_ref,
                 st_ref, qj_ref, kj_ref, bj_ref, vj_ref, qd_ref, kd_ref, oc_ref, x_ref, u_ref, sb_ref, dec_ref):
    C = HG_SUB
    n_blk = T_HG // C

    @pl.when(pl.program_id(2) == 0)
    def _():
        st_ref[...] = jnp.zeros(st_ref.shape, F32)

    lb = lb_ref[...]
    c_floor = jnp.log(jnp.maximum(lb, LB_FLOOR))
    c_rest = jnp.log1p(-lb)
    one_m_lb = 1.0 - lb
    b_run = jnp.zeros((n_blk, LANES), F32)
    for j in range(C):
        fj = f_ref[0, pl.ds(j, n_blk, stride=C), :]
        log_sig = -(jnp.maximum(-fj, 0.0) + jnp.log(1.0 + jnp.exp(-jnp.abs(fj))))
        a, bb = c_floor, c_rest + log_sig
        log_f = jnp.maximum(a, bb) + jnp.log(1.0 + jnp.exp(-jnp.abs(a - bb)))
        b_run = b_run + log_f
        bj_ref[j] = b_run * LOG2E
        kj_ref[j] = one_m_lb * jax.nn.sigmoid(-fj)
        qj_ref[j] = q_ref[0, pl.ds(j, n_blk, stride=C), :]
        vj_ref[j] = i_ref[0, pl.ds(j, n_blk, stride=C), :]
    b_last = bj_ref[C - 1]
    for j in range(C):
        bj = bj_ref[j]
        qd_ref[pl.ds(j, n_blk, stride=HG_PITCH), :] = qj_ref[j] * jnp.exp2(bj)
        kd_ref[pl.ds(j, n_blk, stride=HG_PITCH), :] = kj_ref[j] * jnp.exp2(b_last - bj)

    def kv_products(blk, carry):
        kd = kd_ref[pl.ds(pl.multiple_of(blk * HG_PITCH, 8), C), :].astype(BF16)
        vv = i_ref[0, pl.ds(pl.multiple_of(blk * C, C), C), :].astype(BF16)
        u_ref[blk] = bd_ref[...] * _dot_tn(vv, kd)
        return carry

    lax.fori_loop(0, n_blk, kv_products, 0, unroll=HG_UNROLL)
    dec_ref[...] = jnp.exp2(b_last)

    def scan(blk, st):
        sb_ref[blk] = st.astype(BF16)
        return st * dec_ref[pl.ds(blk, 1), :] + u_ref[blk]

    st_ref[...] = lax.fori_loop(0, n_blk, scan, st_ref[...], unroll=HG_UNROLL)

    def outputs(blk, carry):
        r0 = pl.multiple_of(blk * HG_PITCH, 8)
        oc_ref[pl.ds(r0, C), :] = _dot_nt(qd_ref[pl.ds(r0, C), :].astype(BF16), sb_ref[blk])
        return carry

    lax.fori_loop(0, n_blk, outputs, 0, unroll=HG_UNROLL)

    nw = nw_ref[...]
    for j in range(C):
        oj = oc_ref[pl.ds(j, n_blk, stride=HG_PITCH), :]
        qj = qj_ref[j]
        bj = bj_ref[j]
        for jp in range(j + 1):
            x = qj * kj_ref[jp] * jnp.exp2(bj - bj_ref[jp])
            x_ref[jp * n_blk:(jp + 1) * n_blk, :] = x.astype(BF16)
        att = _dot(x_ref[0:(j + 1) * n_blk, :], ones_ref[...])
        for jp in range(j + 1):
            oj = oj + att[jp * n_blk:(jp + 1) * n_blk] * vj_ref[jp]
        hi, lo = _split_bf16(oj * oj)
        ms = (_dot(hi, ones_ref[...]) + _dot(lo, ones_ref[...])) * (1.0 / HEAD_DIM)
        o_ref[0, pl.ds(j, n_blk, stride=C), :] = oj * lax.rsqrt(ms + RMS_EPS) * nw


def _hgrn(q, f, i, lb, nw, ones_bd, bd_mask):
    b, s, _ = q.shape
    n_blk = T_HG // HG_SUB
    tile = pl.BlockSpec((1, T_HG, LANES), lambda bi, pi, ti: (bi, ti, pi))
    vec = pl.BlockSpec((1, LANES), lambda bi, pi, ti: (0, pi))
    full = lambda a: pl.BlockSpec(a.shape, lambda bi, pi, ti: (0,) * a.ndim)
    jm = pltpu.VMEM((HG_SUB, n_blk, LANES), F32)
    nat = pltpu.VMEM((n_blk * HG_PITCH, LANES), F32)
    return pl.pallas_call(
        _hgrn_kernel,
        grid=(b, D_HG // LANES, s // T_HG),
        in_specs=[tile, tile, tile, vec, vec, full(ones_bd), full(bd_mask)],
        out_specs=tile,
        out_shape=jax.ShapeDtypeStruct((b, s, D_HG), F32),
        scratch_shapes=[pltpu.VMEM((LANES, LANES), F32), jm, jm, jm, jm, nat, nat, nat,
                        pltpu.VMEM((T_HG, LANES), BF16), pltpu.VMEM((n_blk, LANES, LANES), F32),
                        pltpu.VMEM((n_blk, LANES, LANES), BF16), pltpu.VMEM((n_blk, LANES), F32)],
        compiler_params=_params("parallel", "parallel", "arbitrary"), name="hgrn2",
    )(q, f, i, lb, nw, ones_bd, bd_mask)


def _out_kernel(onsa_ref, osb_ref, ohg_ref, z_ref, x_ref, w_ref, g_ref, b_ref, o_ref):
    z = z_ref[...].astype(F32)
    sz = z * jax.nn.sigmoid(z)
    mixed = jnp.concatenate([(onsa_ref[...] * sz[:, 0:D_NSA]).astype(BF16),
                             (osb_ref[...] * sz[:, D_NSA:D_NSA + D_SB]).astype(BF16),
                             (ohg_ref[...] * sz[:, D_NSA + D_SB:D_MIX]).astype(BF16)], axis=1)
    v = ALPHA * x_ref[...] + _dot(mixed, w_ref[...])
    mu = jnp.mean(v, axis=-1, keepdims=True)
    vc = v - mu
    var = jnp.mean(vc * vc, axis=-1, keepdims=True)
    o_ref[...] = vc * lax.rsqrt(var + LN_EPS) * g_ref[...] + b_ref[...]


def _out_proj(o_nsa, o_sb, o_hg, z_all, x2d, w, g, bvec):
    m = x2d.shape[0]
    rows = lambda n: pl.BlockSpec((T_PROJ, n), lambda i: (i, 0))
    full = lambda a: pl.BlockSpec(a.shape, lambda i: (0, 0))
    return pl.pallas_call(
        _out_kernel,
        grid=(m // T_PROJ,),
        in_specs=[rows(D_NSA), rows(D_SB), rows(D_HG), rows(D_MIX), rows(D_MODEL), full(w), full(g), full(bvec)],
        out_specs=rows(D_MODEL),
        out_shape=jax.ShapeDtypeStruct((m, D_MODEL), F32),
        compiler_params=_params("parallel"), name="out_proj_norm",
    )(o_nsa, o_sb, o_hg, z_all, x2d, w, g, bvec)


def _t5_bucket_np(rel):
    n = np.maximum(rel, 0)
    max_exact = NUM_BUCKETS // 2
    large = max_exact + (np.log(np.maximum(n, 1).astype(np.float32) / max_exact)
                         / math.log(MAX_DISTANCE / max_exact) * (NUM_BUCKETS - max_exact)).astype(np.int32)
    large = np.clip(large, 0, NUM_BUCKETS - 1)
    return np.where(n < max_exact, n, large).astype(np.int32)


def _bias_tables(rel_bias, s):
    tbl = ((rel_bias - rel_bias[NUM_BUCKETS - 1]) * LOG2E).astype(F32)

    def expand(rel):
        onehot = (jnp.asarray(_t5_bucket_np(rel).reshape(-1, 1)) == jnp.arange(NUM_BUCKETS)[None, :]).astype(F32)
        return jnp.dot(onehot, tbl, precision=lax.Precision.HIGHEST).reshape(rel.shape + (NSA_HEADS,))

    n_cmp_pad = s // CMP_STRIDE
    n_rel = np.arange(2 * n_cmp_pad)[:, None] - n_cmp_pad
    r = np.arange(T_SEL)[None, :]
    pbias = jnp.transpose(expand(r - CMP_STRIDE * n_rel - (CMP_BLOCK - 1)), (2, 0, 1))
    T = T_NSA
    key = np.arange(T)[:, None]
    qry = np.arange(T)[None, :]
    near = np.stack([qry - key, T + qry - key])
    bias_t = jnp.transpose(expand(near), (0, 1, 3, 2)).reshape(2, T, NSA_HEADS * T)
    return pbias, bias_t


def _static_tables(s):
    n_chunk = s // CMP_STRIDE
    cmp_start = np.arange(n_chunk) * CMP_STRIDE
    slc_start = np.arange(MAX_SLC) * SLC_BLOCK
    ovl_t = ((cmp_start[None, :] < slc_start[:, None] + SLC_BLOCK)
             & (cmp_start[None, :] + CMP_BLOCK > slc_start[:, None])
             & (cmp_start[None, :] + CMP_BLOCK <= s)).astype(np.float32)
    onehot = (np.arange(s)[:, None] // SLC_BLOCK == np.arange(LANES)[None, :]).astype(np.float32)
    tri = (np.arange(T_SB)[None, :] >= np.arange(T_SB)[:, None]).astype(np.float32)
    tri = np.concatenate([tri, tri], axis=1)
    ones_bd = np.kron(np.eye(2), np.ones((HEAD_DIM, HEAD_DIM))).astype(np.float32)
    as_bf16 = lambda a: jnp.asarray(a, dtype=BF16)
    return as_bf16(ovl_t), as_bf16(onehot), as_bf16(tri), as_bf16(ones_bd), jnp.asarray(ones_bd)


def _layer_weights(w_in_l, cmp_pos_l, w_ck1_l, w_ck2_l, w_cv1_l, w_cv2_l):
    offs = np.cumsum((0,) + SPLIT_SIZES)
    (w_q, w_kc, w_vc, w_ks, w_vs, w_kw, w_vw, w_g, w_nz,
     w_sq, w_sk, w_sv, w_sz, w_hq, w_hf, w_hi, w_hz) = [w_in_l[:, offs[i]:offs[i + 1]] for i in range(len(SPLIT_SIZES))]
    w_gp = jnp.concatenate([w_g, jnp.zeros((D_MODEL, LANES - N_GATES), F32)], axis=1)
    w_z = jnp.concatenate([w_nz, w_sz, w_hz], axis=1)
    w_nat = [w_kc, w_vc, w_ks, w_kw, w_z, w_sk, w_hq, w_hf, w_hi]
    dt_nat = [F32, F32, BF16, BF16, BF16, BF16, F32, F32, F32]
    w_tr = [(w_q * QK_SCALE2).T, w_vs.T, w_vw.T, w_gp.T, (w_sq * QK_SCALE2).T, w_sv.T]
    dt_tr = [BF16, BF16, BF16, F32, BF16, BF16]
    w_nat = [w.astype(BF16) for w in w_nat]
    w_tr = [w.astype(BF16) for w in w_tr]

    def block_diag(w):
        z = jnp.zeros_like(w)
        return jnp.concatenate([jnp.concatenate([w, z], axis=-1), jnp.concatenate([z, w], axis=-1)], axis=-2)

    pos2 = jnp.concatenate([cmp_pos_l, cmp_pos_l], axis=1)
    w1k = block_diag(w_ck1_l.reshape(CMP_BLOCK, HEAD_DIM, CMP_HIDDEN)).astype(BF16)
    w1v = block_diag(w_cv1_l.reshape(CMP_BLOCK, HEAD_DIM, CMP_HIDDEN)).astype(BF16)
    w2k = block_diag(w_ck2_l).astype(BF16)
    w2v_t = block_diag(w_cv2_l).T.astype(BF16)
    return (w_nat, dt_nat, w_tr, dt_tr), (pos2, w1k, w2k, w1v, w2v_t)


def kernel(x, w_in, cmp_pos, w_ck1, w_ck2, w_cv1, w_cv2, hg_lb, hg_norm_w, w_out, ln_g, ln_b, rel_bias):
    b, s, d = x.shape
    assert d == D_MODEL and s % T_HG == 0 and s // SLC_BLOCK <= MAX_SLC and s >= WINDOW + T_NSA
    lb_w = jax.nn.softmax(hg_lb.astype(F32), axis=0)
    lb_all = jnp.cumsum(lb_w, axis=0) - lb_w[0]
    pbias, bias_t = _bias_tables(rel_bias, s)
    ovl_t, onehot, tri, ones_bd, bd_mask = _static_tables(s)

    x2d = x.reshape(b * s, d)
    for l in range(DEPTH):
        proj_w, cmp_w = _layer_weights(w_in[l], cmp_pos[l], w_ck1[l], w_ck2[l], w_cv1[l], w_cv2[l])
        nat, (q_t, vs_t, vw_t, g_t, sq_t, sv_t) = _inproj(x2d, *proj_w)
        kc_src, vc_src, ks, kw, z_all, sk, hq, hf, hi = [o.reshape(b, s, o.shape[-1]) for o in nat]
        kc, vc_t = _compress(kc_src, vc_src, *cmp_w)
        ocmp_t, sel_t = _nsa_select(q_t, kc, vc_t, pbias, ovl_t)
        o_nsa = _nsa_attend(q_t, sel_t, ocmp_t, g_t, ks, kw, vs_t, vw_t, bias_t, onehot)
        o_sb = _sb_attention(sq_t, sk, sv_t, tri)
        o_hg = _hgrn(hq, hf, hi, lb_all[l][None, :], hg_norm_w[l][None, :], ones_bd, bd_mask)
        x2d = _out_proj(o_nsa.reshape(b * s, D_NSA), o_sb.reshape(b * s, D_SB), o_hg.reshape(b * s, D_HG),
                        z_all.reshape(b * s, D_MIX), x2d, w_out[l].astype(BF16),
                        ln_g[l][None, :], ln_b[l][None, :])
    return x2d.reshape(b, s, d)
```

```python
import functools
import math

import numpy as np
import jax
import jax.numpy as jnp
from jax import lax
from jax.experimental import pallas as pl
from jax.experimental.pallas import tpu as pltpu

F32 = jnp.float32
BF16 = jnp.bfloat16

D_MODEL = 1024
DEPTH = 2
HEAD_DIM = 64
LANES = 128
NSA_HEADS = 6
NSA_KV_GROUPS = 2
NSA_HPG = NSA_HEADS // NSA_KV_GROUPS
CMP_BLOCK = 32
CMP_STRIDE = 16
CMP_HIDDEN = 2 * HEAD_DIM
SLC_BLOCK = 64
SLC_TOPN = 16
MAX_SLC = 64
WINDOW = 512
FORCE_BONUS = 1000.0
NEG_BIG = -1e30
LB_FLOOR = 1e-30
SB_HEADS = 4
HG_HEADS = 6
HG_SUB = 16
HG_PITCH = 24
HG_UNROLL = 64
NUM_BUCKETS = 32
MAX_DISTANCE = 128
D_NSA = NSA_HEADS * HEAD_DIM
D_KV = NSA_KV_GROUPS * HEAD_DIM
D_SB = SB_HEADS * HEAD_DIM
D_HG = HG_HEADS * HEAD_DIM
D_MIX = D_NSA + D_SB + D_HG
N_GATES = NSA_HEADS * 3
SPLIT_SIZES = (D_NSA, D_KV, D_KV, D_KV, D_KV, D_KV, D_KV, N_GATES, D_NSA,
               D_SB, D_SB, D_SB, D_SB, D_HG, D_HG, D_HG, D_HG)
ALPHA = (2 * DEPTH) ** 0.25
LN_EPS = 1e-5
RMS_EPS = 1e-6
LOG2E = math.log2(math.e)
QK_SCALE2 = LOG2E / math.sqrt(HEAD_DIM)

T_SEL = 128
T_NSA = 256
FAR_KEYS = 512
T_SB = 256
SB_DEAD_BITS = 151.0
T_HG = 1024
T_PROJ = 512
VMEM_LIMIT = 56 * 1024 * 1024

_NT = (((1,), (1,)), ((), ()))
_TN = (((0,), (0,)), ((), ()))


def _dot(a, b):
    return jnp.dot(a, b, preferred_element_type=F32)


def _dot_nt(a, b):
    return lax.dot_general(a, b, _NT, preferred_element_type=F32)


def _dot_tn(a, b):
    return lax.dot_general(a, b, _TN, preferred_element_type=F32)


def _softplus(x):
    return jnp.maximum(x, 0.0) + jnp.log1p(jnp.exp(-jnp.abs(x)))


def _split_bf16(x):
    hi = x.astype(BF16)
    lo = (x - hi.astype(F32)).astype(BF16)
    return hi, lo


def _params(*sem, flags=None):
    return pltpu.CompilerParams(dimension_semantics=sem, vmem_limit_bytes=VMEM_LIMIT, flags=flags)


def _inproj_kernel(x_ref, wn_ref, wt_ref, *o_refs, n_nat):
    xb = x_ref[...].astype(BF16)
    nat = _dot(xb, wn_ref[...])
    off = 0
    for o_ref in o_refs[:n_nat]:
        o_ref[...] = nat[:, off:off + o_ref.shape[1]].astype(o_ref.dtype)
        off += o_ref.shape[1]
    tr = _dot_nt(wt_ref[...], xb)
    off = 0
    for o_ref in o_refs[n_nat:]:
        o_ref[...] = tr[off:off + o_ref.shape[0], :].astype(o_ref.dtype)
        off += o_ref.shape[0]


def _inproj(x2d, w_nat, dt_nat, w_tr, dt_tr):
    m = x2d.shape[0]
    wn = jnp.concatenate(w_nat, axis=1)
    wt = jnp.concatenate(w_tr, axis=0)
    in_specs = [pl.BlockSpec((T_PROJ, D_MODEL), lambda i: (i, 0)),
                pl.BlockSpec(wn.shape, lambda i: (0, 0)), pl.BlockSpec(wt.shape, lambda i: (0, 0))]
    out_specs = [pl.BlockSpec((T_PROJ, w.shape[1]), lambda i: (i, 0)) for w in w_nat]
    out_specs += [pl.BlockSpec((w.shape[0], T_PROJ), lambda i: (0, i)) for w in w_tr]
    out_shape = [jax.ShapeDtypeStruct((m, w.shape[1]), dt) for w, dt in zip(w_nat, dt_nat)]
    out_shape += [jax.ShapeDtypeStruct((w.shape[0], m), dt) for w, dt in zip(w_tr, dt_tr)]
    outs = pl.pallas_call(
        functools.partial(_inproj_kernel, n_nat=len(w_nat)),
        grid=(m // T_PROJ,),
        in_specs=in_specs, out_specs=out_specs, out_shape=out_shape,
        compiler_params=_params("parallel"), name="inproj",
    )(x2d, wn, wt)
    return outs[:len(w_nat)], outs[len(w_nat):]


def _compress_kernel(ksrc_ref, vsrc_ref, pos_ref, w1k_ref, w2k_ref, w1v_ref, w2v_ref,
                     kc_ref, vc_ref, *, n_chunk):
    def hidden(src_ref, w1_ref):
        top = jnp.zeros((n_chunk, 2 * CMP_HIDDEN), F32)
        bot = jnp.zeros((n_chunk, 2 * CMP_HIDDEN), F32)
        for p in range(CMP_STRIDE):
            xp = src_ref[0, pl.ds(p, n_chunk, stride=CMP_STRIDE), :]
            top += _dot((xp + pos_ref[p:p + 1, :]).astype(BF16), w1_ref[p])
            q = CMP_STRIDE + p
            bot += _dot((xp + pos_ref[q:q + 1, :]).astype(BF16), w1_ref[q])
        hid = top + pltpu.roll(bot, n_chunk - 1, 0)
        return jax.nn.gelu(hid).astype(BF16)

    kc_ref[0] = _dot(hidden(ksrc_ref, w1k_ref), w2k_ref[...]).astype(kc_ref.dtype)
    vc_ref[0] = _dot_nt(w2v_ref[...], hidden(vsrc_ref, w1v_ref)).astype(vc_ref.dtype)


def _compress(kc_src, vc_src, pos2, w1k, w2k, w1v, w2v_t):
    b, s, _ = kc_src.shape
    n_chunk = s // CMP_STRIDE
    full = lambda a: pl.BlockSpec(a.shape, lambda i: (0,) * a.ndim)
    src = pl.BlockSpec((1, s, D_KV), lambda i: (i, 0, 0))
    return pl.pallas_call(
        functools.partial(_compress_kernel, n_chunk=n_chunk),
        grid=(b,),
        in_specs=[src, src, full(pos2), full(w1k), full(w2k), full(w1v), full(w2v_t)],
        out_specs=[pl.BlockSpec((1, n_chunk, D_KV), lambda i: (i, 0, 0)),
                   pl.BlockSpec((1, D_KV, n_chunk), lambda i: (i, 0, 0))],
        out_shape=[jax.ShapeDtypeStruct((b, n_chunk, D_KV), BF16),
                   jax.ShapeDtypeStruct((b, D_KV, n_chunk), BF16)],
        compiler_params=_params("parallel"), name="nsa_compress",
    )(kc_src, vc_src, pos2, w1k, w2k, w1v, w2v_t)


def _nsa_select_kernel(qt_ref, kc_ref, vct_ref, pbias_ref, ovl_ref, ocmp_ref, sel_ref, *, n_cmp_pad):
    T = T_SEL
    j = pl.program_id(1)
    t0 = j * T
    n_idx = lax.broadcasted_iota(jnp.int32, (n_cmp_pad, T), 0)
    tok_c = t0 + lax.broadcasted_iota(jnp.int32, (n_cmp_pad, T), 1)
    mask_c = tok_c >= CMP_STRIDE * n_idx + (CMP_BLOCK - 1)
    off = pl.multiple_of(n_cmp_pad - (T // CMP_STRIDE) * j, 8)
    psums = [None] * NSA_KV_GROUPS
    for h in range(NSA_HEADS):
        g = h // NSA_HPG
        qh = qt_ref[HEAD_DIM * h:HEAD_DIM * (h + 1), :]
        zq = jnp.zeros_like(qh)
        qh = jnp.concatenate([qh, zq] if g == 0 else [zq, qh], axis=0)
        s = _dot(kc_ref[0], qh) + pbias_ref[h, pl.ds(off, n_cmp_pad), :]
        s = jnp.where(mask_c, s, NEG_BIG)
        m = jnp.max(s, axis=0, keepdims=True)
        p = jnp.where(mask_c, jnp.exp2(s - m), 0.0)
        l = jnp.sum(p, axis=0, keepdims=True)
        p = p / jnp.where(l > 0.0, l, 1.0)
        o_both = _dot(vct_ref[0], p.astype(BF16))
        ocmp_ref[HEAD_DIM * h:HEAD_DIM * (h + 1), :] = o_both[HEAD_DIM * g:HEAD_DIM * (g + 1)]
        psums[g] = p if psums[g] is None else psums[g] + p

    jblk = lax.broadcasted_iota(jnp.int32, (MAX_SLC, T), 0)
    tok = t0 + lax.broadcasted_iota(jnp.int32, (MAX_SLC, T), 1)
    cur = lax.shift_right_logical(tok, 6)
    forced = (jblk == 0) | (jblk == cur) | (jblk == cur - 1)
    valid = jblk * SLC_BLOCK <= tok
    jsub = lax.broadcasted_iota(jnp.int32, (8, T), 0)
    for g in range(NSA_KV_GROUPS):
        imp = _dot(ovl_ref[...], psums[g].astype(BF16))
        score = jnp.where(valid, imp + jnp.where(forced, FORCE_BONUS, 0.0), NEG_BIG)
        blocks = [score[8 * rb:8 * rb + 8] for rb in range(MAX_SLC // 8)]
        ranks = [jnp.zeros((8, T), F32) for _ in blocks]
        for jp in range(MAX_SLC):
            other = score[jp:jp + 1, :]
            for rb, blk in enumerate(blocks):
                ge = jnp.where(other >= blk, 1.0, 0.0)
                gt = jnp.where(other > blk, 1.0, 0.0)
                if 8 * rb > jp:
                    inc = ge
                elif 8 * rb + 7 < jp:
                    inc = gt
                else:
                    inc = jnp.where(jsub > jp - 8 * rb, ge, gt)
                ranks[rb] = ranks[rb] + inc
        rank = jnp.concatenate(ranks, axis=0)
        sel_ref[0, MAX_SLC * g:MAX_SLC * (g + 1), :] = jnp.where(rank < float(SLC_TOPN), 0.0, NEG_BIG).astype(BF16)


def _nsa_select(q_t, kc, vc_t, pbias, ovl_t):
    b, n_cmp_pad, _ = kc.shape
    m = q_t.shape[1]
    nq = m // b // T_SEL
    cols = lambda r: pl.BlockSpec((r, T_SEL), lambda i, j: (0, i * nq + j))
    per_b = lambda a: pl.BlockSpec((1,) + a.shape[1:], lambda i, j: (i, 0, 0))
    full = lambda a: pl.BlockSpec(a.shape, lambda i, j: (0,) * a.ndim)
    return pl.pallas_call(
        functools.partial(_nsa_select_kernel, n_cmp_pad=n_cmp_pad),
        grid=(b, nq),
        in_specs=[cols(D_NSA), per_b(kc), per_b(vc_t), full(pbias), full(ovl_t)],
        out_specs=[cols(D_NSA), pl.BlockSpec((1, NSA_KV_GROUPS * MAX_SLC, T_SEL), lambda i, j: (i, 0, j))],
        out_shape=[jax.ShapeDtypeStruct((D_NSA, m), F32),
                   jax.ShapeDtypeStruct((b, NSA_KV_GROUPS * MAX_SLC, m // b), BF16)],
        compiler_params=_params("parallel", "parallel"), name="nsa_select",
    )(q_t, kc, vc_t, pbias, ovl_t)


def _online_init(m_ref, l_ref, acc_ref):
    m_ref[...] = jnp.full(m_ref.shape, NEG_BIG, F32)
    l_ref[...] = jnp.zeros(l_ref.shape, F32)
    acc_ref[...] = jnp.zeros(acc_ref.shape, F32)


def _scores(k_tile, q_ref, q_rows, bias_ref=None, ok=None):
    s = _dot(k_tile, q_ref[0:q_rows, :])
    if bias_ref is not None:
        s = s + bias_ref[...]
    if ok is not None:
        s = jnp.where(ok, s, NEG_BIG)
    return s


def _online_step(m_ref, l_ref, acc_ref, s, v_t):
    m = m_ref[...]
    m_new = jnp.maximum(m, jnp.max(s, axis=0, keepdims=True))
    alpha = jnp.exp2(m - m_new)
    p = jnp.exp2(s - m_new)
    m_ref[...] = m_new
    l_ref[...] = alpha * l_ref[...] + jnp.sum(p, axis=0, keepdims=True)
    acc_ref[...] = alpha * acc_ref[...] + _dot(v_t, p.astype(BF16))


def _nsa_attend_kernel(qt_ref, sel_ref, ocmp_ref, gt_ref, ks_ref, kw_ref, vs_ref, vw_ref, bias_ref, onehot_ref,
                       o_ref, kaug_ref, qaug_ref, ms_ref, ls_ref, accs_ref, mw_ref, lw_ref, accw_ref):
    T = T_NSA
    j = pl.program_id(1)

    @pl.when(j == 0)
    def _():
        kaug_ref[:, 0:LANES] = ks_ref[0]
        kaug_ref[:, LANES:2 * LANES] = onehot_ref[...]
        qaug_ref[...] = jnp.zeros(qaug_ref.shape, BF16)

    for h in range(NSA_HEADS):
        g = h // NSA_HPG
        qaug_ref[HEAD_DIM * g:HEAD_DIM * (g + 1), h * T:(h + 1) * T] = qt_ref[HEAD_DIM * h:HEAD_DIM * (h + 1), :]
        qaug_ref[LANES:LANES + MAX_SLC, h * T:(h + 1) * T] = sel_ref[0, MAX_SLC * g:MAX_SLC * (g + 1), :]

    key = lax.broadcasted_iota(jnp.int32, (T, NSA_HEADS * T), 0)
    qry = lax.broadcasted_iota(jnp.int32, (T, NSA_HEADS * T), 1) & (T - 1)
    causal = key <= qry
    st_s = (ms_ref, ls_ref, accs_ref)
    st_w = (mw_ref, lw_ref, accw_ref)
    _online_init(*st_s)
    _online_init(*st_w)

    far_len = jnp.maximum(j - 1, 0) * T
    n_big = lax.shift_right_logical(far_len, FAR_KEYS.bit_length() - 1)

    def far_step(r0, n_keys):
        s = _scores(kaug_ref[pl.ds(r0, n_keys), :], qaug_ref, 2 * LANES)
        _online_step(*st_s, s, vs_ref[:, pl.ds(r0, n_keys)])

    def far_big(c, carry):
        far_step(pl.multiple_of(c * FAR_KEYS, FAR_KEYS), FAR_KEYS)
        return carry

    lax.fori_loop(0, n_big, far_big, 0)

    def far_small(c, carry):
        far_step(pl.multiple_of(n_big * FAR_KEYS + c * T, T), T)
        return carry

    lax.fori_loop(0, lax.shift_right_logical(far_len - n_big * FAR_KEYS, T.bit_length() - 1), far_small, 0)

    n_win = WINDOW // T
    starts = [pl.multiple_of(jnp.maximum(j - d, 0) * T, T) for d in range(n_win + 1)]
    in_range = [key < jnp.where(j - d >= 0, T, -1) for d in range(n_win + 1)]
    sel_scores, win_scores = {}, {}
    for d in (1, 0):
        sel_scores[d] = _scores(kaug_ref[pl.ds(starts[d], T), :], qaug_ref, 2 * LANES, bias_ref.at[d],
                                causal if d == 0 else in_range[d])
    for d in range(n_win + 1):
        if d == 0:
            ok = causal
        elif d == n_win:
            ok = (key > qry) & in_range[d]
        else:
            ok = in_range[d]
        win_scores[d] = _scores(kw_ref[0, pl.ds(starts[d], T), :], qaug_ref, LANES,
                                bias_ref.at[d] if d <= 1 else None, ok)
    for d in (1, 0):
        _online_step(*st_s, sel_scores[d], vs_ref[:, pl.ds(starts[d], T)])
    for d in range(n_win + 1):
        _online_step(*st_w, win_scores[d], vw_ref[:, pl.ds(starts[d], T)])

    o_sel = accs_ref[...] / ls_ref[...]
    o_win = accw_ref[...] / lw_ref[...]
    gates = jax.nn.sigmoid(gt_ref[...])
    heads = []
    for h in range(NSA_HEADS):
        cols = slice(h * T, (h + 1) * T)
        rows = slice(HEAD_DIM * (h // NSA_HPG), HEAD_DIM * (h // NSA_HPG + 1))
        heads.append(gates[3 * h:3 * h + 1, :] * ocmp_ref[HEAD_DIM * h:HEAD_DIM * (h + 1), :]
                     + gates[3 * h + 1:3 * h + 2, :] * o_sel[rows, cols]
                     + gates[3 * h + 2:3 * h + 3, :] * o_win[rows, cols])
    o_ref[0] = jnp.concatenate(heads, axis=0).T.astype(o_ref.dtype)


def _nsa_attend(q_t, sel_t, ocmp_t, g_t, ks, kw, vs_t, vw_t, bias_t, onehot):
    b, s, _ = ks.shape
    T = T_NSA
    nq = s // T
    R = NSA_HEADS * T
    cols = lambda r: pl.BlockSpec((r, T), lambda i, j: (0, i * nq + j))
    row_b = lambda a: pl.BlockSpec((a.shape[0], s), lambda i, j: (0, i))
    per_b = lambda a: pl.BlockSpec((1,) + a.shape[1:], lambda i, j: (i, 0, 0))
    full = lambda a: pl.BlockSpec(a.shape, lambda i, j: (0,) * a.ndim)
    stat = [pltpu.VMEM((1, R), F32), pltpu.VMEM((1, R), F32), pltpu.VMEM((LANES, R), F32)]
    return pl.pallas_call(
        _nsa_attend_kernel,
        grid=(b, nq),
        in_specs=[cols(D_NSA), pl.BlockSpec((1, NSA_KV_GROUPS * MAX_SLC, T), lambda i, j: (i, 0, j)),
                  cols(D_NSA), cols(LANES), per_b(ks), per_b(kw), row_b(vs_t), row_b(vw_t),
                  full(bias_t), full(onehot)],
        out_specs=pl.BlockSpec((1, T, D_NSA), lambda i, j: (i, j, 0)),
        out_shape=jax.ShapeDtypeStruct((b, s, D_NSA), BF16),
        scratch_shapes=[pltpu.VMEM((s, 2 * LANES), BF16), pltpu.VMEM((2 * LANES, R), BF16)] + stat + stat,
        compiler_params=_params("parallel", "arbitrary"), name="nsa_attend",
    )(q_t, sel_t, ocmp_t, g_t, ks, kw, vs_t, vw_t, bias_t, onehot)


def _sb_kernel(qt_ref, k_ref, vt_ref, tri_ref, o_ref, carry_ref, acc_ref):
    T = T_SB
    j = pl.program_id(1)
    n_pair = SB_HEADS // 2
    key = lax.broadcasted_iota(jnp.int32, (T, 2 * T), 0)
    qry = lax.broadcasted_iota(jnp.int32, (T, 2 * T), 1) & (T - 1)
    strict = key < qry
    carry_ref[...] = jnp.zeros(carry_ref.shape, F32)
    acc_ref[...] = jnp.zeros(acc_ref.shape, F32)

    def tiles(cs, mask):
        starts = [pl.multiple_of(c * T, T) for c in cs]
        units = [(ci, p) for ci in range(len(cs)) for p in range(n_pair)]
        z2s, incls = [], []
        for c, p in units:
            q_even = qt_ref[HEAD_DIM * (2 * p):HEAD_DIM * (2 * p + 1), :]
            q_odd = qt_ref[HEAD_DIM * (2 * p + 1):HEAD_DIM * (2 * p + 2), :]
            zq = jnp.zeros_like(q_even)
            q_pair = jnp.concatenate([jnp.concatenate([q_even, zq], axis=0),
                                      jnp.concatenate([zq, q_odd], axis=0)], axis=1)
            z2s.append(_dot(k_ref[0, pl.ds(starts[c], T), LANES * p:LANES * (p + 1)], q_pair))
        for z2 in z2s:
            rest = jnp.maximum(z2, 0.0) + jnp.log2(1.0 + jnp.exp2(-jnp.abs(z2)))
            if mask is not None:
                rest = jnp.where(mask, rest, 0.0)
            hi, lo = _split_bf16(rest)
            incls.append(_dot(tri_ref[...], jnp.concatenate([hi, lo], axis=0)))
        carry = [carry_ref[p] for p in range(n_pair)]
        acc = [acc_ref[p] for p in range(n_pair)]
        for (c, p), z2, incl in zip(units, z2s, incls):
            a = jnp.exp2(z2 - incl - carry[p])
            if mask is not None:
                a = jnp.where(mask, a, 0.0)
            acc[p] = acc[p] + _dot(vt_ref[LANES * p:LANES * (p + 1), pl.ds(starts[c], T)], a.astype(BF16))
            carry[p] = carry[p] + incl[0:1, :]
        for p in range(n_pair):
            carry_ref[p] = carry[p]
            acc_ref[p] = acc[p]

    tiles([j], strict)

    def live(c):
        return (c >= 0) & (jnp.min(carry_ref[...]) < SB_DEAD_BITS)

    def back_one(c):
        tiles([c], None)
        return c - 1

    lax.while_loop(live, back_one, j - 1)
    row = lax.broadcasted_iota(jnp.int32, (LANES, T), 0)
    for p in range(n_pair):
        acc = acc_ref[p]
        o_ref[0, :, LANES * p:LANES * (p + 1)] = (
            jnp.where(row < HEAD_DIM, acc[:, 0:T], acc[:, T:2 * T]).T.astype(o_ref.dtype))


def _sb_attention(q_t, k, v_t, tri_t):
    b, s, _ = k.shape
    T = T_SB
    nq = s // T
    n_pair = SB_HEADS // 2
    return pl.pallas_call(
        _sb_kernel,
        grid=(b, nq),
        in_specs=[pl.BlockSpec((D_SB, T), lambda i, j: (0, i * nq + j)),
                  pl.BlockSpec((1, s, D_SB), lambda i, j: (i, 0, 0)),
                  pl.BlockSpec((D_SB, s), lambda i, j: (0, i)),
                  pl.BlockSpec(tri_t.shape, lambda i, j: (0, 0))],
        out_specs=pl.BlockSpec((1, T, D_SB), lambda i, j: (i, j, 0)),
        out_shape=jax.ShapeDtypeStruct((b, s, D_SB), BF16),
        scratch_shapes=[pltpu.VMEM((n_pair, 1, 2 * T), F32), pltpu.VMEM((n_pair, LANES, 2 * T), F32)],
        compiler_params=_params("parallel", "parallel"), name="sb_attention",
    )(q_t, k, v_t, tri_t)


def _hgrn_kernel(q_ref, f_ref, i_ref, lb_ref, nw_ref, ones_ref, bd_ref, o_ref,
                 st_ref, qj_ref, kj_ref, fj_ref, vj_ref, qd_ref, kd_ref, oc_ref, x_ref, u_ref, sb_ref, dec_ref,
                 att_ref, oall_ref):
    C = HG_SUB
    n_blk = T_HG // C

    @pl.when(pl.program_id(2) == 0)
    def _():
        st_ref[...] = jnp.zeros(st_ref.shape, F32)

    lb = lb_ref[...]
    lb_floor = jnp.maximum(lb, LB_FLOOR)
    one_m_lb = 1.0 - lb
    cum = None
    for j in range(C):
        xj = f_ref[0, pl.ds(j, n_blk, stride=C), :]
        e = jnp.exp(-jnp.abs(xj))
        r = 1.0 / (1.0 + e)
        er = e * r
        pos = xj >= 0.0
        fj = lb_floor + one_m_lb * jnp.where(pos, r, er)
        kj_ref[j] = one_m_lb * jnp.where(pos, er, r)
        fj_ref[j] = fj
        cum = fj if cum is None else cum * fj
        qj = q_ref[0, pl.ds(j, n_blk, stride=C), :]
        qj_ref[j] = qj
        vj_ref[j] = i_ref[0, pl.ds(j, n_blk, stride=C), :]
        qd_ref[pl.ds(j, n_blk, stride=HG_PITCH), :] = qj * cum
    dec_ref[...] = cum
    tail = None
    for j in reversed(range(C)):
        kd_ref[pl.ds(j, n_blk, stride=HG_PITCH), :] = kj_ref[j] if tail is None else kj_ref[j] * tail
        tail = fj_ref[j] if tail is None else tail * fj_ref[j]

    def kv_products(blk, carry):
        kd = kd_ref[pl.ds(pl.multiple_of(blk * HG_PITCH, 8), C), :].astype(BF16)
        vv = i_ref[0, pl.ds(pl.multiple_of(blk * C, C), C), :].astype(BF16)
        u_ref[blk] = bd_ref[...] * _dot_tn(vv, kd)
        return carry

    lax.fori_loop(0, n_blk, kv_products, 0, unroll=HG_UNROLL)

    def scan(blk, st):
        sb_ref[blk] = st.astype(BF16)
        return st * dec_ref[pl.ds(blk, 1), :] + u_ref[blk]

    st_ref[...] = lax.fori_loop(0, n_blk, scan, st_ref[...], unroll=HG_UNROLL)

    def outputs(blk, carry):
        r0 = pl.multiple_of(blk * HG_PITCH, 8)
        oc_ref[pl.ds(r0, C), :] = _dot_nt(qd_ref[pl.ds(r0, C), :].astype(BF16), sb_ref[blk])
        return carry

    lax.fori_loop(0, n_blk, outputs, 0, unroll=HG_UNROLL)

    base = lambda j: (j * (j + 1) // 2) * n_blk
    for j in range(C):
        qdec = qj_ref[j]
        for jp in reversed(range(j + 1)):
            x_ref[base(j) + jp * n_blk:base(j) + (jp + 1) * n_blk, :] = (qdec * kj_ref[jp]).astype(BF16)
            if jp > 0:
                qdec = qdec * fj_ref[jp]
        att_ref[base(j):base(j + 1), :] = _dot(x_ref[base(j):base(j + 1), :], ones_ref[...])
    for j in range(C):
        oj = oc_ref[pl.ds(j, n_blk, stride=HG_PITCH), :]
        for jp in range(j + 1):
            oj = oj + att_ref[base(j) + jp * n_blk:base(j) + (jp + 1) * n_blk, :] * vj_ref[jp]
        oall_ref[j * n_blk:(j + 1) * n_blk, :] = oj
    o_all = oall_ref[...]
    hi, lo = _split_bf16(o_all * o_all)
    ms = (_dot(hi, ones_ref[...]) + _dot(lo, ones_ref[...])) * (1.0 / HEAD_DIM)
    oall_ref[...] = o_all * lax.rsqrt(ms + RMS_EPS) * nw_ref[...]
    for j in range(C):
        o_ref[0, pl.ds(j, n_blk, stride=C), :] = oall_ref[j * n_blk:(j + 1) * n_blk, :]


def _hgrn(q, f, i, lb, nw, ones_bd, bd_mask):
    b, s, _ = q.shape
    n_blk = T_HG // HG_SUB
    tile = pl.BlockSpec((1, T_HG, LANES), lambda bi, pi, ti: (bi, ti, pi))
    vec = pl.BlockSpec((1, LANES), lambda bi, pi, ti: (0, pi))
    full = lambda a: pl.BlockSpec(a.shape, lambda bi, pi, ti: (0,) * a.ndim)
    jm = pltpu.VMEM((HG_SUB, n_blk, LANES), F32)
    nat = pltpu.VMEM((n_blk * HG_PITCH, LANES), F32)
    return pl.pallas_call(
        _hgrn_kernel,
        grid=(b, D_HG // LANES, s // T_HG),
        in_specs=[tile, tile, tile, vec, vec, full(ones_bd), full(bd_mask)],
        out_specs=tile,
        out_shape=jax.ShapeDtypeStruct((b, s, D_HG), F32),
        scratch_shapes=[pltpu.VMEM((LANES, LANES), F32), jm, jm, jm, jm, nat, nat, nat,
                        pltpu.VMEM((HG_SUB * (HG_SUB + 1) // 2 * n_blk, LANES), BF16),
                        pltpu.VMEM((n_blk, LANES, LANES), F32),
                        pltpu.VMEM((n_blk, LANES, LANES), BF16), pltpu.VMEM((n_blk, LANES), F32),
                        pltpu.VMEM((HG_SUB * (HG_SUB + 1) // 2 * n_blk, LANES), F32), pltpu.VMEM((T_HG, LANES), F32)],
        compiler_params=_params("parallel", "parallel", "arbitrary"), name="hgrn2",
    )(q, f, i, lb, nw, ones_bd, bd_mask)


def _out_kernel(onsa_ref, osb_ref, ohg_ref, z_ref, x_ref, w_ref, g_ref, b_ref, o_ref):
    z = z_ref[...].astype(F32)
    sz = z * jax.nn.sigmoid(z)
    mixed = jnp.concatenate([(onsa_ref[...] * sz[:, 0:D_NSA]).astype(BF16),
                             (osb_ref[...] * sz[:, D_NSA:D_NSA + D_SB]).astype(BF16),
                             (ohg_ref[...] * sz[:, D_NSA + D_SB:D_MIX]).astype(BF16)], axis=1)
    v = ALPHA * x_ref[...] + _dot(mixed, w_ref[...])
    mu = jnp.mean(v, axis=-1, keepdims=True)
    vc = v - mu
    var = jnp.mean(vc * vc, axis=-1, keepdims=True)
    o_ref[...] = vc * lax.rsqrt(var + LN_EPS) * g_ref[...] + b_ref[...]


def _out_proj(o_nsa, o_sb, o_hg, z_all, x2d, w, g, bvec):
    m = x2d.shape[0]
    rows = lambda n: pl.BlockSpec((T_PROJ, n), lambda i: (i, 0))
    full = lambda a: pl.BlockSpec(a.shape, lambda i: (0, 0))
    return pl.pallas_call(
        _out_kernel,
        grid=(m // T_PROJ,),
        in_specs=[rows(D_NSA), rows(D_SB), rows(D_HG), rows(D_MIX), rows(D_MODEL), full(w), full(g), full(bvec)],
        out_specs=rows(D_MODEL),
        out_shape=jax.ShapeDtypeStruct((m, D_MODEL), F32),
        compiler_params=_params("parallel"), name="out_proj_norm",
    )(o_nsa, o_sb, o_hg, z_all, x2d, w, g, bvec)


def _t5_bucket_np(rel):
    n = np.maximum(rel, 0)
    max_exact = NUM_BUCKETS // 2
    large = max_exact + (np.log(np.maximum(n, 1).astype(np.float32) / max_exact)
                         / math.log(MAX_DISTANCE / max_exact) * (NUM_BUCKETS - max_exact)).astype(np.int32)
    large = np.clip(large, 0, NUM_BUCKETS - 1)
    return np.where(n < max_exact, n, large).astype(np.int32)


def _bias_tables(rel_bias, s):
    tbl = ((rel_bias - rel_bias[NUM_BUCKETS - 1]) * LOG2E).astype(F32)

    def expand(rel):
        onehot = (jnp.asarray(_t5_bucket_np(rel).reshape(-1, 1)) == jnp.arange(NUM_BUCKETS)[None, :]).astype(F32)
        return jnp.dot(onehot, tbl, precision=lax.Precision.HIGHEST).reshape(rel.shape + (NSA_HEADS,))

    n_cmp_pad = s // CMP_STRIDE
    n_rel = np.arange(2 * n_cmp_pad)[:, None] - n_cmp_pad
    r = np.arange(T_SEL)[None, :]
    pbias = jnp.transpose(expand(r - CMP_STRIDE * n_rel - (CMP_BLOCK - 1)), (2, 0, 1))
    T = T_NSA
    key = np.arange(T)[:, None]
    qry = np.arange(T)[None, :]
    near = np.stack([qry - key, T + qry - key])
    bias_t = jnp.transpose(expand(near), (0, 1, 3, 2)).reshape(2, T, NSA_HEADS * T)
    return pbias, bias_t


def _static_tables(s):
    n_chunk = s // CMP_STRIDE
    cmp_start = np.arange(n_chunk) * CMP_STRIDE
    slc_start = np.arange(MAX_SLC) * SLC_BLOCK
    ovl_t = ((cmp_start[None, :] < slc_start[:, None] + SLC_BLOCK)
             & (cmp_start[None, :] + CMP_BLOCK > slc_start[:, None])
             & (cmp_start[None, :] + CMP_BLOCK <= s)).astype(np.float32)
    onehot = (np.arange(s)[:, None] // SLC_BLOCK == np.arange(LANES)[None, :]).astype(np.float32)
    tri = (np.arange(T_SB)[None, :] >= np.arange(T_SB)[:, None]).astype(np.float32)
    tri = np.concatenate([tri, tri], axis=1)
    ones_bd = np.kron(np.eye(2), np.ones((HEAD_DIM, HEAD_DIM))).astype(np.float32)
    as_bf16 = lambda a: jnp.asarray(a, dtype=BF16)
    return as_bf16(ovl_t), as_bf16(onehot), as_bf16(tri), as_bf16(ones_bd), jnp.asarray(ones_bd)


def _layer_weights(w_in_l, cmp_pos_l, w_ck1_l, w_ck2_l, w_cv1_l, w_cv2_l):
    offs = np.cumsum((0,) + SPLIT_SIZES)
    (w_q, w_kc, w_vc, w_ks, w_vs, w_kw, w_vw, w_g, w_nz,
     w_sq, w_sk, w_sv, w_sz, w_hq, w_hf, w_hi, w_hz) = [w_in_l[:, offs[i]:offs[i + 1]] for i in range(len(SPLIT_SIZES))]
    w_gp = jnp.concatenate([w_g, jnp.zeros((D_MODEL, LANES - N_GATES), F32)], axis=1)
    w_z = jnp.concatenate([w_nz, w_sz, w_hz], axis=1)
    w_nat = [w_kc, w_vc, w_ks, w_kw, w_z, w_sk, w_hq, w_hf, w_hi]
    dt_nat = [F32, F32, BF16, BF16, BF16, BF16, F32, F32, F32]
    w_tr = [(w_q * QK_SCALE2).T, w_vs.T, w_vw.T, w_gp.T, (w_sq * QK_SCALE2).T, w_sv.T]
    dt_tr = [BF16, BF16, BF16, F32, BF16, BF16]
    w_nat = [w.astype(BF16) for w in w_nat]
    w_tr = [w.astype(BF16) for w in w_tr]

    def block_diag(w):
        z = jnp.zeros_like(w)
        return jnp.concatenate([jnp.concatenate([w, z], axis=-1), jnp.concatenate([z, w], axis=-1)], axis=-2)

    pos2 = jnp.concatenate([cmp_pos_l, cmp_pos_l], axis=1)
    w1k = block_diag(w_ck1_l.reshape(CMP_BLOCK, HEAD_DIM, CMP_HIDDEN)).astype(BF16)
    w1v = block_diag(w_cv1_l.reshape(CMP_BLOCK, HEAD_DIM, CMP_HIDDEN)).astype(BF16)
    w2k = block_diag(w_ck2_l).astype(BF16)
    w2v_t = block_diag(w_cv2_l).T.astype(BF16)
    return (w_nat, dt_nat, w_tr, dt_tr), (pos2, w1k, w2k, w1v, w2v_t)


def kernel(x, w_in, cmp_pos, w_ck1, w_ck2, w_cv1, w_cv2, hg_lb, hg_norm_w, w_out, ln_g, ln_b, rel_bias):
    b, s, d = x.shape
    assert d == D_MODEL and s % T_HG == 0 and s // SLC_BLOCK <= MAX_SLC and s >= WINDOW + T_NSA
    lb_w = jax.nn.softmax(hg_lb.astype(F32), axis=0)
    lb_all = jnp.cumsum(lb_w, axis=0) - lb_w[0]
    pbias, bias_t = _bias_tables(rel_bias, s)
    ovl_t, onehot, tri, ones_bd, bd_mask = _static_tables(s)

    x2d = x.reshape(b * s, d)
    for l in range(DEPTH):
        proj_w, cmp_w = _layer_weights(w_in[l], cmp_pos[l], w_ck1[l], w_ck2[l], w_cv1[l], w_cv2[l])
        nat, (q_t, vs_t, vw_t, g_t, sq_t, sv_t) = _inproj(x2d, *proj_w)
        kc_src, vc_src, ks, kw, z_all, sk, hq, hf, hi = [o.reshape(b, s, o.shape[-1]) for o in nat]
        kc, vc_t = _compress(kc_src, vc_src, *cmp_w)
        ocmp_t, sel_t = _nsa_select(q_t, kc, vc_t, pbias, ovl_t)
        o_nsa = _nsa_attend(q_t, sel_t, ocmp_t, g_t, ks, kw, vs_t, vw_t, bias_t, onehot)
        o_sb = _sb_attention(sq_t, sk, sv_t, tri)
        o_hg = _hgrn(hq, hf, hi, lb_all[l][None, :], hg_norm_w[l][None, :], ones_bd, bd_mask)
        x2d = _out_proj(o_nsa.reshape(b * s, D_NSA), o_sb.reshape(b * s, D_SB), o_hg.reshape(b * s, D_HG),
                        z_all.reshape(b * s, D_MIX), x2d, w_out[l].astype(BF16),
                        ln_g[l][None, :], ln_b[l][None, :])
    return x2d.reshape(b, s, d)
```

```python
import functools
import math

import numpy as np
import jax
import jax.numpy as jnp
from jax import lax
from jax.experimental import pallas as pl
from jax.experimental.pallas import tpu as pltpu

F32 = jnp.float32
BF16 = jnp.bfloat16

D_MODEL = 1024
DEPTH = 2
HEAD_DIM = 64
LANES = 128
NSA_HEADS = 6
NSA_KV_GROUPS = 2
NSA_HPG = NSA_HEADS // NSA_KV_GROUPS
CMP_BLOCK = 32
CMP_STRIDE = 16
CMP_HIDDEN = 2 * HEAD_DIM
SLC_BLOCK = 64
SLC_TOPN = 16
MAX_SLC = 64
WINDOW = 512
FORCE_BONUS = 1000.0
NEG_BIG = -1e30
LB_FLOOR = 1e-30
SB_HEADS = 4
HG_HEADS = 6
HG_SUB = 16
HG_PITCH = 24
HG_UNROLL = 64
NUM_BUCKETS = 32
MAX_DISTANCE = 128
D_NSA = NSA_HEADS * HEAD_DIM
D_KV = NSA_KV_GROUPS * HEAD_DIM
D_SB = SB_HEADS * HEAD_DIM
D_HG = HG_HEADS * HEAD_DIM
D_MIX = D_NSA + D_SB + D_HG
N_GATES = NSA_HEADS * 3
SPLIT_SIZES = (D_NSA, D_KV, D_KV, D_KV, D_KV, D_KV, D_KV, N_GATES, D_NSA,
               D_SB, D_SB, D_SB, D_SB, D_HG, D_HG, D_HG, D_HG)
ALPHA = (2 * DEPTH) ** 0.25
LN_EPS = 1e-5
RMS_EPS = 1e-6
LOG2E = math.log2(math.e)
QK_SCALE2 = LOG2E / math.sqrt(HEAD_DIM)

T_SEL = 128
T_NSA = 256
FAR_KEYS = 512
T_SB = 256
SB_DEAD_BITS = 151.0
T_HG = 1024
T_PROJ = 512
VMEM_LIMIT = 56 * 1024 * 1024

_NT = (((1,), (1,)), ((), ()))
_TN = (((0,), (0,)), ((), ()))


def _dot(a, b):
    return jnp.dot(a, b, preferred_element_type=F32)


def _dot_nt(a, b):
    return lax.dot_general(a, b, _NT, preferred_element_type=F32)


def _dot_tn(a, b):
    return lax.dot_general(a, b, _TN, preferred_element_type=F32)


def _softplus(x):
    return jnp.maximum(x, 0.0) + jnp.log1p(jnp.exp(-jnp.abs(x)))


def _split_bf16(x):
    hi = x.astype(BF16)
    lo = (x - hi.astype(F32)).astype(BF16)
    return hi, lo


def _params(*sem, flags=None):
    return pltpu.CompilerParams(dimension_semantics=sem, vmem_limit_bytes=VMEM_LIMIT, flags=flags)


def _inproj_kernel(x_ref, wn_ref, wt_ref, *o_refs, n_nat):
    xb = x_ref[...].astype(BF16)
    nat = _dot(xb, wn_ref[...])
    off = 0
    for o_ref in o_refs[:n_nat]:
        o_ref[...] = nat[:, off:off + o_ref.shape[1]].astype(o_ref.dtype)
        off += o_ref.shape[1]
    tr = _dot_nt(wt_ref[...], xb)
    off = 0
    for o_ref in o_refs[n_nat:]:
        o_ref[...] = tr[off:off + o_ref.shape[0], :].astype(o_ref.dtype)
        off += o_ref.shape[0]


def _inproj(x2d, w_nat, dt_nat, w_tr, dt_tr):
    m = x2d.shape[0]
    wn = jnp.concatenate(w_nat, axis=1)
    wt = jnp.concatenate(w_tr, axis=0)
    in_specs = [pl.BlockSpec((T_PROJ, D_MODEL), lambda i: (i, 0)),
                pl.BlockSpec(wn.shape, lambda i: (0, 0)), pl.BlockSpec(wt.shape, lambda i: (0, 0))]
    out_specs = [pl.BlockSpec((T_PROJ, w.shape[1]), lambda i: (i, 0)) for w in w_nat]
    out_specs += [pl.BlockSpec((w.shape[0], T_PROJ), lambda i: (0, i)) for w in w_tr]
    out_shape = [jax.ShapeDtypeStruct((m, w.shape[1]), dt) for w, dt in zip(w_nat, dt_nat)]
    out_shape += [jax.ShapeDtypeStruct((w.shape[0], m), dt) for w, dt in zip(w_tr, dt_tr)]
    outs = pl.pallas_call(
        functools.partial(_inproj_kernel, n_nat=len(w_nat)),
        grid=(m // T_PROJ,),
        in_specs=in_specs, out_specs=out_specs, out_shape=out_shape,
        compiler_params=_params("parallel"), name="inproj",
    )(x2d, wn, wt)
    return outs[:len(w_nat)], outs[len(w_nat):]


def _compress_kernel(ksrc_ref, vsrc_ref, pos_ref, w1k_ref, w2k_ref, w1v_ref, w2v_ref,
                     kc_ref, vc_ref, *, n_chunk):
    def hidden(src_ref, w1_ref):
        top = jnp.zeros((n_chunk, 2 * CMP_HIDDEN), F32)
        bot = jnp.zeros((n_chunk, 2 * CMP_HIDDEN), F32)
        for p in range(CMP_STRIDE):
            xp = src_ref[0, pl.ds(p, n_chunk, stride=CMP_STRIDE), :]
            top += _dot((xp + pos_ref[p:p + 1, :]).astype(BF16), w1_ref[p])
            q = CMP_STRIDE + p
            bot += _dot((xp + pos_ref[q:q + 1, :]).astype(BF16), w1_ref[q])
        hid = top + pltpu.roll(bot, n_chunk - 1, 0)
        return jax.nn.gelu(hid).astype(BF16)

    kc_ref[0] = _dot(hidden(ksrc_ref, w1k_ref), w2k_ref[...]).astype(kc_ref.dtype)
    vc_ref[0] = _dot_nt(w2v_ref[...], hidden(vsrc_ref, w1v_ref)).astype(vc_ref.dtype)


def _compress(kc_src, vc_src, pos2, w1k, w2k, w1v, w2v_t):
    b, s, _ = kc_src.shape
    n_chunk = s // CMP_STRIDE
    full = lambda a: pl.BlockSpec(a.shape, lambda i: (0,) * a.ndim)
    src = pl.BlockSpec((1, s, D_KV), lambda i: (i, 0, 0))
    return pl.pallas_call(
        functools.partial(_compress_kernel, n_chunk=n_chunk),
        grid=(b,),
        in_specs=[src, src, full(pos2), full(w1k), full(w2k), full(w1v), full(w2v_t)],
        out_specs=[pl.BlockSpec((1, n_chunk, D_KV), lambda i: (i, 0, 0)),
                   pl.BlockSpec((1, D_KV, n_chunk), lambda i: (i, 0, 0))],
        out_shape=[jax.ShapeDtypeStruct((b, n_chunk, D_KV), BF16),
                   jax.ShapeDtypeStruct((b, D_KV, n_chunk), BF16)],
        compiler_params=_params("parallel"), name="nsa_compress",
    )(kc_src, vc_src, pos2, w1k, w2k, w1v, w2v_t)


def _nsa_select_kernel(qt_ref, kc_ref, vct_ref, pbias_ref, ovl_ref, ocmp_ref, sel_ref, *, n_cmp_pad):
    T = T_SEL
    j = pl.program_id(1)
    t0 = j * T
    n_idx = lax.broadcasted_iota(jnp.int32, (n_cmp_pad, T), 0)
    tok_c = t0 + lax.broadcasted_iota(jnp.int32, (n_cmp_pad, T), 1)
    mask_c = tok_c >= CMP_STRIDE * n_idx + (CMP_BLOCK - 1)
    off = pl.multiple_of(n_cmp_pad - (T // CMP_STRIDE) * j, 8)
    psums = [None] * NSA_KV_GROUPS
    for h in range(NSA_HEADS):
        g = h // NSA_HPG
        qh = qt_ref[HEAD_DIM * h:HEAD_DIM * (h + 1), :]
        zq = jnp.zeros_like(qh)
        qh = jnp.concatenate([qh, zq] if g == 0 else [zq, qh], axis=0)
        s = _dot(kc_ref[0], qh) + pbias_ref[h, pl.ds(off, n_cmp_pad), :]
        s = jnp.where(mask_c, s, NEG_BIG)
        m = jnp.max(s, axis=0, keepdims=True)
        p = jnp.where(mask_c, jnp.exp2(s - m), 0.0)
        l = jnp.sum(p, axis=0, keepdims=True)
        p = p / jnp.where(l > 0.0, l, 1.0)
        o_both = _dot(vct_ref[0], p.astype(BF16))
        ocmp_ref[HEAD_DIM * h:HEAD_DIM * (h + 1), :] = o_both[HEAD_DIM * g:HEAD_DIM * (g + 1)]
        psums[g] = p if psums[g] is None else psums[g] + p

    jblk = lax.broadcasted_iota(jnp.int32, (MAX_SLC, T), 0)
    tok = t0 + lax.broadcasted_iota(jnp.int32, (MAX_SLC, T), 1)
    cur = lax.shift_right_logical(tok, 6)
    forced = (jblk == 0) | (jblk == cur) | (jblk == cur - 1)
    valid = jblk * SLC_BLOCK <= tok
    jsub = lax.broadcasted_iota(jnp.int32, (8, T), 0)
    for g in range(NSA_KV_GROUPS):
        imp = _dot(ovl_ref[...], psums[g].astype(BF16))
        score = jnp.where(valid, imp + jnp.where(forced, FORCE_BONUS, 0.0), NEG_BIG)
        blocks = [score[8 * rb:8 * rb + 8] for rb in range(MAX_SLC // 8)]
        ranks = [jnp.zeros((8, T), F32) for _ in blocks]
        for jp in range(MAX_SLC):
            other = score[jp:jp + 1, :]
            for rb, blk in enumerate(blocks):
                ge = jnp.where(other >= blk, 1.0, 0.0)
                gt = jnp.where(other > blk, 1.0, 0.0)
                if 8 * rb > jp:
                    inc = ge
                elif 8 * rb + 7 < jp:
                    inc = gt
                else:
                    inc = jnp.where(jsub > jp - 8 * rb, ge, gt)
                ranks[rb] = ranks[rb] + inc
        rank = jnp.concatenate(ranks, axis=0)
        sel_ref[0, MAX_SLC * g:MAX_SLC * (g + 1), :] = jnp.where(rank < float(SLC_TOPN), 0.0, NEG_BIG).astype(BF16)


def _nsa_select(q_t, kc, vc_t, pbias, ovl_t):
    b, n_cmp_pad, _ = kc.shape
    m = q_t.shape[1]
    nq = m // b // T_SEL
    cols = lambda r: pl.BlockSpec((r, T_SEL), lambda i, j: (0, i * nq + j))
    per_b = lambda a: pl.BlockSpec((1,) + a.shape[1:], lambda i, j: (i, 0, 0))
    full = lambda a: pl.BlockSpec(a.shape, lambda i, j: (0,) * a.ndim)
    return pl.pallas_call(
        functools.partial(_nsa_select_kernel, n_cmp_pad=n_cmp_pad),
        grid=(b, nq),
        in_specs=[cols(D_NSA), per_b(kc), per_b(vc_t), full(pbias), full(ovl_t)],
        out_specs=[cols(D_NSA), pl.BlockSpec((1, NSA_KV_GROUPS * MAX_SLC, T_SEL), lambda i, j: (i, 0, j))],
        out_shape=[jax.ShapeDtypeStruct((D_NSA, m), F32),
                   jax.ShapeDtypeStruct((b, NSA_KV_GROUPS * MAX_SLC, m // b), BF16)],
        compiler_params=_params("parallel", "parallel"), name="nsa_select",
    )(q_t, kc, vc_t, pbias, ovl_t)


def _online_init(m_ref, l_ref, acc_ref):
    m_ref[...] = jnp.full(m_ref.shape, NEG_BIG, F32)
    l_ref[...] = jnp.zeros(l_ref.shape, F32)
    acc_ref[...] = jnp.zeros(acc_ref.shape, F32)


def _scores(k_tile, q_ref, q_rows, bias_ref=None, ok=None):
    s = _dot(k_tile, q_ref[0:q_rows, :])
    if bias_ref is not None:
        s = s + bias_ref[...]
    if ok is not None:
        s = jnp.where(ok, s, NEG_BIG)
    return s


def _online_step(m_ref, l_ref, acc_ref, s, v_t):
    m = m_ref[...]
    m_new = jnp.maximum(m, jnp.max(s, axis=0, keepdims=True))
    alpha = jnp.exp2(m - m_new)
    p = jnp.exp2(s - m_new)
    m_ref[...] = m_new
    l_ref[...] = alpha * l_ref[...] + jnp.sum(p, axis=0, keepdims=True)
    acc_ref[...] = alpha * acc_ref[...] + _dot(v_t, p.astype(BF16))


def _nsa_attend_kernel(qt_ref, sel_ref, ocmp_ref, gt_ref, ks_ref, kw_ref, vs_ref, vw_ref, bias_ref, onehot_ref,
                       o_ref, kaug_ref, qaug_ref, ms_ref, ls_ref, accs_ref, mw_ref, lw_ref, accw_ref):
    T = T_NSA
    j = pl.program_id(1)

    @pl.when(j == 0)
    def _():
        kaug_ref[:, 0:LANES] = ks_ref[0]
        kaug_ref[:, LANES:2 * LANES] = onehot_ref[...]
        qaug_ref[...] = jnp.zeros(qaug_ref.shape, BF16)

    for h in range(NSA_HEADS):
        g = h // NSA_HPG
        qaug_ref[HEAD_DIM * g:HEAD_DIM * (g + 1), h * T:(h + 1) * T] = qt_ref[HEAD_DIM * h:HEAD_DIM * (h + 1), :]
        qaug_ref[LANES:LANES + MAX_SLC, h * T:(h + 1) * T] = sel_ref[0, MAX_SLC * g:MAX_SLC * (g + 1), :]

    key = lax.broadcasted_iota(jnp.int32, (T, NSA_HEADS * T), 0)
    qry = lax.broadcasted_iota(jnp.int32, (T, NSA_HEADS * T), 1) & (T - 1)
    causal = key <= qry
    st_s = (ms_ref, ls_ref, accs_ref)
    st_w = (mw_ref, lw_ref, accw_ref)
    _online_init(*st_s)
    _online_init(*st_w)

    far_len = jnp.maximum(j - 1, 0) * T

    def far_step(r0, n_keys):
        s = _scores(kaug_ref[pl.ds(r0, n_keys), :], qaug_ref, 2 * LANES)
        _online_step(*st_s, s, vs_ref[:, pl.ds(r0, n_keys)])

    def far_big(c, carry):
        r0 = pl.multiple_of(c * (2 * FAR_KEYS), 2 * FAR_KEYS)
        s_lo = _scores(kaug_ref[pl.ds(r0, FAR_KEYS), :], qaug_ref, 2 * LANES)
        s_hi = _scores(kaug_ref[pl.ds(r0 + FAR_KEYS, FAR_KEYS), :], qaug_ref, 2 * LANES)
        _online_step(*st_s, s_lo, vs_ref[:, pl.ds(r0, FAR_KEYS)])
        _online_step(*st_s, s_hi, vs_ref[:, pl.ds(r0 + FAR_KEYS, FAR_KEYS)])
        return carry

    n_big = lax.shift_right_logical(far_len, FAR_KEYS.bit_length())
    lax.fori_loop(0, n_big, far_big, 0)
    done = n_big * (2 * FAR_KEYS)

    @pl.when(far_len - done >= FAR_KEYS)
    def _():
        far_step(pl.multiple_of(done, FAR_KEYS), FAR_KEYS)

    done = done + jnp.where(far_len - done >= FAR_KEYS, FAR_KEYS, 0)

    def far_small(c, carry):
        far_step(pl.multiple_of(done + c * T, T), T)
        return carry

    lax.fori_loop(0, lax.shift_right_logical(far_len - done, T.bit_length() - 1), far_small, 0)

    n_win = WINDOW // T
    starts = [pl.multiple_of(jnp.maximum(j - d, 0) * T, T) for d in range(n_win + 1)]
    in_range = [key < jnp.where(j - d >= 0, T, -1) for d in range(n_win + 1)]
    sel_scores, win_scores = {}, {}
    for d in (1, 0):
        sel_scores[d] = _scores(kaug_ref[pl.ds(starts[d], T), :], qaug_ref, 2 * LANES, bias_ref.at[d],
                                causal if d == 0 else in_range[d])
    for d in range(n_win + 1):
        if d == 0:
            ok = causal
        elif d == n_win:
            ok = (key > qry) & in_range[d]
        else:
            ok = in_range[d]
        win_scores[d] = _scores(kw_ref[0, pl.ds(starts[d], T), :], qaug_ref, LANES,
                                bias_ref.at[d] if d <= 1 else None, ok)
    for d in (1, 0):
        _online_step(*st_s, sel_scores[d], vs_ref[:, pl.ds(starts[d], T)])
    for d in range(n_win + 1):
        _online_step(*st_w, win_scores[d], vw_ref[:, pl.ds(starts[d], T)])

    o_sel = accs_ref[...] / ls_ref[...]
    o_win = accw_ref[...] / lw_ref[...]
    gates = jax.nn.sigmoid(gt_ref[...])
    heads = []
    for h in range(NSA_HEADS):
        cols = slice(h * T, (h + 1) * T)
        rows = slice(HEAD_DIM * (h // NSA_HPG), HEAD_DIM * (h // NSA_HPG + 1))
        heads.append(gates[3 * h:3 * h + 1, :] * ocmp_ref[HEAD_DIM * h:HEAD_DIM * (h + 1), :]
                     + gates[3 * h + 1:3 * h + 2, :] * o_sel[rows, cols]
                     + gates[3 * h + 2:3 * h + 3, :] * o_win[rows, cols])
    o_ref[0] = jnp.concatenate(heads, axis=0).T.astype(o_ref.dtype)


def _nsa_attend(q_t, sel_t, ocmp_t, g_t, ks, kw, vs_t, vw_t, bias_t, onehot):
    b, s, _ = ks.shape
    T = T_NSA
    nq = s // T
    R = NSA_HEADS * T
    cols = lambda r: pl.BlockSpec((r, T), lambda i, j: (0, i * nq + j))
    row_b = lambda a: pl.BlockSpec((a.shape[0], s), lambda i, j: (0, i))
    per_b = lambda a: pl.BlockSpec((1,) + a.shape[1:], lambda i, j: (i, 0, 0))
    full = lambda a: pl.BlockSpec(a.shape, lambda i, j: (0,) * a.ndim)
    stat = [pltpu.VMEM((1, R), F32), pltpu.VMEM((1, R), F32), pltpu.VMEM((LANES, R), F32)]
    return pl.pallas_call(
        _nsa_attend_kernel,
        grid=(b, nq),
        in_specs=[cols(D_NSA), pl.BlockSpec((1, NSA_KV_GROUPS * MAX_SLC, T), lambda i, j: (i, 0, j)),
                  cols(D_NSA), cols(LANES), per_b(ks), per_b(kw), row_b(vs_t), row_b(vw_t),
                  full(bias_t), full(onehot)],
        out_specs=pl.BlockSpec((1, T, D_NSA), lambda i, j: (i, j, 0)),
        out_shape=jax.ShapeDtypeStruct((b, s, D_NSA), BF16),
        scratch_shapes=[pltpu.VMEM((s, 2 * LANES), BF16), pltpu.VMEM((2 * LANES, R), BF16)] + stat + stat,
        compiler_params=_params("parallel", "arbitrary"), name="nsa_attend",
    )(q_t, sel_t, ocmp_t, g_t, ks, kw, vs_t, vw_t, bias_t, onehot)


def _sb_kernel(qt_ref, k_ref, vt_ref, tri_ref, o_ref, carry_ref, acc_ref):
    T = T_SB
    j = pl.program_id(1)
    n_pair = SB_HEADS // 2
    key = lax.broadcasted_iota(jnp.int32, (T, 2 * T), 0)
    qry = lax.broadcasted_iota(jnp.int32, (T, 2 * T), 1) & (T - 1)
    strict = key < qry
    carry_ref[...] = jnp.zeros(carry_ref.shape, F32)
    acc_ref[...] = jnp.zeros(acc_ref.shape, F32)

    def tiles(cs, mask):
        starts = [pl.multiple_of(c * T, T) for c in cs]
        units = [(ci, p) for ci in range(len(cs)) for p in range(n_pair)]
        z2s, incls = [], []
        for c, p in units:
            q_even = qt_ref[HEAD_DIM * (2 * p):HEAD_DIM * (2 * p + 1), :]
            q_odd = qt_ref[HEAD_DIM * (2 * p + 1):HEAD_DIM * (2 * p + 2), :]
            zq = jnp.zeros_like(q_even)
            q_pair = jnp.concatenate([jnp.concatenate([q_even, zq], axis=0),
                                      jnp.concatenate([zq, q_odd], axis=0)], axis=1)
            z2s.append(_dot(k_ref[0, pl.ds(starts[c], T), LANES * p:LANES * (p + 1)], q_pair))
        for z2 in z2s:
            rest = jnp.maximum(z2, 0.0) + jnp.log2(1.0 + jnp.exp2(-jnp.abs(z2)))
            if mask is not None:
                rest = jnp.where(mask, rest, 0.0)
            hi, lo = _split_bf16(rest)
            incls.append(_dot(tri_ref[...], jnp.concatenate([hi, lo], axis=0)))
        carry = [carry_ref[p] for p in range(n_pair)]
        acc = [acc_ref[p] for p in range(n_pair)]
        for (c, p), z2, incl in zip(units, z2s, incls):
            a = jnp.exp2(z2 - incl - carry[p])
            if mask is not None:
                a = jnp.where(mask, a, 0.0)
            acc[p] = acc[p] + _dot(vt_ref[LANES * p:LANES * (p + 1), pl.ds(starts[c], T)], a.astype(BF16))
            carry[p] = carry[p] + incl[0:1, :]
        for p in range(n_pair):
            carry_ref[p] = carry[p]
            acc_ref[p] = acc[p]

    tiles([j], strict)

    def live(c):
        return (c >= 0) & (jnp.min(carry_ref[...]) < SB_DEAD_BITS)

    def back_one(c):
        tiles([c], None)
        return c - 1

    lax.while_loop(live, back_one, j - 1)
    row = lax.broadcasted_iota(jnp.int32, (LANES, T), 0)
    for p in range(n_pair):
        acc = acc_ref[p]
        o_ref[0, :, LANES * p:LANES * (p + 1)] = (
            jnp.where(row < HEAD_DIM, acc[:, 0:T], acc[:, T:2 * T]).T.astype(o_ref.dtype))


def _sb_attention(q_t, k, v_t, tri_t):
    b, s, _ = k.shape
    T = T_SB
    nq = s // T
    n_pair = SB_HEADS // 2
    return pl.pallas_call(
        _sb_kernel,
        grid=(b, nq),
        in_specs=[pl.BlockSpec((D_SB, T), lambda i, j: (0, i * nq + j)),
                  pl.BlockSpec((1, s, D_SB), lambda i, j: (i, 0, 0)),
                  pl.BlockSpec((D_SB, s), lambda i, j: (0, i)),
                  pl.BlockSpec(tri_t.shape, lambda i, j: (0, 0))],
        out_specs=pl.BlockSpec((1, T, D_SB), lambda i, j: (i, j, 0)),
        out_shape=jax.ShapeDtypeStruct((b, s, D_SB), BF16),
        scratch_shapes=[pltpu.VMEM((n_pair, 1, 2 * T), F32), pltpu.VMEM((n_pair, LANES, 2 * T), F32)],
        compiler_params=_params("parallel", "parallel"), name="sb_attention",
    )(q_t, k, v_t, tri_t)


def _hgrn_kernel(q_ref, f_ref, i_ref, lb_ref, nw_ref, ones_ref, bd_ref, o_ref,
                 st_ref, qj_ref, kj_ref, fj_ref, vj_ref, qd_ref, kd_ref, oc_ref, x_ref, u_ref, sb_ref, dec_ref,
                 att_ref, oall_ref):
    C = HG_SUB
    n_blk = T_HG // C

    @pl.when(pl.program_id(2) == 0)
    def _():
        st_ref[...] = jnp.zeros(st_ref.shape, F32)

    lb = lb_ref[...]
    lb_floor = jnp.maximum(lb, LB_FLOOR)
    one_m_lb = 1.0 - lb
    cum = None
    for j in range(C):
        xj = f_ref[0, pl.ds(j, n_blk, stride=C), :]
        e = jnp.exp(-jnp.abs(xj))
        r = 1.0 / (1.0 + e)
        er = e * r
        pos = xj >= 0.0
        fj = lb_floor + one_m_lb * jnp.where(pos, r, er)
        kj_ref[j] = one_m_lb * jnp.where(pos, er, r)
        fj_ref[j] = fj
        cum = fj if cum is None else cum * fj
        qj = q_ref[0, pl.ds(j, n_blk, stride=C), :]
        qj_ref[j] = qj
        vj_ref[j] = i_ref[0, pl.ds(j, n_blk, stride=C), :]
        qd_ref[pl.ds(j, n_blk, stride=HG_PITCH), :] = qj * cum
    dec_ref[...] = cum
    tail = None
    for j in reversed(range(C)):
        kd_ref[pl.ds(j, n_blk, stride=HG_PITCH), :] = kj_ref[j] if tail is None else kj_ref[j] * tail
        tail = fj_ref[j] if tail is None else tail * fj_ref[j]

    def kv_products(blk, carry):
        kd = kd_ref[pl.ds(pl.multiple_of(blk * HG_PITCH, 8), C), :].astype(BF16)
        vv = i_ref[0, pl.ds(pl.multiple_of(blk * C, C), C), :].astype(BF16)
        u_ref[blk] = bd_ref[...] * _dot_tn(vv, kd)
        return carry

    lax.fori_loop(0, n_blk, kv_products, 0, unroll=HG_UNROLL)

    def scan(blk, st):
        sb_ref[blk] = st.astype(BF16)
        return st * dec_ref[pl.ds(blk, 1), :] + u_ref[blk]

    st_ref[...] = lax.fori_loop(0, n_blk, scan, st_ref[...], unroll=HG_UNROLL)

    def outputs(blk, carry):
        r0 = pl.multiple_of(blk * HG_PITCH, 8)
        oc_ref[pl.ds(r0, C), :] = _dot_nt(qd_ref[pl.ds(r0, C), :].astype(BF16), sb_ref[blk])
        return carry

    lax.fori_loop(0, n_blk, outputs, 0, unroll=HG_UNROLL)

    base = lambda j: (j * (j + 1) // 2) * n_blk
    for j in range(C):
        qdec = qj_ref[j]
        for jp in reversed(range(j + 1)):
            x_ref[base(j) + jp * n_blk:base(j) + (jp + 1) * n_blk, :] = (qdec * kj_ref[jp]).astype(BF16)
            if jp > 0:
                qdec = qdec * fj_ref[jp]
        att_ref[base(j):base(j + 1), :] = _dot(x_ref[base(j):base(j + 1), :], ones_ref[...])
    for j in range(C):
        oj = oc_ref[pl.ds(j, n_blk, stride=HG_PITCH), :]
        for jp in range(j + 1):
            oj = oj + att_ref[base(j) + jp * n_blk:base(j) + (jp + 1) * n_blk, :] * vj_ref[jp]
        oall_ref[j * n_blk:(j + 1) * n_blk, :] = oj
    o_all = oall_ref[...]
    hi, lo = _split_bf16(o_all * o_all)
    ms = (_dot(hi, ones_ref[...]) + _dot(lo, ones_ref[...])) * (1.0 / HEAD_DIM)
    oall_ref[...] = o_all * lax.rsqrt(ms + RMS_EPS) * nw_ref[...]
    for j in range(C):
        o_ref[0, pl.ds(j, n_blk, stride=C), :] = oall_ref[j * n_blk:(j + 1) * n_blk, :]


def _hgrn(q, f, i, lb, nw, ones_bd, bd_mask):
    b, s, _ = q.shape
    n_blk = T_HG // HG_SUB
    tile = pl.BlockSpec((1, T_HG, LANES), lambda bi, pi, ti: (bi, ti, pi))
    vec = pl.BlockSpec((1, LANES), lambda bi, pi, ti: (0, pi))
    full = lambda a: pl.BlockSpec(a.shape, lambda bi, pi, ti: (0,) * a.ndim)
    jm = pltpu.VMEM((HG_SUB, n_blk, LANES), F32)
    nat = pltpu.VMEM((n_blk * HG_PITCH, LANES), F32)
    return pl.pallas_call(
        _hgrn_kernel,
        grid=(b, D_HG // LANES, s // T_HG),
        in_specs=[tile, tile, tile, vec, vec, full(ones_bd), full(bd_mask)],
        out_specs=tile,
        out_shape=jax.ShapeDtypeStruct((b, s, D_HG), F32),
        scratch_shapes=[pltpu.VMEM((LANES, LANES), F32), jm, jm, jm, jm, nat, nat, nat,
                        pltpu.VMEM((HG_SUB * (HG_SUB + 1) // 2 * n_blk, LANES), BF16),
                        pltpu.VMEM((n_blk, LANES, LANES), F32),
                        pltpu.VMEM((n_blk, LANES, LANES), BF16), pltpu.VMEM((n_blk, LANES), F32),
                        pltpu.VMEM((HG_SUB * (HG_SUB + 1) // 2 * n_blk, LANES), F32), pltpu.VMEM((T_HG, LANES), F32)],
        compiler_params=_params("parallel", "parallel", "arbitrary"), name="hgrn2",
    )(q, f, i, lb, nw, ones_bd, bd_mask)


def _out_kernel(onsa_ref, osb_ref, ohg_ref, z_ref, x_ref, w_ref, g_ref, b_ref, o_ref):
    z = z_ref[...].astype(F32)
    sz = z * jax.nn.sigmoid(z)
    mixed = jnp.concatenate([(onsa_ref[...] * sz[:, 0:D_NSA]).astype(BF16),
                             (osb_ref[...] * sz[:, D_NSA:D_NSA + D_SB]).astype(BF16),
                             (ohg_ref[...] * sz[:, D_NSA + D_SB:D_MIX]).astype(BF16)], axis=1)
    v = ALPHA * x_ref[...] + _dot(mixed, w_ref[...])
    mu = jnp.mean(v, axis=-1, keepdims=True)
    vc = v - mu
    var = jnp.mean(vc * vc, axis=-1, keepdims=True)
    o_ref[...] = vc * lax.rsqrt(var + LN_EPS) * g_ref[...] + b_ref[...]


def _out_proj(o_nsa, o_sb, o_hg, z_all, x2d, w, g, bvec):
    m = x2d.shape[0]
    rows = lambda n: pl.BlockSpec((T_PROJ, n), lambda i: (i, 0))
    full = lambda a: pl.BlockSpec(a.shape, lambda i: (0, 0))
    return pl.pallas_call(
        _out_kernel,
        grid=(m // T_PROJ,),
        in_specs=[rows(D_NSA), rows(D_SB), rows(D_HG), rows(D_MIX), rows(D_MODEL), full(w), full(g), full(bvec)],
        out_specs=rows(D_MODEL),
        out_shape=jax.ShapeDtypeStruct((m, D_MODEL), F32),
        compiler_params=_params("parallel"), name="out_proj_norm",
    )(o_nsa, o_sb, o_hg, z_all, x2d, w, g, bvec)


def _t5_bucket_np(rel):
    n = np.maximum(rel, 0)
    max_exact = NUM_BUCKETS // 2
    large = max_exact + (np.log(np.maximum(n, 1).astype(np.float32) / max_exact)
                         / math.log(MAX_DISTANCE / max_exact) * (NUM_BUCKETS - max_exact)).astype(np.int32)
    large = np.clip(large, 0, NUM_BUCKETS - 1)
    return np.where(n < max_exact, n, large).astype(np.int32)


def _bias_tables(rel_bias, s):
    tbl = ((rel_bias - rel_bias[NUM_BUCKETS - 1]) * LOG2E).astype(F32)

    def expand(rel):
        onehot = (jnp.asarray(_t5_bucket_np(rel).reshape(-1, 1)) == jnp.arange(NUM_BUCKETS)[None, :]).astype(F32)
        return jnp.dot(onehot, tbl, precision=lax.Precision.HIGHEST).reshape(rel.shape + (NSA_HEADS,))

    n_cmp_pad = s // CMP_STRIDE
    n_rel = np.arange(2 * n_cmp_pad)[:, None] - n_cmp_pad
    r = np.arange(T_SEL)[None, :]
    pbias = jnp.transpose(expand(r - CMP_STRIDE * n_rel - (CMP_BLOCK - 1)), (2, 0, 1))
    T = T_NSA
    key = np.arange(T)[:, None]
    qry = np.arange(T)[None, :]
    near = np.stack([qry - key, T + qry - key])
    bias_t = jnp.transpose(expand(near), (0, 1, 3, 2)).reshape(2, T, NSA_HEADS * T)
    return pbias, bias_t


def _static_tables(s):
    n_chunk = s // CMP_STRIDE
    cmp_start = np.arange(n_chunk) * CMP_STRIDE
    slc_start = np.arange(MAX_SLC) * SLC_BLOCK
    ovl_t = ((cmp_start[None, :] < slc_start[:, None] + SLC_BLOCK)
             & (cmp_start[None, :] + CMP_BLOCK > slc_start[:, None])
             & (cmp_start[None, :] + CMP_BLOCK <= s)).astype(np.float32)
    onehot = (np.arange(s)[:, None] // SLC_BLOCK == np.arange(LANES)[None, :]).astype(np.float32)
    tri = (np.arange(T_SB)[None, :] >= np.arange(T_SB)[:, None]).astype(np.float32)
    tri = np.concatenate([tri, tri], axis=1)
    ones_bd = np.kron(np.eye(2), np.ones((HEAD_DIM, HEAD_DIM))).astype(np.float32)
    as_bf16 = lambda a: jnp.asarray(a, dtype=BF16)
    return as_bf16(ovl_t), as_bf16(onehot), as_bf16(tri), as_bf16(ones_bd), jnp.asarray(ones_bd)


def _layer_weights(w_in_l, cmp_pos_l, w_ck1_l, w_ck2_l, w_cv1_l, w_cv2_l):
    offs = np.cumsum((0,) + SPLIT_SIZES)
    (w_q, w_kc, w_vc, w_ks, w_vs, w_kw, w_vw, w_g, w_nz,
     w_sq, w_sk, w_sv, w_sz, w_hq, w_hf, w_hi, w_hz) = [w_in_l[:, offs[i]:offs[i + 1]] for i in range(len(SPLIT_SIZES))]
    w_gp = jnp.concatenate([w_g, jnp.zeros((D_MODEL, LANES - N_GATES), F32)], axis=1)
    w_z = jnp.concatenate([w_nz, w_sz, w_hz], axis=1)
    w_nat = [w_kc, w_vc, w_ks, w_kw, w_z, w_sk, w_hq, w_hf, w_hi]
    dt_nat = [F32, F32, BF16, BF16, BF16, BF16, F32, F32, F32]
    w_tr = [(w_q * QK_SCALE2).T, w_vs.T, w_vw.T, w_gp.T, (w_sq * QK_SCALE2).T, w_sv.T]
    dt_tr = [BF16, BF16, BF16, F32, BF16, BF16]
    w_nat = [w.astype(BF16) for w in w_nat]
    w_tr = [w.astype(BF16) for w in w_tr]

    def block_diag(w):
        z = jnp.zeros_like(w)
        return jnp.concatenate([jnp.concatenate([w, z], axis=-1), jnp.concatenate([z, w], axis=-1)], axis=-2)

    pos2 = jnp.concatenate([cmp_pos_l, cmp_pos_l], axis=1)
    w1k = block_diag(w_ck1_l.reshape(CMP_BLOCK, HEAD_DIM, CMP_HIDDEN)).astype(BF16)
    w1v = block_diag(w_cv1_l.reshape(CMP_BLOCK, HEAD_DIM, CMP_HIDDEN)).astype(BF16)
    w2k = block_diag(w_ck2_l).astype(BF16)
    w2v_t = block_diag(w_cv2_l).T.astype(BF16)
    return (w_nat, dt_nat, w_tr, dt_tr), (pos2, w1k, w2k, w1v, w2v_t)


def kernel(x, w_in, cmp_pos, w_ck1, w_ck2, w_cv1, w_cv2, hg_lb, hg_norm_w, w_out, ln_g, ln_b, rel_bias):
    b, s, d = x.shape
    assert d == D_MODEL and s % T_HG == 0 and s // SLC_BLOCK <= MAX_SLC and s >= WINDOW + T_NSA
    lb_w = jax.nn.softmax(hg_lb.astype(F32), axis=0)
    lb_all = jnp.cumsum(lb_w, axis=0) - lb_w[0]
    pbias, bias_t = _bias_tables(rel_bias, s)
    ovl_t, onehot, tri, ones_bd, bd_mask = _static_tables(s)

    x2d = x.reshape(b * s, d)
    for l in range(DEPTH):
        proj_w, cmp_w = _layer_weights(w_in[l], cmp_pos[l], w_ck1[l], w_ck2[l], w_cv1[l], w_cv2[l])
        nat, (q_t, vs_t, vw_t, g_t, sq_t, sv_t) = _inproj(x2d, *proj_w)
        kc_src, vc_src, ks, kw, z_all, sk, hq, hf, hi = [o.reshape(b, s, o.shape[-1]) for o in nat]
        kc, vc_t = _compress(kc_src, vc_src, *cmp_w)
        ocmp_t, sel_t = _nsa_select(q_t, kc, vc_t, pbias, ovl_t)
        o_nsa = _nsa_attend(q_t, sel_t, ocmp_t, g_t, ks, kw, vs_t, vw_t, bias_t, onehot)
        o_sb = _sb_attention(sq_t, sk, sv_t, tri)
        o_hg = _hgrn(hq, hf, hi, lb_all[l][None, :], hg_norm_w[l][None, :], ones_bd, bd_mask)
        x2d = _out_proj(o_nsa.reshape(b * s, D_NSA), o_sb.reshape(b * s, D_SB), o_hg.reshape(b * s, D_HG),
                        z_all.reshape(b * s, D_MIX), x2d, w_out[l].astype(BF16),
                        ln_g[l][None, :], ln_b[l][None, :])
    return x2d.reshape(b, s, d)
```

```python
import functools
import math

import numpy as np
import jax
import jax.numpy as jnp
from jax import lax
from jax.experimental import pallas as pl
from jax.experimental.pallas import tpu as pltpu

F32 = jnp.float32
BF16 = jnp.bfloat16

D_MODEL = 1024
DEPTH = 2
HEAD_DIM = 64
LANES = 128
NSA_HEADS = 6
NSA_KV_GROUPS = 2
NSA_HPG = NSA_HEADS // NSA_KV_GROUPS
CMP_BLOCK = 32
CMP_STRIDE = 16
CMP_HIDDEN = 2 * HEAD_DIM
SLC_BLOCK = 64
SLC_TOPN = 16
MAX_SLC = 64
WINDOW = 512
FORCE_BONUS = 1000.0
NEG_BIG = -1e30
LB_FLOOR = 1e-30
SB_HEADS = 4
HG_HEADS = 6
HG_SUB = 16
HG_PITCH = 24
HG_UNROLL = 64
NUM_BUCKETS = 32
MAX_DISTANCE = 128
D_NSA = NSA_HEADS * HEAD_DIM
D_KV = NSA_KV_GROUPS * HEAD_DIM
D_SB = SB_HEADS * HEAD_DIM
D_HG = HG_HEADS * HEAD_DIM
D_MIX = D_NSA + D_SB + D_HG
N_GATES = NSA_HEADS * 3
SPLIT_SIZES = (D_NSA, D_KV, D_KV, D_KV, D_KV, D_KV, D_KV, N_GATES, D_NSA,
               D_SB, D_SB, D_SB, D_SB, D_HG, D_HG, D_HG, D_HG)
ALPHA = (2 * DEPTH) ** 0.25
LN_EPS = 1e-5
RMS_EPS = 1e-6
LOG2E = math.log2(math.e)
QK_SCALE2 = LOG2E / math.sqrt(HEAD_DIM)

T_SEL = 128
T_NSA = 256
FAR_KEYS = 512
T_SB = 256
SB_DEAD_BITS = 151.0
T_HG = 1024
T_PROJ = 512
VMEM_LIMIT = 56 * 1024 * 1024

_NT = (((1,), (1,)), ((), ()))
_TN = (((0,), (0,)), ((), ()))


def _dot(a, b):
    return jnp.dot(a, b, preferred_element_type=F32)


def _dot_nt(a, b):
    return lax.dot_general(a, b, _NT, preferred_element_type=F32)


def _dot_tn(a, b):
    return lax.dot_general(a, b, _TN, preferred_element_type=F32)


def _softplus(x):
    return jnp.maximum(x, 0.0) + jnp.log1p(jnp.exp(-jnp.abs(x)))


def _split_bf16(x):
    hi = x.astype(BF16)
    lo = (x - hi.astype(F32)).astype(BF16)
    return hi, lo


def _params(*sem, flags=None):
    return pltpu.CompilerParams(dimension_semantics=sem, vmem_limit_bytes=VMEM_LIMIT, flags=flags)


def _inproj_kernel(x_ref, wn_ref, wt_ref, *o_refs, n_nat):
    xb = x_ref[...].astype(BF16)
    nat = _dot(xb, wn_ref[...])
    off = 0
    for o_ref in o_refs[:n_nat]:
        o_ref[...] = nat[:, off:off + o_ref.shape[1]].astype(o_ref.dtype)
        off += o_ref.shape[1]
    tr = _dot_nt(wt_ref[...], xb)
    off = 0
    for o_ref in o_refs[n_nat:]:
        o_ref[...] = tr[off:off + o_ref.shape[0], :].astype(o_ref.dtype)
        off += o_ref.shape[0]


def _inproj(x2d, w_nat, dt_nat, w_tr, dt_tr):
    m = x2d.shape[0]
    wn = jnp.concatenate(w_nat, axis=1)
    wt = jnp.concatenate(w_tr, axis=0)
    in_specs = [pl.BlockSpec((T_PROJ, D_MODEL), lambda i: (i, 0)),
                pl.BlockSpec(wn.shape, lambda i: (0, 0)), pl.BlockSpec(wt.shape, lambda i: (0, 0))]
    out_specs = [pl.BlockSpec((T_PROJ, w.shape[1]), lambda i: (i, 0)) for w in w_nat]
    out_specs += [pl.BlockSpec((w.shape[0], T_PROJ), lambda i: (0, i)) for w in w_tr]
    out_shape = [jax.ShapeDtypeStruct((m, w.shape[1]), dt) for w, dt in zip(w_nat, dt_nat)]
    out_shape += [jax.ShapeDtypeStruct((w.shape[0], m), dt) for w, dt in zip(w_tr, dt_tr)]
    outs = pl.pallas_call(
        functools.partial(_inproj_kernel, n_nat=len(w_nat)),
        grid=(m // T_PROJ,),
        in_specs=in_specs, out_specs=out_specs, out_shape=out_shape,
        compiler_params=_params("parallel"), name="inproj",
    )(x2d, wn, wt)
    return outs[:len(w_nat)], outs[len(w_nat):]


def _compress_kernel(ksrc_ref, vsrc_ref, pos_ref, w1k_ref, w2k_ref, w1v_ref, w2v_ref,
                     kc_ref, vc_ref, *, n_chunk):
    def hidden(src_ref, w1_ref):
        top = jnp.zeros((n_chunk, 2 * CMP_HIDDEN), F32)
        bot = jnp.zeros((n_chunk, 2 * CMP_HIDDEN), F32)
        for p in range(CMP_STRIDE):
            xp = src_ref[0, pl.ds(p, n_chunk, stride=CMP_STRIDE), :]
            top += _dot((xp + pos_ref[p:p + 1, :]).astype(BF16), w1_ref[p])
            q = CMP_STRIDE + p
            bot += _dot((xp + pos_ref[q:q + 1, :]).astype(BF16), w1_ref[q])
        hid = top + pltpu.roll(bot, n_chunk - 1, 0)
        return jax.nn.gelu(hid).astype(BF16)

    kc_ref[0] = _dot(hidden(ksrc_ref, w1k_ref), w2k_ref[...]).astype(kc_ref.dtype)
    vc_ref[0] = _dot_nt(w2v_ref[...], hidden(vsrc_ref, w1v_ref)).astype(vc_ref.dtype)


def _compress(kc_src, vc_src, pos2, w1k, w2k, w1v, w2v_t):
    b, s, _ = kc_src.shape
    n_chunk = s // CMP_STRIDE
    full = lambda a: pl.BlockSpec(a.shape, lambda i: (0,) * a.ndim)
    src = pl.BlockSpec((1, s, D_KV), lambda i: (i, 0, 0))
    return pl.pallas_call(
        functools.partial(_compress_kernel, n_chunk=n_chunk),
        grid=(b,),
        in_specs=[src, src, full(pos2), full(w1k), full(w2k), full(w1v), full(w2v_t)],
        out_specs=[pl.BlockSpec((1, n_chunk, D_KV), lambda i: (i, 0, 0)),
                   pl.BlockSpec((1, D_KV, n_chunk), lambda i: (i, 0, 0))],
        out_shape=[jax.ShapeDtypeStruct((b, n_chunk, D_KV), BF16),
                   jax.ShapeDtypeStruct((b, D_KV, n_chunk), BF16)],
        compiler_params=_params("parallel"), name="nsa_compress",
    )(kc_src, vc_src, pos2, w1k, w2k, w1v, w2v_t)


def _nsa_select_kernel(qt_ref, kc_ref, vct_ref, pbias_ref, ovl_ref, ocmp_ref, sel_ref, *, n_cmp_pad):
    T = T_SEL
    j = pl.program_id(1)
    t0 = j * T
    n_idx = lax.broadcasted_iota(jnp.int32, (n_cmp_pad, T), 0)
    tok_c = t0 + lax.broadcasted_iota(jnp.int32, (n_cmp_pad, T), 1)
    mask_c = tok_c >= CMP_STRIDE * n_idx + (CMP_BLOCK - 1)
    off = pl.multiple_of(n_cmp_pad - (T // CMP_STRIDE) * j, 8)
    psums = [None] * NSA_KV_GROUPS
    for h in range(NSA_HEADS):
        g = h // NSA_HPG
        qh = qt_ref[HEAD_DIM * h:HEAD_DIM * (h + 1), :]
        zq = jnp.zeros_like(qh)
        qh = jnp.concatenate([qh, zq] if g == 0 else [zq, qh], axis=0)
        s = _dot(kc_ref[0], qh) + pbias_ref[h, pl.ds(off, n_cmp_pad), :]
        s = jnp.where(mask_c, s, NEG_BIG)
        m = jnp.max(s, axis=0, keepdims=True)
        p = jnp.where(mask_c, jnp.exp2(s - m), 0.0)
        l = jnp.sum(p, axis=0, keepdims=True)
        p = p / jnp.where(l > 0.0, l, 1.0)
        o_both = _dot(vct_ref[0], p.astype(BF16))
        ocmp_ref[HEAD_DIM * h:HEAD_DIM * (h + 1), :] = o_both[HEAD_DIM * g:HEAD_DIM * (g + 1)]
        psums[g] = p if psums[g] is None else psums[g] + p

    jblk = lax.broadcasted_iota(jnp.int32, (MAX_SLC, T), 0)
    tok = t0 + lax.broadcasted_iota(jnp.int32, (MAX_SLC, T), 1)
    cur = lax.shift_right_logical(tok, 6)
    forced = (jblk == 0) | (jblk == cur) | (jblk == cur - 1)
    valid = jblk * SLC_BLOCK <= tok
    jsub = lax.broadcasted_iota(jnp.int32, (8, T), 0)
    for g in range(NSA_KV_GROUPS):
        imp = _dot(ovl_ref[...], psums[g].astype(BF16))
        score = jnp.where(valid, imp + jnp.where(forced, FORCE_BONUS, 0.0), NEG_BIG)
        blocks = [score[8 * rb:8 * rb + 8] for rb in range(MAX_SLC // 8)]
        ranks = [jnp.zeros((8, T), F32) for _ in blocks]
        for jp in range(MAX_SLC):
            other = score[jp:jp + 1, :]
            for rb, blk in enumerate(blocks):
                ge = jnp.where(other >= blk, 1.0, 0.0)
                gt = jnp.where(other > blk, 1.0, 0.0)
                if 8 * rb > jp:
                    inc = ge
                elif 8 * rb + 7 < jp:
                    inc = gt
                else:
                    inc = jnp.where(jsub > jp - 8 * rb, ge, gt)
                ranks[rb] = ranks[rb] + inc
        rank = jnp.concatenate(ranks, axis=0)
        sel_ref[0, MAX_SLC * g:MAX_SLC * (g + 1), :] = jnp.where(rank < float(SLC_TOPN), 0.0, NEG_BIG).astype(BF16)


def _nsa_select(q_t, kc, vc_t, pbias, ovl_t):
    b, n_cmp_pad, _ = kc.shape
    m = q_t.shape[1]
    nq = m // b // T_SEL
    cols = lambda r: pl.BlockSpec((r, T_SEL), lambda i, j: (0, i * nq + j))
    per_b = lambda a: pl.BlockSpec((1,) + a.shape[1:], lambda i, j: (i, 0, 0))
    full = lambda a: pl.BlockSpec(a.shape, lambda i, j: (0,) * a.ndim)
    return pl.pallas_call(
        functools.partial(_nsa_select_kernel, n_cmp_pad=n_cmp_pad),
        grid=(b, nq),
        in_specs=[cols(D_NSA), per_b(kc), per_b(vc_t), full(pbias), full(ovl_t)],
        out_specs=[cols(D_NSA), pl.BlockSpec((1, NSA_KV_GROUPS * MAX_SLC, T_SEL), lambda i, j: (i, 0, j))],
        out_shape=[jax.ShapeDtypeStruct((D_NSA, m), F32),
                   jax.ShapeDtypeStruct((b, NSA_KV_GROUPS * MAX_SLC, m // b), BF16)],
        compiler_params=_params("parallel", "parallel"), name="nsa_select",
    )(q_t, kc, vc_t, pbias, ovl_t)


def _online_init(m_ref, l_ref, acc_ref):
    m_ref[...] = jnp.full(m_ref.shape, NEG_BIG, F32)
    l_ref[...] = jnp.zeros(l_ref.shape, F32)
    acc_ref[...] = jnp.zeros(acc_ref.shape, F32)


def _scores(k_tile, q_ref, q_rows, bias_ref=None, ok=None):
    s = _dot(k_tile, q_ref[0:q_rows, :])
    if bias_ref is not None:
        s = s + bias_ref[...]
    if ok is not None:
        s = jnp.where(ok, s, NEG_BIG)
    return s


def _online_step(m_ref, l_ref, acc_ref, s, v_t):
    m = m_ref[...]
    m_new = jnp.maximum(m, jnp.max(s, axis=0, keepdims=True))
    alpha = jnp.exp2(m - m_new)
    p = jnp.exp2(s - m_new)
    m_ref[...] = m_new
    l_ref[...] = alpha * l_ref[...] + jnp.sum(p, axis=0, keepdims=True)
    acc_ref[...] = alpha * acc_ref[...] + _dot(v_t, p.astype(BF16))


def _nsa_attend_kernel(qt_ref, sel_ref, ocmp_ref, gt_ref, ks_ref, kw_ref, vs_ref, vw_ref, bias_ref, onehot_ref,
                       o_ref, kaug_ref, qaug_ref, ms_ref, ls_ref, accs_ref, mw_ref, lw_ref, accw_ref):
    T = T_NSA
    j = pl.program_id(1)

    @pl.when(j == 0)
    def _():
        kaug_ref[:, 0:LANES] = ks_ref[0]
        kaug_ref[:, LANES:2 * LANES] = onehot_ref[...]
        qaug_ref[...] = jnp.zeros(qaug_ref.shape, BF16)

    for h in range(NSA_HEADS):
        g = h // NSA_HPG
        qaug_ref[HEAD_DIM * g:HEAD_DIM * (g + 1), h * T:(h + 1) * T] = qt_ref[HEAD_DIM * h:HEAD_DIM * (h + 1), :]
        qaug_ref[LANES:LANES + MAX_SLC, h * T:(h + 1) * T] = sel_ref[0, MAX_SLC * g:MAX_SLC * (g + 1), :]

    key = lax.broadcasted_iota(jnp.int32, (T, NSA_HEADS * T), 0)
    qry = lax.broadcasted_iota(jnp.int32, (T, NSA_HEADS * T), 1) & (T - 1)
    causal = key <= qry
    st_s = (ms_ref, ls_ref, accs_ref)
    st_w = (mw_ref, lw_ref, accw_ref)
    _online_init(*st_s)
    _online_init(*st_w)

    far_len = jnp.maximum(j - 1, 0) * T

    def far_step(r0, n_keys):
        s = _scores(kaug_ref[pl.ds(r0, n_keys), :], qaug_ref, 2 * LANES)
        _online_step(*st_s, s, vs_ref[:, pl.ds(r0, n_keys)])

    def far_big(c, carry):
        r0 = pl.multiple_of(c * (2 * FAR_KEYS), 2 * FAR_KEYS)
        s_lo = _scores(kaug_ref[pl.ds(r0, FAR_KEYS), :], qaug_ref, 2 * LANES)
        s_hi = _scores(kaug_ref[pl.ds(r0 + FAR_KEYS, FAR_KEYS), :], qaug_ref, 2 * LANES)
        _online_step(*st_s, s_lo, vs_ref[:, pl.ds(r0, FAR_KEYS)])
        _online_step(*st_s, s_hi, vs_ref[:, pl.ds(r0 + FAR_KEYS, FAR_KEYS)])
        return carry

    n_big = lax.shift_right_logical(far_len, FAR_KEYS.bit_length())
    lax.fori_loop(0, n_big, far_big, 0)
    done = n_big * (2 * FAR_KEYS)

    @pl.when(far_len - done >= FAR_KEYS)
    def _():
        far_step(pl.multiple_of(done, FAR_KEYS), FAR_KEYS)

    done = done + jnp.where(far_len - done >= FAR_KEYS, FAR_KEYS, 0)

    def far_small(c, carry):
        far_step(pl.multiple_of(done + c * T, T), T)
        return carry

    lax.fori_loop(0, lax.shift_right_logical(far_len - done, T.bit_length() - 1), far_small, 0)

    n_win = WINDOW // T
    starts = [pl.multiple_of(jnp.maximum(j - d, 0) * T, T) for d in range(n_win + 1)]
    in_range = [key < jnp.where(j - d >= 0, T, -1) for d in range(n_win + 1)]
    sel_scores, win_scores = {}, {}
    for d in (1, 0):
        sel_scores[d] = _scores(kaug_ref[pl.ds(starts[d], T), :], qaug_ref, 2 * LANES, bias_ref.at[d],
                                causal if d == 0 else in_range[d])
    for d in range(n_win + 1):
        if d == 0:
            ok = causal
        elif d == n_win:
            ok = (key > qry) & in_range[d]
        else:
            ok = in_range[d]
        win_scores[d] = _scores(kw_ref[0, pl.ds(starts[d], T), :], qaug_ref, LANES,
                                bias_ref.at[d] if d <= 1 else None, ok)
    for d in (1, 0):
        _online_step(*st_s, sel_scores[d], vs_ref[:, pl.ds(starts[d], T)])
    for d in range(n_win + 1):
        _online_step(*st_w, win_scores[d], vw_ref[:, pl.ds(starts[d], T)])

    gates = jax.nn.sigmoid(gt_ref[...])
    heads = []
    for h in range(NSA_HEADS):
        cols = slice(h * T, (h + 1) * T)
        rows = slice(HEAD_DIM * (h // NSA_HPG), HEAD_DIM * (h // NSA_HPG + 1))
        heads.append(gates[3 * h:3 * h + 1, :] * ocmp_ref[HEAD_DIM * h:HEAD_DIM * (h + 1), :]
                     + (gates[3 * h + 1:3 * h + 2, :] / ls_ref[:, cols]) * accs_ref[rows, cols]
                     + (gates[3 * h + 2:3 * h + 3, :] / lw_ref[:, cols]) * accw_ref[rows, cols])
    o_ref[0] = jnp.concatenate(heads, axis=0).T.astype(o_ref.dtype)


def _nsa_attend(q_t, sel_t, ocmp_t, g_t, ks, kw, vs_t, vw_t, bias_t, onehot):
    b, s, _ = ks.shape
    T = T_NSA
    nq = s // T
    R = NSA_HEADS * T
    cols = lambda r: pl.BlockSpec((r, T), lambda i, j: (0, i * nq + j))
    row_b = lambda a: pl.BlockSpec((a.shape[0], s), lambda i, j: (0, i))
    per_b = lambda a: pl.BlockSpec((1,) + a.shape[1:], lambda i, j: (i, 0, 0))
    full = lambda a: pl.BlockSpec(a.shape, lambda i, j: (0,) * a.ndim)
    stat = [pltpu.VMEM((1, R), F32), pltpu.VMEM((1, R), F32), pltpu.VMEM((LANES, R), F32)]
    return pl.pallas_call(
        _nsa_attend_kernel,
        grid=(b, nq),
        in_specs=[cols(D_NSA), pl.BlockSpec((1, NSA_KV_GROUPS * MAX_SLC, T), lambda i, j: (i, 0, j)),
                  cols(D_NSA), cols(LANES), per_b(ks), per_b(kw), row_b(vs_t), row_b(vw_t),
                  full(bias_t), full(onehot)],
        out_specs=pl.BlockSpec((1, T, D_NSA), lambda i, j: (i, j, 0)),
        out_shape=jax.ShapeDtypeStruct((b, s, D_NSA), BF16),
        scratch_shapes=[pltpu.VMEM((s, 2 * LANES), BF16), pltpu.VMEM((2 * LANES, R), BF16)] + stat + stat,
        compiler_params=_params("parallel", "arbitrary"), name="nsa_attend",
    )(q_t, sel_t, ocmp_t, g_t, ks, kw, vs_t, vw_t, bias_t, onehot)


def _sb_kernel(qt_ref, k_ref, vt_ref, tri_ref, o_ref, carry_ref, acc_ref):
    T = T_SB
    j = pl.program_id(1)
    n_pair = SB_HEADS // 2
    key = lax.broadcasted_iota(jnp.int32, (T, 2 * T), 0)
    qry = lax.broadcasted_iota(jnp.int32, (T, 2 * T), 1) & (T - 1)
    strict = key < qry
    carry_ref[...] = jnp.zeros(carry_ref.shape, F32)
    acc_ref[...] = jnp.zeros(acc_ref.shape, F32)

    def tiles(cs, masks):
        starts = [pl.multiple_of(c * T, T) for c in cs]
        units = [(ci, p) for ci in range(len(cs)) for p in range(n_pair)]
        z2s, incls = [], []
        for c, p in units:
            q_even = qt_ref[HEAD_DIM * (2 * p):HEAD_DIM * (2 * p + 1), :]
            q_odd = qt_ref[HEAD_DIM * (2 * p + 1):HEAD_DIM * (2 * p + 2), :]
            zq = jnp.zeros_like(q_even)
            q_pair = jnp.concatenate([jnp.concatenate([q_even, zq], axis=0),
                                      jnp.concatenate([zq, q_odd], axis=0)], axis=1)
            z2s.append(_dot(k_ref[0, pl.ds(starts[c], T), LANES * p:LANES * (p + 1)], q_pair))
        for (c, p), z2 in zip(units, z2s):
            rest = jnp.maximum(z2, 0.0) + jnp.log2(1.0 + jnp.exp2(-jnp.abs(z2)))
            if masks[c] is not None:
                rest = jnp.where(masks[c], rest, 0.0)
            hi, lo = _split_bf16(rest)
            incls.append(_dot(tri_ref[...], jnp.concatenate([hi, lo], axis=0)))
        carry = [carry_ref[p] for p in range(n_pair)]
        acc = [acc_ref[p] for p in range(n_pair)]
        for (c, p), z2, incl in zip(units, z2s, incls):
            a = jnp.exp2(z2 - incl - carry[p])
            if masks[c] is not None:
                a = jnp.where(masks[c], a, 0.0)
            acc[p] = acc[p] + _dot(vt_ref[LANES * p:LANES * (p + 1), pl.ds(starts[c], T)], a.astype(BF16))
            carry[p] = carry[p] + incl[0:1, :]
        for p in range(n_pair):
            carry_ref[p] = carry[p]
            acc_ref[p] = acc[p]

    @pl.when(j == 0)
    def _():
        tiles([0], [strict])

    @pl.when(j > 0)
    def _():
        tiles([j, j - 1], [strict, None])

    def live(c):
        return (c >= 0) & (jnp.min(carry_ref[...]) < SB_DEAD_BITS)

    def back_one(c):
        tiles([c], [None])
        return c - 1

    lax.while_loop(live, back_one, j - 2)
    row = lax.broadcasted_iota(jnp.int32, (LANES, T), 0)
    for p in range(n_pair):
        acc = acc_ref[p]
        o_ref[0, :, LANES * p:LANES * (p + 1)] = (
            jnp.where(row < HEAD_DIM, acc[:, 0:T], acc[:, T:2 * T]).T.astype(o_ref.dtype))


def _sb_attention(q_t, k, v_t, tri_t):
    b, s, _ = k.shape
    T = T_SB
    nq = s // T
    n_pair = SB_HEADS // 2
    return pl.pallas_call(
        _sb_kernel,
        grid=(b, nq),
        in_specs=[pl.BlockSpec((D_SB, T), lambda i, j: (0, i * nq + j)),
                  pl.BlockSpec((1, s, D_SB), lambda i, j: (i, 0, 0)),
                  pl.BlockSpec((D_SB, s), lambda i, j: (0, i)),
                  pl.BlockSpec(tri_t.shape, lambda i, j: (0, 0))],
        out_specs=pl.BlockSpec((1, T, D_SB), lambda i, j: (i, j, 0)),
        out_shape=jax.ShapeDtypeStruct((b, s, D_SB), BF16),
        scratch_shapes=[pltpu.VMEM((n_pair, 1, 2 * T), F32), pltpu.VMEM((n_pair, LANES, 2 * T), F32)],
        compiler_params=_params("parallel", "parallel"), name="sb_attention",
    )(q_t, k, v_t, tri_t)


def _hgrn_kernel(q_ref, f_ref, i_ref, lb_ref, nw_ref, ones_ref, bd_ref, o_ref,
                 st_ref, qj_ref, kj_ref, fj_ref, vj_ref, qd_ref, kd_ref, oc_ref, x_ref, u_ref, sb_ref, dec_ref,
                 att_ref, oall_ref):
    C = HG_SUB
    n_blk = T_HG // C

    @pl.when(pl.program_id(2) == 0)
    def _():
        st_ref[...] = jnp.zeros(st_ref.shape, F32)

    lb = lb_ref[...]
    lb_floor = jnp.maximum(lb, LB_FLOOR)
    one_m_lb = 1.0 - lb
    cum = None
    for j in range(C):
        xj = f_ref[0, pl.ds(j, n_blk, stride=C), :]
        e = jnp.exp(-jnp.abs(xj))
        r = 1.0 / (1.0 + e)
        er = e * r
        pos = xj >= 0.0
        fj = lb_floor + one_m_lb * jnp.where(pos, r, er)
        kj_ref[j] = one_m_lb * jnp.where(pos, er, r)
        fj_ref[j] = fj
        cum = fj if cum is None else cum * fj
        qj = q_ref[0, pl.ds(j, n_blk, stride=C), :]
        qj_ref[j] = qj
        vj_ref[j] = i_ref[0, pl.ds(j, n_blk, stride=C), :]
        qd_ref[pl.ds(j, n_blk, stride=HG_PITCH), :] = qj * cum
    dec_ref[...] = cum
    tail = None
    for j in reversed(range(C)):
        kd_ref[pl.ds(j, n_blk, stride=HG_PITCH), :] = kj_ref[j] if tail is None else kj_ref[j] * tail
        tail = fj_ref[j] if tail is None else tail * fj_ref[j]

    def kv_products(blk, carry):
        kd = kd_ref[pl.ds(pl.multiple_of(blk * HG_PITCH, 8), C), :].astype(BF16)
        vv = i_ref[0, pl.ds(pl.multiple_of(blk * C, C), C), :].astype(BF16)
        u_ref[blk] = bd_ref[...] * _dot_tn(vv, kd)
        return carry

    lax.fori_loop(0, n_blk, kv_products, 0, unroll=HG_UNROLL)

    def scan(blk, st):
        sb_ref[blk] = st.astype(BF16)
        return st * dec_ref[pl.ds(blk, 1), :] + u_ref[blk]

    st_ref[...] = lax.fori_loop(0, n_blk, scan, st_ref[...], unroll=HG_UNROLL)

    def outputs(blk, carry):
        r0 = pl.multiple_of(blk * HG_PITCH, 8)
        oc_ref[pl.ds(r0, C), :] = _dot_nt(qd_ref[pl.ds(r0, C), :].astype(BF16), sb_ref[blk])
        return carry

    lax.fori_loop(0, n_blk, outputs, 0, unroll=HG_UNROLL)

    base = lambda j: (j * (j + 1) // 2) * n_blk
    for j in range(C):
        qdec = qj_ref[j]
        for jp in reversed(range(j + 1)):
            x_ref[base(j) + jp * n_blk:base(j) + (jp + 1) * n_blk, :] = (qdec * kj_ref[jp]).astype(BF16)
            if jp > 0:
                qdec = qdec * fj_ref[jp]
        att_ref[base(j):base(j + 1), :] = _dot(x_ref[base(j):base(j + 1), :], ones_ref[...])
    for j in range(C):
        oj = oc_ref[pl.ds(j, n_blk, stride=HG_PITCH), :]
        for jp in range(j + 1):
            oj = oj + att_ref[base(j) + jp * n_blk:base(j) + (jp + 1) * n_blk, :] * vj_ref[jp]
        oall_ref[j * n_blk:(j + 1) * n_blk, :] = oj
    o_all = oall_ref[...]
    hi, lo = _split_bf16(o_all * o_all)
    ms = (_dot(hi, ones_ref[...]) + _dot(lo, ones_ref[...])) * (1.0 / HEAD_DIM)
    oall_ref[...] = o_all * lax.rsqrt(ms + RMS_EPS) * nw_ref[...]
    for j in range(C):
        o_ref[0, pl.ds(j, n_blk, stride=C), :] = oall_ref[j * n_blk:(j + 1) * n_blk, :]


def _hgrn(q, f, i, lb, nw, ones_bd, bd_mask):
    b, s, _ = q.shape
    n_blk = T_HG // HG_SUB
    tile = pl.BlockSpec((1, T_HG, LANES), lambda bi, pi, ti: (bi, ti, pi))
    vec = pl.BlockSpec((1, LANES), lambda bi, pi, ti: (0, pi))
    full = lambda a: pl.BlockSpec(a.shape, lambda bi, pi, ti: (0,) * a.ndim)
    jm = pltpu.VMEM((HG_SUB, n_blk, LANES), F32)
    nat = pltpu.VMEM((n_blk * HG_PITCH, LANES), F32)
    return pl.pallas_call(
        _hgrn_kernel,
        grid=(b, D_HG // LANES, s // T_HG),
        in_specs=[tile, tile, tile, vec, vec, full(ones_bd), full(bd_mask)],
        out_specs=tile,
        out_shape=jax.ShapeDtypeStruct((b, s, D_HG), F32),
        scratch_shapes=[pltpu.VMEM((LANES, LANES), F32), jm, jm, jm, jm, nat, nat, nat,
                        pltpu.VMEM((HG_SUB * (HG_SUB + 1) // 2 * n_blk, LANES), BF16),
                        pltpu.VMEM((n_blk, LANES, LANES), F32),
                        pltpu.VMEM((n_blk, LANES, LANES), BF16), pltpu.VMEM((n_blk, LANES), F32),
                        pltpu.VMEM((HG_SUB * (HG_SUB + 1) // 2 * n_blk, LANES), F32), pltpu.VMEM((T_HG, LANES), F32)],
        compiler_params=_params("parallel", "parallel", "arbitrary"), name="hgrn2",
    )(q, f, i, lb, nw, ones_bd, bd_mask)


def _out_kernel(onsa_ref, osb_ref, ohg_ref, z_ref, x_ref, w_ref, g_ref, b_ref, o_ref):
    z = z_ref[...].astype(F32)
    sz = z * jax.nn.sigmoid(z)
    mixed = jnp.concatenate([(onsa_ref[...] * sz[:, 0:D_NSA]).astype(BF16),
                             (osb_ref[...] * sz[:, D_NSA:D_NSA + D_SB]).astype(BF16),
                             (ohg_ref[...] * sz[:, D_NSA + D_SB:D_MIX]).astype(BF16)], axis=1)
    v = ALPHA * x_ref[...] + _dot(mixed, w_ref[...])
    mu = jnp.mean(v, axis=-1, keepdims=True)
    vc = v - mu
    var = jnp.mean(vc * vc, axis=-1, keepdims=True)
    o_ref[...] = vc * lax.rsqrt(var + LN_EPS) * g_ref[...] + b_ref[...]


def _out_proj(o_nsa, o_sb, o_hg, z_all, x2d, w, g, bvec):
    m = x2d.shape[0]
    rows = lambda n: pl.BlockSpec((T_PROJ, n), lambda i: (i, 0))
    full = lambda a: pl.BlockSpec(a.shape, lambda i: (0, 0))
    return pl.pallas_call(
        _out_kernel,
        grid=(m // T_PROJ,),
        in_specs=[rows(D_NSA), rows(D_SB), rows(D_HG), rows(D_MIX), rows(D_MODEL), full(w), full(g), full(bvec)],
        out_specs=rows(D_MODEL),
        out_shape=jax.ShapeDtypeStruct((m, D_MODEL), F32),
        compiler_params=_params("parallel"), name="out_proj_norm",
    )(o_nsa, o_sb, o_hg, z_all, x2d, w, g, bvec)


def _t5_bucket_np(rel):
    n = np.maximum(rel, 0)
    max_exact = NUM_BUCKETS // 2
    large = max_exact + (np.log(np.maximum(n, 1).astype(np.float32) / max_exact)
                         / math.log(MAX_DISTANCE / max_exact) * (NUM_BUCKETS - max_exact)).astype(np.int32)
    large = np.clip(large, 0, NUM_BUCKETS - 1)
    return np.where(n < max_exact, n, large).astype(np.int32)


def _bias_tables(rel_bias, s):
    tbl = ((rel_bias - rel_bias[NUM_BUCKETS - 1]) * LOG2E).astype(F32)

    def expand(rel):
        onehot = (jnp.arange(NUM_BUCKETS)[:, None] == jnp.asarray(_t5_bucket_np(rel).reshape(1, -1))).astype(F32)
        return jnp.dot(tbl.T, onehot, precision=lax.Precision.HIGHEST).reshape((NSA_HEADS,) + rel.shape)

    n_cmp_pad = s // CMP_STRIDE
    n_rel = np.arange(2 * n_cmp_pad)[:, None] - n_cmp_pad
    r = np.arange(T_SEL)[None, :]
    pbias = expand(r - CMP_STRIDE * n_rel - (CMP_BLOCK - 1))
    T = T_NSA
    key = np.arange(T)[:, None]
    qry = np.arange(T)[None, :]
    near = np.stack([qry - key, T + qry - key])
    bias_t = jnp.transpose(expand(near), (1, 2, 0, 3)).reshape(2, T, NSA_HEADS * T)
    return pbias, bias_t


def _static_tables(s):
    n_chunk = s // CMP_STRIDE
    cmp_start = np.arange(n_chunk) * CMP_STRIDE
    slc_start = np.arange(MAX_SLC) * SLC_BLOCK
    ovl_t = ((cmp_start[None, :] < slc_start[:, None] + SLC_BLOCK)
             & (cmp_start[None, :] + CMP_BLOCK > slc_start[:, None])
             & (cmp_start[None, :] + CMP_BLOCK <= s)).astype(np.float32)
    onehot = (np.arange(s)[:, None] // SLC_BLOCK == np.arange(LANES)[None, :]).astype(np.float32)
    tri = (np.arange(T_SB)[None, :] >= np.arange(T_SB)[:, None]).astype(np.float32)
    tri = np.concatenate([tri, tri], axis=1)
    ones_bd = np.kron(np.eye(2), np.ones((HEAD_DIM, HEAD_DIM))).astype(np.float32)
    as_bf16 = lambda a: jnp.asarray(a, dtype=BF16)
    return as_bf16(ovl_t), as_bf16(onehot), as_bf16(tri), as_bf16(ones_bd), jnp.asarray(ones_bd)


def _layer_weights(w_in_l, cmp_pos_l, w_ck1_l, w_ck2_l, w_cv1_l, w_cv2_l):
    offs = np.cumsum((0,) + SPLIT_SIZES)
    (w_q, w_kc, w_vc, w_ks, w_vs, w_kw, w_vw, w_g, w_nz,
     w_sq, w_sk, w_sv, w_sz, w_hq, w_hf, w_hi, w_hz) = [w_in_l[:, offs[i]:offs[i + 1]] for i in range(len(SPLIT_SIZES))]
    w_gp = jnp.concatenate([w_g, jnp.zeros((D_MODEL, LANES - N_GATES), F32)], axis=1)
    w_z = jnp.concatenate([w_nz, w_sz, w_hz], axis=1)
    w_nat = [w_kc, w_vc, w_ks, w_kw, w_z, w_sk, w_hq, w_hf, w_hi]
    dt_nat = [F32, F32, BF16, BF16, BF16, BF16, F32, F32, F32]
    w_tr = [(w_q * QK_SCALE2).T, w_vs.T, w_vw.T, w_gp.T, (w_sq * QK_SCALE2).T, w_sv.T]
    dt_tr = [BF16, BF16, BF16, F32, BF16, BF16]
    w_nat = [w.astype(BF16) for w in w_nat]
    w_tr = [w.astype(BF16) for w in w_tr]

    def block_diag(w):
        z = jnp.zeros_like(w)
        return jnp.concatenate([jnp.concatenate([w, z], axis=-1), jnp.concatenate([z, w], axis=-1)], axis=-2)

    pos2 = jnp.concatenate([cmp_pos_l, cmp_pos_l], axis=1)
    w1k = block_diag(w_ck1_l.reshape(CMP_BLOCK, HEAD_DIM, CMP_HIDDEN)).astype(BF16)
    w1v = block_diag(w_cv1_l.reshape(CMP_BLOCK, HEAD_DIM, CMP_HIDDEN)).astype(BF16)
    w2k = block_diag(w_ck2_l).astype(BF16)
    w2v_t = block_diag(w_cv2_l).T.astype(BF16)
    return (w_nat, dt_nat, w_tr, dt_tr), (pos2, w1k, w2k, w1v, w2v_t)


def kernel(x, w_in, cmp_pos, w_ck1, w_ck2, w_cv1, w_cv2, hg_lb, hg_norm_w, w_out, ln_g, ln_b, rel_bias):
    b, s, d = x.shape
    assert d == D_MODEL and s % T_HG == 0 and s // SLC_BLOCK <= MAX_SLC and s >= WINDOW + T_NSA
    lb_w = jax.nn.softmax(hg_lb.astype(F32), axis=0)
    lb_all = jnp.cumsum(lb_w, axis=0) - lb_w[0]
    pbias, bias_t = _bias_tables(rel_bias, s)
    ovl_t, onehot, tri, ones_bd, bd_mask = _static_tables(s)

    x2d = x.reshape(b * s, d)
    for l in range(DEPTH):
        proj_w, cmp_w = _layer_weights(w_in[l], cmp_pos[l], w_ck1[l], w_ck2[l], w_cv1[l], w_cv2[l])
        nat, (q_t, vs_t, vw_t, g_t, sq_t, sv_t) = _inproj(x2d, *proj_w)
        kc_src, vc_src, ks, kw, z_all, sk, hq, hf, hi = [o.reshape(b, s, o.shape[-1]) for o in nat]
        kc, vc_t = _compress(kc_src, vc_src, *cmp_w)
        ocmp_t, sel_t = _nsa_select(q_t, kc, vc_t, pbias, ovl_t)
        o_nsa = _nsa_attend(q_t, sel_t, ocmp_t, g_t, ks, kw, vs_t, vw_t, bias_t, onehot)
        o_sb = _sb_attention(sq_t, sk, sv_t, tri)
        o_hg = _hgrn(hq, hf, hi, lb_all[l][None, :], hg_norm_w[l][None, :], ones_bd, bd_mask)
        x2d = _out_proj(o_nsa.reshape(b * s, D_NSA), o_sb.reshape(b * s, D_SB), o_hg.reshape(b * s, D_HG),
                        z_all.reshape(b * s, D_MIX), x2d, w_out[l].astype(BF16),
                        ln_g[l][None, :], ln_b[l][None, :])
    return x2d.reshape(b, s, d)
```

```python
import functools
import math

import numpy as np
import jax
import jax.numpy as jnp
from jax import lax
from jax.experimental import pallas as pl
from jax.experimental.pallas import tpu as pltpu

F32 = jnp.float32
BF16 = jnp.bfloat16

D_MODEL = 1024
DEPTH = 2
HEAD_DIM = 64
LANES = 128
NSA_HEADS = 6
NSA_KV_GROUPS = 2
NSA_HPG = NSA_HEADS // NSA_KV_GROUPS
CMP_BLOCK = 32
CMP_STRIDE = 16
CMP_HIDDEN = 2 * HEAD_DIM
SLC_BLOCK = 64
SLC_TOPN = 16
MAX_SLC = 64
WINDOW = 512
FORCE_BONUS = 1000.0
NEG_BIG = -1e30
LB_FLOOR = 1e-30
SB_HEADS = 4
HG_HEADS = 6
HG_SUB = 16
HG_PITCH = 24
HG_UNROLL = 64
NUM_BUCKETS = 32
MAX_DISTANCE = 128
D_NSA = NSA_HEADS * HEAD_DIM
D_KV = NSA_KV_GROUPS * HEAD_DIM
D_SB = SB_HEADS * HEAD_DIM
D_HG = HG_HEADS * HEAD_DIM
D_MIX = D_NSA + D_SB + D_HG
N_GATES = NSA_HEADS * 3
SPLIT_SIZES = (D_NSA, D_KV, D_KV, D_KV, D_KV, D_KV, D_KV, N_GATES, D_NSA,
               D_SB, D_SB, D_SB, D_SB, D_HG, D_HG, D_HG, D_HG)
ALPHA = (2 * DEPTH) ** 0.25
LN_EPS = 1e-5
RMS_EPS = 1e-6
LOG2E = math.log2(math.e)
QK_SCALE2 = LOG2E / math.sqrt(HEAD_DIM)

T_SEL = 128
T_NSA = 256
FAR_KEYS = 512
T_SB = 256
SB_DEAD_BITS = 151.0
T_HG = 1024
T_PROJ = 512
VMEM_LIMIT = 56 * 1024 * 1024

_NT = (((1,), (1,)), ((), ()))
_TN = (((0,), (0,)), ((), ()))


def _dot(a, b):
    return jnp.dot(a, b, preferred_element_type=F32)


def _dot_nt(a, b):
    return lax.dot_general(a, b, _NT, preferred_element_type=F32)


def _dot_tn(a, b):
    return lax.dot_general(a, b, _TN, preferred_element_type=F32)


def _split_bf16(x):
    hi = x.astype(BF16)
    lo = (x - hi.astype(F32)).astype(BF16)
    return hi, lo


def _params(*sem, flags=None):
    return pltpu.CompilerParams(dimension_semantics=sem, vmem_limit_bytes=VMEM_LIMIT, flags=flags)


def _project(x, wn_ref, wt_ref, o_refs, n_nat):
    xb = x.astype(BF16)
    nat = _dot(xb, wn_ref[...])
    off = 0
    for o_ref in o_refs[:n_nat]:
        o_ref[...] = nat[:, off:off + o_ref.shape[1]].astype(o_ref.dtype)
        off += o_ref.shape[1]
    tr = _dot_nt(wt_ref[...], xb)
    off = 0
    for o_ref in o_refs[n_nat:]:
        o_ref[...] = tr[off:off + o_ref.shape[0], :].astype(o_ref.dtype)
        off += o_ref.shape[0]


def _inproj_kernel(x_ref, wn_ref, wt_ref, *o_refs, n_nat):
    _project(x_ref[...], wn_ref, wt_ref, o_refs, n_nat)


def _proj_specs(m, w_nat, dt_nat, w_tr, dt_tr):
    wn = jnp.concatenate(w_nat, axis=1)
    wt = jnp.concatenate(w_tr, axis=0)
    w_specs = [pl.BlockSpec(wn.shape, lambda i: (0, 0)), pl.BlockSpec(wt.shape, lambda i: (0, 0))]
    out_specs = [pl.BlockSpec((T_PROJ, w.shape[1]), lambda i: (i, 0)) for w in w_nat]
    out_specs += [pl.BlockSpec((w.shape[0], T_PROJ), lambda i: (0, i)) for w in w_tr]
    out_shape = [jax.ShapeDtypeStruct((m, w.shape[1]), dt) for w, dt in zip(w_nat, dt_nat)]
    out_shape += [jax.ShapeDtypeStruct((w.shape[0], m), dt) for w, dt in zip(w_tr, dt_tr)]
    return (wn, wt), w_specs, out_specs, out_shape


def _inproj(x2d, w_nat, dt_nat, w_tr, dt_tr):
    m = x2d.shape[0]
    w, w_specs, out_specs, out_shape = _proj_specs(m, w_nat, dt_nat, w_tr, dt_tr)
    outs = pl.pallas_call(
        functools.partial(_inproj_kernel, n_nat=len(w_nat)),
        grid=(m // T_PROJ,),
        in_specs=[pl.BlockSpec((T_PROJ, D_MODEL), lambda i: (i, 0))] + w_specs,
        out_specs=out_specs, out_shape=out_shape,
        compiler_params=_params("parallel"), name="inproj",
    )(x2d, *w)
    return outs[:len(w_nat)], outs[len(w_nat):]


def _compress_kernel(ksrc_ref, vsrc_ref, pos_ref, w1k_ref, w2k_ref, w1v_ref, w2v_ref,
                     kc_ref, vc_ref, *, n_chunk):
    def hidden(src_ref, w1_ref):
        top = jnp.zeros((n_chunk, 2 * CMP_HIDDEN), F32)
        bot = jnp.zeros((n_chunk, 2 * CMP_HIDDEN), F32)
        for p in range(CMP_STRIDE):
            xp = src_ref[0, pl.ds(p, n_chunk, stride=CMP_STRIDE), :]
            top += _dot((xp + pos_ref[p:p + 1, :]).astype(BF16), w1_ref[p])
            q = CMP_STRIDE + p
            bot += _dot((xp + pos_ref[q:q + 1, :]).astype(BF16), w1_ref[q])
        hid = top + pltpu.roll(bot, n_chunk - 1, 0)
        return jax.nn.gelu(hid).astype(BF16)

    kc_ref[0] = _dot(hidden(ksrc_ref, w1k_ref), w2k_ref[...]).astype(kc_ref.dtype)
    vc_ref[0] = _dot_nt(w2v_ref[...], hidden(vsrc_ref, w1v_ref)).astype(vc_ref.dtype)


def _compress(kc_src, vc_src, pos2, w1k, w2k, w1v, w2v_t):
    b, s, _ = kc_src.shape
    n_chunk = s // CMP_STRIDE
    full = lambda a: pl.BlockSpec(a.shape, lambda i: (0,) * a.ndim)
    src = pl.BlockSpec((1, s, D_KV), lambda i: (i, 0, 0))
    return pl.pallas_call(
        functools.partial(_compress_kernel, n_chunk=n_chunk),
        grid=(b,),
        in_specs=[src, src, full(pos2), full(w1k), full(w2k), full(w1v), full(w2v_t)],
        out_specs=[pl.BlockSpec((1, n_chunk, D_KV), lambda i: (i, 0, 0)),
                   pl.BlockSpec((1, D_KV, n_chunk), lambda i: (i, 0, 0))],
        out_shape=[jax.ShapeDtypeStruct((b, n_chunk, D_KV), BF16),
                   jax.ShapeDtypeStruct((b, D_KV, n_chunk), BF16)],
        compiler_params=_params("parallel"), name="nsa_compress",
    )(kc_src, vc_src, pos2, w1k, w2k, w1v, w2v_t)


def _nsa_select_kernel(qt_ref, kc_ref, vct_ref, pbias_ref, ovl_ref, ocmp_ref, sel_ref, *, n_cmp_pad):
    T = T_SEL
    j = pl.program_id(1)
    t0 = j * T
    n_idx = lax.broadcasted_iota(jnp.int32, (n_cmp_pad, T), 0)
    tok_c = t0 + lax.broadcasted_iota(jnp.int32, (n_cmp_pad, T), 1)
    mask_c = tok_c >= CMP_STRIDE * n_idx + (CMP_BLOCK - 1)
    off = pl.multiple_of(n_cmp_pad - (T // CMP_STRIDE) * j, 8)
    psums = [None] * NSA_KV_GROUPS
    for h in range(NSA_HEADS):
        g = h // NSA_HPG
        qh = qt_ref[HEAD_DIM * h:HEAD_DIM * (h + 1), :]
        zq = jnp.zeros_like(qh)
        qh = jnp.concatenate([qh, zq] if g == 0 else [zq, qh], axis=0)
        s = _dot(kc_ref[0], qh) + pbias_ref[h, pl.ds(off, n_cmp_pad), :]
        s = jnp.where(mask_c, s, NEG_BIG)
        m = jnp.max(s, axis=0, keepdims=True)
        p = jnp.where(mask_c, jnp.exp2(s - m), 0.0)
        l = jnp.sum(p, axis=0, keepdims=True)
        p = p / jnp.where(l > 0.0, l, 1.0)
        o_both = _dot(vct_ref[0], p.astype(BF16))
        ocmp_ref[HEAD_DIM * h:HEAD_DIM * (h + 1), :] = o_both[HEAD_DIM * g:HEAD_DIM * (g + 1)]
        psums[g] = p if psums[g] is None else psums[g] + p

    jblk = lax.broadcasted_iota(jnp.int32, (MAX_SLC, T), 0)
    tok = t0 + lax.broadcasted_iota(jnp.int32, (MAX_SLC, T), 1)
    cur = lax.shift_right_logical(tok, 6)
    forced = (jblk == 0) | (jblk == cur) | (jblk == cur - 1)
    valid = jblk * SLC_BLOCK <= tok
    jsub = lax.broadcasted_iota(jnp.int32, (8, T), 0)
    for g in range(NSA_KV_GROUPS):
        imp = _dot(ovl_ref[...], psums[g].astype(BF16))
        score = jnp.where(valid, imp + jnp.where(forced, FORCE_BONUS, 0.0), NEG_BIG)
        blocks = [score[8 * rb:8 * rb + 8] for rb in range(MAX_SLC // 8)]
        ranks = [jnp.zeros((8, T), F32) for _ in blocks]
        for jp in range(MAX_SLC):
            other = score[jp:jp + 1, :]
            for rb, blk in enumerate(blocks):
                ge = jnp.where(other >= blk, 1.0, 0.0)
                gt = jnp.where(other > blk, 1.0, 0.0)
                if 8 * rb > jp:
                    inc = ge
                elif 8 * rb + 7 < jp:
                    inc = gt
                else:
                    inc = jnp.where(jsub > jp - 8 * rb, ge, gt)
                ranks[rb] = ranks[rb] + inc
        rank = jnp.concatenate(ranks, axis=0)
        sel_ref[0, MAX_SLC * g:MAX_SLC * (g + 1), :] = jnp.where(rank < float(SLC_TOPN), 0.0, NEG_BIG).astype(BF16)


def _nsa_select(q_t, kc, vc_t, pbias, ovl_t):
    b, n_cmp_pad, _ = kc.shape
    m = q_t.shape[1]
    nq = m // b // T_SEL
    cols = lambda r: pl.BlockSpec((r, T_SEL), lambda i, j: (0, i * nq + j))
    per_b = lambda a: pl.BlockSpec((1,) + a.shape[1:], lambda i, j: (i, 0, 0))
    full = lambda a: pl.BlockSpec(a.shape, lambda i, j: (0,) * a.ndim)
    return pl.pallas_call(
        functools.partial(_nsa_select_kernel, n_cmp_pad=n_cmp_pad),
        grid=(b, nq),
        in_specs=[cols(D_NSA), per_b(kc), per_b(vc_t), full(pbias), full(ovl_t)],
        out_specs=[cols(D_NSA), pl.BlockSpec((1, NSA_KV_GROUPS * MAX_SLC, T_SEL), lambda i, j: (i, 0, j))],
        out_shape=[jax.ShapeDtypeStruct((D_NSA, m), F32),
                   jax.ShapeDtypeStruct((b, NSA_KV_GROUPS * MAX_SLC, m // b), BF16)],
        compiler_params=_params("parallel", "parallel"), name="nsa_select",
    )(q_t, kc, vc_t, pbias, ovl_t)


def _online_init(m_ref, l_ref, acc_ref):
    m_ref[...] = jnp.full(m_ref.shape, NEG_BIG, F32)
    l_ref[...] = jnp.zeros(l_ref.shape, F32)
    acc_ref[...] = jnp.zeros(acc_ref.shape, F32)


def _scores(k_tile, q_ref, q_rows, bias_ref=None, ok=None):
    s = _dot(k_tile, q_ref[0:q_rows, :])
    if bias_ref is not None:
        s = s + bias_ref[...]
    if ok is not None:
        s = jnp.where(ok, s, NEG_BIG)
    return s


def _online_step(m_ref, l_ref, acc_ref, s, v_t):
    m = m_ref[...]
    m_new = jnp.maximum(m, jnp.max(s, axis=0, keepdims=True))
    alpha = jnp.exp2(m - m_new)
    p = jnp.exp2(s - m_new)
    m_ref[...] = m_new
    l_ref[...] = alpha * l_ref[...] + jnp.sum(p, axis=0, keepdims=True)
    acc_ref[...] = alpha * acc_ref[...] + _dot(v_t, p.astype(BF16))


def _nsa_attend_kernel(qt_ref, sel_ref, ocmp_ref, gt_ref, ks_ref, kw_ref, vs_ref, vw_ref, bias_ref, onehot_ref,
                       o_ref, kaug_ref, qaug_ref, ms_ref, ls_ref, accs_ref, mw_ref, lw_ref, accw_ref):
    T = T_NSA
    j = pl.program_id(1)

    @pl.when(j == 0)
    def _():
        kaug_ref[:, 0:LANES] = ks_ref[0]
        kaug_ref[:, LANES:2 * LANES] = onehot_ref[...]
        qaug_ref[...] = jnp.zeros(qaug_ref.shape, BF16)

    for h in range(NSA_HEADS):
        g = h // NSA_HPG
        qaug_ref[HEAD_DIM * g:HEAD_DIM * (g + 1), h * T:(h + 1) * T] = qt_ref[HEAD_DIM * h:HEAD_DIM * (h + 1), :]
        qaug_ref[LANES:LANES + MAX_SLC, h * T:(h + 1) * T] = sel_ref[0, MAX_SLC * g:MAX_SLC * (g + 1), :]

    key = lax.broadcasted_iota(jnp.int32, (T, NSA_HEADS * T), 0)
    qry = lax.broadcasted_iota(jnp.int32, (T, NSA_HEADS * T), 1) & (T - 1)
    causal = key <= qry
    st_s = (ms_ref, ls_ref, accs_ref)
    st_w = (mw_ref, lw_ref, accw_ref)
    _online_init(*st_s)
    _online_init(*st_w)

    far_len = jnp.maximum(j - 1, 0) * T

    def far_step(r0, n_keys):
        s = _scores(kaug_ref[pl.ds(r0, n_keys), :], qaug_ref, 2 * LANES)
        _online_step(*st_s, s, vs_ref[:, pl.ds(r0, n_keys)])

    def far_big(c, carry):
        r0 = pl.multiple_of(c * (2 * FAR_KEYS), 2 * FAR_KEYS)
        s_lo = _scores(kaug_ref[pl.ds(r0, FAR_KEYS), :], qaug_ref, 2 * LANES)
        s_hi = _scores(kaug_ref[pl.ds(r0 + FAR_KEYS, FAR_KEYS), :], qaug_ref, 2 * LANES)
        _online_step(*st_s, s_lo, vs_ref[:, pl.ds(r0, FAR_KEYS)])
        _online_step(*st_s, s_hi, vs_ref[:, pl.ds(r0 + FAR_KEYS, FAR_KEYS)])
        return carry

    n_big = lax.shift_right_logical(far_len, FAR_KEYS.bit_length())
    lax.fori_loop(0, n_big, far_big, 0)
    done = n_big * (2 * FAR_KEYS)

    @pl.when(far_len - done >= FAR_KEYS)
    def _():
        far_step(pl.multiple_of(done, FAR_KEYS), FAR_KEYS)

    done = done + jnp.where(far_len - done >= FAR_KEYS, FAR_KEYS, 0)

    def far_small(c, carry):
        far_step(pl.multiple_of(done + c * T, T), T)
        return carry

    lax.fori_loop(0, lax.shift_right_logical(far_len - done, T.bit_length() - 1), far_small, 0)

    n_win = WINDOW // T
    starts = [pl.multiple_of(jnp.maximum(j - d, 0) * T, T) for d in range(n_win + 1)]
    in_range = [key < jnp.where(j - d >= 0, T, -1) for d in range(n_win + 1)]
    sel_scores, win_scores = {}, {}
    for d in (1, 0):
        sel_scores[d] = _scores(kaug_ref[pl.ds(starts[d], T), :], qaug_ref, 2 * LANES, bias_ref.at[d],
                                causal if d == 0 else in_range[d])
    for d in range(n_win + 1):
        if d == 0:
            ok = causal
        elif d == n_win:
            ok = (key > qry) & in_range[d]
        else:
            ok = in_range[d]
        win_scores[d] = _scores(kw_ref[0, pl.ds(starts[d], T), :], qaug_ref, LANES,
                                bias_ref.at[d] if d <= 1 else None, ok)
    for d in (1, 0):
        _online_step(*st_s, sel_scores[d], vs_ref[:, pl.ds(starts[d], T)])
    for d in range(n_win + 1):
        _online_step(*st_w, win_scores[d], vw_ref[:, pl.ds(starts[d], T)])

    gates = jax.nn.sigmoid(gt_ref[...])
    heads = []
    for h in range(NSA_HEADS):
        cols = slice(h * T, (h + 1) * T)
        rows = slice(HEAD_DIM * (h // NSA_HPG), HEAD_DIM * (h // NSA_HPG + 1))
        heads.append(gates[3 * h:3 * h + 1, :] * ocmp_ref[HEAD_DIM * h:HEAD_DIM * (h + 1), :]
                     + (gates[3 * h + 1:3 * h + 2, :] / ls_ref[:, cols]) * accs_ref[rows, cols]
                     + (gates[3 * h + 2:3 * h + 3, :] / lw_ref[:, cols]) * accw_ref[rows, cols])
    o_ref[0] = jnp.concatenate(heads, axis=0).T.astype(o_ref.dtype)


def _nsa_attend(q_t, sel_t, ocmp_t, g_t, ks, kw, vs_t, vw_t, bias_t, onehot):
    b, s, _ = ks.shape
    T = T_NSA
    nq = s // T
    R = NSA_HEADS * T
    cols = lambda r: pl.BlockSpec((r, T), lambda i, j: (0, i * nq + j))
    row_b = lambda a: pl.BlockSpec((a.shape[0], s), lambda i, j: (0, i))
    per_b = lambda a: pl.BlockSpec((1,) + a.shape[1:], lambda i, j: (i, 0, 0))
    full = lambda a: pl.BlockSpec(a.shape, lambda i, j: (0,) * a.ndim)
    stat = [pltpu.VMEM((1, R), F32), pltpu.VMEM((1, R), F32), pltpu.VMEM((LANES, R), F32)]
    return pl.pallas_call(
        _nsa_attend_kernel,
        grid=(b, nq),
        in_specs=[cols(D_NSA), pl.BlockSpec((1, NSA_KV_GROUPS * MAX_SLC, T), lambda i, j: (i, 0, j)),
                  cols(D_NSA), cols(LANES), per_b(ks), per_b(kw), row_b(vs_t), row_b(vw_t),
                  full(bias_t), full(onehot)],
        out_specs=pl.BlockSpec((1, T, D_NSA), lambda i, j: (i, j, 0)),
        out_shape=jax.ShapeDtypeStruct((b, s, D_NSA), BF16),
        scratch_shapes=[pltpu.VMEM((s, 2 * LANES), BF16), pltpu.VMEM((2 * LANES, R), BF16)] + stat + stat,
        compiler_params=_params("parallel", "arbitrary"), name="nsa_attend",
    )(q_t, sel_t, ocmp_t, g_t, ks, kw, vs_t, vw_t, bias_t, onehot)


def _sb_kernel(qt_ref, k_ref, vt_ref, tri_ref, o_ref, carry_ref, acc_ref):
    T = T_SB
    j = pl.program_id(1)
    n_pair = SB_HEADS // 2
    key = lax.broadcasted_iota(jnp.int32, (T, 2 * T), 0)
    qry = lax.broadcasted_iota(jnp.int32, (T, 2 * T), 1) & (T - 1)
    strict = key < qry
    carry_ref[...] = jnp.zeros(carry_ref.shape, F32)
    acc_ref[...] = jnp.zeros(acc_ref.shape, F32)

    def tiles(cs, masks):
        starts = [pl.multiple_of(c * T, T) for c in cs]
        units = [(ci, p) for ci in range(len(cs)) for p in range(n_pair)]
        z2s, incls = [], []
        for c, p in units:
            q_even = qt_ref[HEAD_DIM * (2 * p):HEAD_DIM * (2 * p + 1), :]
            q_odd = qt_ref[HEAD_DIM * (2 * p + 1):HEAD_DIM * (2 * p + 2), :]
            zq = jnp.zeros_like(q_even)
            q_pair = jnp.concatenate([jnp.concatenate([q_even, zq], axis=0),
                                      jnp.concatenate([zq, q_odd], axis=0)], axis=1)
            z2s.append(_dot(k_ref[0, pl.ds(starts[c], T), LANES * p:LANES * (p + 1)], q_pair))
        for (c, p), z2 in zip(units, z2s):
            rest = jnp.maximum(z2, 0.0) + jnp.log2(1.0 + jnp.exp2(-jnp.abs(z2)))
            if masks[c] is not None:
                rest = jnp.where(masks[c], rest, 0.0)
            hi, lo = _split_bf16(rest)
            incls.append(_dot(tri_ref[...], jnp.concatenate([hi, lo], axis=0)))
        carry = [carry_ref[p] for p in range(n_pair)]
        acc = [acc_ref[p] for p in range(n_pair)]
        for (c, p), z2, incl in zip(units, z2s, incls):
            a = jnp.exp2(z2 - incl - carry[p])
            if masks[c] is not None:
                a = jnp.where(masks[c], a, 0.0)
            acc[p] = acc[p] + _dot(vt_ref[LANES * p:LANES * (p + 1), pl.ds(starts[c], T)], a.astype(BF16))
            carry[p] = carry[p] + incl[0:1, :]
        for p in range(n_pair):
            carry_ref[p] = carry[p]
            acc_ref[p] = acc[p]

    @pl.when(j == 0)
    def _():
        tiles([0], [strict])

    @pl.when(j > 0)
    def _():
        tiles([j, j - 1], [strict, None])

    def live(c):
        return (c >= 0) & (jnp.min(carry_ref[...]) < SB_DEAD_BITS)

    def back_one(c):
        tiles([c], [None])
        return c - 1

    lax.while_loop(live, back_one, j - 2)
    row = lax.broadcasted_iota(jnp.int32, (LANES, T), 0)
    for p in range(n_pair):
        acc = acc_ref[p]
        o_ref[0, :, LANES * p:LANES * (p + 1)] = (
            jnp.where(row < HEAD_DIM, acc[:, 0:T], acc[:, T:2 * T]).T.astype(o_ref.dtype))


def _sb_attention(q_t, k, v_t, tri_t):
    b, s, _ = k.shape
    T = T_SB
    nq = s // T
    n_pair = SB_HEADS // 2
    return pl.pallas_call(
        _sb_kernel,
        grid=(b, nq),
        in_specs=[pl.BlockSpec((D_SB, T), lambda i, j: (0, i * nq + j)),
                  pl.BlockSpec((1, s, D_SB), lambda i, j: (i, 0, 0)),
                  pl.BlockSpec((D_SB, s), lambda i, j: (0, i)),
                  pl.BlockSpec(tri_t.shape, lambda i, j: (0, 0))],
        out_specs=pl.BlockSpec((1, T, D_SB), lambda i, j: (i, j, 0)),
        out_shape=jax.ShapeDtypeStruct((b, s, D_SB), BF16),
        scratch_shapes=[pltpu.VMEM((n_pair, 1, 2 * T), F32), pltpu.VMEM((n_pair, LANES, 2 * T), F32)],
        compiler_params=_params("parallel", "parallel"), name="sb_attention",
    )(q_t, k, v_t, tri_t)


def _hgrn_kernel(q_ref, f_ref, i_ref, lb_ref, nw_ref, ones_ref, bd_ref, o_ref,
                 st_ref, qj_ref, kj_ref, fj_ref, vj_ref, qd_ref, kd_ref, oc_ref, x_ref, u_ref, sb_ref, dec_ref,
                 att_ref, oall_ref):
    C = HG_SUB
    n_blk = T_HG // C

    @pl.when(pl.program_id(2) == 0)
    def _():
        st_ref[...] = jnp.zeros(st_ref.shape, F32)

    lb = lb_ref[...]
    lb_floor = jnp.maximum(lb, LB_FLOOR)
    one_m_lb = 1.0 - lb
    cum = None
    for j in range(C):
        xj = f_ref[0, pl.ds(j, n_blk, stride=C), :]
        e = jnp.exp(-jnp.abs(xj))
        r = 1.0 / (1.0 + e)
        er = e * r
        pos = xj >= 0.0
        fj = lb_floor + one_m_lb * jnp.where(pos, r, er)
        kj_ref[j] = one_m_lb * jnp.where(pos, er, r)
        fj_ref[j] = fj
        cum = fj if cum is None else cum * fj
        qj = q_ref[0, pl.ds(j, n_blk, stride=C), :]
        qj_ref[j] = qj
        vj_ref[j] = i_ref[0, pl.ds(j, n_blk, stride=C), :]
        qd_ref[pl.ds(j, n_blk, stride=HG_PITCH), :] = qj * cum
    dec_ref[...] = cum
    tail = None
    for j in reversed(range(C)):
        kd_ref[pl.ds(j, n_blk, stride=HG_PITCH), :] = kj_ref[j] if tail is None else kj_ref[j] * tail
        tail = fj_ref[j] if tail is None else tail * fj_ref[j]

    def kv_products(blk, carry):
        kd = kd_ref[pl.ds(pl.multiple_of(blk * HG_PITCH, 8), C), :].astype(BF16)
        vv = i_ref[0, pl.ds(pl.multiple_of(blk * C, C), C), :].astype(BF16)
        u_ref[blk] = bd_ref[...] * _dot_tn(vv, kd)
        return carry

    lax.fori_loop(0, n_blk, kv_products, 0, unroll=HG_UNROLL)

    def scan(blk, st):
        sb_ref[blk] = st.astype(BF16)
        return st * dec_ref[pl.ds(blk, 1), :] + u_ref[blk]

    st_ref[...] = lax.fori_loop(0, n_blk, scan, st_ref[...], unroll=HG_UNROLL)

    def outputs(blk, carry):
        r0 = pl.multiple_of(blk * HG_PITCH, 8)
        oc_ref[pl.ds(r0, C), :] = _dot_nt(qd_ref[pl.ds(r0, C), :].astype(BF16), sb_ref[blk])
        return carry

    lax.fori_loop(0, n_blk, outputs, 0, unroll=HG_UNROLL)

    base = lambda j: (j * (j + 1) // 2) * n_blk
    for j in range(C):
        qdec = qj_ref[j]
        for jp in reversed(range(j + 1)):
            x_ref[base(j) + jp * n_blk:base(j) + (jp + 1) * n_blk, :] = (qdec * kj_ref[jp]).astype(BF16)
            if jp > 0:
                qdec = qdec * fj_ref[jp]
        att_ref[base(j):base(j + 1), :] = _dot(x_ref[base(j):base(j + 1), :], ones_ref[...])
    for j in range(C):
        oj = oc_ref[pl.ds(j, n_blk, stride=HG_PITCH), :]
        for jp in range(j + 1):
            oj = oj + att_ref[base(j) + jp * n_blk:base(j) + (jp + 1) * n_blk, :] * vj_ref[jp]
        oall_ref[j * n_blk:(j + 1) * n_blk, :] = oj
    o_all = oall_ref[...]
    hi, lo = _split_bf16(o_all * o_all)
    ms = (_dot(hi, ones_ref[...]) + _dot(lo, ones_ref[...])) * (1.0 / HEAD_DIM)
    oall_ref[...] = o_all * lax.rsqrt(ms + RMS_EPS) * nw_ref[...]
    for j in range(C):
        o_ref[0, pl.ds(j, n_blk, stride=C), :] = oall_ref[j * n_blk:(j + 1) * n_blk, :]


def _hgrn(q, f, i, lb, nw, ones_bd, bd_mask):
    b, s, _ = q.shape
    n_blk = T_HG // HG_SUB
    tile = pl.BlockSpec((1, T_HG, LANES), lambda bi, pi, ti: (bi, ti, pi))
    vec = pl.BlockSpec((1, LANES), lambda bi, pi, ti: (0, pi))
    full = lambda a: pl.BlockSpec(a.shape, lambda bi, pi, ti: (0,) * a.ndim)
    jm = pltpu.VMEM((HG_SUB, n_blk, LANES), F32)
    nat = pltpu.VMEM((n_blk * HG_PITCH, LANES), F32)
    return pl.pallas_call(
        _hgrn_kernel,
        grid=(b, D_HG // LANES, s // T_HG),
        in_specs=[tile, tile, tile, vec, vec, full(ones_bd), full(bd_mask)],
        out_specs=tile,
        out_shape=jax.ShapeDtypeStruct((b, s, D_HG), F32),
        scratch_shapes=[pltpu.VMEM((LANES, LANES), F32), jm, jm, jm, jm, nat, nat, nat,
                        pltpu.VMEM((HG_SUB * (HG_SUB + 1) // 2 * n_blk, LANES), BF16),
                        pltpu.VMEM((n_blk, LANES, LANES), F32),
                        pltpu.VMEM((n_blk, LANES, LANES), BF16), pltpu.VMEM((n_blk, LANES), F32),
                        pltpu.VMEM((HG_SUB * (HG_SUB + 1) // 2 * n_blk, LANES), F32), pltpu.VMEM((T_HG, LANES), F32)],
        compiler_params=_params("parallel", "parallel", "arbitrary"), name="hgrn2",
    )(q, f, i, lb, nw, ones_bd, bd_mask)


def _mix_norm(onsa_ref, osb_ref, ohg_ref, z_ref, x_ref, w_ref, g_ref, b_ref):
    z = z_ref[...].astype(F32)
    sz = z * jax.nn.sigmoid(z)
    mixed = jnp.concatenate([(onsa_ref[...] * sz[:, 0:D_NSA]).astype(BF16),
                             (osb_ref[...] * sz[:, D_NSA:D_NSA + D_SB]).astype(BF16),
                             (ohg_ref[...] * sz[:, D_NSA + D_SB:D_MIX]).astype(BF16)], axis=1)
    v = ALPHA * x_ref[...] + _dot(mixed, w_ref[...])
    mu = jnp.mean(v, axis=-1, keepdims=True)
    vc = v - mu
    var = jnp.mean(vc * vc, axis=-1, keepdims=True)
    return vc * lax.rsqrt(var + LN_EPS) * g_ref[...] + b_ref[...]


def _out_kernel(*refs):
    refs[8][...] = _mix_norm(*refs[:8])


def _out_in_kernel(*refs, n_nat):
    x_new = _mix_norm(*refs[:8])
    refs[10][...] = x_new
    _project(x_new, refs[8], refs[9], refs[11:], n_nat)


def _out_proj(o_nsa, o_sb, o_hg, z_all, x2d, w, g, bvec, next_proj=None):
    m = x2d.shape[0]
    rows = lambda n: pl.BlockSpec((T_PROJ, n), lambda i: (i, 0))
    full = lambda a: pl.BlockSpec(a.shape, lambda i: (0, 0))
    in_specs = [rows(D_NSA), rows(D_SB), rows(D_HG), rows(D_MIX), rows(D_MODEL), full(w), full(g), full(bvec)]
    x_shape = jax.ShapeDtypeStruct((m, D_MODEL), F32)
    if next_proj is None:
        return pl.pallas_call(
            _out_kernel, grid=(m // T_PROJ,), in_specs=in_specs, out_specs=rows(D_MODEL), out_shape=x_shape,
            compiler_params=_params("parallel"), name="out_proj_norm",
        )(o_nsa, o_sb, o_hg, z_all, x2d, w, g, bvec)
    w_nat = next_proj[0]
    wp, w_specs, out_specs, out_shape = _proj_specs(m, *next_proj)
    outs = pl.pallas_call(
        functools.partial(_out_in_kernel, n_nat=len(w_nat)), grid=(m // T_PROJ,),
        in_specs=in_specs + w_specs, out_specs=[rows(D_MODEL)] + out_specs, out_shape=[x_shape] + out_shape,
        compiler_params=_params("parallel"), name="out_proj_norm_inproj",
    )(o_nsa, o_sb, o_hg, z_all, x2d, w, g, bvec, *wp)
    return outs[0], outs[1:1 + len(w_nat)], outs[1 + len(w_nat):]


def _t5_bucket_np(rel):
    n = np.maximum(rel, 0)
    max_exact = NUM_BUCKETS // 2
    large = max_exact + (np.log(np.maximum(n, 1).astype(np.float32) / max_exact)
                         / math.log(MAX_DISTANCE / max_exact) * (NUM_BUCKETS - max_exact)).astype(np.int32)
    large = np.clip(large, 0, NUM_BUCKETS - 1)
    return np.where(n < max_exact, n, large).astype(np.int32)


def _bias_tables(rel_bias, s):
    tbl = ((rel_bias - rel_bias[NUM_BUCKETS - 1]) * LOG2E).astype(F32)

    def expand(rel):
        onehot = (jnp.arange(NUM_BUCKETS)[:, None] == jnp.asarray(_t5_bucket_np(rel).reshape(1, -1))).astype(F32)
        return jnp.dot(tbl.T, onehot, precision=lax.Precision.HIGHEST).reshape((NSA_HEADS,) + rel.shape)

    n_cmp_pad = s // CMP_STRIDE
    n_rel = np.arange(2 * n_cmp_pad)[:, None] - n_cmp_pad
    r = np.arange(T_SEL)[None, :]
    pbias = expand(r - CMP_STRIDE * n_rel - (CMP_BLOCK - 1))
    T = T_NSA
    key = np.arange(T)[:, None]
    qry = np.arange(T)[None, :]
    near = np.stack([qry - key, T + qry - key])
    bias_t = jnp.transpose(expand(near), (1, 2, 0, 3)).reshape(2, T, NSA_HEADS * T)
    return pbias, bias_t


def _static_tables(s):
    n_chunk = s // CMP_STRIDE
    cmp_start = np.arange(n_chunk) * CMP_STRIDE
    slc_start = np.arange(MAX_SLC) * SLC_BLOCK
    ovl_t = ((cmp_start[None, :] < slc_start[:, None] + SLC_BLOCK)
             & (cmp_start[None, :] + CMP_BLOCK > slc_start[:, None])
             & (cmp_start[None, :] + CMP_BLOCK <= s)).astype(np.float32)
    onehot = (np.arange(s)[:, None] // SLC_BLOCK == np.arange(LANES)[None, :]).astype(np.float32)
    tri = (np.arange(T_SB)[None, :] >= np.arange(T_SB)[:, None]).astype(np.float32)
    tri = np.concatenate([tri, tri], axis=1)
    ones_bd = np.kron(np.eye(2), np.ones((HEAD_DIM, HEAD_DIM))).astype(np.float32)
    as_bf16 = lambda a: jnp.asarray(a, dtype=BF16)
    return as_bf16(ovl_t), as_bf16(onehot), as_bf16(tri), as_bf16(ones_bd), jnp.asarray(ones_bd)


def _layer_weights(w_in_l, cmp_pos_l, w_ck1_l, w_ck2_l, w_cv1_l, w_cv2_l):
    offs = np.cumsum((0,) + SPLIT_SIZES)
    (w_q, w_kc, w_vc, w_ks, w_vs, w_kw, w_vw, w_g, w_nz,
     w_sq, w_sk, w_sv, w_sz, w_hq, w_hf, w_hi, w_hz) = [w_in_l[:, offs[i]:offs[i + 1]] for i in range(len(SPLIT_SIZES))]
    w_gp = jnp.concatenate([w_g, jnp.zeros((D_MODEL, LANES - N_GATES), F32)], axis=1)
    w_z = jnp.concatenate([w_nz, w_sz, w_hz], axis=1)
    w_nat = [w_kc, w_vc, w_ks, w_kw, w_z, w_sk, w_hq, w_hf, w_hi]
    dt_nat = [F32, F32, BF16, BF16, BF16, BF16, F32, F32, F32]
    w_tr = [(w_q * QK_SCALE2).T, w_vs.T, w_vw.T, w_gp.T, (w_sq * QK_SCALE2).T, w_sv.T]
    dt_tr = [BF16, BF16, BF16, F32, BF16, BF16]
    w_nat = [w.astype(BF16) for w in w_nat]
    w_tr = [w.astype(BF16) for w in w_tr]

    def block_diag(w):
        z = jnp.zeros_like(w)
        return jnp.concatenate([jnp.concatenate([w, z], axis=-1), jnp.concatenate([z, w], axis=-1)], axis=-2)

    pos2 = jnp.concatenate([cmp_pos_l, cmp_pos_l], axis=1)
    w1k = block_diag(w_ck1_l.reshape(CMP_BLOCK, HEAD_DIM, CMP_HIDDEN)).astype(BF16)
    w1v = block_diag(w_cv1_l.reshape(CMP_BLOCK, HEAD_DIM, CMP_HIDDEN)).astype(BF16)
    w2k = block_diag(w_ck2_l).astype(BF16)
    w2v_t = block_diag(w_cv2_l).T.astype(BF16)
    return (w_nat, dt_nat, w_tr, dt_tr), (pos2, w1k, w2k, w1v, w2v_t)


def kernel(x, w_in, cmp_pos, w_ck1, w_ck2, w_cv1, w_cv2, hg_lb, hg_norm_w, w_out, ln_g, ln_b, rel_bias):
    b, s, d = x.shape
    assert d == D_MODEL and s % T_HG == 0 and s // SLC_BLOCK <= MAX_SLC and s >= WINDOW + T_NSA
    lb_w = jax.nn.softmax(hg_lb.astype(F32), axis=0)
    lb_all = jnp.cumsum(lb_w, axis=0) - lb_w[0]
    pbias, bias_t = _bias_tables(rel_bias, s)
    ovl_t, onehot, tri, ones_bd, bd_mask = _static_tables(s)

    x2d = x.reshape(b * s, d)
    weights = [_layer_weights(w_in[l], cmp_pos[l], w_ck1[l], w_ck2[l], w_cv1[l], w_cv2[l]) for l in range(DEPTH)]
    nat, trs = _inproj(x2d, *weights[0][0])
    for l in range(DEPTH):
        q_t, vs_t, vw_t, g_t, sq_t, sv_t = trs
        kc_src, vc_src, ks, kw, z_all, sk, hq, hf, hi = [o.reshape(b, s, o.shape[-1]) for o in nat]
        kc, vc_t = _compress(kc_src, vc_src, *weights[l][1])
        ocmp_t, sel_t = _nsa_select(q_t, kc, vc_t, pbias, ovl_t)
        o_nsa = _nsa_attend(q_t, sel_t, ocmp_t, g_t, ks, kw, vs_t, vw_t, bias_t, onehot)
        o_sb = _sb_attention(sq_t, sk, sv_t, tri)
        o_hg = _hgrn(hq, hf, hi, lb_all[l][None, :], hg_norm_w[l][None, :], ones_bd, bd_mask)
        res = _out_proj(o_nsa.reshape(b * s, D_NSA), o_sb.reshape(b * s, D_SB), o_hg.reshape(b * s, D_HG),
                        z_all.reshape(b * s, D_MIX), x2d, w_out[l].astype(BF16),
                        ln_g[l][None, :], ln_b[l][None, :],
                        next_proj=weights[l + 1][0] if l + 1 < DEPTH else None)
        if l + 1 < DEPTH:
            x2d, nat, trs = res
        else:
            x2d = res
    return x2d.reshape(b, s, d)
```

```python
import functools
import math

import numpy as np
import jax
import jax.numpy as jnp
from jax import lax
from jax.experimental import pallas as pl
from jax.experimental.pallas import tpu as pltpu

F32 = jnp.float32
BF16 = jnp.bfloat16

D_MODEL = 1024
DEPTH = 2
HEAD_DIM = 64
LANES = 128
NSA_HEADS = 6
NSA_KV_GROUPS = 2
NSA_HPG = NSA_HEADS // NSA_KV_GROUPS
CMP_BLOCK = 32
CMP_STRIDE = 16
CMP_HIDDEN = 2 * HEAD_DIM
SLC_BLOCK = 64
SLC_TOPN = 16
MAX_SLC = 64
WINDOW = 512
FORCE_BONUS = 1000.0
NEG_BIG = -1e30
LB_FLOOR = 1e-30
SB_HEADS = 4
HG_HEADS = 6
HG_SUB = 16
HG_PITCH = 24
NUM_BUCKETS = 32
MAX_DISTANCE = 128
D_NSA = NSA_HEADS * HEAD_DIM
D_KV = NSA_KV_GROUPS * HEAD_DIM
D_SB = SB_HEADS * HEAD_DIM
D_HG = HG_HEADS * HEAD_DIM
D_MIX = D_NSA + D_SB + D_HG
N_GATES = NSA_HEADS * 3
SPLIT_SIZES = (D_NSA, D_KV, D_KV, D_KV, D_KV, D_KV, D_KV, N_GATES, D_NSA,
               D_SB, D_SB, D_SB, D_SB, D_HG, D_HG, D_HG, D_HG)
ALPHA = (2 * DEPTH) ** 0.25
LN_EPS = 1e-5
RMS_EPS = 1e-6
LOG2E = math.log2(math.e)
QK_SCALE2 = LOG2E / math.sqrt(HEAD_DIM)

T_SEL = 128
T_NSA = 256
FAR_KEYS = 512
T_SB = 256
SB_DEAD_BITS = 151.0
T_HG = 1024
HG_UNROLL = T_HG // HG_SUB
T_PROJ = 512
VMEM_LIMIT = 56 * 1024 * 1024

_NT = (((1,), (1,)), ((), ()))
_TN = (((0,), (0,)), ((), ()))


def _dot(a, b):
    return jnp.dot(a, b, preferred_element_type=F32)


def _dot_nt(a, b):
    return lax.dot_general(a, b, _NT, preferred_element_type=F32)


def _dot_tn(a, b):
    return lax.dot_general(a, b, _TN, preferred_element_type=F32)


def _split_bf16(x):
    hi = x.astype(BF16)
    lo = (x - hi.astype(F32)).astype(BF16)
    return hi, lo


def _params(*sem):
    return pltpu.CompilerParams(dimension_semantics=sem, vmem_limit_bytes=VMEM_LIMIT)


def _project(x, wn_ref, wt_ref, o_refs, n_nat):
    xb = x.astype(BF16)
    nat = _dot(xb, wn_ref[...])
    off = 0
    for o_ref in o_refs[:n_nat]:
        o_ref[...] = nat[:, off:off + o_ref.shape[1]].astype(o_ref.dtype)
        off += o_ref.shape[1]
    tr = _dot_nt(wt_ref[...], xb)
    off = 0
    for o_ref in o_refs[n_nat:]:
        o_ref[...] = tr[off:off + o_ref.shape[0], :].astype(o_ref.dtype)
        off += o_ref.shape[0]


def _inproj_kernel(x_ref, wn_ref, wt_ref, *o_refs, n_nat):
    _project(x_ref[...], wn_ref, wt_ref, o_refs, n_nat)


def _proj_specs(m, w_nat, dt_nat, w_tr, dt_tr):
    wn = jnp.concatenate(w_nat, axis=1)
    wt = jnp.concatenate(w_tr, axis=0)
    w_specs = [pl.BlockSpec(wn.shape, lambda i: (0, 0)), pl.BlockSpec(wt.shape, lambda i: (0, 0))]
    out_specs = [pl.BlockSpec((T_PROJ, w.shape[1]), lambda i: (i, 0)) for w in w_nat]
    out_specs += [pl.BlockSpec((w.shape[0], T_PROJ), lambda i: (0, i)) for w in w_tr]
    out_shape = [jax.ShapeDtypeStruct((m, w.shape[1]), dt) for w, dt in zip(w_nat, dt_nat)]
    out_shape += [jax.ShapeDtypeStruct((w.shape[0], m), dt) for w, dt in zip(w_tr, dt_tr)]
    return (wn, wt), w_specs, out_specs, out_shape


def _inproj(x2d, w_nat, dt_nat, w_tr, dt_tr):
    m = x2d.shape[0]
    w, w_specs, out_specs, out_shape = _proj_specs(m, w_nat, dt_nat, w_tr, dt_tr)
    outs = pl.pallas_call(
        functools.partial(_inproj_kernel, n_nat=len(w_nat)),
        grid=(m // T_PROJ,),
        in_specs=[pl.BlockSpec((T_PROJ, D_MODEL), lambda i: (i, 0))] + w_specs,
        out_specs=out_specs, out_shape=out_shape,
        compiler_params=_params("parallel"), name="inproj",
    )(x2d, *w)
    return outs[:len(w_nat)], outs[len(w_nat):]


def _compress_kernel(ksrc_ref, vsrc_ref, pos_ref, w1k_ref, w2k_ref, w1v_ref, w2v_ref,
                     kc_ref, vc_ref, *, n_chunk):
    def hidden(src_ref, w1_ref):
        top = jnp.zeros((n_chunk, 2 * CMP_HIDDEN), F32)
        bot = jnp.zeros((n_chunk, 2 * CMP_HIDDEN), F32)
        for p in range(CMP_STRIDE):
            xp = src_ref[0, pl.ds(p, n_chunk, stride=CMP_STRIDE), :]
            top += _dot((xp + pos_ref[p:p + 1, :]).astype(BF16), w1_ref[p])
            q = CMP_STRIDE + p
            bot += _dot((xp + pos_ref[q:q + 1, :]).astype(BF16), w1_ref[q])
        hid = top + pltpu.roll(bot, n_chunk - 1, 0)
        return jax.nn.gelu(hid).astype(BF16)

    kc_ref[0] = _dot(hidden(ksrc_ref, w1k_ref), w2k_ref[...]).astype(kc_ref.dtype)
    vc_ref[0] = _dot_nt(w2v_ref[...], hidden(vsrc_ref, w1v_ref)).astype(vc_ref.dtype)


def _compress(kc_src, vc_src, pos2, w1k, w2k, w1v, w2v_t):
    b, s, _ = kc_src.shape
    n_chunk = s // CMP_STRIDE
    full = lambda a: pl.BlockSpec(a.shape, lambda i: (0,) * a.ndim)
    src = pl.BlockSpec((1, s, D_KV), lambda i: (i, 0, 0))
    return pl.pallas_call(
        functools.partial(_compress_kernel, n_chunk=n_chunk),
        grid=(b,),
        in_specs=[src, src, full(pos2), full(w1k), full(w2k), full(w1v), full(w2v_t)],
        out_specs=[pl.BlockSpec((1, n_chunk, D_KV), lambda i: (i, 0, 0)),
                   pl.BlockSpec((1, D_KV, n_chunk), lambda i: (i, 0, 0))],
        out_shape=[jax.ShapeDtypeStruct((b, n_chunk, D_KV), BF16),
                   jax.ShapeDtypeStruct((b, D_KV, n_chunk), BF16)],
        compiler_params=_params("parallel"), name="nsa_compress",
    )(kc_src, vc_src, pos2, w1k, w2k, w1v, w2v_t)


def _nsa_select_kernel(qt_ref, kc_ref, vct_ref, pbias_ref, ovl_ref, ocmp_ref, sel_ref, *, n_cmp_pad):
    T = T_SEL
    j = pl.program_id(1)
    t0 = j * T
    n_idx = lax.broadcasted_iota(jnp.int32, (n_cmp_pad, T), 0)
    tok_c = t0 + lax.broadcasted_iota(jnp.int32, (n_cmp_pad, T), 1)
    mask_c = tok_c >= CMP_STRIDE * n_idx + (CMP_BLOCK - 1)
    off = pl.multiple_of(n_cmp_pad - (T // CMP_STRIDE) * j, 8)
    psums = [None] * NSA_KV_GROUPS
    for h in range(NSA_HEADS):
        g = h // NSA_HPG
        qh = qt_ref[HEAD_DIM * h:HEAD_DIM * (h + 1), :]
        zq = jnp.zeros_like(qh)
        qh = jnp.concatenate([qh, zq] if g == 0 else [zq, qh], axis=0)
        s = _dot(kc_ref[0], qh) + pbias_ref[h, pl.ds(off, n_cmp_pad), :]
        s = jnp.where(mask_c, s, NEG_BIG)
        m = jnp.max(s, axis=0, keepdims=True)
        p = jnp.where(mask_c, jnp.exp2(s - m), 0.0)
        l = jnp.sum(p, axis=0, keepdims=True)
        p = p / jnp.where(l > 0.0, l, 1.0)
        o_both = _dot(vct_ref[0], p.astype(BF16))
        ocmp_ref[HEAD_DIM * h:HEAD_DIM * (h + 1), :] = o_both[HEAD_DIM * g:HEAD_DIM * (g + 1)]
        psums[g] = p if psums[g] is None else psums[g] + p

    jblk = lax.broadcasted_iota(jnp.int32, (MAX_SLC, T), 0)
    tok = t0 + lax.broadcasted_iota(jnp.int32, (MAX_SLC, T), 1)
    cur = lax.shift_right_logical(tok, 6)
    forced = (jblk == 0) | (jblk == cur) | (jblk == cur - 1)
    valid = jblk * SLC_BLOCK <= tok
    jsub = lax.broadcasted_iota(jnp.int32, (8, T), 0)
    for g in range(NSA_KV_GROUPS):
        imp = _dot(ovl_ref[...], psums[g].astype(BF16))
        score = jnp.where(valid, imp + jnp.where(forced, FORCE_BONUS, 0.0), NEG_BIG)
        blocks = [score[8 * rb:8 * rb + 8] for rb in range(MAX_SLC // 8)]
        ranks = [jnp.zeros((8, T), F32) for _ in blocks]
        for jp in range(MAX_SLC):
            other = score[jp:jp + 1, :]
            for rb, blk in enumerate(blocks):
                ge = jnp.where(other >= blk, 1.0, 0.0)
                gt = jnp.where(other > blk, 1.0, 0.0)
                if 8 * rb > jp:
                    inc = ge
                elif 8 * rb + 7 < jp:
                    inc = gt
                else:
                    inc = jnp.where(jsub > jp - 8 * rb, ge, gt)
                ranks[rb] = ranks[rb] + inc
        rank = jnp.concatenate(ranks, axis=0)
        sel_ref[0, MAX_SLC * g:MAX_SLC * (g + 1), :] = jnp.where(rank < float(SLC_TOPN), 0.0, NEG_BIG).astype(BF16)


def _nsa_select(q_t, kc, vc_t, pbias, ovl_t):
    b, n_cmp_pad, _ = kc.shape
    m = q_t.shape[1]
    nq = m // b // T_SEL
    cols = lambda r: pl.BlockSpec((r, T_SEL), lambda i, j: (0, i * nq + j))
    per_b = lambda a: pl.BlockSpec((1,) + a.shape[1:], lambda i, j: (i, 0, 0))
    full = lambda a: pl.BlockSpec(a.shape, lambda i, j: (0,) * a.ndim)
    return pl.pallas_call(
        functools.partial(_nsa_select_kernel, n_cmp_pad=n_cmp_pad),
        grid=(b, nq),
        in_specs=[cols(D_NSA), per_b(kc), per_b(vc_t), full(pbias), full(ovl_t)],
        out_specs=[cols(D_NSA), pl.BlockSpec((1, NSA_KV_GROUPS * MAX_SLC, T_SEL), lambda i, j: (i, 0, j))],
        out_shape=[jax.ShapeDtypeStruct((D_NSA, m), F32),
                   jax.ShapeDtypeStruct((b, NSA_KV_GROUPS * MAX_SLC, m // b), BF16)],
        compiler_params=_params("parallel", "parallel"), name="nsa_select",
    )(q_t, kc, vc_t, pbias, ovl_t)


def _online_init(m_ref, l_ref, acc_ref):
    m_ref[...] = jnp.full(m_ref.shape, NEG_BIG, F32)
    l_ref[...] = jnp.zeros(l_ref.shape, F32)
    acc_ref[...] = jnp.zeros(acc_ref.shape, F32)


def _scores(k_tile, q_ref, q_rows, bias_ref=None, ok=None):
    s = _dot(k_tile, q_ref[0:q_rows, :])
    if bias_ref is not None:
        s = s + bias_ref[...]
    if ok is not None:
        s = jnp.where(ok, s, NEG_BIG)
    return s


def _online_step(m_ref, l_ref, acc_ref, s, v_t):
    m = m_ref[...]
    m_new = jnp.maximum(m, jnp.max(s, axis=0, keepdims=True))
    alpha = jnp.exp2(m - m_new)
    p = jnp.exp2(s - m_new)
    m_ref[...] = m_new
    l_ref[...] = alpha * l_ref[...] + jnp.sum(p, axis=0, keepdims=True)
    acc_ref[...] = alpha * acc_ref[...] + _dot(v_t, p.astype(BF16))


def _nsa_attend_kernel(qt_ref, sel_ref, ocmp_ref, gt_ref, ks_ref, kw_ref, vs_ref, vw_ref, bias_ref, onehot_ref,
                       o_ref, kaug_ref, qaug_ref, ms_ref, ls_ref, accs_ref, mw_ref, lw_ref, accw_ref):
    T = T_NSA
    j = pl.program_id(1)

    @pl.when(j == 0)
    def _():
        kaug_ref[:, 0:LANES] = ks_ref[0]
        kaug_ref[:, LANES:2 * LANES] = onehot_ref[...]
        qaug_ref[...] = jnp.zeros(qaug_ref.shape, BF16)

    for h in range(NSA_HEADS):
        g = h // NSA_HPG
        qaug_ref[HEAD_DIM * g:HEAD_DIM * (g + 1), h * T:(h + 1) * T] = qt_ref[HEAD_DIM * h:HEAD_DIM * (h + 1), :]
        qaug_ref[LANES:LANES + MAX_SLC, h * T:(h + 1) * T] = sel_ref[0, MAX_SLC * g:MAX_SLC * (g + 1), :]

    key = lax.broadcasted_iota(jnp.int32, (T, NSA_HEADS * T), 0)
    qry = lax.broadcasted_iota(jnp.int32, (T, NSA_HEADS * T), 1) & (T - 1)
    causal = key <= qry
    st_s = (ms_ref, ls_ref, accs_ref)
    st_w = (mw_ref, lw_ref, accw_ref)
    _online_init(*st_s)
    _online_init(*st_w)

    far_len = jnp.maximum(j - 1, 0) * T

    def far_step(r0, n_keys):
        s = _scores(kaug_ref[pl.ds(r0, n_keys), :], qaug_ref, 2 * LANES)
        _online_step(*st_s, s, vs_ref[:, pl.ds(r0, n_keys)])

    def far_big(c, carry):
        r0 = pl.multiple_of(c * (2 * FAR_KEYS), 2 * FAR_KEYS)
        s_lo = _scores(kaug_ref[pl.ds(r0, FAR_KEYS), :], qaug_ref, 2 * LANES)
        s_hi = _scores(kaug_ref[pl.ds(r0 + FAR_KEYS, FAR_KEYS), :], qaug_ref, 2 * LANES)
        _online_step(*st_s, s_lo, vs_ref[:, pl.ds(r0, FAR_KEYS)])
        _online_step(*st_s, s_hi, vs_ref[:, pl.ds(r0 + FAR_KEYS, FAR_KEYS)])
        return carry

    n_big = lax.shift_right_logical(far_len, FAR_KEYS.bit_length())
    lax.fori_loop(0, n_big, far_big, 0)
    done = n_big * (2 * FAR_KEYS)

    @pl.when(far_len - done >= FAR_KEYS)
    def _():
        far_step(pl.multiple_of(done, FAR_KEYS), FAR_KEYS)

    done = done + jnp.where(far_len - done >= FAR_KEYS, FAR_KEYS, 0)

    def far_small(c, carry):
        far_step(pl.multiple_of(done + c * T, T), T)
        return carry

    lax.fori_loop(0, lax.shift_right_logical(far_len - done, T.bit_length() - 1), far_small, 0)

    n_win = WINDOW // T
    starts = [pl.multiple_of(jnp.maximum(j - d, 0) * T, T) for d in range(n_win + 1)]
    in_range = [key < jnp.where(j - d >= 0, T, -1) for d in range(n_win + 1)]
    sel_scores, win_scores = {}, {}
    for d in (1, 0):
        sel_scores[d] = _scores(kaug_ref[pl.ds(starts[d], T), :], qaug_ref, 2 * LANES, bias_ref.at[d],
                                causal if d == 0 else in_range[d])
    for d in range(n_win + 1):
        if d == 0:
            ok = causal
        elif d == n_win:
            ok = (key > qry) & in_range[d]
        else:
            ok = in_range[d]
        win_scores[d] = _scores(kw_ref[0, pl.ds(starts[d], T), :], qaug_ref, LANES,
                                bias_ref.at[d] if d <= 1 else None, ok)
    for d in (1, 0):
        _online_step(*st_s, sel_scores[d], vs_ref[:, pl.ds(starts[d], T)])
    for d in range(n_win + 1):
        _online_step(*st_w, win_scores[d], vw_ref[:, pl.ds(starts[d], T)])

    gates = jax.nn.sigmoid(gt_ref[...])
    heads = []
    for h in range(NSA_HEADS):
        cols = slice(h * T, (h + 1) * T)
        rows = slice(HEAD_DIM * (h // NSA_HPG), HEAD_DIM * (h // NSA_HPG + 1))
        heads.append(gates[3 * h:3 * h + 1, :] * ocmp_ref[HEAD_DIM * h:HEAD_DIM * (h + 1), :]
                     + (gates[3 * h + 1:3 * h + 2, :] / ls_ref[:, cols]) * accs_ref[rows, cols]
                     + (gates[3 * h + 2:3 * h + 3, :] / lw_ref[:, cols]) * accw_ref[rows, cols])
    o_ref[0] = jnp.concatenate(heads, axis=0).T.astype(o_ref.dtype)


def _nsa_attend(q_t, sel_t, ocmp_t, g_t, ks, kw, vs_t, vw_t, bias_t, onehot):
    b, s, _ = ks.shape
    T = T_NSA
    nq = s // T
    R = NSA_HEADS * T
    cols = lambda r: pl.BlockSpec((r, T), lambda i, j: (0, i * nq + j))
    row_b = lambda a: pl.BlockSpec((a.shape[0], s), lambda i, j: (0, i))
    per_b = lambda a: pl.BlockSpec((1,) + a.shape[1:], lambda i, j: (i, 0, 0))
    full = lambda a: pl.BlockSpec(a.shape, lambda i, j: (0,) * a.ndim)
    stat = [pltpu.VMEM((1, R), F32), pltpu.VMEM((1, R), F32), pltpu.VMEM((LANES, R), F32)]
    return pl.pallas_call(
        _nsa_attend_kernel,
        grid=(b, nq),
        in_specs=[cols(D_NSA), pl.BlockSpec((1, NSA_KV_GROUPS * MAX_SLC, T), lambda i, j: (i, 0, j)),
                  cols(D_NSA), cols(LANES), per_b(ks), per_b(kw), row_b(vs_t), row_b(vw_t),
                  full(bias_t), full(onehot)],
        out_specs=pl.BlockSpec((1, T, D_NSA), lambda i, j: (i, j, 0)),
        out_shape=jax.ShapeDtypeStruct((b, s, D_NSA), BF16),
        scratch_shapes=[pltpu.VMEM((s, 2 * LANES), BF16), pltpu.VMEM((2 * LANES, R), BF16)] + stat + stat,
        compiler_params=_params("parallel", "arbitrary"), name="nsa_attend",
    )(q_t, sel_t, ocmp_t, g_t, ks, kw, vs_t, vw_t, bias_t, onehot)


def _sb_kernel(qt_ref, k_ref, vt_ref, tri_ref, o_ref, carry_ref, acc_ref):
    T = T_SB
    j = pl.program_id(1)
    n_pair = SB_HEADS // 2
    key = lax.broadcasted_iota(jnp.int32, (T, 2 * T), 0)
    qry = lax.broadcasted_iota(jnp.int32, (T, 2 * T), 1) & (T - 1)
    strict = key < qry
    carry_ref[...] = jnp.zeros(carry_ref.shape, F32)
    acc_ref[...] = jnp.zeros(acc_ref.shape, F32)

    def tiles(cs, masks):
        starts = [pl.multiple_of(c * T, T) for c in cs]
        units = [(ci, p) for ci in range(len(cs)) for p in range(n_pair)]
        z2s, incls = [], []
        for c, p in units:
            q_even = qt_ref[HEAD_DIM * (2 * p):HEAD_DIM * (2 * p + 1), :]
            q_odd = qt_ref[HEAD_DIM * (2 * p + 1):HEAD_DIM * (2 * p + 2), :]
            zq = jnp.zeros_like(q_even)
            q_pair = jnp.concatenate([jnp.concatenate([q_even, zq], axis=0),
                                      jnp.concatenate([zq, q_odd], axis=0)], axis=1)
            z2s.append(_dot(k_ref[0, pl.ds(starts[c], T), LANES * p:LANES * (p + 1)], q_pair))
        for (c, p), z2 in zip(units, z2s):
            rest = jnp.maximum(z2, 0.0) + jnp.log2(1.0 + jnp.exp2(-jnp.abs(z2)))
            if masks[c] is not None:
                rest = jnp.where(masks[c], rest, 0.0)
            hi, lo = _split_bf16(rest)
            incls.append(_dot(tri_ref[...], jnp.concatenate([hi, lo], axis=0)))
        carry = [carry_ref[p] for p in range(n_pair)]
        acc = [acc_ref[p] for p in range(n_pair)]
        for (c, p), z2, incl in zip(units, z2s, incls):
            a = jnp.exp2(z2 - incl - carry[p])
            if masks[c] is not None:
                a = jnp.where(masks[c], a, 0.0)
            acc[p] = acc[p] + _dot(vt_ref[LANES * p:LANES * (p + 1), pl.ds(starts[c], T)], a.astype(BF16))
            carry[p] = carry[p] + incl[0:1, :]
        for p in range(n_pair):
            carry_ref[p] = carry[p]
            acc_ref[p] = acc[p]

    @pl.when(j == 0)
    def _():
        tiles([0], [strict])

    @pl.when(j > 0)
    def _():
        tiles([j, j - 1], [strict, None])

    def live(c):
        return (c >= 0) & (jnp.min(carry_ref[...]) < SB_DEAD_BITS)

    def back_one(c):
        tiles([c], [None])
        return c - 1

    lax.while_loop(live, back_one, j - 2)
    row = lax.broadcasted_iota(jnp.int32, (LANES, T), 0)
    for p in range(n_pair):
        acc = acc_ref[p]
        o_ref[0, :, LANES * p:LANES * (p + 1)] = (
            jnp.where(row < HEAD_DIM, acc[:, 0:T], acc[:, T:2 * T]).T.astype(o_ref.dtype))


def _sb_attention(q_t, k, v_t, tri_t):
    b, s, _ = k.shape
    T = T_SB
    nq = s // T
    n_pair = SB_HEADS // 2
    return pl.pallas_call(
        _sb_kernel,
        grid=(b, nq),
        in_specs=[pl.BlockSpec((D_SB, T), lambda i, j: (0, i * nq + j)),
                  pl.BlockSpec((1, s, D_SB), lambda i, j: (i, 0, 0)),
                  pl.BlockSpec((D_SB, s), lambda i, j: (0, i)),
                  pl.BlockSpec(tri_t.shape, lambda i, j: (0, 0))],
        out_specs=pl.BlockSpec((1, T, D_SB), lambda i, j: (i, j, 0)),
        out_shape=jax.ShapeDtypeStruct((b, s, D_SB), BF16),
        scratch_shapes=[pltpu.VMEM((n_pair, 1, 2 * T), F32), pltpu.VMEM((n_pair, LANES, 2 * T), F32)],
        compiler_params=_params("parallel", "parallel"), name="sb_attention",
    )(q_t, k, v_t, tri_t)


def _hgrn_kernel(q_ref, f_ref, i_ref, lb_ref, nw_ref, ones_ref, bd_ref, o_ref,
                 st_ref, qj_ref, kj_ref, fj_ref, vj_ref, qd_ref, kd_ref, oc_ref, x_ref, u_ref, sb_ref, dec_ref,
                 att_ref, oall_ref):
    C = HG_SUB
    n_blk = T_HG // C

    @pl.when(pl.program_id(2) == 0)
    def _():
        st_ref[...] = jnp.zeros(st_ref.shape, F32)

    lb = lb_ref[...]
    lb_floor = jnp.maximum(lb, LB_FLOOR)
    one_m_lb = 1.0 - lb
    cum = None
    for j in range(C):
        xj = f_ref[0, pl.ds(j, n_blk, stride=C), :]
        e = jnp.exp(-jnp.abs(xj))
        r = 1.0 / (1.0 + e)
        er = e * r
        pos = xj >= 0.0
        fj = lb_floor + one_m_lb * jnp.where(pos, r, er)
        kj_ref[j] = one_m_lb * jnp.where(pos, er, r)
        fj_ref[j] = fj
        cum = fj if cum is None else cum * fj
        qj = q_ref[0, pl.ds(j, n_blk, stride=C), :]
        qj_ref[j] = qj
        vj_ref[j] = i_ref[0, pl.ds(j, n_blk, stride=C), :]
        qd_ref[pl.ds(j, n_blk, stride=HG_PITCH), :] = qj * cum
    dec_ref[...] = cum
    tail = None
    for j in reversed(range(C)):
        kd_ref[pl.ds(j, n_blk, stride=HG_PITCH), :] = kj_ref[j] if tail is None else kj_ref[j] * tail
        tail = fj_ref[j] if tail is None else tail * fj_ref[j]

    def kv_products(blk, carry):
        kd = kd_ref[pl.ds(pl.multiple_of(blk * HG_PITCH, 8), C), :].astype(BF16)
        vv = i_ref[0, pl.ds(pl.multiple_of(blk * C, C), C), :].astype(BF16)
        u_ref[blk] = bd_ref[...] * _dot_tn(vv, kd)
        return carry

    lax.fori_loop(0, n_blk, kv_products, 0, unroll=HG_UNROLL)

    def scan(blk, st):
        sb_ref[blk] = st.astype(BF16)
        return st * dec_ref[pl.ds(blk, 1), :] + u_ref[blk]

    st_ref[...] = lax.fori_loop(0, n_blk, scan, st_ref[...], unroll=HG_UNROLL)

    def outputs(blk, carry):
        r0 = pl.multiple_of(blk * HG_PITCH, 8)
        oc_ref[pl.ds(r0, C), :] = _dot_nt(qd_ref[pl.ds(r0, C), :].astype(BF16), sb_ref[blk])
        return carry

    lax.fori_loop(0, n_blk, outputs, 0, unroll=HG_UNROLL)

    base = lambda j: (j * (j + 1) // 2) * n_blk
    for j in range(C):
        qdec = qj_ref[j]
        for jp in reversed(range(j + 1)):
            x_ref[base(j) + jp * n_blk:base(j) + (jp + 1) * n_blk, :] = (qdec * kj_ref[jp]).astype(BF16)
            if jp > 0:
                qdec = qdec * fj_ref[jp]
        att_ref[base(j):base(j + 1), :] = _dot(x_ref[base(j):base(j + 1), :], ones_ref[...])
    for j in range(C):
        oj = oc_ref[pl.ds(j, n_blk, stride=HG_PITCH), :]
        for jp in range(j + 1):
            oj = oj + att_ref[base(j) + jp * n_blk:base(j) + (jp + 1) * n_blk, :] * vj_ref[jp]
        oall_ref[j * n_blk:(j + 1) * n_blk, :] = oj
    o_all = oall_ref[...]
    hi, lo = _split_bf16(o_all * o_all)
    ms = (_dot(hi, ones_ref[...]) + _dot(lo, ones_ref[...])) * (1.0 / HEAD_DIM)
    oall_ref[...] = o_all * lax.rsqrt(ms + RMS_EPS) * nw_ref[...]
    for j in range(C):
        o_ref[0, pl.ds(j, n_blk, stride=C), :] = oall_ref[j * n_blk:(j + 1) * n_blk, :]


def _hgrn(q, f, i, lb, nw, ones_bd, bd_mask):
    b, s, _ = q.shape
    n_blk = T_HG // HG_SUB
    tile = pl.BlockSpec((1, T_HG, LANES), lambda bi, pi, ti: (bi, ti, pi))
    vec = pl.BlockSpec((1, LANES), lambda bi, pi, ti: (0, pi))
    full = lambda a: pl.BlockSpec(a.shape, lambda bi, pi, ti: (0,) * a.ndim)
    jm = pltpu.VMEM((HG_SUB, n_blk, LANES), F32)
    nat = pltpu.VMEM((n_blk * HG_PITCH, LANES), F32)
    return pl.pallas_call(
        _hgrn_kernel,
        grid=(b, D_HG // LANES, s // T_HG),
        in_specs=[tile, tile, tile, vec, vec, full(ones_bd), full(bd_mask)],
        out_specs=tile,
        out_shape=jax.ShapeDtypeStruct((b, s, D_HG), F32),
        scratch_shapes=[pltpu.VMEM((LANES, LANES), F32), jm, jm, jm, jm, nat, nat, nat,
                        pltpu.VMEM((HG_SUB * (HG_SUB + 1) // 2 * n_blk, LANES), BF16),
                        pltpu.VMEM((n_blk, LANES, LANES), F32),
                        pltpu.VMEM((n_blk, LANES, LANES), BF16), pltpu.VMEM((n_blk, LANES), F32),
                        pltpu.VMEM((HG_SUB * (HG_SUB + 1) // 2 * n_blk, LANES), F32), pltpu.VMEM((T_HG, LANES), F32)],
        compiler_params=_params("parallel", "parallel", "arbitrary"), name="hgrn2",
    )(q, f, i, lb, nw, ones_bd, bd_mask)


def _mix_norm(onsa_ref, osb_ref, ohg_ref, z_ref, x_ref, w_ref, g_ref, b_ref):
    z = z_ref[...].astype(F32)
    sz = z * jax.nn.sigmoid(z)
    mixed = jnp.concatenate([(onsa_ref[...] * sz[:, 0:D_NSA]).astype(BF16),
                             (osb_ref[...] * sz[:, D_NSA:D_NSA + D_SB]).astype(BF16),
                             (ohg_ref[...] * sz[:, D_NSA + D_SB:D_MIX]).astype(BF16)], axis=1)
    v = ALPHA * x_ref[...] + _dot(mixed, w_ref[...])
    mu = jnp.mean(v, axis=-1, keepdims=True)
    vc = v - mu
    var = jnp.mean(vc * vc, axis=-1, keepdims=True)
    return vc * lax.rsqrt(var + LN_EPS) * g_ref[...] + b_ref[...]


def _out_kernel(*refs):
    refs[8][...] = _mix_norm(*refs[:8])


def _out_in_kernel(*refs, n_nat):
    x_new = _mix_norm(*refs[:8])
    refs[10][...] = x_new
    _project(x_new, refs[8], refs[9], refs[11:], n_nat)


def _out_proj(o_nsa, o_sb, o_hg, z_all, x2d, w, g, bvec, next_proj=None):
    m = x2d.shape[0]
    rows = lambda n: pl.BlockSpec((T_PROJ, n), lambda i: (i, 0))
    full = lambda a: pl.BlockSpec(a.shape, lambda i: (0, 0))
    in_specs = [rows(D_NSA), rows(D_SB), rows(D_HG), rows(D_MIX), rows(D_MODEL), full(w), full(g), full(bvec)]
    x_shape = jax.ShapeDtypeStruct((m, D_MODEL), F32)
    if next_proj is None:
        return pl.pallas_call(
            _out_kernel, grid=(m // T_PROJ,), in_specs=in_specs, out_specs=rows(D_MODEL), out_shape=x_shape,
            compiler_params=_params("parallel"), name="out_proj_norm",
        )(o_nsa, o_sb, o_hg, z_all, x2d, w, g, bvec)
    w_nat = next_proj[0]
    wp, w_specs, out_specs, out_shape = _proj_specs(m, *next_proj)
    outs = pl.pallas_call(
        functools.partial(_out_in_kernel, n_nat=len(w_nat)), grid=(m // T_PROJ,),
        in_specs=in_specs + w_specs, out_specs=[rows(D_MODEL)] + out_specs, out_shape=[x_shape] + out_shape,
        compiler_params=_params("parallel"), name="out_proj_norm_inproj",
    )(o_nsa, o_sb, o_hg, z_all, x2d, w, g, bvec, *wp)
    return outs[0], outs[1:1 + len(w_nat)], outs[1 + len(w_nat):]


def _t5_bucket_np(rel):
    n = np.maximum(rel, 0)
    max_exact = NUM_BUCKETS // 2
    large = max_exact + (np.log(np.maximum(n, 1).astype(np.float32) / max_exact)
                         / math.log(MAX_DISTANCE / max_exact) * (NUM_BUCKETS - max_exact)).astype(np.int32)
    large = np.clip(large, 0, NUM_BUCKETS - 1)
    return np.where(n < max_exact, n, large).astype(np.int32)


def _bias_tables(rel_bias, s):
    tbl = ((rel_bias - rel_bias[NUM_BUCKETS - 1]) * LOG2E).astype(F32)

    def expand(rel):
        onehot = (jnp.arange(NUM_BUCKETS)[:, None] == jnp.asarray(_t5_bucket_np(rel).reshape(1, -1))).astype(F32)
        return jnp.dot(tbl.T, onehot, precision=lax.Precision.HIGHEST).reshape((NSA_HEADS,) + rel.shape)

    n_cmp_pad = s // CMP_STRIDE
    n_rel = np.arange(2 * n_cmp_pad)[:, None] - n_cmp_pad
    r = np.arange(T_SEL)[None, :]
    pbias = expand(r - CMP_STRIDE * n_rel - (CMP_BLOCK - 1))
    T = T_NSA
    key = np.arange(T)[:, None]
    qry = np.arange(T)[None, :]
    near = np.stack([qry - key, T + qry - key])
    bias_t = jnp.transpose(expand(near), (1, 2, 0, 3)).reshape(2, T, NSA_HEADS * T)
    return pbias, bias_t


def _static_tables(s):
    n_chunk = s // CMP_STRIDE
    cmp_start = np.arange(n_chunk) * CMP_STRIDE
    slc_start = np.arange(MAX_SLC) * SLC_BLOCK
    ovl_t = ((cmp_start[None, :] < slc_start[:, None] + SLC_BLOCK)
             & (cmp_start[None, :] + CMP_BLOCK > slc_start[:, None])
             & (cmp_start[None, :] + CMP_BLOCK <= s)).astype(np.float32)
    onehot = (np.arange(s)[:, None] // SLC_BLOCK == np.arange(LANES)[None, :]).astype(np.float32)
    tri = (np.arange(T_SB)[None, :] >= np.arange(T_SB)[:, None]).astype(np.float32)
    tri = np.concatenate([tri, tri], axis=1)
    ones_bd = np.kron(np.eye(2), np.ones((HEAD_DIM, HEAD_DIM))).astype(np.float32)
    as_bf16 = lambda a: jnp.asarray(a, dtype=BF16)
    return as_bf16(ovl_t), as_bf16(onehot), as_bf16(tri), as_bf16(ones_bd), jnp.asarray(ones_bd)


def _layer_weights(w_in_l, cmp_pos_l, w_ck1_l, w_ck2_l, w_cv1_l, w_cv2_l):
    offs = np.cumsum((0,) + SPLIT_SIZES)
    (w_q, w_kc, w_vc, w_ks, w_vs, w_kw, w_vw, w_g, w_nz,
     w_sq, w_sk, w_sv, w_sz, w_hq, w_hf, w_hi, w_hz) = [w_in_l[:, offs[i]:offs[i + 1]] for i in range(len(SPLIT_SIZES))]
    w_gp = jnp.concatenate([w_g, jnp.zeros((D_MODEL, LANES - N_GATES), F32)], axis=1)
    w_z = jnp.concatenate([w_nz, w_sz, w_hz], axis=1)
    w_nat = [w_kc, w_vc, w_ks, w_kw, w_z, w_sk, w_hq, w_hf, w_hi]
    dt_nat = [F32, F32, BF16, BF16, BF16, BF16, F32, F32, F32]
    w_tr = [(w_q * QK_SCALE2).T, w_vs.T, w_vw.T, w_gp.T, (w_sq * QK_SCALE2).T, w_sv.T]
    dt_tr = [BF16, BF16, BF16, F32, BF16, BF16]
    w_nat = [w.astype(BF16) for w in w_nat]
    w_tr = [w.astype(BF16) for w in w_tr]

    def block_diag(w):
        z = jnp.zeros_like(w)
        return jnp.concatenate([jnp.concatenate([w, z], axis=-1), jnp.concatenate([z, w], axis=-1)], axis=-2)

    pos2 = jnp.concatenate([cmp_pos_l, cmp_pos_l], axis=1)
    w1k = block_diag(w_ck1_l.reshape(CMP_BLOCK, HEAD_DIM, CMP_HIDDEN)).astype(BF16)
    w1v = block_diag(w_cv1_l.reshape(CMP_BLOCK, HEAD_DIM, CMP_HIDDEN)).astype(BF16)
    w2k = block_diag(w_ck2_l).astype(BF16)
    w2v_t = block_diag(w_cv2_l).T.astype(BF16)
    return (w_nat, dt_nat, w_tr, dt_tr), (pos2, w1k, w2k, w1v, w2v_t)


def kernel(x, w_in, cmp_pos, w_ck1, w_ck2, w_cv1, w_cv2, hg_lb, hg_norm_w, w_out, ln_g, ln_b, rel_bias):
    b, s, d = x.shape
    assert d == D_MODEL and s % T_HG == 0 and s // SLC_BLOCK <= MAX_SLC and s >= WINDOW + T_NSA
    lb_w = jax.nn.softmax(hg_lb.astype(F32), axis=0)
    lb_all = jnp.cumsum(lb_w, axis=0) - lb_w[0]
    pbias, bias_t = _bias_tables(rel_bias, s)
    ovl_t, onehot, tri, ones_bd, bd_mask = _static_tables(s)

    x2d = x.reshape(b * s, d)
    weights = [_layer_weights(w_in[l], cmp_pos[l], w_ck1[l], w_ck2[l], w_cv1[l], w_cv2[l]) for l in range(DEPTH)]
    nat, trs = _inproj(x2d, *weights[0][0])
    for l in range(DEPTH):
        q_t, vs_t, vw_t, g_t, sq_t, sv_t = trs
        kc_src, vc_src, ks, kw, z_all, sk, hq, hf, hi = [o.reshape(b, s, o.shape[-1]) for o in nat]
        kc, vc_t = _compress(kc_src, vc_src, *weights[l][1])
        ocmp_t, sel_t = _nsa_select(q_t, kc, vc_t, pbias, ovl_t)
        o_nsa = _nsa_attend(q_t, sel_t, ocmp_t, g_t, ks, kw, vs_t, vw_t, bias_t, onehot)
        o_sb = _sb_attention(sq_t, sk, sv_t, tri)
        o_hg = _hgrn(hq, hf, hi, lb_all[l][None, :], hg_norm_w[l][None, :], ones_bd, bd_mask)
        res = _out_proj(o_nsa.reshape(b * s, D_NSA), o_sb.reshape(b * s, D_SB), o_hg.reshape(b * s, D_HG),
                        z_all.reshape(b * s, D_MIX), x2d, w_out[l].astype(BF16),
                        ln_g[l][None, :], ln_b[l][None, :],
                        next_proj=weights[l + 1][0] if l + 1 < DEPTH else None)
        if l + 1 < DEPTH:
            x2d, nat, trs = res
        else:
            x2d = res
    return x2d.reshape(b, s, d)
```

```python
import functools
import math

import numpy as np
import jax
import jax.numpy as jnp
from jax import lax
from jax.experimental import pallas as pl
from jax.experimental.pallas import tpu as pltpu

F32 = jnp.float32
BF16 = jnp.bfloat16

D_MODEL = 1024
DEPTH = 2
HEAD_DIM = 64
LANES = 128
NSA_HEADS = 6
NSA_KV_GROUPS = 2
NSA_HPG = NSA_HEADS // NSA_KV_GROUPS
CMP_BLOCK = 32
CMP_STRIDE = 16
CMP_HIDDEN = 2 * HEAD_DIM
SLC_BLOCK = 64
SLC_TOPN = 16
MAX_SLC = 64
WINDOW = 512
FORCE_BONUS = 1000.0
NEG_BIG = -1e30
LB_FLOOR = 1e-30
SB_HEADS = 4
HG_HEADS = 6
HG_SUB = 16
HG_PITCH = 24
NUM_BUCKETS = 32
MAX_DISTANCE = 128
D_NSA = NSA_HEADS * HEAD_DIM
D_KV = NSA_KV_GROUPS * HEAD_DIM
D_SB = SB_HEADS * HEAD_DIM
D_HG = HG_HEADS * HEAD_DIM
D_MIX = D_NSA + D_SB + D_HG
N_GATES = NSA_HEADS * 3
SPLIT_SIZES = (D_NSA, D_KV, D_KV, D_KV, D_KV, D_KV, D_KV, N_GATES, D_NSA,
               D_SB, D_SB, D_SB, D_SB, D_HG, D_HG, D_HG, D_HG)
ALPHA = (2 * DEPTH) ** 0.25
LN_EPS = 1e-5
RMS_EPS = 1e-6
LOG2E = math.log2(math.e)
QK_SCALE2 = LOG2E / math.sqrt(HEAD_DIM)

T_SEL = 128
T_NSA = 256
FAR_KEYS = 512
ONES_ROWS = 16
T_SB = 256
SB_DEAD_BITS = 151.0
T_HG = 1024
HG_UNROLL = T_HG // HG_SUB
T_PROJ = 512
VMEM_LIMIT = 56 * 1024 * 1024

_NT = (((1,), (1,)), ((), ()))
_TN = (((0,), (0,)), ((), ()))


def _dot(a, b):
    return jnp.dot(a, b, preferred_element_type=F32)


def _dot_nt(a, b):
    return lax.dot_general(a, b, _NT, preferred_element_type=F32)


def _dot_tn(a, b):
    return lax.dot_general(a, b, _TN, preferred_element_type=F32)


def _split_bf16(x):
    hi = x.astype(BF16)
    lo = (x - hi.astype(F32)).astype(BF16)
    return hi, lo


def _params(*sem):
    return pltpu.CompilerParams(dimension_semantics=sem, vmem_limit_bytes=VMEM_LIMIT)


def _project(x, wn_ref, wt_ref, o_refs, n_nat):
    xb = x.astype(BF16)
    nat = _dot(xb, wn_ref[...])
    off = 0
    for o_ref in o_refs[:n_nat]:
        o_ref[...] = nat[:, off:off + o_ref.shape[1]].astype(o_ref.dtype)
        off += o_ref.shape[1]
    tr = _dot_nt(wt_ref[...], xb)
    off = 0
    for o_ref in o_refs[n_nat:]:
        o_ref[...] = tr[off:off + o_ref.shape[0], :].astype(o_ref.dtype)
        off += o_ref.shape[0]


def _inproj_kernel(x_ref, wn_ref, wt_ref, *o_refs, n_nat):
    _project(x_ref[...], wn_ref, wt_ref, o_refs, n_nat)


def _proj_specs(m, w_nat, dt_nat, w_tr, dt_tr):
    wn = jnp.concatenate(w_nat, axis=1)
    wt = jnp.concatenate(w_tr, axis=0)
    w_specs = [pl.BlockSpec(wn.shape, lambda i: (0, 0)), pl.BlockSpec(wt.shape, lambda i: (0, 0))]
    out_specs = [pl.BlockSpec((T_PROJ, w.shape[1]), lambda i: (i, 0)) for w in w_nat]
    out_specs += [pl.BlockSpec((w.shape[0], T_PROJ), lambda i: (0, i)) for w in w_tr]
    out_shape = [jax.ShapeDtypeStruct((m, w.shape[1]), dt) for w, dt in zip(w_nat, dt_nat)]
    out_shape += [jax.ShapeDtypeStruct((w.shape[0], m), dt) for w, dt in zip(w_tr, dt_tr)]
    return (wn, wt), w_specs, out_specs, out_shape


def _inproj(x2d, w_nat, dt_nat, w_tr, dt_tr):
    m = x2d.shape[0]
    w, w_specs, out_specs, out_shape = _proj_specs(m, w_nat, dt_nat, w_tr, dt_tr)
    outs = pl.pallas_call(
        functools.partial(_inproj_kernel, n_nat=len(w_nat)),
        grid=(m // T_PROJ,),
        in_specs=[pl.BlockSpec((T_PROJ, D_MODEL), lambda i: (i, 0))] + w_specs,
        out_specs=out_specs, out_shape=out_shape,
        compiler_params=_params("parallel"), name="inproj",
    )(x2d, *w)
    return outs[:len(w_nat)], outs[len(w_nat):]


def _compress_kernel(ksrc_ref, vsrc_ref, pos_ref, w1k_ref, w2k_ref, w1v_ref, w2v_ref,
                     kc_ref, vc_ref, *, n_chunk):
    def hidden(src_ref, w1_ref):
        top = jnp.zeros((n_chunk, 2 * CMP_HIDDEN), F32)
        bot = jnp.zeros((n_chunk, 2 * CMP_HIDDEN), F32)
        for p in range(CMP_STRIDE):
            xp = src_ref[0, pl.ds(p, n_chunk, stride=CMP_STRIDE), :]
            top += _dot((xp + pos_ref[p:p + 1, :]).astype(BF16), w1_ref[p])
            q = CMP_STRIDE + p
            bot += _dot((xp + pos_ref[q:q + 1, :]).astype(BF16), w1_ref[q])
        hid = top + pltpu.roll(bot, n_chunk - 1, 0)
        return jax.nn.gelu(hid).astype(BF16)

    kc_ref[0] = _dot(hidden(ksrc_ref, w1k_ref), w2k_ref[...]).astype(kc_ref.dtype)
    vc_ref[0] = _dot_nt(w2v_ref[...], hidden(vsrc_ref, w1v_ref)).astype(vc_ref.dtype)


def _compress(kc_src, vc_src, pos2, w1k, w2k, w1v, w2v_t):
    b, s, _ = kc_src.shape
    n_chunk = s // CMP_STRIDE
    full = lambda a: pl.BlockSpec(a.shape, lambda i: (0,) * a.ndim)
    src = pl.BlockSpec((1, s, D_KV), lambda i: (i, 0, 0))
    return pl.pallas_call(
        functools.partial(_compress_kernel, n_chunk=n_chunk),
        grid=(b,),
        in_specs=[src, src, full(pos2), full(w1k), full(w2k), full(w1v), full(w2v_t)],
        out_specs=[pl.BlockSpec((1, n_chunk, D_KV), lambda i: (i, 0, 0)),
                   pl.BlockSpec((1, D_KV, n_chunk), lambda i: (i, 0, 0))],
        out_shape=[jax.ShapeDtypeStruct((b, n_chunk, D_KV), BF16),
                   jax.ShapeDtypeStruct((b, D_KV, n_chunk), BF16)],
        compiler_params=_params("parallel"), name="nsa_compress",
    )(kc_src, vc_src, pos2, w1k, w2k, w1v, w2v_t)


def _nsa_select_kernel(qt_ref, kc_ref, vct_ref, pbias_ref, ovl_ref, ocmp_ref, sel_ref, *, n_cmp_pad):
    T = T_SEL
    j = pl.program_id(1)
    t0 = j * T
    n_idx = lax.broadcasted_iota(jnp.int32, (n_cmp_pad, T), 0)
    tok_c = t0 + lax.broadcasted_iota(jnp.int32, (n_cmp_pad, T), 1)
    mask_c = tok_c >= CMP_STRIDE * n_idx + (CMP_BLOCK - 1)
    off = pl.multiple_of(n_cmp_pad - (T // CMP_STRIDE) * j, 8)
    psums = [None] * NSA_KV_GROUPS
    for h in range(NSA_HEADS):
        g = h // NSA_HPG
        qh = qt_ref[HEAD_DIM * h:HEAD_DIM * (h + 1), :]
        zq = jnp.zeros_like(qh)
        qh = jnp.concatenate([qh, zq] if g == 0 else [zq, qh], axis=0)
        s = _dot(kc_ref[0], qh) + pbias_ref[h, pl.ds(off, n_cmp_pad), :]
        s = jnp.where(mask_c, s, NEG_BIG)
        m = jnp.max(s, axis=0, keepdims=True)
        p = jnp.where(mask_c, jnp.exp2(s - m), 0.0)
        l = jnp.sum(p, axis=0, keepdims=True)
        p = p / jnp.where(l > 0.0, l, 1.0)
        o_both = _dot(vct_ref[0], p.astype(BF16))
        ocmp_ref[HEAD_DIM * h:HEAD_DIM * (h + 1), :] = o_both[HEAD_DIM * g:HEAD_DIM * (g + 1)]
        psums[g] = p if psums[g] is None else psums[g] + p

    jblk = lax.broadcasted_iota(jnp.int32, (MAX_SLC, T), 0)
    tok = t0 + lax.broadcasted_iota(jnp.int32, (MAX_SLC, T), 1)
    cur = lax.shift_right_logical(tok, 6)
    forced = (jblk == 0) | (jblk == cur) | (jblk == cur - 1)
    valid = jblk * SLC_BLOCK <= tok
    jsub = lax.broadcasted_iota(jnp.int32, (8, T), 0)
    for g in range(NSA_KV_GROUPS):
        imp = _dot(ovl_ref[...], psums[g].astype(BF16))
        score = jnp.where(valid, imp + jnp.where(forced, FORCE_BONUS, 0.0), NEG_BIG)
        blocks = [score[8 * rb:8 * rb + 8] for rb in range(MAX_SLC // 8)]
        ranks = [jnp.zeros((8, T), F32) for _ in blocks]
        for jp in range(MAX_SLC):
            other = score[jp:jp + 1, :]
            for rb, blk in enumerate(blocks):
                ge = jnp.where(other >= blk, 1.0, 0.0)
                gt = jnp.where(other > blk, 1.0, 0.0)
                if 8 * rb > jp:
                    inc = ge
                elif 8 * rb + 7 < jp:
                    inc = gt
                else:
                    inc = jnp.where(jsub > jp - 8 * rb, ge, gt)
                ranks[rb] = ranks[rb] + inc
        rank = jnp.concatenate(ranks, axis=0)
        sel_ref[0, MAX_SLC * g:MAX_SLC * (g + 1), :] = jnp.where(rank < float(SLC_TOPN), 0.0, NEG_BIG).astype(BF16)


def _nsa_select(q_t, kc, vc_t, pbias, ovl_t):
    b, n_cmp_pad, _ = kc.shape
    m = q_t.shape[1]
    nq = m // b // T_SEL
    cols = lambda r: pl.BlockSpec((r, T_SEL), lambda i, j: (0, i * nq + j))
    per_b = lambda a: pl.BlockSpec((1,) + a.shape[1:], lambda i, j: (i, 0, 0))
    full = lambda a: pl.BlockSpec(a.shape, lambda i, j: (0,) * a.ndim)
    return pl.pallas_call(
        functools.partial(_nsa_select_kernel, n_cmp_pad=n_cmp_pad),
        grid=(b, nq),
        in_specs=[cols(D_NSA), per_b(kc), per_b(vc_t), full(pbias), full(ovl_t)],
        out_specs=[cols(D_NSA), pl.BlockSpec((1, NSA_KV_GROUPS * MAX_SLC, T_SEL), lambda i, j: (i, 0, j))],
        out_shape=[jax.ShapeDtypeStruct((D_NSA, m), F32),
                   jax.ShapeDtypeStruct((b, NSA_KV_GROUPS * MAX_SLC, m // b), BF16)],
        compiler_params=_params("parallel", "parallel"), name="nsa_select",
    )(q_t, kc, vc_t, pbias, ovl_t)


def _online_init(m_ref, acc_ref):
    m_ref[...] = jnp.full(m_ref.shape, NEG_BIG, F32)
    acc_ref[...] = jnp.zeros(acc_ref.shape, F32)


def _scores(k_tile, q_ref, q_rows, bias_ref=None, ok=None):
    s = _dot(k_tile, q_ref[0:q_rows, :])
    if bias_ref is not None:
        s = s + bias_ref[...]
    if ok is not None:
        s = jnp.where(ok, s, NEG_BIG)
    return s


def _online_step(m_ref, acc_ref, s, v_ones):
    m = m_ref[...]
    m_new = jnp.maximum(m, jnp.max(s, axis=0, keepdims=True))
    alpha = jnp.exp2(m - m_new)
    p = jnp.exp2(s - m_new)
    m_ref[...] = m_new
    acc_ref[...] = alpha * acc_ref[...] + _dot(v_ones, p.astype(BF16))


def _nsa_attend_kernel(qt_ref, sel_ref, ocmp_ref, gt_ref, ks_ref, kw_ref, vs_ref, vw_ref, bias_ref, onehot_ref,
                       o_ref, kaug_ref, qaug_ref, vs1_ref, vw1_ref, ms_ref, accs_ref, mw_ref, accw_ref):
    T = T_NSA
    j = pl.program_id(1)

    @pl.when(j == 0)
    def _():
        kaug_ref[:, 0:LANES] = ks_ref[0]
        kaug_ref[:, LANES:2 * LANES] = onehot_ref[...]
        qaug_ref[...] = jnp.zeros(qaug_ref.shape, BF16)
        for v1_ref, v_ref in ((vs1_ref, vs_ref), (vw1_ref, vw_ref)):
            v1_ref[0:LANES, :] = v_ref[...]
            v1_ref[LANES:LANES + ONES_ROWS, :] = jnp.ones((ONES_ROWS, v1_ref.shape[1]), BF16)

    for h in range(NSA_HEADS):
        g = h // NSA_HPG
        qaug_ref[HEAD_DIM * g:HEAD_DIM * (g + 1), h * T:(h + 1) * T] = qt_ref[HEAD_DIM * h:HEAD_DIM * (h + 1), :]
        qaug_ref[LANES:LANES + MAX_SLC, h * T:(h + 1) * T] = sel_ref[0, MAX_SLC * g:MAX_SLC * (g + 1), :]

    key = lax.broadcasted_iota(jnp.int32, (T, NSA_HEADS * T), 0)
    qry = lax.broadcasted_iota(jnp.int32, (T, NSA_HEADS * T), 1) & (T - 1)
    causal = key <= qry
    st_s = (ms_ref, accs_ref)
    st_w = (mw_ref, accw_ref)
    _online_init(*st_s)
    _online_init(*st_w)

    far_len = jnp.maximum(j - 1, 0) * T

    def far_step(r0, n_keys):
        s = _scores(kaug_ref[pl.ds(r0, n_keys), :], qaug_ref, 2 * LANES)
        _online_step(*st_s, s, vs1_ref[:, pl.ds(r0, n_keys)])

    def far_big(c, carry):
        r0 = pl.multiple_of(c * (2 * FAR_KEYS), 2 * FAR_KEYS)
        s_lo = _scores(kaug_ref[pl.ds(r0, FAR_KEYS), :], qaug_ref, 2 * LANES)
        s_hi = _scores(kaug_ref[pl.ds(r0 + FAR_KEYS, FAR_KEYS), :], qaug_ref, 2 * LANES)
        _online_step(*st_s, s_lo, vs1_ref[:, pl.ds(r0, FAR_KEYS)])
        _online_step(*st_s, s_hi, vs1_ref[:, pl.ds(r0 + FAR_KEYS, FAR_KEYS)])
        return carry

    n_big = lax.shift_right_logical(far_len, FAR_KEYS.bit_length())
    lax.fori_loop(0, n_big, far_big, 0)
    done = n_big * (2 * FAR_KEYS)

    @pl.when(far_len - done >= FAR_KEYS)
    def _():
        far_step(pl.multiple_of(done, FAR_KEYS), FAR_KEYS)

    done = done + jnp.where(far_len - done >= FAR_KEYS, FAR_KEYS, 0)

    def far_small(c, carry):
        far_step(pl.multiple_of(done + c * T, T), T)
        return carry

    lax.fori_loop(0, lax.shift_right_logical(far_len - done, T.bit_length() - 1), far_small, 0)

    n_win = WINDOW // T
    starts = [pl.multiple_of(jnp.maximum(j - d, 0) * T, T) for d in range(n_win + 1)]
    in_range = [key < jnp.where(j - d >= 0, T, -1) for d in range(n_win + 1)]
    sel_scores, win_scores = {}, {}
    for d in (1, 0):
        sel_scores[d] = _scores(kaug_ref[pl.ds(starts[d], T), :], qaug_ref, 2 * LANES, bias_ref.at[d],
                                causal if d == 0 else in_range[d])
    for d in range(n_win + 1):
        if d == 0:
            ok = causal
        elif d == n_win:
            ok = (key > qry) & in_range[d]
        else:
            ok = in_range[d]
        win_scores[d] = _scores(kw_ref[0, pl.ds(starts[d], T), :], qaug_ref, LANES,
                                bias_ref.at[d] if d <= 1 else None, ok)
    for d in (1, 0):
        _online_step(*st_s, sel_scores[d], vs1_ref[:, pl.ds(starts[d], T)])
    for d in range(n_win + 1):
        _online_step(*st_w, win_scores[d], vw1_ref[:, pl.ds(starts[d], T)])

    gates = jax.nn.sigmoid(gt_ref[...])
    heads = []
    for h in range(NSA_HEADS):
        cols = slice(h * T, (h + 1) * T)
        rows = slice(HEAD_DIM * (h // NSA_HPG), HEAD_DIM * (h // NSA_HPG + 1))
        heads.append(gates[3 * h:3 * h + 1, :] * ocmp_ref[HEAD_DIM * h:HEAD_DIM * (h + 1), :]
                     + (gates[3 * h + 1:3 * h + 2, :] / accs_ref[LANES:LANES + 1, cols]) * accs_ref[rows, cols]
                     + (gates[3 * h + 2:3 * h + 3, :] / accw_ref[LANES:LANES + 1, cols]) * accw_ref[rows, cols])
    o_ref[0] = jnp.concatenate(heads, axis=0).T.astype(o_ref.dtype)


def _nsa_attend(q_t, sel_t, ocmp_t, g_t, ks, kw, vs_t, vw_t, bias_t, onehot):
    b, s, _ = ks.shape
    T = T_NSA
    nq = s // T
    R = NSA_HEADS * T
    cols = lambda r: pl.BlockSpec((r, T), lambda i, j: (0, i * nq + j))
    row_b = lambda a: pl.BlockSpec((a.shape[0], s), lambda i, j: (0, i))
    per_b = lambda a: pl.BlockSpec((1,) + a.shape[1:], lambda i, j: (i, 0, 0))
    full = lambda a: pl.BlockSpec(a.shape, lambda i, j: (0,) * a.ndim)
    v_ones = pltpu.VMEM((LANES + ONES_ROWS, s), BF16)
    stat = [pltpu.VMEM((1, R), F32), pltpu.VMEM((LANES + ONES_ROWS, R), F32)]
    return pl.pallas_call(
        _nsa_attend_kernel,
        grid=(b, nq),
        in_specs=[cols(D_NSA), pl.BlockSpec((1, NSA_KV_GROUPS * MAX_SLC, T), lambda i, j: (i, 0, j)),
                  cols(D_NSA), cols(LANES), per_b(ks), per_b(kw), row_b(vs_t), row_b(vw_t),
                  full(bias_t), full(onehot)],
        out_specs=pl.BlockSpec((1, T, D_NSA), lambda i, j: (i, j, 0)),
        out_shape=jax.ShapeDtypeStruct((b, s, D_NSA), BF16),
        scratch_shapes=[pltpu.VMEM((s, 2 * LANES), BF16), pltpu.VMEM((2 * LANES, R), BF16), v_ones, v_ones]
        + stat + stat,
        compiler_params=_params("parallel", "arbitrary"), name="nsa_attend",
    )(q_t, sel_t, ocmp_t, g_t, ks, kw, vs_t, vw_t, bias_t, onehot)


def _sb_kernel(qt_ref, k_ref, vt_ref, tri_ref, o_ref, carry_ref, acc_ref):
    T = T_SB
    j = pl.program_id(1)
    n_pair = SB_HEADS // 2
    key = lax.broadcasted_iota(jnp.int32, (T, 2 * T), 0)
    qry = lax.broadcasted_iota(jnp.int32, (T, 2 * T), 1) & (T - 1)
    strict = key < qry
    carry_ref[...] = jnp.zeros(carry_ref.shape, F32)
    acc_ref[...] = jnp.zeros(acc_ref.shape, F32)

    def tiles(cs, masks):
        starts = [pl.multiple_of(c * T, T) for c in cs]
        units = [(ci, p) for ci in range(len(cs)) for p in range(n_pair)]
        z2s, incls = [], []
        for c, p in units:
            q_even = qt_ref[HEAD_DIM * (2 * p):HEAD_DIM * (2 * p + 1), :]
            q_odd = qt_ref[HEAD_DIM * (2 * p + 1):HEAD_DIM * (2 * p + 2), :]
            zq = jnp.zeros_like(q_even)
            q_pair = jnp.concatenate([jnp.concatenate([q_even, zq], axis=0),
                                      jnp.concatenate([zq, q_odd], axis=0)], axis=1)
            z2s.append(_dot(k_ref[0, pl.ds(starts[c], T), LANES * p:LANES * (p + 1)], q_pair))
        for (c, p), z2 in zip(units, z2s):
            rest = jnp.maximum(z2, 0.0) + jnp.log2(1.0 + jnp.exp2(-jnp.abs(z2)))
            if masks[c] is not None:
                rest = jnp.where(masks[c], rest, 0.0)
            hi, lo = _split_bf16(rest)
            incls.append(_dot(tri_ref[...], jnp.concatenate([hi, lo], axis=0)))
        carry = [carry_ref[p] for p in range(n_pair)]
        acc = [acc_ref[p] for p in range(n_pair)]
        for (c, p), z2, incl in zip(units, z2s, incls):
            a = jnp.exp2(z2 - incl - carry[p])
            if masks[c] is not None:
                a = jnp.where(masks[c], a, 0.0)
            acc[p] = acc[p] + _dot(vt_ref[LANES * p:LANES * (p + 1), pl.ds(starts[c], T)], a.astype(BF16))
            carry[p] = carry[p] + incl[0:1, :]
        for p in range(n_pair):
            carry_ref[p] = carry[p]
            acc_ref[p] = acc[p]

    @pl.when(j == 0)
    def _():
        tiles([0], [strict])

    @pl.when(j > 0)
    def _():
        tiles([j, j - 1], [strict, None])

    def live(c):
        return (c >= 0) & (jnp.min(carry_ref[...]) < SB_DEAD_BITS)

    def back_one(c):
        tiles([c], [None])
        return c - 1

    lax.while_loop(live, back_one, j - 2)
    row = lax.broadcasted_iota(jnp.int32, (LANES, T), 0)
    for p in range(n_pair):
        acc = acc_ref[p]
        o_ref[0, :, LANES * p:LANES * (p + 1)] = (
            jnp.where(row < HEAD_DIM, acc[:, 0:T], acc[:, T:2 * T]).T.astype(o_ref.dtype))


def _sb_attention(q_t, k, v_t, tri_t):
    b, s, _ = k.shape
    T = T_SB
    nq = s // T
    n_pair = SB_HEADS // 2
    return pl.pallas_call(
        _sb_kernel,
        grid=(b, nq),
        in_specs=[pl.BlockSpec((D_SB, T), lambda i, j: (0, i * nq + j)),
                  pl.BlockSpec((1, s, D_SB), lambda i, j: (i, 0, 0)),
                  pl.BlockSpec((D_SB, s), lambda i, j: (0, i)),
                  pl.BlockSpec(tri_t.shape, lambda i, j: (0, 0))],
        out_specs=pl.BlockSpec((1, T, D_SB), lambda i, j: (i, j, 0)),
        out_shape=jax.ShapeDtypeStruct((b, s, D_SB), BF16),
        scratch_shapes=[pltpu.VMEM((n_pair, 1, 2 * T), F32), pltpu.VMEM((n_pair, LANES, 2 * T), F32)],
        compiler_params=_params("parallel", "parallel"), name="sb_attention",
    )(q_t, k, v_t, tri_t)


def _hgrn_kernel(q_ref, f_ref, i_ref, lb_ref, nw_ref, ones_ref, bd_ref, o_ref,
                 st_ref, qj_ref, kj_ref, fj_ref, vj_ref, qd_ref, kd_ref, oc_ref, x_ref, u_ref, sb_ref, dec_ref,
                 att_ref, oall_ref):
    C = HG_SUB
    n_blk = T_HG // C

    @pl.when(pl.program_id(2) == 0)
    def _():
        st_ref[...] = jnp.zeros(st_ref.shape, F32)

    lb = lb_ref[...]
    lb_floor = jnp.maximum(lb, LB_FLOOR)
    one_m_lb = 1.0 - lb
    cum = None
    for j in range(C):
        xj = f_ref[0, pl.ds(j, n_blk, stride=C), :]
        e = jnp.exp(-jnp.abs(xj))
        r = 1.0 / (1.0 + e)
        er = e * r
        pos = xj >= 0.0
        fj = lb_floor + one_m_lb * jnp.where(pos, r, er)
        kj_ref[j] = one_m_lb * jnp.where(pos, er, r)
        fj_ref[j] = fj
        cum = fj if cum is None else cum * fj
        qj = q_ref[0, pl.ds(j, n_blk, stride=C), :]
        qj_ref[j] = qj
        vj_ref[j] = i_ref[0, pl.ds(j, n_blk, stride=C), :]
        qd_ref[pl.ds(j, n_blk, stride=HG_PITCH), :] = qj * cum
    dec_ref[...] = cum
    tail = None
    for j in reversed(range(C)):
        kd_ref[pl.ds(j, n_blk, stride=HG_PITCH), :] = kj_ref[j] if tail is None else kj_ref[j] * tail
        tail = fj_ref[j] if tail is None else tail * fj_ref[j]

    def kv_products(blk, carry):
        kd = kd_ref[pl.ds(pl.multiple_of(blk * HG_PITCH, 8), C), :].astype(BF16)
        vv = i_ref[0, pl.ds(pl.multiple_of(blk * C, C), C), :].astype(BF16)
        u_ref[blk] = bd_ref[...] * _dot_tn(vv, kd)
        return carry

    lax.fori_loop(0, n_blk, kv_products, 0, unroll=HG_UNROLL)

    def scan(blk, st):
        sb_ref[blk] = st.astype(BF16)
        return st * dec_ref[pl.ds(blk, 1), :] + u_ref[blk]

    st_ref[...] = lax.fori_loop(0, n_blk, scan, st_ref[...], unroll=HG_UNROLL)

    def outputs(blk, carry):
        r0 = pl.multiple_of(blk * HG_PITCH, 8)
        oc_ref[pl.ds(r0, C), :] = _dot_nt(qd_ref[pl.ds(r0, C), :].astype(BF16), sb_ref[blk])
        return carry

    lax.fori_loop(0, n_blk, outputs, 0, unroll=HG_UNROLL)

    base = lambda j: (j * (j + 1) // 2) * n_blk
    for j in range(C):
        qdec = qj_ref[j]
        for jp in reversed(range(j + 1)):
            x_ref[base(j) + jp * n_blk:base(j) + (jp + 1) * n_blk, :] = (qdec * kj_ref[jp]).astype(BF16)
            if jp > 0:
                qdec = qdec * fj_ref[jp]
        att_ref[base(j):base(j + 1), :] = _dot(x_ref[base(j):base(j + 1), :], ones_ref[...])
    for j in range(C):
        oj = oc_ref[pl.ds(j, n_blk, stride=HG_PITCH), :]
        for jp in range(j + 1):
            oj = oj + att_ref[base(j) + jp * n_blk:base(j) + (jp + 1) * n_blk, :] * vj_ref[jp]
        oall_ref[j * n_blk:(j + 1) * n_blk, :] = oj
    o_all = oall_ref[...]
    hi, lo = _split_bf16(o_all * o_all)
    ms = (_dot(hi, ones_ref[...]) + _dot(lo, ones_ref[...])) * (1.0 / HEAD_DIM)
    oall_ref[...] = o_all * lax.rsqrt(ms + RMS_EPS) * nw_ref[...]
    for j in range(C):
        o_ref[0, pl.ds(j, n_blk, stride=C), :] = oall_ref[j * n_blk:(j + 1) * n_blk, :]


def _hgrn(q, f, i, lb, nw, ones_bd, bd_mask):
    b, s, _ = q.shape
    n_blk = T_HG // HG_SUB
    tile = pl.BlockSpec((1, T_HG, LANES), lambda bi, pi, ti: (bi, ti, pi))
    vec = pl.BlockSpec((1, LANES), lambda bi, pi, ti: (0, pi))
    full = lambda a: pl.BlockSpec(a.shape, lambda bi, pi, ti: (0,) * a.ndim)
    jm = pltpu.VMEM((HG_SUB, n_blk, LANES), F32)
    nat = pltpu.VMEM((n_blk * HG_PITCH, LANES), F32)
    return pl.pallas_call(
        _hgrn_kernel,
        grid=(b, D_HG // LANES, s // T_HG),
        in_specs=[tile, tile, tile, vec, vec, full(ones_bd), full(bd_mask)],
        out_specs=tile,
        out_shape=jax.ShapeDtypeStruct((b, s, D_HG), F32),
        scratch_shapes=[pltpu.VMEM((LANES, LANES), F32), jm, jm, jm, jm, nat, nat, nat,
                        pltpu.VMEM((HG_SUB * (HG_SUB + 1) // 2 * n_blk, LANES), BF16),
                        pltpu.VMEM((n_blk, LANES, LANES), F32),
                        pltpu.VMEM((n_blk, LANES, LANES), BF16), pltpu.VMEM((n_blk, LANES), F32),
                        pltpu.VMEM((HG_SUB * (HG_SUB + 1) // 2 * n_blk, LANES), F32), pltpu.VMEM((T_HG, LANES), F32)],
        compiler_params=_params("parallel", "parallel", "arbitrary"), name="hgrn2",
    )(q, f, i, lb, nw, ones_bd, bd_mask)


def _mix_norm(onsa_ref, osb_ref, ohg_ref, z_ref, x_ref, w_ref, g_ref, b_ref):
    z = z_ref[...].astype(F32)
    sz = z * jax.nn.sigmoid(z)
    mixed = jnp.concatenate([(onsa_ref[...] * sz[:, 0:D_NSA]).astype(BF16),
                             (osb_ref[...] * sz[:, D_NSA:D_NSA + D_SB]).astype(BF16),
                             (ohg_ref[...] * sz[:, D_NSA + D_SB:D_MIX]).astype(BF16)], axis=1)
    v = ALPHA * x_ref[...] + _dot(mixed, w_ref[...])
    mu = jnp.mean(v, axis=-1, keepdims=True)
    vc = v - mu
    var = jnp.mean(vc * vc, axis=-1, keepdims=True)
    return vc * lax.rsqrt(var + LN_EPS) * g_ref[...] + b_ref[...]


def _out_kernel(*refs):
    refs[8][...] = _mix_norm(*refs[:8])


def _out_in_kernel(*refs, n_nat):
    x_new = _mix_norm(*refs[:8])
    refs[10][...] = x_new
    _project(x_new, refs[8], refs[9], refs[11:], n_nat)


def _out_proj(o_nsa, o_sb, o_hg, z_all, x2d, w, g, bvec, next_proj=None):
    m = x2d.shape[0]
    rows = lambda n: pl.BlockSpec((T_PROJ, n), lambda i: (i, 0))
    full = lambda a: pl.BlockSpec(a.shape, lambda i: (0, 0))
    in_specs = [rows(D_NSA), rows(D_SB), rows(D_HG), rows(D_MIX), rows(D_MODEL), full(w), full(g), full(bvec)]
    x_shape = jax.ShapeDtypeStruct((m, D_MODEL), F32)
    if next_proj is None:
        return pl.pallas_call(
            _out_kernel, grid=(m // T_PROJ,), in_specs=in_specs, out_specs=rows(D_MODEL), out_shape=x_shape,
            compiler_params=_params("parallel"), name="out_proj_norm",
        )(o_nsa, o_sb, o_hg, z_all, x2d, w, g, bvec)
    w_nat = next_proj[0]
    wp, w_specs, out_specs, out_shape = _proj_specs(m, *next_proj)
    outs = pl.pallas_call(
        functools.partial(_out_in_kernel, n_nat=len(w_nat)), grid=(m // T_PROJ,),
        in_specs=in_specs + w_specs, out_specs=[rows(D_MODEL)] + out_specs, out_shape=[x_shape] + out_shape,
        compiler_params=_params("parallel"), name="out_proj_norm_inproj",
    )(o_nsa, o_sb, o_hg, z_all, x2d, w, g, bvec, *wp)
    return outs[0], outs[1:1 + len(w_nat)], outs[1 + len(w_nat):]


def _t5_bucket_np(rel):
    n = np.maximum(rel, 0)
    max_exact = NUM_BUCKETS // 2
    large = max_exact + (np.log(np.maximum(n, 1).astype(np.float32) / max_exact)
                         / math.log(MAX_DISTANCE / max_exact) * (NUM_BUCKETS - max_exact)).astype(np.int32)
    large = np.clip(large, 0, NUM_BUCKETS - 1)
    return np.where(n < max_exact, n, large).astype(np.int32)


def _bias_tables(rel_bias, s):
    tbl = ((rel_bias - rel_bias[NUM_BUCKETS - 1]) * LOG2E).astype(F32)

    def expand(rel):
        onehot = (jnp.arange(NUM_BUCKETS)[:, None] == jnp.asarray(_t5_bucket_np(rel).reshape(1, -1))).astype(F32)
        return jnp.dot(tbl.T, onehot, precision=lax.Precision.HIGHEST).reshape((NSA_HEADS,) + rel.shape)

    n_cmp_pad = s // CMP_STRIDE
    n_rel = np.arange(2 * n_cmp_pad)[:, None] - n_cmp_pad
    r = np.arange(T_SEL)[None, :]
    pbias = expand(r - CMP_STRIDE * n_rel - (CMP_BLOCK - 1))
    T = T_NSA
    key = np.arange(T)[:, None]
    qry = np.arange(T)[None, :]
    near = np.stack([qry - key, T + qry - key])
    bias_t = jnp.transpose(expand(near), (1, 2, 0, 3)).reshape(2, T, NSA_HEADS * T)
    return pbias, bias_t


def _static_tables(s):
    n_chunk = s // CMP_STRIDE
    cmp_start = np.arange(n_chunk) * CMP_STRIDE
    slc_start = np.arange(MAX_SLC) * SLC_BLOCK
    ovl_t = ((cmp_start[None, :] < slc_start[:, None] + SLC_BLOCK)
             & (cmp_start[None, :] + CMP_BLOCK > slc_start[:, None])
             & (cmp_start[None, :] + CMP_BLOCK <= s)).astype(np.float32)
    onehot = (np.arange(s)[:, None] // SLC_BLOCK == np.arange(LANES)[None, :]).astype(np.float32)
    tri = (np.arange(T_SB)[None, :] >= np.arange(T_SB)[:, None]).astype(np.float32)
    tri = np.concatenate([tri, tri], axis=1)
    ones_bd = np.kron(np.eye(2), np.ones((HEAD_DIM, HEAD_DIM))).astype(np.float32)
    as_bf16 = lambda a: jnp.asarray(a, dtype=BF16)
    return as_bf16(ovl_t), as_bf16(onehot), as_bf16(tri), as_bf16(ones_bd), jnp.asarray(ones_bd)


def _layer_weights(w_in_l, cmp_pos_l, w_ck1_l, w_ck2_l, w_cv1_l, w_cv2_l):
    offs = np.cumsum((0,) + SPLIT_SIZES)
    (w_q, w_kc, w_vc, w_ks, w_vs, w_kw, w_vw, w_g, w_nz,
     w_sq, w_sk, w_sv, w_sz, w_hq, w_hf, w_hi, w_hz) = [w_in_l[:, offs[i]:offs[i + 1]] for i in range(len(SPLIT_SIZES))]
    w_gp = jnp.concatenate([w_g, jnp.zeros((D_MODEL, LANES - N_GATES), F32)], axis=1)
    w_z = jnp.concatenate([w_nz, w_sz, w_hz], axis=1)
    w_nat = [w_kc, w_vc, w_ks, w_kw, w_z, w_sk, w_hq, w_hf, w_hi]
    dt_nat = [F32, F32, BF16, BF16, BF16, BF16, F32, F32, F32]
    w_tr = [(w_q * QK_SCALE2).T, w_vs.T, w_vw.T, w_gp.T, (w_sq * QK_SCALE2).T, w_sv.T]
    dt_tr = [BF16, BF16, BF16, F32, BF16, BF16]
    w_nat = [w.astype(BF16) for w in w_nat]
    w_tr = [w.astype(BF16) for w in w_tr]

    def block_diag(w):
        z = jnp.zeros_like(w)
        return jnp.concatenate([jnp.concatenate([w, z], axis=-1), jnp.concatenate([z, w], axis=-1)], axis=-2)

    pos2 = jnp.concatenate([cmp_pos_l, cmp_pos_l], axis=1)
    w1k = block_diag(w_ck1_l.reshape(CMP_BLOCK, HEAD_DIM, CMP_HIDDEN)).astype(BF16)
    w1v = block_diag(w_cv1_l.reshape(CMP_BLOCK, HEAD_DIM, CMP_HIDDEN)).astype(BF16)
    w2k = block_diag(w_ck2_l).astype(BF16)
    w2v_t = block_diag(w_cv2_l).T.astype(BF16)
    return (w_nat, dt_nat, w_tr, dt_tr), (pos2, w1k, w2k, w1v, w2v_t)


def kernel(x, w_in, cmp_pos, w_ck1, w_ck2, w_cv1, w_cv2, hg_lb, hg_norm_w, w_out, ln_g, ln_b, rel_bias):
    b, s, d = x.shape
    assert d == D_MODEL and s % T_HG == 0 and s // SLC_BLOCK <= MAX_SLC and s >= WINDOW + T_NSA
    lb_w = jax.nn.softmax(hg_lb.astype(F32), axis=0)
    lb_all = jnp.cumsum(lb_w, axis=0) - lb_w[0]
    pbias, bias_t = _bias_tables(rel_bias, s)
    ovl_t, onehot, tri, ones_bd, bd_mask = _static_tables(s)

    x2d = x.reshape(b * s, d)
    weights = [_layer_weights(w_in[l], cmp_pos[l], w_ck1[l], w_ck2[l], w_cv1[l], w_cv2[l]) for l in range(DEPTH)]
    nat, trs = _inproj(x2d, *weights[0][0])
    for l in range(DEPTH):
        q_t, vs_t, vw_t, g_t, sq_t, sv_t = trs
        kc_src, vc_src, ks, kw, z_all, sk, hq, hf, hi = [o.reshape(b, s, o.shape[-1]) for o in nat]
        kc, vc_t = _compress(kc_src, vc_src, *weights[l][1])
        ocmp_t, sel_t = _nsa_select(q_t, kc, vc_t, pbias, ovl_t)
        o_nsa = _nsa_attend(q_t, sel_t, ocmp_t, g_t, ks, kw, vs_t, vw_t, bias_t, onehot)
        o_sb = _sb_attention(sq_t, sk, sv_t, tri)
        o_hg = _hgrn(hq, hf, hi, lb_all[l][None, :], hg_norm_w[l][None, :], ones_bd, bd_mask)
        res = _out_proj(o_nsa.reshape(b * s, D_NSA), o_sb.reshape(b * s, D_SB), o_hg.reshape(b * s, D_HG),
                        z_all.reshape(b * s, D_MIX), x2d, w_out[l].astype(BF16),
                        ln_g[l][None, :], ln_b[l][None, :],
                        next_proj=weights[l + 1][0] if l + 1 < DEPTH else None)
        if l + 1 < DEPTH:
            x2d, nat, trs = res
        else:
            x2d = res
    return x2d.reshape(b, s, d)
```

```python
import functools
import math

import numpy as np
import jax
import jax.numpy as jnp
from jax import lax
from jax.experimental import pallas as pl
from jax.experimental.pallas import tpu as pltpu

F32 = jnp.float32
BF16 = jnp.bfloat16

D_MODEL = 1024
DEPTH = 2
HEAD_DIM = 64
LANES = 128
NSA_HEADS = 6
NSA_KV_GROUPS = 2
NSA_HPG = NSA_HEADS // NSA_KV_GROUPS
CMP_BLOCK = 32
CMP_STRIDE = 16
CMP_HIDDEN = 2 * HEAD_DIM
SLC_BLOCK = 64
SLC_TOPN = 16
MAX_SLC = 64
WINDOW = 512
FORCE_BONUS = 1000.0
NEG_BIG = -1e30
LB_FLOOR = 1e-30
SB_HEADS = 4
HG_HEADS = 6
HG_SUB = 16
HG_PITCH = 24
NUM_BUCKETS = 32
MAX_DISTANCE = 128
D_NSA = NSA_HEADS * HEAD_DIM
D_KV = NSA_KV_GROUPS * HEAD_DIM
D_SB = SB_HEADS * HEAD_DIM
D_HG = HG_HEADS * HEAD_DIM
D_MIX = D_NSA + D_SB + D_HG
N_GATES = NSA_HEADS * 3
SPLIT_SIZES = (D_NSA, D_KV, D_KV, D_KV, D_KV, D_KV, D_KV, N_GATES, D_NSA,
               D_SB, D_SB, D_SB, D_SB, D_HG, D_HG, D_HG, D_HG)
ALPHA = (2 * DEPTH) ** 0.25
LN_EPS = 1e-5
RMS_EPS = 1e-6
LOG2E = math.log2(math.e)
QK_SCALE2 = LOG2E / math.sqrt(HEAD_DIM)

T_SEL = 128
T_NSA = 256
FAR_KEYS = 512
ONES_ROWS = 16
T_SB = 256
SB_DEAD_BITS = 151.0
T_HG = 1024
HG_UNROLL = T_HG // HG_SUB
T_PROJ = 512
VMEM_LIMIT = 56 * 1024 * 1024

_NT = (((1,), (1,)), ((), ()))
_TN = (((0,), (0,)), ((), ()))


def _dot(a, b):
    return jnp.dot(a, b, preferred_element_type=F32)


def _dot_nt(a, b):
    return lax.dot_general(a, b, _NT, preferred_element_type=F32)


def _dot_tn(a, b):
    return lax.dot_general(a, b, _TN, preferred_element_type=F32)


def _split_bf16(x):
    hi = x.astype(BF16)
    lo = (x - hi.astype(F32)).astype(BF16)
    return hi, lo


def _params(*sem):
    return pltpu.CompilerParams(dimension_semantics=sem, vmem_limit_bytes=VMEM_LIMIT)


def _project(x, wn_ref, wt_ref, o_refs, n_nat):
    xb = x.astype(BF16)
    nat = _dot(xb, wn_ref[...])
    off = 0
    for o_ref in o_refs[:n_nat]:
        o_ref[...] = nat[:, off:off + o_ref.shape[1]].astype(o_ref.dtype)
        off += o_ref.shape[1]
    tr = _dot_nt(wt_ref[...], xb)
    off = 0
    for o_ref in o_refs[n_nat:]:
        o_ref[...] = tr[off:off + o_ref.shape[0], :].astype(o_ref.dtype)
        off += o_ref.shape[0]


def _inproj_kernel(x_ref, wn_ref, wt_ref, *o_refs, n_nat):
    _project(x_ref[...], wn_ref, wt_ref, o_refs, n_nat)


def _proj_specs(m, w_nat, dt_nat, w_tr, dt_tr):
    wn = jnp.concatenate(w_nat, axis=1)
    wt = jnp.concatenate(w_tr, axis=0)
    w_specs = [pl.BlockSpec(wn.shape, lambda i: (0, 0)), pl.BlockSpec(wt.shape, lambda i: (0, 0))]
    out_specs = [pl.BlockSpec((T_PROJ, w.shape[1]), lambda i: (i, 0)) for w in w_nat]
    out_specs += [pl.BlockSpec((w.shape[0], T_PROJ), lambda i: (0, i)) for w in w_tr]
    out_shape = [jax.ShapeDtypeStruct((m, w.shape[1]), dt) for w, dt in zip(w_nat, dt_nat)]
    out_shape += [jax.ShapeDtypeStruct((w.shape[0], m), dt) for w, dt in zip(w_tr, dt_tr)]
    return (wn, wt), w_specs, out_specs, out_shape


def _inproj(x2d, w_nat, dt_nat, w_tr, dt_tr):
    m = x2d.shape[0]
    w, w_specs, out_specs, out_shape = _proj_specs(m, w_nat, dt_nat, w_tr, dt_tr)
    outs = pl.pallas_call(
        functools.partial(_inproj_kernel, n_nat=len(w_nat)),
        grid=(m // T_PROJ,),
        in_specs=[pl.BlockSpec((T_PROJ, D_MODEL), lambda i: (i, 0))] + w_specs,
        out_specs=out_specs, out_shape=out_shape,
        compiler_params=_params("parallel"), name="inproj",
    )(x2d, *w)
    return outs[:len(w_nat)], outs[len(w_nat):]


def _compress_kernel(ksrc_ref, vsrc_ref, pos_ref, w1k_ref, w2k_ref, w1v_ref, w2v_ref,
                     kc_ref, vc_ref, *, n_chunk):
    def hidden(src_ref, w1_ref):
        top = jnp.zeros((n_chunk, 2 * CMP_HIDDEN), F32)
        bot = jnp.zeros((n_chunk, 2 * CMP_HIDDEN), F32)
        for p in range(CMP_STRIDE):
            xp = src_ref[0, pl.ds(p, n_chunk, stride=CMP_STRIDE), :]
            top += _dot((xp + pos_ref[p:p + 1, :]).astype(BF16), w1_ref[p])
            q = CMP_STRIDE + p
            bot += _dot((xp + pos_ref[q:q + 1, :]).astype(BF16), w1_ref[q])
        hid = top + pltpu.roll(bot, n_chunk - 1, 0)
        return jax.nn.gelu(hid).astype(BF16)

    kc_ref[0] = _dot(hidden(ksrc_ref, w1k_ref), w2k_ref[...]).astype(kc_ref.dtype)
    vc_ref[0] = _dot_nt(w2v_ref[...], hidden(vsrc_ref, w1v_ref)).astype(vc_ref.dtype)


def _compress(kc_src, vc_src, pos2, w1k, w2k, w1v, w2v_t):
    b, s, _ = kc_src.shape
    n_chunk = s // CMP_STRIDE
    full = lambda a: pl.BlockSpec(a.shape, lambda i: (0,) * a.ndim)
    src = pl.BlockSpec((1, s, D_KV), lambda i: (i, 0, 0))
    return pl.pallas_call(
        functools.partial(_compress_kernel, n_chunk=n_chunk),
        grid=(b,),
        in_specs=[src, src, full(pos2), full(w1k), full(w2k), full(w1v), full(w2v_t)],
        out_specs=[pl.BlockSpec((1, n_chunk, D_KV), lambda i: (i, 0, 0)),
                   pl.BlockSpec((1, D_KV, n_chunk), lambda i: (i, 0, 0))],
        out_shape=[jax.ShapeDtypeStruct((b, n_chunk, D_KV), BF16),
                   jax.ShapeDtypeStruct((b, D_KV, n_chunk), BF16)],
        compiler_params=_params("parallel"), name="nsa_compress",
    )(kc_src, vc_src, pos2, w1k, w2k, w1v, w2v_t)


def _nsa_select_kernel(qt_ref, kc_ref, vct_ref, pbias_ref, ovl_ref, ocmp_ref, sel_ref, *, n_cmp_pad):
    T = T_SEL
    j = pl.program_id(1)
    t0 = j * T
    n_idx = lax.broadcasted_iota(jnp.int32, (n_cmp_pad, T), 0)
    tok_c = t0 + lax.broadcasted_iota(jnp.int32, (n_cmp_pad, T), 1)
    mask_c = tok_c >= CMP_STRIDE * n_idx + (CMP_BLOCK - 1)
    off = pl.multiple_of(n_cmp_pad - (T // CMP_STRIDE) * j, 8)
    psums = [None] * NSA_KV_GROUPS
    for h in range(NSA_HEADS):
        g = h // NSA_HPG
        qh = qt_ref[HEAD_DIM * h:HEAD_DIM * (h + 1), :]
        zq = jnp.zeros_like(qh)
        qh = jnp.concatenate([qh, zq] if g == 0 else [zq, qh], axis=0)
        s = _dot(kc_ref[0], qh) + pbias_ref[h, pl.ds(off, n_cmp_pad), :]
        s = jnp.where(mask_c, s, NEG_BIG)
        m = jnp.max(s, axis=0, keepdims=True)
        p = jnp.where(mask_c, jnp.exp2(s - m), 0.0)
        l = jnp.sum(p, axis=0, keepdims=True)
        p = p / jnp.where(l > 0.0, l, 1.0)
        o_both = _dot(vct_ref[0], p.astype(BF16))
        ocmp_ref[HEAD_DIM * h:HEAD_DIM * (h + 1), :] = o_both[HEAD_DIM * g:HEAD_DIM * (g + 1)]
        psums[g] = p if psums[g] is None else psums[g] + p

    jblk = lax.broadcasted_iota(jnp.int32, (MAX_SLC, T), 0)
    tok = t0 + lax.broadcasted_iota(jnp.int32, (MAX_SLC, T), 1)
    cur = lax.shift_right_logical(tok, 6)
    forced = (jblk == 0) | (jblk == cur) | (jblk == cur - 1)
    valid = jblk * SLC_BLOCK <= tok
    jsub = lax.broadcasted_iota(jnp.int32, (8, T), 0)
    for g in range(NSA_KV_GROUPS):
        imp = _dot(ovl_ref[...], psums[g].astype(BF16))
        score = jnp.where(valid, imp + jnp.where(forced, FORCE_BONUS, 0.0), NEG_BIG)
        blocks = [score[8 * rb:8 * rb + 8] for rb in range(MAX_SLC // 8)]
        ranks = [jnp.zeros((8, T), F32) for _ in blocks]
        for jp in range(MAX_SLC):
            other = score[jp:jp + 1, :]
            for rb, blk in enumerate(blocks):
                ge = jnp.where(other >= blk, 1.0, 0.0)
                gt = jnp.where(other > blk, 1.0, 0.0)
                if 8 * rb > jp:
                    inc = ge
                elif 8 * rb + 7 < jp:
                    inc = gt
                else:
                    inc = jnp.where(jsub > jp - 8 * rb, ge, gt)
                ranks[rb] = ranks[rb] + inc
        rank = jnp.concatenate(ranks, axis=0)
        sel_ref[0, MAX_SLC * g:MAX_SLC * (g + 1), :] = jnp.where(rank < float(SLC_TOPN), 0.0, NEG_BIG).astype(BF16)


def _nsa_select(q_t, kc, vc_t, pbias, ovl_t):
    b, n_cmp_pad, _ = kc.shape
    m = q_t.shape[1]
    nq = m // b // T_SEL
    cols = lambda r: pl.BlockSpec((r, T_SEL), lambda i, j: (0, i * nq + j))
    per_b = lambda a: pl.BlockSpec((1,) + a.shape[1:], lambda i, j: (i, 0, 0))
    full = lambda a: pl.BlockSpec(a.shape, lambda i, j: (0,) * a.ndim)
    return pl.pallas_call(
        functools.partial(_nsa_select_kernel, n_cmp_pad=n_cmp_pad),
        grid=(b, nq),
        in_specs=[cols(D_NSA), per_b(kc), per_b(vc_t), full(pbias), full(ovl_t)],
        out_specs=[cols(D_NSA), pl.BlockSpec((1, NSA_KV_GROUPS * MAX_SLC, T_SEL), lambda i, j: (i, 0, j))],
        out_shape=[jax.ShapeDtypeStruct((D_NSA, m), F32),
                   jax.ShapeDtypeStruct((b, NSA_KV_GROUPS * MAX_SLC, m // b), BF16)],
        compiler_params=_params("parallel", "parallel"), name="nsa_select",
    )(q_t, kc, vc_t, pbias, ovl_t)


def _online_init(m_ref, acc_ref):
    m_ref[...] = jnp.full(m_ref.shape, NEG_BIG, F32)
    acc_ref[...] = jnp.zeros(acc_ref.shape, F32)


def _scores(k_tile, q_ref, q_rows, bias_ref=None, ok=None):
    s = _dot(k_tile, q_ref[0:q_rows, :])
    if bias_ref is not None:
        s = s + bias_ref[...]
    if ok is not None:
        s = jnp.where(ok, s, NEG_BIG)
    return s


def _online_step(m_ref, acc_ref, s, v_ones):
    m = m_ref[...]
    m_new = jnp.maximum(m, jnp.max(s, axis=0, keepdims=True))
    alpha = jnp.exp2(m - m_new)
    p = jnp.exp2(s - m_new).astype(BF16)
    m_ref[...] = m_new
    half = p.shape[1] // NSA_KV_GROUPS
    for g in range(NSA_KV_GROUPS):
        cols = slice(g * half, (g + 1) * half)
        acc_ref[g] = alpha[:, cols] * acc_ref[g] + _dot(v_ones[g], p[:, cols])


def _nsa_attend_kernel(qt_ref, sel_ref, ocmp_ref, gt_ref, ks_ref, kw_ref, vs_ref, vw_ref, bias_ref, onehot_ref,
                       o_ref, kaug_ref, qaug_ref, vs1_ref, vw1_ref, ms_ref, accs_ref, mw_ref, accw_ref):
    T = T_NSA
    j = pl.program_id(1)

    @pl.when(j == 0)
    def _():
        kaug_ref[:, 0:LANES] = ks_ref[0]
        kaug_ref[:, LANES:2 * LANES] = onehot_ref[...]
        qaug_ref[...] = jnp.zeros(qaug_ref.shape, BF16)
        for v1_ref, v_ref in ((vs1_ref, vs_ref), (vw1_ref, vw_ref)):
            for g in range(NSA_KV_GROUPS):
                v1_ref[g, 0:HEAD_DIM, :] = v_ref[HEAD_DIM * g:HEAD_DIM * (g + 1), :]
                v1_ref[g, HEAD_DIM:HEAD_DIM + ONES_ROWS, :] = jnp.ones((ONES_ROWS, v1_ref.shape[2]), BF16)

    for h in range(NSA_HEADS):
        g = h // NSA_HPG
        qaug_ref[HEAD_DIM * g:HEAD_DIM * (g + 1), h * T:(h + 1) * T] = qt_ref[HEAD_DIM * h:HEAD_DIM * (h + 1), :]
        qaug_ref[LANES:LANES + MAX_SLC, h * T:(h + 1) * T] = sel_ref[0, MAX_SLC * g:MAX_SLC * (g + 1), :]

    key = lax.broadcasted_iota(jnp.int32, (T, NSA_HEADS * T), 0)
    qry = lax.broadcasted_iota(jnp.int32, (T, NSA_HEADS * T), 1) & (T - 1)
    causal = key <= qry
    st_s = (ms_ref, accs_ref)
    st_w = (mw_ref, accw_ref)
    _online_init(*st_s)
    _online_init(*st_w)

    far_len = jnp.maximum(j - 1, 0) * T

    def far_step(r0, n_keys):
        s = _scores(kaug_ref[pl.ds(r0, n_keys), :], qaug_ref, 2 * LANES)
        _online_step(*st_s, s, vs1_ref.at[:, :, pl.ds(r0, n_keys)])

    def far_big(c, carry):
        r0 = pl.multiple_of(c * (2 * FAR_KEYS), 2 * FAR_KEYS)
        s_lo = _scores(kaug_ref[pl.ds(r0, FAR_KEYS), :], qaug_ref, 2 * LANES)
        s_hi = _scores(kaug_ref[pl.ds(r0 + FAR_KEYS, FAR_KEYS), :], qaug_ref, 2 * LANES)
        _online_step(*st_s, s_lo, vs1_ref.at[:, :, pl.ds(r0, FAR_KEYS)])
        _online_step(*st_s, s_hi, vs1_ref.at[:, :, pl.ds(r0 + FAR_KEYS, FAR_KEYS)])
        return carry

    n_big = lax.shift_right_logical(far_len, FAR_KEYS.bit_length())
    lax.fori_loop(0, n_big, far_big, 0)
    done = n_big * (2 * FAR_KEYS)

    @pl.when(far_len - done >= FAR_KEYS)
    def _():
        far_step(pl.multiple_of(done, FAR_KEYS), FAR_KEYS)

    done = done + jnp.where(far_len - done >= FAR_KEYS, FAR_KEYS, 0)

    def far_small(c, carry):
        far_step(pl.multiple_of(done + c * T, T), T)
        return carry

    lax.fori_loop(0, lax.shift_right_logical(far_len - done, T.bit_length() - 1), far_small, 0)

    n_win = WINDOW // T
    starts = [pl.multiple_of(jnp.maximum(j - d, 0) * T, T) for d in range(n_win + 1)]
    in_range = [key < jnp.where(j - d >= 0, T, -1) for d in range(n_win + 1)]
    sel_scores, win_scores = {}, {}
    for d in (1, 0):
        sel_scores[d] = _scores(kaug_ref[pl.ds(starts[d], T), :], qaug_ref, 2 * LANES, bias_ref.at[d],
                                causal if d == 0 else in_range[d])
    for d in range(n_win + 1):
        if d == 0:
            ok = causal
        elif d == n_win:
            ok = (key > qry) & in_range[d]
        else:
            ok = in_range[d]
        win_scores[d] = _scores(kw_ref[0, pl.ds(starts[d], T), :], qaug_ref, LANES,
                                bias_ref.at[d] if d <= 1 else None, ok)
    for d in (1, 0):
        _online_step(*st_s, sel_scores[d], vs1_ref.at[:, :, pl.ds(starts[d], T)])
    for d in range(n_win + 1):
        _online_step(*st_w, win_scores[d], vw1_ref.at[:, :, pl.ds(starts[d], T)])

    gates = jax.nn.sigmoid(gt_ref[...])
    heads = []
    for h in range(NSA_HEADS):
        g = h // NSA_HPG
        cols = slice((h % NSA_HPG) * T, (h % NSA_HPG + 1) * T)
        o_sel = accs_ref[g, 0:HEAD_DIM, cols] * (gates[3 * h + 1:3 * h + 2, :] / accs_ref[g, HEAD_DIM:HEAD_DIM + 1, cols])
        o_win = accw_ref[g, 0:HEAD_DIM, cols] * (gates[3 * h + 2:3 * h + 3, :] / accw_ref[g, HEAD_DIM:HEAD_DIM + 1, cols])
        heads.append(gates[3 * h:3 * h + 1, :] * ocmp_ref[HEAD_DIM * h:HEAD_DIM * (h + 1), :] + o_sel + o_win)
    o_ref[0] = jnp.concatenate(heads, axis=0).T.astype(o_ref.dtype)


def _nsa_attend(q_t, sel_t, ocmp_t, g_t, ks, kw, vs_t, vw_t, bias_t, onehot):
    b, s, _ = ks.shape
    T = T_NSA
    nq = s // T
    R = NSA_HEADS * T
    cols = lambda r: pl.BlockSpec((r, T), lambda i, j: (0, i * nq + j))
    row_b = lambda a: pl.BlockSpec((a.shape[0], s), lambda i, j: (0, i))
    per_b = lambda a: pl.BlockSpec((1,) + a.shape[1:], lambda i, j: (i, 0, 0))
    full = lambda a: pl.BlockSpec(a.shape, lambda i, j: (0,) * a.ndim)
    v_ones = pltpu.VMEM((NSA_KV_GROUPS, HEAD_DIM + ONES_ROWS, s), BF16)
    stat = [pltpu.VMEM((1, R), F32), pltpu.VMEM((NSA_KV_GROUPS, HEAD_DIM + ONES_ROWS, R // NSA_KV_GROUPS), F32)]
    return pl.pallas_call(
        _nsa_attend_kernel,
        grid=(b, nq),
        in_specs=[cols(D_NSA), pl.BlockSpec((1, NSA_KV_GROUPS * MAX_SLC, T), lambda i, j: (i, 0, j)),
                  cols(D_NSA), cols(LANES), per_b(ks), per_b(kw), row_b(vs_t), row_b(vw_t),
                  full(bias_t), full(onehot)],
        out_specs=pl.BlockSpec((1, T, D_NSA), lambda i, j: (i, j, 0)),
        out_shape=jax.ShapeDtypeStruct((b, s, D_NSA), BF16),
        scratch_shapes=[pltpu.VMEM((s, 2 * LANES), BF16), pltpu.VMEM((2 * LANES, R), BF16), v_ones, v_ones]
        + stat + stat,
        compiler_params=_params("parallel", "arbitrary"), name="nsa_attend",
    )(q_t, sel_t, ocmp_t, g_t, ks, kw, vs_t, vw_t, bias_t, onehot)


def _sb_kernel(qt_ref, k_ref, vt_ref, tri_ref, o_ref, carry_ref, acc_ref):
    T = T_SB
    j = pl.program_id(1)
    n_pair = SB_HEADS // 2
    key = lax.broadcasted_iota(jnp.int32, (T, 2 * T), 0)
    qry = lax.broadcasted_iota(jnp.int32, (T, 2 * T), 1) & (T - 1)
    strict = key < qry
    carry_ref[...] = jnp.zeros(carry_ref.shape, F32)
    acc_ref[...] = jnp.zeros(acc_ref.shape, F32)

    def tiles(cs, masks):
        starts = [pl.multiple_of(c * T, T) for c in cs]
        units = [(ci, p) for ci in range(len(cs)) for p in range(n_pair)]
        z2s, incls = [], []
        for c, p in units:
            q_even = qt_ref[HEAD_DIM * (2 * p):HEAD_DIM * (2 * p + 1), :]
            q_odd = qt_ref[HEAD_DIM * (2 * p + 1):HEAD_DIM * (2 * p + 2), :]
            zq = jnp.zeros_like(q_even)
            q_pair = jnp.concatenate([jnp.concatenate([q_even, zq], axis=0),
                                      jnp.concatenate([zq, q_odd], axis=0)], axis=1)
            z2s.append(_dot(k_ref[0, pl.ds(starts[c], T), LANES * p:LANES * (p + 1)], q_pair))
        for (c, p), z2 in zip(units, z2s):
            rest = jnp.maximum(z2, 0.0) + jnp.log2(1.0 + jnp.exp2(-jnp.abs(z2)))
            if masks[c] is not None:
                rest = jnp.where(masks[c], rest, 0.0)
            hi, lo = _split_bf16(rest)
            incls.append(_dot(tri_ref[...], jnp.concatenate([hi, lo], axis=0)))
        carry = [carry_ref[p] for p in range(n_pair)]
        acc = [acc_ref[p] for p in range(n_pair)]
        for (c, p), z2, incl in zip(units, z2s, incls):
            a = jnp.exp2(z2 - incl - carry[p])
            if masks[c] is not None:
                a = jnp.where(masks[c], a, 0.0)
            acc[p] = acc[p] + _dot(vt_ref[LANES * p:LANES * (p + 1), pl.ds(starts[c], T)], a.astype(BF16))
            carry[p] = carry[p] + incl[0:1, :]
        for p in range(n_pair):
            carry_ref[p] = carry[p]
            acc_ref[p] = acc[p]

    @pl.when(j == 0)
    def _():
        tiles([0], [strict])

    @pl.when(j > 0)
    def _():
        tiles([j, j - 1], [strict, None])

    def live(c):
        return (c >= 0) & (jnp.min(carry_ref[...]) < SB_DEAD_BITS)

    def back_one(c):
        tiles([c], [None])
        return c - 1

    lax.while_loop(live, back_one, j - 2)
    row = lax.broadcasted_iota(jnp.int32, (LANES, T), 0)
    for p in range(n_pair):
        acc = acc_ref[p]
        o_ref[0, :, LANES * p:LANES * (p + 1)] = (
            jnp.where(row < HEAD_DIM, acc[:, 0:T], acc[:, T:2 * T]).T.astype(o_ref.dtype))


def _sb_attention(q_t, k, v_t, tri_t):
    b, s, _ = k.shape
    T = T_SB
    nq = s // T
    n_pair = SB_HEADS // 2
    return pl.pallas_call(
        _sb_kernel,
        grid=(b, nq),
        in_specs=[pl.BlockSpec((D_SB, T), lambda i, j: (0, i * nq + j)),
                  pl.BlockSpec((1, s, D_SB), lambda i, j: (i, 0, 0)),
                  pl.BlockSpec((D_SB, s), lambda i, j: (0, i)),
                  pl.BlockSpec(tri_t.shape, lambda i, j: (0, 0))],
        out_specs=pl.BlockSpec((1, T, D_SB), lambda i, j: (i, j, 0)),
        out_shape=jax.ShapeDtypeStruct((b, s, D_SB), BF16),
        scratch_shapes=[pltpu.VMEM((n_pair, 1, 2 * T), F32), pltpu.VMEM((n_pair, LANES, 2 * T), F32)],
        compiler_params=_params("parallel", "parallel"), name="sb_attention",
    )(q_t, k, v_t, tri_t)


def _hgrn_kernel(q_ref, f_ref, i_ref, lb_ref, nw_ref, ones_ref, bd_ref, o_ref,
                 st_ref, qj_ref, kj_ref, fj_ref, vj_ref, qd_ref, kd_ref, oc_ref, x_ref, u_ref, sb_ref, dec_ref,
                 att_ref, oall_ref):
    C = HG_SUB
    n_blk = T_HG // C

    @pl.when(pl.program_id(2) == 0)
    def _():
        st_ref[...] = jnp.zeros(st_ref.shape, F32)

    lb = lb_ref[...]
    lb_floor = jnp.maximum(lb, LB_FLOOR)
    one_m_lb = 1.0 - lb
    cum = None
    for j in range(C):
        xj = f_ref[0, pl.ds(j, n_blk, stride=C), :]
        e = jnp.exp(-jnp.abs(xj))
        r = 1.0 / (1.0 + e)
        er = e * r
        pos = xj >= 0.0
        fj = lb_floor + one_m_lb * jnp.where(pos, r, er)
        kj_ref[j] = one_m_lb * jnp.where(pos, er, r)
        fj_ref[j] = fj
        cum = fj if cum is None else cum * fj
        qj = q_ref[0, pl.ds(j, n_blk, stride=C), :]
        qj_ref[j] = qj
        vj_ref[j] = i_ref[0, pl.ds(j, n_blk, stride=C), :]
        qd_ref[pl.ds(j, n_blk, stride=HG_PITCH), :] = qj * cum
    dec_ref[...] = cum
    tail = None
    for j in reversed(range(C)):
        kd_ref[pl.ds(j, n_blk, stride=HG_PITCH), :] = kj_ref[j] if tail is None else kj_ref[j] * tail
        tail = fj_ref[j] if tail is None else tail * fj_ref[j]

    def kv_products(blk, carry):
        kd = kd_ref[pl.ds(pl.multiple_of(blk * HG_PITCH, 8), C), :].astype(BF16)
        vv = i_ref[0, pl.ds(pl.multiple_of(blk * C, C), C), :].astype(BF16)
        u_ref[blk] = bd_ref[...] * _dot_tn(vv, kd)
        return carry

    lax.fori_loop(0, n_blk, kv_products, 0, unroll=HG_UNROLL)

    def scan(blk, st):
        sb_ref[blk] = st.astype(BF16)
        return st * dec_ref[pl.ds(blk, 1), :] + u_ref[blk]

    st_ref[...] = lax.fori_loop(0, n_blk, scan, st_ref[...], unroll=HG_UNROLL)

    def outputs(blk, carry):
        r0 = pl.multiple_of(blk * HG_PITCH, 8)
        oc_ref[pl.ds(r0, C), :] = _dot_nt(qd_ref[pl.ds(r0, C), :].astype(BF16), sb_ref[blk])
        return carry

    lax.fori_loop(0, n_blk, outputs, 0, unroll=HG_UNROLL)

    base = lambda j: (j * (j + 1) // 2) * n_blk
    for j in range(C):
        qdec = qj_ref[j]
        for jp in reversed(range(j + 1)):
            x_ref[base(j) + jp * n_blk:base(j) + (jp + 1) * n_blk, :] = (qdec * kj_ref[jp]).astype(BF16)
            if jp > 0:
                qdec = qdec * fj_ref[jp]
        att_ref[base(j):base(j + 1), :] = _dot(x_ref[base(j):base(j + 1), :], ones_ref[...])
    for j in range(C):
        oj = oc_ref[pl.ds(j, n_blk, stride=HG_PITCH), :]
        for jp in range(j + 1):
            oj = oj + att_ref[base(j) + jp * n_blk:base(j) + (jp + 1) * n_blk, :] * vj_ref[jp]
        oall_ref[j * n_blk:(j + 1) * n_blk, :] = oj
    o_all = oall_ref[...]
    hi, lo = _split_bf16(o_all * o_all)
    ms = (_dot(hi, ones_ref[...]) + _dot(lo, ones_ref[...])) * (1.0 / HEAD_DIM)
    oall_ref[...] = o_all * lax.rsqrt(ms + RMS_EPS) * nw_ref[...]
    for j in range(C):
        o_ref[0, pl.ds(j, n_blk, stride=C), :] = oall_ref[j * n_blk:(j + 1) * n_blk, :]


def _hgrn(q, f, i, lb, nw, ones_bd, bd_mask):
    b, s, _ = q.shape
    n_blk = T_HG // HG_SUB
    tile = pl.BlockSpec((1, T_HG, LANES), lambda bi, pi, ti: (bi, ti, pi))
    vec = pl.BlockSpec((1, LANES), lambda bi, pi, ti: (0, pi))
    full = lambda a: pl.BlockSpec(a.shape, lambda bi, pi, ti: (0,) * a.ndim)
    jm = pltpu.VMEM((HG_SUB, n_blk, LANES), F32)
    nat = pltpu.VMEM((n_blk * HG_PITCH, LANES), F32)
    return pl.pallas_call(
        _hgrn_kernel,
        grid=(b, D_HG // LANES, s // T_HG),
        in_specs=[tile, tile, tile, vec, vec, full(ones_bd), full(bd_mask)],
        out_specs=tile,
        out_shape=jax.ShapeDtypeStruct((b, s, D_HG), F32),
        scratch_shapes=[pltpu.VMEM((LANES, LANES), F32), jm, jm, jm, jm, nat, nat, nat,
                        pltpu.VMEM((HG_SUB * (HG_SUB + 1) // 2 * n_blk, LANES), BF16),
                        pltpu.VMEM((n_blk, LANES, LANES), F32),
                        pltpu.VMEM((n_blk, LANES, LANES), BF16), pltpu.VMEM((n_blk, LANES), F32),
                        pltpu.VMEM((HG_SUB * (HG_SUB + 1) // 2 * n_blk, LANES), F32), pltpu.VMEM((T_HG, LANES), F32)],
        compiler_params=_params("parallel", "parallel", "arbitrary"), name="hgrn2",
    )(q, f, i, lb, nw, ones_bd, bd_mask)


def _mix_norm(onsa_ref, osb_ref, ohg_ref, z_ref, x_ref, w_ref, g_ref, b_ref):
    z = z_ref[...].astype(F32)
    sz = z * jax.nn.sigmoid(z)
    mixed = jnp.concatenate([(onsa_ref[...] * sz[:, 0:D_NSA]).astype(BF16),
                             (osb_ref[...] * sz[:, D_NSA:D_NSA + D_SB]).astype(BF16),
                             (ohg_ref[...] * sz[:, D_NSA + D_SB:D_MIX]).astype(BF16)], axis=1)
    v = ALPHA * x_ref[...] + _dot(mixed, w_ref[...])
    mu = jnp.mean(v, axis=-1, keepdims=True)
    vc = v - mu
    var = jnp.mean(vc * vc, axis=-1, keepdims=True)
    return vc * lax.rsqrt(var + LN_EPS) * g_ref[...] + b_ref[...]


def _out_kernel(*refs):
    refs[8][...] = _mix_norm(*refs[:8])


def _out_in_kernel(*refs, n_nat):
    x_new = _mix_norm(*refs[:8])
    refs[10][...] = x_new
    _project(x_new, refs[8], refs[9], refs[11:], n_nat)


def _out_proj(o_nsa, o_sb, o_hg, z_all, x2d, w, g, bvec, next_proj=None):
    m = x2d.shape[0]
    rows = lambda n: pl.BlockSpec((T_PROJ, n), lambda i: (i, 0))
    full = lambda a: pl.BlockSpec(a.shape, lambda i: (0, 0))
    in_specs = [rows(D_NSA), rows(D_SB), rows(D_HG), rows(D_MIX), rows(D_MODEL), full(w), full(g), full(bvec)]
    x_shape = jax.ShapeDtypeStruct((m, D_MODEL), F32)
    if next_proj is None:
        return pl.pallas_call(
            _out_kernel, grid=(m // T_PROJ,), in_specs=in_specs, out_specs=rows(D_MODEL), out_shape=x_shape,
            compiler_params=_params("parallel"), name="out_proj_norm",
        )(o_nsa, o_sb, o_hg, z_all, x2d, w, g, bvec)
    w_nat = next_proj[0]
    wp, w_specs, out_specs, out_shape = _proj_specs(m, *next_proj)
    outs = pl.pallas_call(
        functools.partial(_out_in_kernel, n_nat=len(w_nat)), grid=(m // T_PROJ,),
        in_specs=in_specs + w_specs, out_specs=[rows(D_MODEL)] + out_specs, out_shape=[x_shape] + out_shape,
        compiler_params=_params("parallel"), name="out_proj_norm_inproj",
    )(o_nsa, o_sb, o_hg, z_all, x2d, w, g, bvec, *wp)
    return outs[0], outs[1:1 + len(w_nat)], outs[1 + len(w_nat):]


def _t5_bucket_np(rel):
    n = np.maximum(rel, 0)
    max_exact = NUM_BUCKETS // 2
    large = max_exact + (np.log(np.maximum(n, 1).astype(np.float32) / max_exact)
                         / math.log(MAX_DISTANCE / max_exact) * (NUM_BUCKETS - max_exact)).astype(np.int32)
    large = np.clip(large, 0, NUM_BUCKETS - 1)
    return np.where(n < max_exact, n, large).astype(np.int32)


def _bias_tables(rel_bias, s):
    tbl = ((rel_bias - rel_bias[NUM_BUCKETS - 1]) * LOG2E).astype(F32)

    def expand(rel):
        onehot = (jnp.arange(NUM_BUCKETS)[:, None] == jnp.asarray(_t5_bucket_np(rel).reshape(1, -1))).astype(F32)
        return jnp.dot(tbl.T, onehot, precision=lax.Precision.HIGHEST).reshape((NSA_HEADS,) + rel.shape)

    n_cmp_pad = s // CMP_STRIDE
    n_rel = np.arange(2 * n_cmp_pad)[:, None] - n_cmp_pad
    r = np.arange(T_SEL)[None, :]
    pbias = expand(r - CMP_STRIDE * n_rel - (CMP_BLOCK - 1))
    T = T_NSA
    key = np.arange(T)[:, None]
    qry = np.arange(T)[None, :]
    near = np.stack([qry - key, T + qry - key])
    bias_t = jnp.transpose(expand(near), (1, 2, 0, 3)).reshape(2, T, NSA_HEADS * T)
    return pbias, bias_t


def _static_tables(s):
    n_chunk = s // CMP_STRIDE
    cmp_start = np.arange(n_chunk) * CMP_STRIDE
    slc_start = np.arange(MAX_SLC) * SLC_BLOCK
    ovl_t = ((cmp_start[None, :] < slc_start[:, None] + SLC_BLOCK)
             & (cmp_start[None, :] + CMP_BLOCK > slc_start[:, None])
             & (cmp_start[None, :] + CMP_BLOCK <= s)).astype(np.float32)
    onehot = (np.arange(s)[:, None] // SLC_BLOCK == np.arange(LANES)[None, :]).astype(np.float32)
    tri = (np.arange(T_SB)[None, :] >= np.arange(T_SB)[:, None]).astype(np.float32)
    tri = np.concatenate([tri, tri], axis=1)
    ones_bd = np.kron(np.eye(2), np.ones((HEAD_DIM, HEAD_DIM))).astype(np.float32)
    as_bf16 = lambda a: jnp.asarray(a, dtype=BF16)
    return as_bf16(ovl_t), as_bf16(onehot), as_bf16(tri), as_bf16(ones_bd), jnp.asarray(ones_bd)


def _layer_weights(w_in_l, cmp_pos_l, w_ck1_l, w_ck2_l, w_cv1_l, w_cv2_l):
    offs = np.cumsum((0,) + SPLIT_SIZES)
    (w_q, w_kc, w_vc, w_ks, w_vs, w_kw, w_vw, w_g, w_nz,
     w_sq, w_sk, w_sv, w_sz, w_hq, w_hf, w_hi, w_hz) = [w_in_l[:, offs[i]:offs[i + 1]] for i in range(len(SPLIT_SIZES))]
    w_gp = jnp.concatenate([w_g, jnp.zeros((D_MODEL, LANES - N_GATES), F32)], axis=1)
    w_z = jnp.concatenate([w_nz, w_sz, w_hz], axis=1)
    w_nat = [w_kc, w_vc, w_ks, w_kw, w_z, w_sk, w_hq, w_hf, w_hi]
    dt_nat = [F32, F32, BF16, BF16, BF16, BF16, F32, F32, F32]
    w_tr = [(w_q * QK_SCALE2).T, w_vs.T, w_vw.T, w_gp.T, (w_sq * QK_SCALE2).T, w_sv.T]
    dt_tr = [BF16, BF16, BF16, F32, BF16, BF16]
    w_nat = [w.astype(BF16) for w in w_nat]
    w_tr = [w.astype(BF16) for w in w_tr]

    def block_diag(w):
        z = jnp.zeros_like(w)
        return jnp.concatenate([jnp.concatenate([w, z], axis=-1), jnp.concatenate([z, w], axis=-1)], axis=-2)

    pos2 = jnp.concatenate([cmp_pos_l, cmp_pos_l], axis=1)
    w1k = block_diag(w_ck1_l.reshape(CMP_BLOCK, HEAD_DIM, CMP_HIDDEN)).astype(BF16)
    w1v = block_diag(w_cv1_l.reshape(CMP_BLOCK, HEAD_DIM, CMP_HIDDEN)).astype(BF16)
    w2k = block_diag(w_ck2_l).astype(BF16)
    w2v_t = block_diag(w_cv2_l).T.astype(BF16)
    return (w_nat, dt_nat, w_tr, dt_tr), (pos2, w1k, w2k, w1v, w2v_t)


def kernel(x, w_in, cmp_pos, w_ck1, w_ck2, w_cv1, w_cv2, hg_lb, hg_norm_w, w_out, ln_g, ln_b, rel_bias):
    b, s, d = x.shape
    assert d == D_MODEL and s % T_HG == 0 and s // SLC_BLOCK <= MAX_SLC and s >= WINDOW + T_NSA
    lb_w = jax.nn.softmax(hg_lb.astype(F32), axis=0)
    lb_all = jnp.cumsum(lb_w, axis=0) - lb_w[0]
    pbias, bias_t = _bias_tables(rel_bias, s)
    ovl_t, onehot, tri, ones_bd, bd_mask = _static_tables(s)

    x2d = x.reshape(b * s, d)
    weights = [_layer_weights(w_in[l], cmp_pos[l], w_ck1[l], w_ck2[l], w_cv1[l], w_cv2[l]) for l in range(DEPTH)]
    nat, trs = _inproj(x2d, *weights[0][0])
    for l in range(DEPTH):
        q_t, vs_t, vw_t, g_t, sq_t, sv_t = trs
        kc_src, vc_src, ks, kw, z_all, sk, hq, hf, hi = [o.reshape(b, s, o.shape[-1]) for o in nat]
        kc, vc_t = _compress(kc_src, vc_src, *weights[l][1])
        ocmp_t, sel_t = _nsa_select(q_t, kc, vc_t, pbias, ovl_t)
        o_nsa = _nsa_attend(q_t, sel_t, ocmp_t, g_t, ks, kw, vs_t, vw_t, bias_t, onehot)
        o_sb = _sb_attention(sq_t, sk, sv_t, tri)
        o_hg = _hgrn(hq, hf, hi, lb_all[l][None, :], hg_norm_w[l][None, :], ones_bd, bd_mask)
        res = _out_proj(o_nsa.reshape(b * s, D_NSA), o_sb.reshape(b * s, D_SB), o_hg.reshape(b * s, D_HG),
                        z_all.reshape(b * s, D_MIX), x2d, w_out[l].astype(BF16),
                        ln_g[l][None, :], ln_b[l][None, :],
                        next_proj=weights[l + 1][0] if l + 1 < DEPTH else None)
        if l + 1 < DEPTH:
            x2d, nat, trs = res
        else:
            x2d = res
    return x2d.reshape(b, s, d)
```

```python
import functools
import math

import numpy as np
import jax
import jax.numpy as jnp
from jax import lax
from jax.experimental import pallas as pl
from jax.experimental.pallas import tpu as pltpu

F32 = jnp.float32
BF16 = jnp.bfloat16

D_MODEL = 1024
DEPTH = 2
HEAD_DIM = 64
LANES = 128
NSA_HEADS = 6
NSA_KV_GROUPS = 2
NSA_HPG = NSA_HEADS // NSA_KV_GROUPS
CMP_BLOCK = 32
CMP_STRIDE = 16
CMP_HIDDEN = 2 * HEAD_DIM
SLC_BLOCK = 64
SLC_TOPN = 16
MAX_SLC = 64
WINDOW = 512
FORCE_BONUS = 1000.0
NEG_BIG = -1e30
LB_FLOOR = 1e-30
SB_HEADS = 4
HG_HEADS = 6
HG_SUB = 16
HG_PITCH = 24
NUM_BUCKETS = 32
MAX_DISTANCE = 128
D_NSA = NSA_HEADS * HEAD_DIM
D_KV = NSA_KV_GROUPS * HEAD_DIM
D_SB = SB_HEADS * HEAD_DIM
D_HG = HG_HEADS * HEAD_DIM
D_MIX = D_NSA + D_SB + D_HG
N_GATES = NSA_HEADS * 3
SPLIT_SIZES = (D_NSA, D_KV, D_KV, D_KV, D_KV, D_KV, D_KV, N_GATES, D_NSA,
               D_SB, D_SB, D_SB, D_SB, D_HG, D_HG, D_HG, D_HG)
ALPHA = (2 * DEPTH) ** 0.25
LN_EPS = 1e-5
RMS_EPS = 1e-6
LOG2E = math.log2(math.e)
QK_SCALE2 = LOG2E / math.sqrt(HEAD_DIM)

T_SEL = 128
T_NSA = 256
FAR_KEYS = 512
ONES_ROWS = 16
T_SB = 256
SB_DEAD_BITS = 151.0
T_HG = 1024
HG_UNROLL = T_HG // HG_SUB
N_POS = 3
T_PROJ = 512
VMEM_LIMIT = 56 * 1024 * 1024

_NT = (((1,), (1,)), ((), ()))
_TN = (((0,), (0,)), ((), ()))


def _dot(a, b):
    return jnp.dot(a, b, preferred_element_type=F32)


def _dot_nt(a, b):
    return lax.dot_general(a, b, _NT, preferred_element_type=F32)


def _dot_tn(a, b):
    return lax.dot_general(a, b, _TN, preferred_element_type=F32)


def _split_bf16(x):
    hi = x.astype(BF16)
    lo = (x - hi.astype(F32)).astype(BF16)
    return hi, lo


def _params(*sem):
    return pltpu.CompilerParams(dimension_semantics=sem, vmem_limit_bytes=VMEM_LIMIT)


def _project(x, wn_ref, wt_ref, o_refs, n_nat, pos_ref):
    xb = x.astype(BF16)
    nat = _dot(xb, wn_ref[...])
    off = 0
    for k, o_ref in enumerate(o_refs[:n_nat]):
        width = o_ref.shape[-1]
        a = k - (n_nat - N_POS)
        if a < 0:
            o_ref[...] = nat[:, off:off + width].astype(o_ref.dtype)
        else:
            n_slab = width // LANES
            for c in range(n_slab):
                pos_ref[a * n_slab + c] = nat[:, off + LANES * c:off + LANES * (c + 1)]
            for j in range(HG_SUB):
                for c in range(n_slab):
                    o_ref[0, j, :, LANES * c:LANES * (c + 1)] = (
                        pos_ref[a * n_slab + c, pl.ds(j, T_PROJ // HG_SUB, stride=HG_SUB), :])
        off += width
    tr = _dot_nt(wt_ref[...], xb)
    off = 0
    for o_ref in o_refs[n_nat:]:
        o_ref[...] = tr[off:off + o_ref.shape[0], :].astype(o_ref.dtype)
        off += o_ref.shape[0]


def _inproj_kernel(x_ref, wn_ref, wt_ref, *refs, n_nat):
    _project(x_ref[...], wn_ref, wt_ref, refs[:-1], n_nat, refs[-1])


def _proj_specs(m, w_nat, dt_nat, w_tr, dt_tr):
    wn = jnp.concatenate(w_nat, axis=1)
    wt = jnp.concatenate(w_tr, axis=0)
    w_specs = [pl.BlockSpec(wn.shape, lambda i: (0, 0)), pl.BlockSpec(wt.shape, lambda i: (0, 0))]
    n_plain = len(w_nat) - N_POS
    pos_block = (1, HG_SUB, T_PROJ // HG_SUB, D_HG)
    out_specs = [pl.BlockSpec((T_PROJ, w.shape[1]), lambda i: (i, 0)) for w in w_nat[:n_plain]]
    out_specs += [pl.BlockSpec(pos_block, lambda i: (i, 0, 0, 0))] * N_POS
    out_specs += [pl.BlockSpec((w.shape[0], T_PROJ), lambda i: (0, i)) for w in w_tr]
    out_shape = [jax.ShapeDtypeStruct((m, w.shape[1]), dt) for w, dt in zip(w_nat[:n_plain], dt_nat)]
    out_shape += [jax.ShapeDtypeStruct((m // T_PROJ,) + pos_block[1:], F32)] * N_POS
    out_shape += [jax.ShapeDtypeStruct((w.shape[0], m), dt) for w, dt in zip(w_tr, dt_tr)]
    scratch = [pltpu.VMEM((N_POS * D_HG // LANES, T_PROJ, LANES), F32)]
    return (wn, wt), w_specs, out_specs, out_shape, scratch


def _inproj(x2d, w_nat, dt_nat, w_tr, dt_tr):
    m = x2d.shape[0]
    w, w_specs, out_specs, out_shape, scratch = _proj_specs(m, w_nat, dt_nat, w_tr, dt_tr)
    outs = pl.pallas_call(
        functools.partial(_inproj_kernel, n_nat=len(w_nat)),
        grid=(m // T_PROJ,),
        in_specs=[pl.BlockSpec((T_PROJ, D_MODEL), lambda i: (i, 0))] + w_specs,
        out_specs=out_specs, out_shape=out_shape, scratch_shapes=scratch,
        compiler_params=_params("parallel"), name="inproj",
    )(x2d, *w)
    return outs[:len(w_nat)], outs[len(w_nat):]


def _compress_kernel(ksrc_ref, vsrc_ref, pos_ref, w1k_ref, w2k_ref, w1v_ref, w2v_ref,
                     kc_ref, vc_ref, *, n_chunk):
    def hidden(src_ref, w1_ref):
        top = jnp.zeros((n_chunk, 2 * CMP_HIDDEN), F32)
        bot = jnp.zeros((n_chunk, 2 * CMP_HIDDEN), F32)
        for p in range(CMP_STRIDE):
            xp = src_ref[0, pl.ds(p, n_chunk, stride=CMP_STRIDE), :]
            top += _dot((xp + pos_ref[p:p + 1, :]).astype(BF16), w1_ref[p])
            q = CMP_STRIDE + p
            bot += _dot((xp + pos_ref[q:q + 1, :]).astype(BF16), w1_ref[q])
        hid = top + pltpu.roll(bot, n_chunk - 1, 0)
        return jax.nn.gelu(hid).astype(BF16)

    kc_ref[0] = _dot(hidden(ksrc_ref, w1k_ref), w2k_ref[...]).astype(kc_ref.dtype)
    vc_ref[0] = _dot_nt(w2v_ref[...], hidden(vsrc_ref, w1v_ref)).astype(vc_ref.dtype)


def _compress(kc_src, vc_src, pos2, w1k, w2k, w1v, w2v_t):
    b, s, _ = kc_src.shape
    n_chunk = s // CMP_STRIDE
    full = lambda a: pl.BlockSpec(a.shape, lambda i: (0,) * a.ndim)
    src = pl.BlockSpec((1, s, D_KV), lambda i: (i, 0, 0))
    return pl.pallas_call(
        functools.partial(_compress_kernel, n_chunk=n_chunk),
        grid=(b,),
        in_specs=[src, src, full(pos2), full(w1k), full(w2k), full(w1v), full(w2v_t)],
        out_specs=[pl.BlockSpec((1, n_chunk, D_KV), lambda i: (i, 0, 0)),
                   pl.BlockSpec((1, D_KV, n_chunk), lambda i: (i, 0, 0))],
        out_shape=[jax.ShapeDtypeStruct((b, n_chunk, D_KV), BF16),
                   jax.ShapeDtypeStruct((b, D_KV, n_chunk), BF16)],
        compiler_params=_params("parallel"), name="nsa_compress",
    )(kc_src, vc_src, pos2, w1k, w2k, w1v, w2v_t)


def _nsa_select_kernel(qt_ref, kc_ref, vct_ref, pbias_ref, ovl_ref, ocmp_ref, sel_ref, *, n_cmp_pad):
    T = T_SEL
    j = pl.program_id(1)
    t0 = j * T
    n_idx = lax.broadcasted_iota(jnp.int32, (n_cmp_pad, T), 0)
    tok_c = t0 + lax.broadcasted_iota(jnp.int32, (n_cmp_pad, T), 1)
    mask_c = tok_c >= CMP_STRIDE * n_idx + (CMP_BLOCK - 1)
    off = pl.multiple_of(n_cmp_pad - (T // CMP_STRIDE) * j, 8)
    psums = [None] * NSA_KV_GROUPS
    for h in range(NSA_HEADS):
        g = h // NSA_HPG
        qh = qt_ref[HEAD_DIM * h:HEAD_DIM * (h + 1), :]
        zq = jnp.zeros_like(qh)
        qh = jnp.concatenate([qh, zq] if g == 0 else [zq, qh], axis=0)
        s = _dot(kc_ref[0], qh) + pbias_ref[h, pl.ds(off, n_cmp_pad), :]
        s = jnp.where(mask_c, s, NEG_BIG)
        m = jnp.max(s, axis=0, keepdims=True)
        p = jnp.where(mask_c, jnp.exp2(s - m), 0.0)
        l = jnp.sum(p, axis=0, keepdims=True)
        p = p / jnp.where(l > 0.0, l, 1.0)
        o_both = _dot(vct_ref[0], p.astype(BF16))
        ocmp_ref[HEAD_DIM * h:HEAD_DIM * (h + 1), :] = o_both[HEAD_DIM * g:HEAD_DIM * (g + 1)]
        psums[g] = p if psums[g] is None else psums[g] + p

    jblk = lax.broadcasted_iota(jnp.int32, (MAX_SLC, T), 0)
    tok = t0 + lax.broadcasted_iota(jnp.int32, (MAX_SLC, T), 1)
    cur = lax.shift_right_logical(tok, 6)
    forced = (jblk == 0) | (jblk == cur) | (jblk == cur - 1)
    valid = jblk * SLC_BLOCK <= tok
    jsub = lax.broadcasted_iota(jnp.int32, (8, T), 0)
    for g in range(NSA_KV_GROUPS):
        imp = _dot(ovl_ref[...], psums[g].astype(BF16))
        score = jnp.where(valid, imp + jnp.where(forced, FORCE_BONUS, 0.0), NEG_BIG)
        blocks = [score[8 * rb:8 * rb + 8] for rb in range(MAX_SLC // 8)]
        ranks = [jnp.zeros((8, T), F32) for _ in blocks]
        for jp in range(MAX_SLC):
            other = score[jp:jp + 1, :]
            for rb, blk in enumerate(blocks):
                ge = jnp.where(other >= blk, 1.0, 0.0)
                gt = jnp.where(other > blk, 1.0, 0.0)
                if 8 * rb > jp:
                    inc = ge
                elif 8 * rb + 7 < jp:
                    inc = gt
                else:
                    inc = jnp.where(jsub > jp - 8 * rb, ge, gt)
                ranks[rb] = ranks[rb] + inc
        rank = jnp.concatenate(ranks, axis=0)
        sel_ref[0, MAX_SLC * g:MAX_SLC * (g + 1), :] = jnp.where(rank < float(SLC_TOPN), 0.0, NEG_BIG).astype(BF16)


def _nsa_select(q_t, kc, vc_t, pbias, ovl_t):
    b, n_cmp_pad, _ = kc.shape
    m = q_t.shape[1]
    nq = m // b // T_SEL
    cols = lambda r: pl.BlockSpec((r, T_SEL), lambda i, j: (0, i * nq + j))
    per_b = lambda a: pl.BlockSpec((1,) + a.shape[1:], lambda i, j: (i, 0, 0))
    full = lambda a: pl.BlockSpec(a.shape, lambda i, j: (0,) * a.ndim)
    return pl.pallas_call(
        functools.partial(_nsa_select_kernel, n_cmp_pad=n_cmp_pad),
        grid=(b, nq),
        in_specs=[cols(D_NSA), per_b(kc), per_b(vc_t), full(pbias), full(ovl_t)],
        out_specs=[cols(D_NSA), pl.BlockSpec((1, NSA_KV_GROUPS * MAX_SLC, T_SEL), lambda i, j: (i, 0, j))],
        out_shape=[jax.ShapeDtypeStruct((D_NSA, m), F32),
                   jax.ShapeDtypeStruct((b, NSA_KV_GROUPS * MAX_SLC, m // b), BF16)],
        compiler_params=_params("parallel", "parallel"), name="nsa_select",
    )(q_t, kc, vc_t, pbias, ovl_t)


def _online_init(m_ref, acc_ref):
    m_ref[...] = jnp.full(m_ref.shape, NEG_BIG, F32)
    acc_ref[...] = jnp.zeros(acc_ref.shape, F32)


def _scores(k_tile, q_ref, q_rows, bias_ref=None, ok=None):
    s = _dot(k_tile, q_ref[0:q_rows, :])
    if bias_ref is not None:
        s = s + bias_ref[...]
    if ok is not None:
        s = jnp.where(ok, s, NEG_BIG)
    return s


def _online_step(m_ref, acc_ref, s, v_ones):
    m = m_ref[...]
    m_new = jnp.maximum(m, jnp.max(s, axis=0, keepdims=True))
    alpha = jnp.exp2(m - m_new)
    p = jnp.exp2(s - m_new).astype(BF16)
    m_ref[...] = m_new
    half = p.shape[1] // NSA_KV_GROUPS
    for g in range(NSA_KV_GROUPS):
        cols = slice(g * half, (g + 1) * half)
        acc_ref[g] = alpha[:, cols] * acc_ref[g] + _dot(v_ones[g], p[:, cols])


def _nsa_attend_kernel(qt_ref, sel_ref, ocmp_ref, gt_ref, ks_ref, kw_ref, vs_ref, vw_ref, bias_ref, onehot_ref,
                       o_ref, kaug_ref, qaug_ref, vs1_ref, vw1_ref, ms_ref, accs_ref, mw_ref, accw_ref):
    T = T_NSA
    j = pl.program_id(1)

    @pl.when(j == 0)
    def _():
        kaug_ref[:, 0:LANES] = ks_ref[0]
        kaug_ref[:, LANES:2 * LANES] = onehot_ref[...]
        qaug_ref[...] = jnp.zeros(qaug_ref.shape, BF16)
        for v1_ref, v_ref in ((vs1_ref, vs_ref), (vw1_ref, vw_ref)):
            for g in range(NSA_KV_GROUPS):
                v1_ref[g, 0:HEAD_DIM, :] = v_ref[HEAD_DIM * g:HEAD_DIM * (g + 1), :]
                v1_ref[g, HEAD_DIM:HEAD_DIM + ONES_ROWS, :] = jnp.ones((ONES_ROWS, v1_ref.shape[2]), BF16)

    for h in range(NSA_HEADS):
        g = h // NSA_HPG
        qaug_ref[HEAD_DIM * g:HEAD_DIM * (g + 1), h * T:(h + 1) * T] = qt_ref[HEAD_DIM * h:HEAD_DIM * (h + 1), :]
        qaug_ref[LANES:LANES + MAX_SLC, h * T:(h + 1) * T] = sel_ref[0, MAX_SLC * g:MAX_SLC * (g + 1), :]

    key = lax.broadcasted_iota(jnp.int32, (T, NSA_HEADS * T), 0)
    qry = lax.broadcasted_iota(jnp.int32, (T, NSA_HEADS * T), 1) & (T - 1)
    causal = key <= qry
    st_s = (ms_ref, accs_ref)
    st_w = (mw_ref, accw_ref)
    _online_init(*st_s)
    _online_init(*st_w)

    far_len = jnp.maximum(j - 1, 0) * T

    def far_step(r0, n_keys):
        s = _scores(kaug_ref[pl.ds(r0, n_keys), :], qaug_ref, 2 * LANES)
        _online_step(*st_s, s, vs1_ref.at[:, :, pl.ds(r0, n_keys)])

    def far_big(c, carry):
        r0 = pl.multiple_of(c * (2 * FAR_KEYS), 2 * FAR_KEYS)
        s_lo = _scores(kaug_ref[pl.ds(r0, FAR_KEYS), :], qaug_ref, 2 * LANES)
        s_hi = _scores(kaug_ref[pl.ds(r0 + FAR_KEYS, FAR_KEYS), :], qaug_ref, 2 * LANES)
        _online_step(*st_s, s_lo, vs1_ref.at[:, :, pl.ds(r0, FAR_KEYS)])
        _online_step(*st_s, s_hi, vs1_ref.at[:, :, pl.ds(r0 + FAR_KEYS, FAR_KEYS)])
        return carry

    n_big = lax.shift_right_logical(far_len, FAR_KEYS.bit_length())
    lax.fori_loop(0, n_big, far_big, 0)
    done = n_big * (2 * FAR_KEYS)

    @pl.when(far_len - done >= FAR_KEYS)
    def _():
        far_step(pl.multiple_of(done, FAR_KEYS), FAR_KEYS)

    done = done + jnp.where(far_len - done >= FAR_KEYS, FAR_KEYS, 0)

    def far_small(c, carry):
        far_step(pl.multiple_of(done + c * T, T), T)
        return carry

    lax.fori_loop(0, lax.shift_right_logical(far_len - done, T.bit_length() - 1), far_small, 0)

    n_win = WINDOW // T
    starts = [pl.multiple_of(jnp.maximum(j - d, 0) * T, T) for d in range(n_win + 1)]
    in_range = [key < jnp.where(j - d >= 0, T, -1) for d in range(n_win + 1)]
    sel_scores, win_scores = {}, {}
    for d in (1, 0):
        sel_scores[d] = _scores(kaug_ref[pl.ds(starts[d], T), :], qaug_ref, 2 * LANES, bias_ref.at[d],
                                causal if d == 0 else in_range[d])
    for d in range(n_win + 1):
        if d == 0:
            ok = causal
        elif d == n_win:
            ok = (key > qry) & in_range[d]
        else:
            ok = in_range[d]
        win_scores[d] = _scores(kw_ref[0, pl.ds(starts[d], T), :], qaug_ref, LANES,
                                bias_ref.at[d] if d <= 1 else None, ok)
    for d in (1, 0):
        _online_step(*st_s, sel_scores[d], vs1_ref.at[:, :, pl.ds(starts[d], T)])
    for d in range(n_win + 1):
        _online_step(*st_w, win_scores[d], vw1_ref.at[:, :, pl.ds(starts[d], T)])

    gates = jax.nn.sigmoid(gt_ref[...])
    heads = []
    for h in range(NSA_HEADS):
        g = h // NSA_HPG
        cols = slice((h % NSA_HPG) * T, (h % NSA_HPG + 1) * T)
        o_sel = accs_ref[g, 0:HEAD_DIM, cols] * (gates[3 * h + 1:3 * h + 2, :] / accs_ref[g, HEAD_DIM:HEAD_DIM + 1, cols])
        o_win = accw_ref[g, 0:HEAD_DIM, cols] * (gates[3 * h + 2:3 * h + 3, :] / accw_ref[g, HEAD_DIM:HEAD_DIM + 1, cols])
        heads.append(gates[3 * h:3 * h + 1, :] * ocmp_ref[HEAD_DIM * h:HEAD_DIM * (h + 1), :] + o_sel + o_win)
    o_ref[0] = jnp.concatenate(heads, axis=0).T.astype(o_ref.dtype)


def _nsa_attend(q_t, sel_t, ocmp_t, g_t, ks, kw, vs_t, vw_t, bias_t, onehot):
    b, s, _ = ks.shape
    T = T_NSA
    nq = s // T
    R = NSA_HEADS * T
    cols = lambda r: pl.BlockSpec((r, T), lambda i, j: (0, i * nq + j))
    row_b = lambda a: pl.BlockSpec((a.shape[0], s), lambda i, j: (0, i))
    per_b = lambda a: pl.BlockSpec((1,) + a.shape[1:], lambda i, j: (i, 0, 0))
    full = lambda a: pl.BlockSpec(a.shape, lambda i, j: (0,) * a.ndim)
    v_ones = pltpu.VMEM((NSA_KV_GROUPS, HEAD_DIM + ONES_ROWS, s), BF16)
    stat = [pltpu.VMEM((1, R), F32), pltpu.VMEM((NSA_KV_GROUPS, HEAD_DIM + ONES_ROWS, R // NSA_KV_GROUPS), F32)]
    return pl.pallas_call(
        _nsa_attend_kernel,
        grid=(b, nq),
        in_specs=[cols(D_NSA), pl.BlockSpec((1, NSA_KV_GROUPS * MAX_SLC, T), lambda i, j: (i, 0, j)),
                  cols(D_NSA), cols(LANES), per_b(ks), per_b(kw), row_b(vs_t), row_b(vw_t),
                  full(bias_t), full(onehot)],
        out_specs=pl.BlockSpec((1, T, D_NSA), lambda i, j: (i, j, 0)),
        out_shape=jax.ShapeDtypeStruct((b, s, D_NSA), BF16),
        scratch_shapes=[pltpu.VMEM((s, 2 * LANES), BF16), pltpu.VMEM((2 * LANES, R), BF16), v_ones, v_ones]
        + stat + stat,
        compiler_params=_params("parallel", "arbitrary"), name="nsa_attend",
    )(q_t, sel_t, ocmp_t, g_t, ks, kw, vs_t, vw_t, bias_t, onehot)


def _sb_kernel(qt_ref, k_ref, vt_ref, tri_ref, o_ref, carry_ref, acc_ref):
    T = T_SB
    j = pl.program_id(1)
    n_pair = SB_HEADS // 2
    key = lax.broadcasted_iota(jnp.int32, (T, 2 * T), 0)
    qry = lax.broadcasted_iota(jnp.int32, (T, 2 * T), 1) & (T - 1)
    strict = key < qry
    carry_ref[...] = jnp.zeros(carry_ref.shape, F32)
    acc_ref[...] = jnp.zeros(acc_ref.shape, F32)

    def tiles(cs, masks):
        starts = [pl.multiple_of(c * T, T) for c in cs]
        units = [(ci, p) for ci in range(len(cs)) for p in range(n_pair)]
        z2s, incls = [], []
        for c, p in units:
            q_even = qt_ref[HEAD_DIM * (2 * p):HEAD_DIM * (2 * p + 1), :]
            q_odd = qt_ref[HEAD_DIM * (2 * p + 1):HEAD_DIM * (2 * p + 2), :]
            zq = jnp.zeros_like(q_even)
            q_pair = jnp.concatenate([jnp.concatenate([q_even, zq], axis=0),
                                      jnp.concatenate([zq, q_odd], axis=0)], axis=1)
            z2s.append(_dot(k_ref[0, pl.ds(starts[c], T), LANES * p:LANES * (p + 1)], q_pair))
        for (c, p), z2 in zip(units, z2s):
            rest = jnp.maximum(z2, 0.0) + jnp.log2(1.0 + jnp.exp2(-jnp.abs(z2)))
            if masks[c] is not None:
                rest = jnp.where(masks[c], rest, 0.0)
            hi, lo = _split_bf16(rest)
            incls.append(_dot(tri_ref[...], jnp.concatenate([hi, lo], axis=0)))
        carry = [carry_ref[p] for p in range(n_pair)]
        acc = [acc_ref[p] for p in range(n_pair)]
        for (c, p), z2, incl in zip(units, z2s, incls):
            a = jnp.exp2(z2 - incl - carry[p])
            if masks[c] is not None:
                a = jnp.where(masks[c], a, 0.0)
            acc[p] = acc[p] + _dot(vt_ref[LANES * p:LANES * (p + 1), pl.ds(starts[c], T)], a.astype(BF16))
            carry[p] = carry[p] + incl[0:1, :]
        for p in range(n_pair):
            carry_ref[p] = carry[p]
            acc_ref[p] = acc[p]

    @pl.when(j == 0)
    def _():
        tiles([0], [strict])

    @pl.when(j > 0)
    def _():
        tiles([j, j - 1], [strict, None])

    def live(c):
        return (c >= 0) & (jnp.min(carry_ref[...]) < SB_DEAD_BITS)

    def back_one(c):
        tiles([c], [None])
        return c - 1

    lax.while_loop(live, back_one, j - 2)
    row = lax.broadcasted_iota(jnp.int32, (LANES, T), 0)
    for p in range(n_pair):
        acc = acc_ref[p]
        o_ref[0, :, LANES * p:LANES * (p + 1)] = (
            jnp.where(row < HEAD_DIM, acc[:, 0:T], acc[:, T:2 * T]).T.astype(o_ref.dtype))


def _sb_attention(q_t, k, v_t, tri_t):
    b, s, _ = k.shape
    T = T_SB
    nq = s // T
    n_pair = SB_HEADS // 2
    return pl.pallas_call(
        _sb_kernel,
        grid=(b, nq),
        in_specs=[pl.BlockSpec((D_SB, T), lambda i, j: (0, i * nq + j)),
                  pl.BlockSpec((1, s, D_SB), lambda i, j: (i, 0, 0)),
                  pl.BlockSpec((D_SB, s), lambda i, j: (0, i)),
                  pl.BlockSpec(tri_t.shape, lambda i, j: (0, 0))],
        out_specs=pl.BlockSpec((1, T, D_SB), lambda i, j: (i, j, 0)),
        out_shape=jax.ShapeDtypeStruct((b, s, D_SB), BF16),
        scratch_shapes=[pltpu.VMEM((n_pair, 1, 2 * T), F32), pltpu.VMEM((n_pair, LANES, 2 * T), F32)],
        compiler_params=_params("parallel", "parallel"), name="sb_attention",
    )(q_t, k, v_t, tri_t)


def _hgrn_kernel(q_ref, f_ref, i_ref, lb_ref, nw_ref, ones_ref, bd_ref, o_ref,
                 st_ref, qj_ref, kj_ref, fj_ref, vj_ref, qd_ref, kd_ref, oc_ref, x_ref, u_ref, sb_ref, dec_ref,
                 att_ref, oall_ref, vn_ref):
    C = HG_SUB
    n_blk = T_HG // C

    def at_position(ref, j):
        return jnp.concatenate([ref[t, j] for t in range(T_HG // T_PROJ)], axis=0)

    @pl.when(pl.program_id(2) == 0)
    def _():
        st_ref[...] = jnp.zeros(st_ref.shape, F32)

    lb = lb_ref[...]
    lb_floor = jnp.maximum(lb, LB_FLOOR)
    one_m_lb = 1.0 - lb
    cum = None
    for j in range(C):
        xj = at_position(f_ref, j)
        e = jnp.exp(-jnp.abs(xj))
        r = 1.0 / (1.0 + e)
        er = e * r
        pos = xj >= 0.0
        fj = lb_floor + one_m_lb * jnp.where(pos, r, er)
        kj_ref[j] = one_m_lb * jnp.where(pos, er, r)
        fj_ref[j] = fj
        cum = fj if cum is None else cum * fj
        qj = at_position(q_ref, j)
        vj = at_position(i_ref, j)
        qj_ref[j] = qj
        vj_ref[j] = vj
        qd_ref[pl.ds(j, n_blk, stride=HG_PITCH), :] = qj * cum
        vn_ref[pl.ds(j, n_blk, stride=HG_PITCH), :] = vj
    dec_ref[...] = cum
    tail = None
    for j in reversed(range(C)):
        kd_ref[pl.ds(j, n_blk, stride=HG_PITCH), :] = kj_ref[j] if tail is None else kj_ref[j] * tail
        tail = fj_ref[j] if tail is None else tail * fj_ref[j]

    def kv_products(blk, carry):
        r0 = pl.multiple_of(blk * HG_PITCH, 8)
        kd = kd_ref[pl.ds(r0, C), :].astype(BF16)
        vv = vn_ref[pl.ds(r0, C), :].astype(BF16)
        u_ref[blk] = bd_ref[...] * _dot_tn(vv, kd)
        return carry

    lax.fori_loop(0, n_blk, kv_products, 0, unroll=HG_UNROLL)

    def scan(blk, st):
        sb_ref[blk] = st.astype(BF16)
        return st * dec_ref[pl.ds(blk, 1), :] + u_ref[blk]

    st_ref[...] = lax.fori_loop(0, n_blk, scan, st_ref[...], unroll=HG_UNROLL)

    def outputs(blk, carry):
        r0 = pl.multiple_of(blk * HG_PITCH, 8)
        oc_ref[pl.ds(r0, C), :] = _dot_nt(qd_ref[pl.ds(r0, C), :].astype(BF16), sb_ref[blk])
        return carry

    lax.fori_loop(0, n_blk, outputs, 0, unroll=HG_UNROLL)

    base = lambda j: (j * (j + 1) // 2) * n_blk
    for j in range(C):
        qdec = qj_ref[j]
        for jp in reversed(range(j + 1)):
            x_ref[base(j) + jp * n_blk:base(j) + (jp + 1) * n_blk, :] = (qdec * kj_ref[jp]).astype(BF16)
            if jp > 0:
                qdec = qdec * fj_ref[jp]
        att_ref[base(j):base(j + 1), :] = _dot(x_ref[base(j):base(j + 1), :], ones_ref[...])
    for j in range(C):
        oj = oc_ref[pl.ds(j, n_blk, stride=HG_PITCH), :]
        for jp in range(j + 1):
            oj = oj + att_ref[base(j) + jp * n_blk:base(j) + (jp + 1) * n_blk, :] * vj_ref[jp]
        oall_ref[j * n_blk:(j + 1) * n_blk, :] = oj
    o_all = oall_ref[...]
    hi, lo = _split_bf16(o_all * o_all)
    ms = (_dot(hi, ones_ref[...]) + _dot(lo, ones_ref[...])) * (1.0 / HEAD_DIM)
    oall_ref[...] = o_all * lax.rsqrt(ms + RMS_EPS) * nw_ref[...]
    for j in range(C):
        o_ref[0, pl.ds(j, n_blk, stride=C), :] = oall_ref[j * n_blk:(j + 1) * n_blk, :]


def _hgrn(q, f, i, lb, nw, ones_bd, bd_mask, b, s):
    n_blk = T_HG // HG_SUB
    n_t = s // T_HG
    tile = pl.BlockSpec((1, T_HG, LANES), lambda bi, pi, ti: (bi, ti, pi))
    pos = pl.BlockSpec((T_HG // T_PROJ, HG_SUB, T_PROJ // HG_SUB, LANES), lambda bi, pi, ti: (bi * n_t + ti, 0, 0, pi))
    vec = pl.BlockSpec((1, LANES), lambda bi, pi, ti: (0, pi))
    full = lambda a: pl.BlockSpec(a.shape, lambda bi, pi, ti: (0,) * a.ndim)
    jm = pltpu.VMEM((HG_SUB, n_blk, LANES), F32)
    nat = pltpu.VMEM((n_blk * HG_PITCH, LANES), F32)
    return pl.pallas_call(
        _hgrn_kernel,
        grid=(b, D_HG // LANES, n_t),
        in_specs=[pos, pos, pos, vec, vec, full(ones_bd), full(bd_mask)],
        out_specs=tile,
        out_shape=jax.ShapeDtypeStruct((b, s, D_HG), F32),
        scratch_shapes=[pltpu.VMEM((LANES, LANES), F32), jm, jm, jm, jm, nat, nat, nat,
                        pltpu.VMEM((HG_SUB * (HG_SUB + 1) // 2 * n_blk, LANES), BF16),
                        pltpu.VMEM((n_blk, LANES, LANES), F32),
                        pltpu.VMEM((n_blk, LANES, LANES), BF16), pltpu.VMEM((n_blk, LANES), F32),
                        pltpu.VMEM((HG_SUB * (HG_SUB + 1) // 2 * n_blk, LANES), F32), pltpu.VMEM((T_HG, LANES), F32),
                        nat],
        compiler_params=_params("parallel", "parallel", "arbitrary"), name="hgrn2",
    )(q, f, i, lb, nw, ones_bd, bd_mask)


def _mix_norm(onsa_ref, osb_ref, ohg_ref, z_ref, x_ref, w_ref, g_ref, b_ref):
    z = z_ref[...].astype(F32)
    sz = z * jax.nn.sigmoid(z)
    mixed = jnp.concatenate([(onsa_ref[...] * sz[:, 0:D_NSA]).astype(BF16),
                             (osb_ref[...] * sz[:, D_NSA:D_NSA + D_SB]).astype(BF16),
                             (ohg_ref[...] * sz[:, D_NSA + D_SB:D_MIX]).astype(BF16)], axis=1)
    v = ALPHA * x_ref[...] + _dot(mixed, w_ref[...])
    mu = jnp.mean(v, axis=-1, keepdims=True)
    vc = v - mu
    var = jnp.mean(vc * vc, axis=-1, keepdims=True)
    return vc * lax.rsqrt(var + LN_EPS) * g_ref[...] + b_ref[...]


def _out_kernel(*refs):
    refs[8][...] = _mix_norm(*refs[:8])


def _out_in_kernel(*refs, n_nat):
    x_new = _mix_norm(*refs[:8])
    refs[10][...] = x_new
    _project(x_new, refs[8], refs[9], refs[11:-1], n_nat, refs[-1])


def _out_proj(o_nsa, o_sb, o_hg, z_all, x2d, w, g, bvec, next_proj=None):
    m = x2d.shape[0]
    rows = lambda n: pl.BlockSpec((T_PROJ, n), lambda i: (i, 0))
    full = lambda a: pl.BlockSpec(a.shape, lambda i: (0, 0))
    in_specs = [rows(D_NSA), rows(D_SB), rows(D_HG), rows(D_MIX), rows(D_MODEL), full(w), full(g), full(bvec)]
    x_shape = jax.ShapeDtypeStruct((m, D_MODEL), F32)
    if next_proj is None:
        return pl.pallas_call(
            _out_kernel, grid=(m // T_PROJ,), in_specs=in_specs, out_specs=rows(D_MODEL), out_shape=x_shape,
            compiler_params=_params("parallel"), name="out_proj_norm",
        )(o_nsa, o_sb, o_hg, z_all, x2d, w, g, bvec)
    w_nat = next_proj[0]
    wp, w_specs, out_specs, out_shape, scratch = _proj_specs(m, *next_proj)
    outs = pl.pallas_call(
        functools.partial(_out_in_kernel, n_nat=len(w_nat)), grid=(m // T_PROJ,),
        in_specs=in_specs + w_specs, out_specs=[rows(D_MODEL)] + out_specs, out_shape=[x_shape] + out_shape,
        scratch_shapes=scratch, compiler_params=_params("parallel"), name="out_proj_norm_inproj",
    )(o_nsa, o_sb, o_hg, z_all, x2d, w, g, bvec, *wp)
    return outs[0], outs[1:1 + len(w_nat)], outs[1 + len(w_nat):]


def _t5_bucket_np(rel):
    n = np.maximum(rel, 0)
    max_exact = NUM_BUCKETS // 2
    large = max_exact + (np.log(np.maximum(n, 1).astype(np.float32) / max_exact)
                         / math.log(MAX_DISTANCE / max_exact) * (NUM_BUCKETS - max_exact)).astype(np.int32)
    large = np.clip(large, 0, NUM_BUCKETS - 1)
    return np.where(n < max_exact, n, large).astype(np.int32)


def _bias_tables(rel_bias, s):
    tbl = ((rel_bias - rel_bias[NUM_BUCKETS - 1]) * LOG2E).astype(F32)

    def expand(rel):
        onehot = (jnp.arange(NUM_BUCKETS)[:, None] == jnp.asarray(_t5_bucket_np(rel).reshape(1, -1))).astype(F32)
        return jnp.dot(tbl.T, onehot, precision=lax.Precision.HIGHEST).reshape((NSA_HEADS,) + rel.shape)

    n_cmp_pad = s // CMP_STRIDE
    n_rel = np.arange(2 * n_cmp_pad)[:, None] - n_cmp_pad
    r = np.arange(T_SEL)[None, :]
    pbias = expand(r - CMP_STRIDE * n_rel - (CMP_BLOCK - 1))
    T = T_NSA
    key = np.arange(T)[:, None]
    qry = np.arange(T)[None, :]
    near = np.stack([qry - key, T + qry - key])
    bias_t = jnp.transpose(expand(near), (1, 2, 0, 3)).reshape(2, T, NSA_HEADS * T)
    return pbias, bias_t


def _static_tables(s):
    n_chunk = s // CMP_STRIDE
    cmp_start = np.arange(n_chunk) * CMP_STRIDE
    slc_start = np.arange(MAX_SLC) * SLC_BLOCK
    ovl_t = ((cmp_start[None, :] < slc_start[:, None] + SLC_BLOCK)
             & (cmp_start[None, :] + CMP_BLOCK > slc_start[:, None])
             & (cmp_start[None, :] + CMP_BLOCK <= s)).astype(np.float32)
    onehot = (np.arange(s)[:, None] // SLC_BLOCK == np.arange(LANES)[None, :]).astype(np.float32)
    tri = (np.arange(T_SB)[None, :] >= np.arange(T_SB)[:, None]).astype(np.float32)
    tri = np.concatenate([tri, tri], axis=1)
    ones_bd = np.kron(np.eye(2), np.ones((HEAD_DIM, HEAD_DIM))).astype(np.float32)
    as_bf16 = lambda a: jnp.asarray(a, dtype=BF16)
    return as_bf16(ovl_t), as_bf16(onehot), as_bf16(tri), as_bf16(ones_bd), jnp.asarray(ones_bd)


def _layer_weights(w_in_l, cmp_pos_l, w_ck1_l, w_ck2_l, w_cv1_l, w_cv2_l):
    offs = np.cumsum((0,) + SPLIT_SIZES)
    (w_q, w_kc, w_vc, w_ks, w_vs, w_kw, w_vw, w_g, w_nz,
     w_sq, w_sk, w_sv, w_sz, w_hq, w_hf, w_hi, w_hz) = [w_in_l[:, offs[i]:offs[i + 1]] for i in range(len(SPLIT_SIZES))]
    w_gp = jnp.concatenate([w_g, jnp.zeros((D_MODEL, LANES - N_GATES), F32)], axis=1)
    w_z = jnp.concatenate([w_nz, w_sz, w_hz], axis=1)
    w_nat = [w_kc, w_vc, w_ks, w_kw, w_z, w_sk, w_hq, w_hf, w_hi]
    dt_nat = [F32, F32, BF16, BF16, BF16, BF16, F32, F32, F32]
    w_tr = [(w_q * QK_SCALE2).T, w_vs.T, w_vw.T, w_gp.T, (w_sq * QK_SCALE2).T, w_sv.T]
    dt_tr = [BF16, BF16, BF16, F32, BF16, BF16]
    w_nat = [w.astype(BF16) for w in w_nat]
    w_tr = [w.astype(BF16) for w in w_tr]

    def block_diag(w):
        z = jnp.zeros_like(w)
        return jnp.concatenate([jnp.concatenate([w, z], axis=-1), jnp.concatenate([z, w], axis=-1)], axis=-2)

    pos2 = jnp.concatenate([cmp_pos_l, cmp_pos_l], axis=1)
    w1k = block_diag(w_ck1_l.reshape(CMP_BLOCK, HEAD_DIM, CMP_HIDDEN)).astype(BF16)
    w1v = block_diag(w_cv1_l.reshape(CMP_BLOCK, HEAD_DIM, CMP_HIDDEN)).astype(BF16)
    w2k = block_diag(w_ck2_l).astype(BF16)
    w2v_t = block_diag(w_cv2_l).T.astype(BF16)
    return (w_nat, dt_nat, w_tr, dt_tr), (pos2, w1k, w2k, w1v, w2v_t)


def kernel(x, w_in, cmp_pos, w_ck1, w_ck2, w_cv1, w_cv2, hg_lb, hg_norm_w, w_out, ln_g, ln_b, rel_bias):
    b, s, d = x.shape
    assert d == D_MODEL and s % T_HG == 0 and s // SLC_BLOCK <= MAX_SLC and s >= WINDOW + T_NSA
    assert T_HG % T_PROJ == 0 and T_PROJ % HG_SUB == 0
    lb_w = jax.nn.softmax(hg_lb.astype(F32), axis=0)
    lb_all = jnp.cumsum(lb_w, axis=0) - lb_w[0]
    pbias, bias_t = _bias_tables(rel_bias, s)
    ovl_t, onehot, tri, ones_bd, bd_mask = _static_tables(s)

    x2d = x.reshape(b * s, d)
    weights = [_layer_weights(w_in[l], cmp_pos[l], w_ck1[l], w_ck2[l], w_cv1[l], w_cv2[l]) for l in range(DEPTH)]
    nat, trs = _inproj(x2d, *weights[0][0])
    for l in range(DEPTH):
        q_t, vs_t, vw_t, g_t, sq_t, sv_t = trs
        kc_src, vc_src, ks, kw, z_all, sk = [o.reshape(b, s, o.shape[-1]) for o in nat[:-N_POS]]
        hq, hf, hi = nat[-N_POS:]
        kc, vc_t = _compress(kc_src, vc_src, *weights[l][1])
        ocmp_t, sel_t = _nsa_select(q_t, kc, vc_t, pbias, ovl_t)
        o_nsa = _nsa_attend(q_t, sel_t, ocmp_t, g_t, ks, kw, vs_t, vw_t, bias_t, onehot)
        o_sb = _sb_attention(sq_t, sk, sv_t, tri)
        o_hg = _hgrn(hq, hf, hi, lb_all[l][None, :], hg_norm_w[l][None, :], ones_bd, bd_mask, b, s)
        res = _out_proj(o_nsa.reshape(b * s, D_NSA), o_sb.reshape(b * s, D_SB), o_hg.reshape(b * s, D_HG),
                        z_all.reshape(b * s, D_MIX), x2d, w_out[l].astype(BF16),
                        ln_g[l][None, :], ln_b[l][None, :],
                        next_proj=weights[l + 1][0] if l + 1 < DEPTH else None)
        if l + 1 < DEPTH:
            x2d, nat, trs = res
        else:
            x2d = res
    return x2d.reshape(b, s, d)
```

```python
import functools
import math

import numpy as np
import jax
import jax.numpy as jnp
from jax import lax
from jax.experimental import pallas as pl
from jax.experimental.pallas import tpu as pltpu

F32 = jnp.float32
BF16 = jnp.bfloat16

D_MODEL = 1024
DEPTH = 2
HEAD_DIM = 64
LANES = 128
NSA_HEADS = 6
NSA_KV_GROUPS = 2
NSA_HPG = NSA_HEADS // NSA_KV_GROUPS
CMP_BLOCK = 32
CMP_STRIDE = 16
CMP_HIDDEN = 2 * HEAD_DIM
SLC_BLOCK = 64
SLC_TOPN = 16
MAX_SLC = 64
WINDOW = 512
FORCE_BONUS = 1000.0
NEG_BIG = -1e30
LB_FLOOR = 1e-30
SB_HEADS = 4
HG_HEADS = 6
HG_SUB = 16
HG_PITCH = 24
NUM_BUCKETS = 32
MAX_DISTANCE = 128
D_NSA = NSA_HEADS * HEAD_DIM
D_KV = NSA_KV_GROUPS * HEAD_DIM
D_SB = SB_HEADS * HEAD_DIM
D_HG = HG_HEADS * HEAD_DIM
D_MIX = D_NSA + D_SB + D_HG
N_GATES = NSA_HEADS * 3
SPLIT_SIZES = (D_NSA, D_KV, D_KV, D_KV, D_KV, D_KV, D_KV, N_GATES, D_NSA,
               D_SB, D_SB, D_SB, D_SB, D_HG, D_HG, D_HG, D_HG)
ALPHA = (2 * DEPTH) ** 0.25
LN_EPS = 1e-5
RMS_EPS = 1e-6
LOG2E = math.log2(math.e)
QK_SCALE2 = LOG2E / math.sqrt(HEAD_DIM)

T_SEL = 128
T_NSA = 256
FAR_KEYS = 512
ONES_ROWS = 16
T_SB = 256
SB_DEAD_BITS = 151.0
T_HG = 1024
HG_UNROLL = T_HG // HG_SUB
T_PROJ = 512
VMEM_LIMIT = 56 * 1024 * 1024

_NT = (((1,), (1,)), ((), ()))
_TN = (((0,), (0,)), ((), ()))


def _dot(a, b):
    return jnp.dot(a, b, preferred_element_type=F32)


def _dot_nt(a, b):
    return lax.dot_general(a, b, _NT, preferred_element_type=F32)


def _dot_tn(a, b):
    return lax.dot_general(a, b, _TN, preferred_element_type=F32)


def _split_bf16(x):
    hi = x.astype(BF16)
    lo = (x - hi.astype(F32)).astype(BF16)
    return hi, lo


def _params(*sem):
    return pltpu.CompilerParams(dimension_semantics=sem, vmem_limit_bytes=VMEM_LIMIT)


def _project(x, wn_ref, wt_ref, o_refs, n_nat):
    xb = x.astype(BF16)
    nat = _dot(xb, wn_ref[...])
    off = 0
    for o_ref in o_refs[:n_nat]:
        o_ref[...] = nat[:, off:off + o_ref.shape[1]].astype(o_ref.dtype)
        off += o_ref.shape[1]
    tr = _dot_nt(wt_ref[...], xb)
    off = 0
    for o_ref in o_refs[n_nat:]:
        o_ref[...] = tr[off:off + o_ref.shape[0], :].astype(o_ref.dtype)
        off += o_ref.shape[0]


def _inproj_kernel(x_ref, wn_ref, wt_ref, *o_refs, n_nat):
    _project(x_ref[...], wn_ref, wt_ref, o_refs, n_nat)


def _proj_specs(m, w_nat, dt_nat, w_tr, dt_tr):
    wn = jnp.concatenate(w_nat, axis=1)
    wt = jnp.concatenate(w_tr, axis=0)
    w_specs = [pl.BlockSpec(wn.shape, lambda i: (0, 0)), pl.BlockSpec(wt.shape, lambda i: (0, 0))]
    out_specs = [pl.BlockSpec((T_PROJ, w.shape[1]), lambda i: (i, 0)) for w in w_nat]
    out_specs += [pl.BlockSpec((w.shape[0], T_PROJ), lambda i: (0, i)) for w in w_tr]
    out_shape = [jax.ShapeDtypeStruct((m, w.shape[1]), dt) for w, dt in zip(w_nat, dt_nat)]
    out_shape += [jax.ShapeDtypeStruct((w.shape[0], m), dt) for w, dt in zip(w_tr, dt_tr)]
    return (wn, wt), w_specs, out_specs, out_shape


def _inproj(x2d, w_nat, dt_nat, w_tr, dt_tr):
    m = x2d.shape[0]
    w, w_specs, out_specs, out_shape = _proj_specs(m, w_nat, dt_nat, w_tr, dt_tr)
    outs = pl.pallas_call(
        functools.partial(_inproj_kernel, n_nat=len(w_nat)),
        grid=(m // T_PROJ,),
        in_specs=[pl.BlockSpec((T_PROJ, D_MODEL), lambda i: (i, 0))] + w_specs,
        out_specs=out_specs, out_shape=out_shape,
        compiler_params=_params("parallel"), name="inproj",
    )(x2d, *w)
    return outs[:len(w_nat)], outs[len(w_nat):]


def _compress_kernel(ksrc_ref, vsrc_ref, pos_ref, w1k_ref, w2k_ref, w1v_ref, w2v_ref,
                     kc_ref, vc_ref, *, n_chunk):
    def hidden(src_ref, w1_ref):
        top = jnp.zeros((n_chunk, 2 * CMP_HIDDEN), F32)
        bot = jnp.zeros((n_chunk, 2 * CMP_HIDDEN), F32)
        for p in range(CMP_STRIDE):
            xp = src_ref[0, pl.ds(p, n_chunk, stride=CMP_STRIDE), :]
            top += _dot((xp + pos_ref[p:p + 1, :]).astype(BF16), w1_ref[p])
            q = CMP_STRIDE + p
            bot += _dot((xp + pos_ref[q:q + 1, :]).astype(BF16), w1_ref[q])
        hid = top + pltpu.roll(bot, n_chunk - 1, 0)
        return jax.nn.gelu(hid).astype(BF16)

    kc_ref[0] = _dot(hidden(ksrc_ref, w1k_ref), w2k_ref[...]).astype(kc_ref.dtype)
    vc_ref[0] = _dot_nt(w2v_ref[...], hidden(vsrc_ref, w1v_ref)).astype(vc_ref.dtype)


def _compress(kc_src, vc_src, pos2, w1k, w2k, w1v, w2v_t):
    b, s, _ = kc_src.shape
    n_chunk = s // CMP_STRIDE
    full = lambda a: pl.BlockSpec(a.shape, lambda i: (0,) * a.ndim)
    src = pl.BlockSpec((1, s, D_KV), lambda i: (i, 0, 0))
    return pl.pallas_call(
        functools.partial(_compress_kernel, n_chunk=n_chunk),
        grid=(b,),
        in_specs=[src, src, full(pos2), full(w1k), full(w2k), full(w1v), full(w2v_t)],
        out_specs=[pl.BlockSpec((1, n_chunk, D_KV), lambda i: (i, 0, 0)),
                   pl.BlockSpec((1, D_KV, n_chunk), lambda i: (i, 0, 0))],
        out_shape=[jax.ShapeDtypeStruct((b, n_chunk, D_KV), BF16),
                   jax.ShapeDtypeStruct((b, D_KV, n_chunk), BF16)],
        compiler_params=_params("parallel"), name="nsa_compress",
    )(kc_src, vc_src, pos2, w1k, w2k, w1v, w2v_t)


def _nsa_select_kernel(qt_ref, kc_ref, vct_ref, pbias_ref, ovl_ref, ocmp_ref, sel_ref, *, n_cmp_pad):
    T = T_SEL
    j = pl.program_id(1)
    t0 = j * T
    n_idx = lax.broadcasted_iota(jnp.int32, (n_cmp_pad, T), 0)
    tok_c = t0 + lax.broadcasted_iota(jnp.int32, (n_cmp_pad, T), 1)
    mask_c = tok_c >= CMP_STRIDE * n_idx + (CMP_BLOCK - 1)
    off = pl.multiple_of(n_cmp_pad - (T // CMP_STRIDE) * j, 8)
    psums = [None] * NSA_KV_GROUPS
    for h in range(NSA_HEADS):
        g = h // NSA_HPG
        qh = qt_ref[HEAD_DIM * h:HEAD_DIM * (h + 1), :]
        zq = jnp.zeros_like(qh)
        qh = jnp.concatenate([qh, zq] if g == 0 else [zq, qh], axis=0)
        s = _dot(kc_ref[0], qh) + pbias_ref[h, pl.ds(off, n_cmp_pad), :]
        s = jnp.where(mask_c, s, NEG_BIG)
        m = jnp.max(s, axis=0, keepdims=True)
        p = jnp.where(mask_c, jnp.exp2(s - m), 0.0)
        l = jnp.sum(p, axis=0, keepdims=True)
        p = p / jnp.where(l > 0.0, l, 1.0)
        o_both = _dot(vct_ref[0], p.astype(BF16))
        ocmp_ref[HEAD_DIM * h:HEAD_DIM * (h + 1), :] = o_both[HEAD_DIM * g:HEAD_DIM * (g + 1)]
        psums[g] = p if psums[g] is None else psums[g] + p

    jblk = lax.broadcasted_iota(jnp.int32, (MAX_SLC, T), 0)
    tok = t0 + lax.broadcasted_iota(jnp.int32, (MAX_SLC, T), 1)
    cur = lax.shift_right_logical(tok, 6)
    forced = (jblk == 0) | (jblk == cur) | (jblk == cur - 1)
    valid = jblk * SLC_BLOCK <= tok
    jsub = lax.broadcasted_iota(jnp.int32, (8, T), 0)
    for g in range(NSA_KV_GROUPS):
        imp = _dot(ovl_ref[...], psums[g].astype(BF16))
        score = jnp.where(valid, imp + jnp.where(forced, FORCE_BONUS, 0.0), NEG_BIG)
        blocks = [score[8 * rb:8 * rb + 8] for rb in range(MAX_SLC // 8)]
        ranks = [jnp.zeros((8, T), F32) for _ in blocks]
        for jp in range(MAX_SLC):
            other = score[jp:jp + 1, :]
            for rb, blk in enumerate(blocks):
                ge = jnp.where(other >= blk, 1.0, 0.0)
                gt = jnp.where(other > blk, 1.0, 0.0)
                if 8 * rb > jp:
                    inc = ge
                elif 8 * rb + 7 < jp:
                    inc = gt
                else:
                    inc = jnp.where(jsub > jp - 8 * rb, ge, gt)
                ranks[rb] = ranks[rb] + inc
        rank = jnp.concatenate(ranks, axis=0)
        sel_ref[0, MAX_SLC * g:MAX_SLC * (g + 1), :] = jnp.where(rank < float(SLC_TOPN), 0.0, NEG_BIG).astype(BF16)


def _nsa_select(q_t, kc, vc_t, pbias, ovl_t):
    b, n_cmp_pad, _ = kc.shape
    m = q_t.shape[1]
    nq = m // b // T_SEL
    cols = lambda r: pl.BlockSpec((r, T_SEL), lambda i, j: (0, i * nq + j))
    per_b = lambda a: pl.BlockSpec((1,) + a.shape[1:], lambda i, j: (i, 0, 0))
    full = lambda a: pl.BlockSpec(a.shape, lambda i, j: (0,) * a.ndim)
    return pl.pallas_call(
        functools.partial(_nsa_select_kernel, n_cmp_pad=n_cmp_pad),
        grid=(b, nq),
        in_specs=[cols(D_NSA), per_b(kc), per_b(vc_t), full(pbias), full(ovl_t)],
        out_specs=[cols(D_NSA), pl.BlockSpec((1, NSA_KV_GROUPS * MAX_SLC, T_SEL), lambda i, j: (i, 0, j))],
        out_shape=[jax.ShapeDtypeStruct((D_NSA, m), F32),
                   jax.ShapeDtypeStruct((b, NSA_KV_GROUPS * MAX_SLC, m // b), BF16)],
        compiler_params=_params("parallel", "parallel"), name="nsa_select",
    )(q_t, kc, vc_t, pbias, ovl_t)


def _online_init(m_ref, acc_ref):
    m_ref[...] = jnp.full(m_ref.shape, NEG_BIG, F32)
    acc_ref[...] = jnp.zeros(acc_ref.shape, F32)


def _scores(k_tile, q_ref, q_rows, bias_ref=None, ok=None):
    s = _dot(k_tile, q_ref[0:q_rows, :])
    if bias_ref is not None:
        s = s + bias_ref[...]
    if ok is not None:
        s = jnp.where(ok, s, NEG_BIG)
    return s


def _online_step(m_ref, acc_ref, s, v_ones):
    m = m_ref[...]
    m_new = jnp.maximum(m, jnp.max(s, axis=0, keepdims=True))
    alpha = jnp.exp2(m - m_new)
    p = jnp.exp2(s - m_new).astype(BF16)
    m_ref[...] = m_new
    half = p.shape[1] // NSA_KV_GROUPS
    for g in range(NSA_KV_GROUPS):
        cols = slice(g * half, (g + 1) * half)
        acc_ref[g] = alpha[:, cols] * acc_ref[g] + _dot(v_ones[g], p[:, cols])


def _nsa_attend_kernel(qt_ref, sel_ref, ocmp_ref, gt_ref, ks_ref, kw_ref, vs_ref, vw_ref, bias_ref, onehot_ref,
                       o_ref, kaug_ref, qaug_ref, vs1_ref, vw1_ref, ms_ref, accs_ref, mw_ref, accw_ref):
    T = T_NSA
    j = pl.program_id(1)

    @pl.when(j == 0)
    def _():
        kaug_ref[:, 0:LANES] = ks_ref[0]
        kaug_ref[:, LANES:2 * LANES] = onehot_ref[...]
        qaug_ref[...] = jnp.zeros(qaug_ref.shape, BF16)
        for v1_ref, v_ref in ((vs1_ref, vs_ref), (vw1_ref, vw_ref)):
            for g in range(NSA_KV_GROUPS):
                v1_ref[g, 0:HEAD_DIM, :] = v_ref[HEAD_DIM * g:HEAD_DIM * (g + 1), :]
                v1_ref[g, HEAD_DIM:HEAD_DIM + ONES_ROWS, :] = jnp.ones((ONES_ROWS, v1_ref.shape[2]), BF16)

    for h in range(NSA_HEADS):
        g = h // NSA_HPG
        qaug_ref[HEAD_DIM * g:HEAD_DIM * (g + 1), h * T:(h + 1) * T] = qt_ref[HEAD_DIM * h:HEAD_DIM * (h + 1), :]
        qaug_ref[LANES:LANES + MAX_SLC, h * T:(h + 1) * T] = sel_ref[0, MAX_SLC * g:MAX_SLC * (g + 1), :]

    key = lax.broadcasted_iota(jnp.int32, (T, NSA_HEADS * T), 0)
    qry = lax.broadcasted_iota(jnp.int32, (T, NSA_HEADS * T), 1) & (T - 1)
    causal = key <= qry
    st_s = (ms_ref, accs_ref)
    st_w = (mw_ref, accw_ref)
    _online_init(*st_s)
    _online_init(*st_w)

    far_len = jnp.maximum(j - 1, 0) * T

    def far_step(r0, n_keys):
        s = _scores(kaug_ref[pl.ds(r0, n_keys), :], qaug_ref, 2 * LANES)
        _online_step(*st_s, s, vs1_ref.at[:, :, pl.ds(r0, n_keys)])

    def far_big(c, carry):
        r0 = pl.multiple_of(c * (2 * FAR_KEYS), 2 * FAR_KEYS)
        s_lo = _scores(kaug_ref[pl.ds(r0, FAR_KEYS), :], qaug_ref, 2 * LANES)
        s_hi = _scores(kaug_ref[pl.ds(r0 + FAR_KEYS, FAR_KEYS), :], qaug_ref, 2 * LANES)
        _online_step(*st_s, s_lo, vs1_ref.at[:, :, pl.ds(r0, FAR_KEYS)])
        _online_step(*st_s, s_hi, vs1_ref.at[:, :, pl.ds(r0 + FAR_KEYS, FAR_KEYS)])
        return carry

    n_big = lax.shift_right_logical(far_len, FAR_KEYS.bit_length())
    lax.fori_loop(0, n_big, far_big, 0)
    done = n_big * (2 * FAR_KEYS)

    @pl.when(far_len - done >= FAR_KEYS)
    def _():
        far_step(pl.multiple_of(done, FAR_KEYS), FAR_KEYS)

    done = done + jnp.where(far_len - done >= FAR_KEYS, FAR_KEYS, 0)

    def far_small(c, carry):
        far_step(pl.multiple_of(done + c * T, T), T)
        return carry

    lax.fori_loop(0, lax.shift_right_logical(far_len - done, T.bit_length() - 1), far_small, 0)

    n_win = WINDOW // T
    starts = [pl.multiple_of(jnp.maximum(j - d, 0) * T, T) for d in range(n_win + 1)]
    in_range = [key < jnp.where(j - d >= 0, T, -1) for d in range(n_win + 1)]
    sel_scores, win_scores = {}, {}
    for d in (1, 0):
        sel_scores[d] = _scores(kaug_ref[pl.ds(starts[d], T), :], qaug_ref, 2 * LANES, bias_ref.at[d],
                                causal if d == 0 else in_range[d])
    for d in range(n_win + 1):
        if d == 0:
            ok = causal
        elif d == n_win:
            ok = (key > qry) & in_range[d]
        else:
            ok = in_range[d]
        win_scores[d] = _scores(kw_ref[0, pl.ds(starts[d], T), :], qaug_ref, LANES,
                                bias_ref.at[d] if d <= 1 else None, ok)
    for d in (1, 0):
        _online_step(*st_s, sel_scores[d], vs1_ref.at[:, :, pl.ds(starts[d], T)])
    for d in range(n_win + 1):
        _online_step(*st_w, win_scores[d], vw1_ref.at[:, :, pl.ds(starts[d], T)])

    gates = jax.nn.sigmoid(gt_ref[...])
    heads = []
    for h in range(NSA_HEADS):
        g = h // NSA_HPG
        cols = slice((h % NSA_HPG) * T, (h % NSA_HPG + 1) * T)
        o_sel = accs_ref[g, 0:HEAD_DIM, cols] * (gates[3 * h + 1:3 * h + 2, :] / accs_ref[g, HEAD_DIM:HEAD_DIM + 1, cols])
        o_win = accw_ref[g, 0:HEAD_DIM, cols] * (gates[3 * h + 2:3 * h + 3, :] / accw_ref[g, HEAD_DIM:HEAD_DIM + 1, cols])
        heads.append(gates[3 * h:3 * h + 1, :] * ocmp_ref[HEAD_DIM * h:HEAD_DIM * (h + 1), :] + o_sel + o_win)
    o_ref[0] = jnp.concatenate(heads, axis=0).T.astype(o_ref.dtype)


def _nsa_attend(q_t, sel_t, ocmp_t, g_t, ks, kw, vs_t, vw_t, bias_t, onehot):
    b, s, _ = ks.shape
    T = T_NSA
    nq = s // T
    R = NSA_HEADS * T
    cols = lambda r: pl.BlockSpec((r, T), lambda i, j: (0, i * nq + j))
    row_b = lambda a: pl.BlockSpec((a.shape[0], s), lambda i, j: (0, i))
    per_b = lambda a: pl.BlockSpec((1,) + a.shape[1:], lambda i, j: (i, 0, 0))
    full = lambda a: pl.BlockSpec(a.shape, lambda i, j: (0,) * a.ndim)
    v_ones = pltpu.VMEM((NSA_KV_GROUPS, HEAD_DIM + ONES_ROWS, s), BF16)
    stat = [pltpu.VMEM((1, R), F32), pltpu.VMEM((NSA_KV_GROUPS, HEAD_DIM + ONES_ROWS, R // NSA_KV_GROUPS), F32)]
    return pl.pallas_call(
        _nsa_attend_kernel,
        grid=(b, nq),
        in_specs=[cols(D_NSA), pl.BlockSpec((1, NSA_KV_GROUPS * MAX_SLC, T), lambda i, j: (i, 0, j)),
                  cols(D_NSA), cols(LANES), per_b(ks), per_b(kw), row_b(vs_t), row_b(vw_t),
                  full(bias_t), full(onehot)],
        out_specs=pl.BlockSpec((1, T, D_NSA), lambda i, j: (i, j, 0)),
        out_shape=jax.ShapeDtypeStruct((b, s, D_NSA), BF16),
        scratch_shapes=[pltpu.VMEM((s, 2 * LANES), BF16), pltpu.VMEM((2 * LANES, R), BF16), v_ones, v_ones]
        + stat + stat,
        compiler_params=_params("parallel", "arbitrary"), name="nsa_attend",
    )(q_t, sel_t, ocmp_t, g_t, ks, kw, vs_t, vw_t, bias_t, onehot)


def _sb_kernel(qt_ref, k_ref, vt_ref, tri_ref, o_ref, carry_ref, acc_ref):
    T = T_SB
    j = pl.program_id(1)
    n_pair = SB_HEADS // 2
    key = lax.broadcasted_iota(jnp.int32, (T, 2 * T), 0)
    qry = lax.broadcasted_iota(jnp.int32, (T, 2 * T), 1) & (T - 1)
    strict = key < qry
    carry_ref[...] = jnp.zeros(carry_ref.shape, F32)
    acc_ref[...] = jnp.zeros(acc_ref.shape, F32)

    def tiles(cs, masks):
        starts = [pl.multiple_of(c * T, T) for c in cs]
        units = [(ci, p) for ci in range(len(cs)) for p in range(n_pair)]
        z2s, incls = [], []
        for c, p in units:
            q_even = qt_ref[HEAD_DIM * (2 * p):HEAD_DIM * (2 * p + 1), :]
            q_odd = qt_ref[HEAD_DIM * (2 * p + 1):HEAD_DIM * (2 * p + 2), :]
            zq = jnp.zeros_like(q_even)
            q_pair = jnp.concatenate([jnp.concatenate([q_even, zq], axis=0),
                                      jnp.concatenate([zq, q_odd], axis=0)], axis=1)
            z2s.append(_dot(k_ref[0, pl.ds(starts[c], T), LANES * p:LANES * (p + 1)], q_pair))
        for (c, p), z2 in zip(units, z2s):
            rest = jnp.maximum(z2, 0.0) + jnp.log2(1.0 + jnp.exp2(-jnp.abs(z2)))
            if masks[c] is not None:
                rest = jnp.where(masks[c], rest, 0.0)
            hi, lo = _split_bf16(rest)
            incls.append(_dot(tri_ref[...], jnp.concatenate([hi, lo], axis=0)))
        carry = [carry_ref[p] for p in range(n_pair)]
        acc = [acc_ref[p] for p in range(n_pair)]
        for (c, p), z2, incl in zip(units, z2s, incls):
            a = jnp.exp2(z2 - incl - carry[p])
            if masks[c] is not None:
                a = jnp.where(masks[c], a, 0.0)
            acc[p] = acc[p] + _dot(vt_ref[LANES * p:LANES * (p + 1), pl.ds(starts[c], T)], a.astype(BF16))
            carry[p] = carry[p] + incl[0:1, :]
        for p in range(n_pair):
            carry_ref[p] = carry[p]
            acc_ref[p] = acc[p]

    @pl.when(j == 0)
    def _():
        tiles([0], [strict])

    @pl.when(j > 0)
    def _():
        tiles([j, j - 1], [strict, None])

    def live(c):
        return (c >= 0) & (jnp.min(carry_ref[...]) < SB_DEAD_BITS)

    def back_one(c):
        tiles([c], [None])
        return c - 1

    lax.while_loop(live, back_one, j - 2)
    row = lax.broadcasted_iota(jnp.int32, (LANES, T), 0)
    for p in range(n_pair):
        acc = acc_ref[p]
        o_ref[0, :, LANES * p:LANES * (p + 1)] = (
            jnp.where(row < HEAD_DIM, acc[:, 0:T], acc[:, T:2 * T]).T.astype(o_ref.dtype))


def _sb_attention(q_t, k, v_t, tri_t):
    b, s, _ = k.shape
    T = T_SB
    nq = s // T
    n_pair = SB_HEADS // 2
    return pl.pallas_call(
        _sb_kernel,
        grid=(b, nq),
        in_specs=[pl.BlockSpec((D_SB, T), lambda i, j: (0, i * nq + j)),
                  pl.BlockSpec((1, s, D_SB), lambda i, j: (i, 0, 0)),
                  pl.BlockSpec((D_SB, s), lambda i, j: (0, i)),
                  pl.BlockSpec(tri_t.shape, lambda i, j: (0, 0))],
        out_specs=pl.BlockSpec((1, T, D_SB), lambda i, j: (i, j, 0)),
        out_shape=jax.ShapeDtypeStruct((b, s, D_SB), BF16),
        scratch_shapes=[pltpu.VMEM((n_pair, 1, 2 * T), F32), pltpu.VMEM((n_pair, LANES, 2 * T), F32)],
        compiler_params=_params("parallel", "parallel"), name="sb_attention",
    )(q_t, k, v_t, tri_t)


def _hgrn_kernel(q_ref, f_ref, i_ref, lb_ref, nw_ref, ones_ref, bd_ref, o_ref,
                 st_ref, qj_ref, kj_ref, fj_ref, vj_ref, qd_ref, kd_ref, oc_ref, x_ref, u_ref, sb_ref, dec_ref,
                 att_ref, oall_ref, fp_ref, qp_ref, vn_ref):
    C = HG_SUB
    n_blk = T_HG // C
    for src_ref, dst_ref in ((f_ref, fp_ref), (q_ref, qp_ref), (i_ref, vn_ref)):
        for blk in range(n_blk):
            dst_ref[HG_PITCH * blk:HG_PITCH * blk + C, :] = src_ref[0, C * blk:C * (blk + 1), :]

    @pl.when(pl.program_id(2) == 0)
    def _():
        st_ref[...] = jnp.zeros(st_ref.shape, F32)

    lb = lb_ref[...]
    lb_floor = jnp.maximum(lb, LB_FLOOR)
    one_m_lb = 1.0 - lb
    cum = None
    for j in range(C):
        xj = fp_ref[pl.ds(j, n_blk, stride=HG_PITCH), :]
        e = jnp.exp(-jnp.abs(xj))
        r = 1.0 / (1.0 + e)
        er = e * r
        pos = xj >= 0.0
        fj = lb_floor + one_m_lb * jnp.where(pos, r, er)
        kj_ref[j] = one_m_lb * jnp.where(pos, er, r)
        fj_ref[j] = fj
        cum = fj if cum is None else cum * fj
        qj = qp_ref[pl.ds(j, n_blk, stride=HG_PITCH), :]
        qj_ref[j] = qj
        vj_ref[j] = vn_ref[pl.ds(j, n_blk, stride=HG_PITCH), :]
        qd_ref[pl.ds(j, n_blk, stride=HG_PITCH), :] = qj * cum
    dec_ref[...] = cum
    tail = None
    for j in reversed(range(C)):
        kd_ref[pl.ds(j, n_blk, stride=HG_PITCH), :] = kj_ref[j] if tail is None else kj_ref[j] * tail
        tail = fj_ref[j] if tail is None else tail * fj_ref[j]

    def kv_products(blk, carry):
        kd = kd_ref[pl.ds(pl.multiple_of(blk * HG_PITCH, 8), C), :].astype(BF16)
        vv = i_ref[0, pl.ds(pl.multiple_of(blk * C, C), C), :].astype(BF16)
        u_ref[blk] = bd_ref[...] * _dot_tn(vv, kd)
        return carry

    lax.fori_loop(0, n_blk, kv_products, 0, unroll=HG_UNROLL)

    def scan(blk, st):
        sb_ref[blk] = st.astype(BF16)
        return st * dec_ref[pl.ds(blk, 1), :] + u_ref[blk]

    st_ref[...] = lax.fori_loop(0, n_blk, scan, st_ref[...], unroll=HG_UNROLL)

    def outputs(blk, carry):
        r0 = pl.multiple_of(blk * HG_PITCH, 8)
        oc_ref[pl.ds(r0, C), :] = _dot_nt(qd_ref[pl.ds(r0, C), :].astype(BF16), sb_ref[blk])
        return carry

    lax.fori_loop(0, n_blk, outputs, 0, unroll=HG_UNROLL)

    base = lambda j: (j * (j + 1) // 2) * n_blk
    for j in range(C):
        qdec = qj_ref[j]
        for jp in reversed(range(j + 1)):
            x_ref[base(j) + jp * n_blk:base(j) + (jp + 1) * n_blk, :] = (qdec * kj_ref[jp]).astype(BF16)
            if jp > 0:
                qdec = qdec * fj_ref[jp]
        att_ref[base(j):base(j + 1), :] = _dot(x_ref[base(j):base(j + 1), :], ones_ref[...])
    for j in range(C):
        oj = oc_ref[pl.ds(j, n_blk, stride=HG_PITCH), :]
        for jp in range(j + 1):
            oj = oj + att_ref[base(j) + jp * n_blk:base(j) + (jp + 1) * n_blk, :] * vj_ref[jp]
        oall_ref[j * n_blk:(j + 1) * n_blk, :] = oj
    o_all = oall_ref[...]
    hi, lo = _split_bf16(o_all * o_all)
    ms = (_dot(hi, ones_ref[...]) + _dot(lo, ones_ref[...])) * (1.0 / HEAD_DIM)
    oall_ref[...] = o_all * lax.rsqrt(ms + RMS_EPS) * nw_ref[...]
    for j in range(C):
        o_ref[0, pl.ds(j, n_blk, stride=C), :] = oall_ref[j * n_blk:(j + 1) * n_blk, :]


def _hgrn(q, f, i, lb, nw, ones_bd, bd_mask):
    b, s, _ = q.shape
    n_blk = T_HG // HG_SUB
    tile = pl.BlockSpec((1, T_HG, LANES), lambda bi, pi, ti: (bi, ti, pi))
    vec = pl.BlockSpec((1, LANES), lambda bi, pi, ti: (0, pi))
    full = lambda a: pl.BlockSpec(a.shape, lambda bi, pi, ti: (0,) * a.ndim)
    jm = pltpu.VMEM((HG_SUB, n_blk, LANES), F32)
    nat = pltpu.VMEM((n_blk * HG_PITCH, LANES), F32)
    return pl.pallas_call(
        _hgrn_kernel,
        grid=(b, D_HG // LANES, s // T_HG),
        in_specs=[tile, tile, tile, vec, vec, full(ones_bd), full(bd_mask)],
        out_specs=tile,
        out_shape=jax.ShapeDtypeStruct((b, s, D_HG), F32),
        scratch_shapes=[pltpu.VMEM((LANES, LANES), F32), jm, jm, jm, jm, nat, nat, nat,
                        pltpu.VMEM((HG_SUB * (HG_SUB + 1) // 2 * n_blk, LANES), BF16),
                        pltpu.VMEM((n_blk, LANES, LANES), F32),
                        pltpu.VMEM((n_blk, LANES, LANES), BF16), pltpu.VMEM((n_blk, LANES), F32),
                        pltpu.VMEM((HG_SUB * (HG_SUB + 1) // 2 * n_blk, LANES), F32), pltpu.VMEM((T_HG, LANES), F32),
                        nat, nat, nat],
        compiler_params=_params("parallel", "parallel", "arbitrary"), name="hgrn2",
    )(q, f, i, lb, nw, ones_bd, bd_mask)


def _mix_norm(onsa_ref, osb_ref, ohg_ref, z_ref, x_ref, w_ref, g_ref, b_ref):
    z = z_ref[...].astype(F32)
    sz = z * jax.nn.sigmoid(z)
    mixed = jnp.concatenate([(onsa_ref[...] * sz[:, 0:D_NSA]).astype(BF16),
                             (osb_ref[...] * sz[:, D_NSA:D_NSA + D_SB]).astype(BF16),
                             (ohg_ref[...] * sz[:, D_NSA + D_SB:D_MIX]).astype(BF16)], axis=1)
    v = ALPHA * x_ref[...] + _dot(mixed, w_ref[...])
    mu = jnp.mean(v, axis=-1, keepdims=True)
    vc = v - mu
    var = jnp.mean(vc * vc, axis=-1, keepdims=True)
    return vc * lax.rsqrt(var + LN_EPS) * g_ref[...] + b_ref[...]


def _out_kernel(*refs):
    refs[8][...] = _mix_norm(*refs[:8])


def _out_in_kernel(*refs, n_nat):
    x_new = _mix_norm(*refs[:8])
    refs[10][...] = x_new
    _project(x_new, refs[8], refs[9], refs[11:], n_nat)


def _out_proj(o_nsa, o_sb, o_hg, z_all, x2d, w, g, bvec, next_proj=None):
    m = x2d.shape[0]
    rows = lambda n: pl.BlockSpec((T_PROJ, n), lambda i: (i, 0))
    full = lambda a: pl.BlockSpec(a.shape, lambda i: (0, 0))
    in_specs = [rows(D_NSA), rows(D_SB), rows(D_HG), rows(D_MIX), rows(D_MODEL), full(w), full(g), full(bvec)]
    x_shape = jax.ShapeDtypeStruct((m, D_MODEL), F32)
    if next_proj is None:
        return pl.pallas_call(
            _out_kernel, grid=(m // T_PROJ,), in_specs=in_specs, out_specs=rows(D_MODEL), out_shape=x_shape,
            compiler_params=_params("parallel"), name="out_proj_norm",
        )(o_nsa, o_sb, o_hg, z_all, x2d, w, g, bvec)
    w_nat = next_proj[0]
    wp, w_specs, out_specs, out_shape = _proj_specs(m, *next_proj)
    outs = pl.pallas_call(
        functools.partial(_out_in_kernel, n_nat=len(w_nat)), grid=(m // T_PROJ,),
        in_specs=in_specs + w_specs, out_specs=[rows(D_MODEL)] + out_specs, out_shape=[x_shape] + out_shape,
        compiler_params=_params("parallel"), name="out_proj_norm_inproj",
    )(o_nsa, o_sb, o_hg, z_all, x2d, w, g, bvec, *wp)
    return outs[0], outs[1:1 + len(w_nat)], outs[1 + len(w_nat):]


def _t5_bucket_np(rel):
    n = np.maximum(rel, 0)
    max_exact = NUM_BUCKETS // 2
    large = max_exact + (np.log(np.maximum(n, 1).astype(np.float32) / max_exact)
                         / math.log(MAX_DISTANCE / max_exact) * (NUM_BUCKETS - max_exact)).astype(np.int32)
    large = np.clip(large, 0, NUM_BUCKETS - 1)
    return np.where(n < max_exact, n, large).astype(np.int32)


def _bias_tables(rel_bias, s):
    tbl = ((rel_bias - rel_bias[NUM_BUCKETS - 1]) * LOG2E).astype(F32)

    def expand(rel):
        onehot = (jnp.arange(NUM_BUCKETS)[:, None] == jnp.asarray(_t5_bucket_np(rel).reshape(1, -1))).astype(F32)
        return jnp.dot(tbl.T, onehot, precision=lax.Precision.HIGHEST).reshape((NSA_HEADS,) + rel.shape)

    n_cmp_pad = s // CMP_STRIDE
    n_rel = np.arange(2 * n_cmp_pad)[:, None] - n_cmp_pad
    r = np.arange(T_SEL)[None, :]
    pbias = expand(r - CMP_STRIDE * n_rel - (CMP_BLOCK - 1))
    T = T_NSA
    key = np.arange(T)[:, None]
    qry = np.arange(T)[None, :]
    near = np.stack([qry - key, T + qry - key])
    bias_t = jnp.transpose(expand(near), (1, 2, 0, 3)).reshape(2, T, NSA_HEADS * T)
    return pbias, bias_t


def _static_tables(s):
    n_chunk = s // CMP_STRIDE
    cmp_start = np.arange(n_chunk) * CMP_STRIDE
    slc_start = np.arange(MAX_SLC) * SLC_BLOCK
    ovl_t = ((cmp_start[None, :] < slc_start[:, None] + SLC_BLOCK)
             & (cmp_start[None, :] + CMP_BLOCK > slc_start[:, None])
             & (cmp_start[None, :] + CMP_BLOCK <= s)).astype(np.float32)
    onehot = (np.arange(s)[:, None] // SLC_BLOCK == np.arange(LANES)[None, :]).astype(np.float32)
    tri = (np.arange(T_SB)[None, :] >= np.arange(T_SB)[:, None]).astype(np.float32)
    tri = np.concatenate([tri, tri], axis=1)
    ones_bd = np.kron(np.eye(2), np.ones((HEAD_DIM, HEAD_DIM))).astype(np.float32)
    as_bf16 = lambda a: jnp.asarray(a, dtype=BF16)
    return as_bf16(ovl_t), as_bf16(onehot), as_bf16(tri), as_bf16(ones_bd), jnp.asarray(ones_bd)


def _layer_weights(w_in_l, cmp_pos_l, w_ck1_l, w_ck2_l, w_cv1_l, w_cv2_l):
    offs = np.cumsum((0,) + SPLIT_SIZES)
    (w_q, w_kc, w_vc, w_ks, w_vs, w_kw, w_vw, w_g, w_nz,
     w_sq, w_sk, w_sv, w_sz, w_hq, w_hf, w_hi, w_hz) = [w_in_l[:, offs[i]:offs[i + 1]] for i in range(len(SPLIT_SIZES))]
    w_gp = jnp.concatenate([w_g, jnp.zeros((D_MODEL, LANES - N_GATES), F32)], axis=1)
    w_z = jnp.concatenate([w_nz, w_sz, w_hz], axis=1)
    w_nat = [w_kc, w_vc, w_ks, w_kw, w_z, w_sk, w_hq, w_hf, w_hi]
    dt_nat = [F32, F32, BF16, BF16, BF16, BF16, F32, F32, F32]
    w_tr = [(w_q * QK_SCALE2).T, w_vs.T, w_vw.T, w_gp.T, (w_sq * QK_SCALE2).T, w_sv.T]
    dt_tr = [BF16, BF16, BF16, F32, BF16, BF16]
    w_nat = [w.astype(BF16) for w in w_nat]
    w_tr = [w.astype(BF16) for w in w_tr]

    def block_diag(w):
        z = jnp.zeros_like(w)
        return jnp.concatenate([jnp.concatenate([w, z], axis=-1), jnp.concatenate([z, w], axis=-1)], axis=-2)

    pos2 = jnp.concatenate([cmp_pos_l, cmp_pos_l], axis=1)
    w1k = block_diag(w_ck1_l.reshape(CMP_BLOCK, HEAD_DIM, CMP_HIDDEN)).astype(BF16)
    w1v = block_diag(w_cv1_l.reshape(CMP_BLOCK, HEAD_DIM, CMP_HIDDEN)).astype(BF16)
    w2k = block_diag(w_ck2_l).astype(BF16)
    w2v_t = block_diag(w_cv2_l).T.astype(BF16)
    return (w_nat, dt_nat, w_tr, dt_tr), (pos2, w1k, w2k, w1v, w2v_t)


def kernel(x, w_in, cmp_pos, w_ck1, w_ck2, w_cv1, w_cv2, hg_lb, hg_norm_w, w_out, ln_g, ln_b, rel_bias):
    b, s, d = x.shape
    assert d == D_MODEL and s % T_HG == 0 and s // SLC_BLOCK <= MAX_SLC and s >= WINDOW + T_NSA
    lb_w = jax.nn.softmax(hg_lb.astype(F32), axis=0)
    lb_all = jnp.cumsum(lb_w, axis=0) - lb_w[0]
    pbias, bias_t = _bias_tables(rel_bias, s)
    ovl_t, onehot, tri, ones_bd, bd_mask = _static_tables(s)

    x2d = x.reshape(b * s, d)
    weights = [_layer_weights(w_in[l], cmp_pos[l], w_ck1[l], w_ck2[l], w_cv1[l], w_cv2[l]) for l in range(DEPTH)]
    nat, trs = _inproj(x2d, *weights[0][0])
    for l in range(DEPTH):
        q_t, vs_t, vw_t, g_t, sq_t, sv_t = trs
        kc_src, vc_src, ks, kw, z_all, sk, hq, hf, hi = [o.reshape(b, s, o.shape[-1]) for o in nat]
        kc, vc_t = _compress(kc_src, vc_src, *weights[l][1])
        ocmp_t, sel_t = _nsa_select(q_t, kc, vc_t, pbias, ovl_t)
        o_nsa = _nsa_attend(q_t, sel_t, ocmp_t, g_t, ks, kw, vs_t, vw_t, bias_t, onehot)
        o_sb = _sb_attention(sq_t, sk, sv_t, tri)
        o_hg = _hgrn(hq, hf, hi, lb_all[l][None, :], hg_norm_w[l][None, :], ones_bd, bd_mask)
        res = _out_proj(o_nsa.reshape(b * s, D_NSA), o_sb.reshape(b * s, D_SB), o_hg.reshape(b * s, D_HG),
                        z_all.reshape(b * s, D_MIX), x2d, w_out[l].astype(BF16),
                        ln_g[l][None, :], ln_b[l][None, :],
                        next_proj=weights[l + 1][0] if l + 1 < DEPTH else None)
        if l + 1 < DEPTH:
            x2d, nat, trs = res
        else:
            x2d = res
    return x2d.reshape(b, s, d)
```
